```python
import jax, jax.numpy as jnp
from jax import lax
import numpy as np

D_MODEL = 1024
BATCH = 32
SEQ = 256
DEPTH = 2
DEC_BATCH = 8
DEC_SEQ = 2048
PAST_LEN = 256

GRID_W = 64
D_MIX = 1024
EPS = 1e-6
ROPE_THETA = 10000.0
NEG_INF = -1e30
N_MOD = 9
D_FF = 2816
QBLOCK = 128
H_A = 4
KV_A = 2
G_A = H_A // KV_A
HD_A = 64
WINDOW_A = 128
BLOCK_A = 128
H_B = 4
DK_B = 64
DV_B = 64
CHUNK_B = 16
H_C = 4
Q_LORA_C = 256
KV_LORA_C = 128
NOPE_C = 64
ROPE_C = 32
V_C = 64
POOL_WINDOWS = (2, 4, 8, 16)
POOL_GROUPS = 4
POOL_CH = 64

SPLIT_SIZES = (H_A * HD_A, KV_A * HD_A, KV_A * HD_A,
               H_B * DK_B, H_B * DV_B, H_B * DK_B, H_B * DK_B, H_B * DV_B,
               Q_LORA_C, KV_LORA_C, ROPE_C,
               POOL_GROUPS * POOL_CH)
SPLIT_POINTS = tuple(int(s) for s in np.cumsum(SPLIT_SIZES)[:-1])
D_IN = int(sum(SPLIT_SIZES))

kernel_name = "hybrid_prefix_diffusion_trunk_step"


def rms_norm(x, g):
    xf = x.astype(jnp.float32)
    y = xf * lax.rsqrt(jnp.mean(xf * xf, axis=-1, keepdims=True) + EPS)
    return (y * g.astype(jnp.float32)).astype(x.dtype)


def swiglu(h, wg, wu, wd):
    return (jax.nn.silu(h @ wg) * (h @ wu)) @ wd


def modulation(cvec, w_ada, b_ada):
    m = jax.nn.silu(cvec) @ w_ada + b_ada
    return m.reshape(cvec.shape[0], 1, N_MOD, D_MODEL)


def axial_rope_tables(rows, dim):
    n_freq = dim // 4
    inv = ROPE_THETA ** (-jnp.arange(n_freq, dtype=jnp.float32) / n_freq)
    row_id = jnp.repeat(jnp.arange(rows, dtype=jnp.float32), GRID_W)
    col_id = jnp.tile(jnp.arange(GRID_W, dtype=jnp.float32), rows)
    ang = jnp.concatenate([row_id[:, None] * inv, col_id[:, None] * inv], axis=-1)
    return jnp.cos(ang), jnp.sin(ang)


def apply_rope(x, cos, sin):
    half = x.shape[-1] // 2
    xf = x.astype(jnp.float32)
    x1, x2 = xf[..., :half], xf[..., half:]
    return jnp.concatenate([x1 * cos - x2 * sin, x1 * sin + x2 * cos], axis=-1).astype(x.dtype)


def sink_softmax(logits, sink):
    full = jnp.concatenate([logits, jnp.broadcast_to(sink, logits.shape[:-1] + (1,))], axis=-1)
    return jax.nn.softmax(full, axis=-1)[..., :-1]


def block_attention(q, k, v, sink=None):
    bq, hk, g, t, d = q.shape
    nb = t // QBLOCK
    scale = d ** -0.5
    qb = jnp.moveaxis(q.reshape(bq, hk, g, nb, QBLOCK, d), 3, 0)

    def attend(qblk):
        s = jnp.einsum('bkgqd,bksd->bkgqs', qblk, k).astype(jnp.float32) * scale
        p = jax.nn.softmax(s, axis=-1) if sink is None else sink_softmax(s, sink[None, :, :, None, None])
        return jnp.einsum('bkgqs,bksd->bkgqd', p.astype(v.dtype), v)

    o = lax.map(attend, qb)
    return jnp.moveaxis(o, 0, 3).reshape(bq, hk, g, t, v.shape[-1])


def window_attention(q, k, v, kc, vc, sink):
    bq, t = q.shape[:2]
    nb = t // BLOCK_A
    scale = HD_A ** -0.5
    qb = q.reshape(bq, nb, BLOCK_A, KV_A, G_A, HD_A)

    def band(x):
        xp = jnp.pad(x, ((0, 0), (BLOCK_A, BLOCK_A), (0, 0), (0, 0))).reshape(bq, nb + 2, BLOCK_A, KV_A, HD_A)
        return jnp.concatenate([xp[:, :-2], xp[:, 1:-1], xp[:, 2:]], axis=2)

    kb, vb = band(k), band(v)
    s_loc = jnp.einsum('bnqkgd,bnskd->bkgnqs', qb, kb).astype(jnp.float32) * scale
    s_ctx = jnp.einsum('bnqkgd,bksd->bkgnqs', qb, kc).astype(jnp.float32) * scale
    q_pos = jnp.arange(nb)[:, None, None] * BLOCK_A + jnp.arange(BLOCK_A)[None, :, None]
    k_pos = (jnp.arange(nb)[:, None, None] - 1) * BLOCK_A + jnp.arange(3 * BLOCK_A)[None, None, :]
    valid = (jnp.abs(k_pos - q_pos) <= WINDOW_A) & (k_pos >= 0) & (k_pos < t)
    s_loc = jnp.where(valid, s_loc, NEG_INF)
    p = sink_softmax(jnp.concatenate([s_loc, s_ctx], axis=-1), sink[None, :, :, None, None, None])
    p_loc, p_ctx = p[..., :3 * BLOCK_A].astype(v.dtype), p[..., 3 * BLOCK_A:].astype(v.dtype)
    o = (jnp.einsum('bkgnqs,bnskd->bnqkgd', p_loc, vb)
         + jnp.einsum('bkgnqs,bksd->bnqkgd', p_ctx, vc))
    return o.reshape(bq, t, H_A * HD_A)


def hgrn_lower_bounds(lb_logits):
    p = jax.nn.softmax(lb_logits.astype(jnp.float32), axis=0)
    return jnp.cumsum(p, axis=0) - p[0:1]


def gla_chunk_scan(q, k, v, log_f, s0):
    bq, h, t, _ = q.shape
    n = t // CHUNK_B

    def chunks(a):
        return a.astype(jnp.float32).reshape(bq, h, n, CHUNK_B, a.shape[-1])

    qc, kc, vc = chunks(q), chunks(k), chunks(v)
    b = jnp.cumsum(chunks(log_f), axis=3)
    b_last = b[:, :, :, -1]
    causal = jnp.tril(jnp.ones((CHUNK_B, CHUNK_B), dtype=bool))[:, :, None]
    decay = jnp.exp(jnp.where(causal, b[:, :, :, :, None, :] - b[:, :, :, None, :, :], -jnp.inf))
    scores = jnp.einsum('bhntd,bhnsd,bhntsd->bhnts', qc, kc, decay)
    o_intra = jnp.einsum('bhnts,bhnsv->bhntv', scores, vc)
    u = jnp.einsum('bhnsd,bhnsv->bhndv', kc * jnp.exp(b_last[:, :, :, None, :] - b), vc)

    def step(s, xs):
        chunk_decay, chunk_u = xs
        return chunk_decay[..., None] * s + chunk_u, s

    s_final, s_prev = lax.scan(step, s0, (jnp.moveaxis(jnp.exp(b_last), 2, 0), jnp.moveaxis(u, 2, 0)))
    o_inter = jnp.einsum('bhntd,nbhdv->bhntv', qc * jnp.exp(b), s_prev)
    return (o_intra + o_inter).reshape(bq, h, t, v.shape[-1]), s_final


def hgrn_mixer(q, i, f_fwd, f_bwd, g, lb_fwd, lb_bwd, norm_g, s0_fwd, s0_bwd):
    bq, t, _ = q.shape

    def heads(a, d):
        return a.reshape(bq, t, H_B, d).transpose(0, 2, 1, 3)

    qh, vh = heads(q, DK_B), heads(i, DV_B)

    def scan_dir(f_pre, lb, s0, reverse):
        f = lb + (1.0 - lb) * jax.nn.sigmoid(f_pre.astype(jnp.float32))
        args = (qh, heads(1.0 - f, DK_B), vh, heads(jnp.log(f), DK_B))
        if reverse:
            args = tuple(jnp.flip(a, axis=2) for a in args)
        o, s_fin = gla_chunk_scan(*args, s0.astype(jnp.float32))
        return (jnp.flip(o, axis=2) if reverse else o), s_fin

    o_f, s_f = scan_dir(f_fwd, lb_fwd, s0_fwd, False)
    o_b, s_b = scan_dir(f_bwd, lb_bwd, s0_bwd, True)
    o = (o_f + o_b).transpose(0, 2, 1, 3)
    o = rms_norm(o, norm_g) * jax.nn.silu(g.reshape(bq, t, H_B, DV_B).astype(jnp.float32))
    return o.reshape(bq, t, H_B * DV_B).astype(q.dtype), jnp.stack([s_f, s_b], axis=1).astype(q.dtype)


def mla_project(c_q, c_kv, q_norm, kv_norm, w_uq):
    bq, t, _ = c_q.shape
    qf = (rms_norm(c_q, q_norm) @ w_uq).reshape(bq, t, H_C, NOPE_C + ROPE_C)
    return qf[..., :NOPE_C], qf[..., NOPE_C:], rms_norm(c_kv, kv_norm)


def mla_keys_values(ckv, k_rope, w_ukv):
    bq, s, _ = ckv.shape
    kv = (ckv @ w_ukv).reshape(bq, s, H_C, NOPE_C + V_C)
    k = jnp.concatenate([kv[..., :NOPE_C], jnp.broadcast_to(k_rope[:, :, None, :], (bq, s, H_C, ROPE_C))], axis=-1)
    return k, kv[..., NOPE_C:]


def mla_attend(q, k, v):
    bq, t = q.shape[:2]
    o = block_attention(q.transpose(0, 2, 1, 3)[:, :, None], k.transpose(0, 2, 1, 3), v.transpose(0, 2, 1, 3))
    return o[:, :, 0].transpose(0, 2, 1, 3).reshape(bq, t, H_C * V_C)


def pool_mixer(h, w_pool, scale):
    bq, t, _ = h.shape
    hg = h.reshape(bq, t, POOL_GROUPS, POOL_CH)
    cs = jnp.concatenate([jnp.zeros((bq, 1, POOL_GROUPS, POOL_CH), jnp.float32),
                          jnp.cumsum(hg.astype(jnp.float32), axis=1)], axis=1)
    pos = jnp.arange(t)
    outs = []
    for gi, w in enumerate(POOL_WINDOWS):
        lo = jnp.clip(pos - w // 2, 0, t)
        hi = jnp.clip(pos - w // 2 + w, 0, t)
        cs_g = cs[:, :, gi]
        mean = (cs_g[:, hi] - cs_g[:, lo]) / (hi - lo).astype(jnp.float32)[:, None]
        outs.append((mean.astype(h.dtype) - hg[:, :, gi]) @ w_pool[gi])
    return jnp.concatenate(outs, axis=-1) * scale


def ffn_sublayer(x, m, j, norm_g, wg, wu, wd):
    h = rms_norm(x, norm_g) * (1.0 + m[:, :, 3 * j + 1]) + m[:, :, 3 * j]
    return x + 0.5 * m[:, :, 3 * j + 2] * swiglu(h, wg, wu, wd)


def mixer_inputs(x, m, norm_g, w_in):
    h = rms_norm(x, norm_g) * (1.0 + m[:, :, 4]) + m[:, :, 3]
    return jnp.split(h @ w_in, SPLIT_POINTS, axis=-1)


def context_mixers(parts, lw):
    a_q, a_k, a_v, b_q, b_i, b_ff, b_fb, b_g, c_q, c_kv, c_kr, d_h = parts
    bq, t, _ = a_q.shape
    qa = a_q.reshape(bq, t, KV_A, G_A, HD_A).transpose(0, 2, 3, 1, 4)
    ka = a_k.reshape(bq, t, KV_A, HD_A).transpose(0, 2, 1, 3)
    va = a_v.reshape(bq, t, KV_A, HD_A).transpose(0, 2, 1, 3)
    oa = block_attention(qa, ka, va, lw['sink']).transpose(0, 3, 1, 2, 4).reshape(bq, t, H_A * HD_A)
    s0 = jnp.zeros((bq, H_B, DK_B, DV_B), jnp.float32)
    ob, st = hgrn_mixer(b_q, b_i, b_ff, b_fb, b_g, lw['lb_fwd'], lw['lb_bwd'], lw['hgrn_norm'], s0, s0)
    qn, qr, ckv = mla_project(c_q, c_kv, lw['q_norm'], lw['kv_norm'], lw['w_uq'])
    kc, vc = mla_keys_values(ckv, c_kr, lw['w_ukv'])
    oc = mla_attend(jnp.concatenate([qn, qr], axis=-1), kc, vc)
    od = pool_mixer(d_h, lw['pool_w'], lw['pool_scale'])
    return jnp.concatenate([oa, ob, oc, od], axis=-1), (ka, va, st, ckv, c_kr)


def latent_mixers(parts, lw, kc_a, vc_a, st, ckv_c, kr_c, rope_a, rope_c):
    a_q, a_k, a_v, b_q, b_i, b_ff, b_fb, b_g, c_q, c_kv, c_kr, d_h = parts
    bq, t, _ = a_q.shape
    cos_a, sin_a = rope_a
    cos_c, sin_c = rope_c
    qa = apply_rope(a_q.reshape(bq, t, H_A, HD_A), cos_a[:, None, :], sin_a[:, None, :])
    ka = apply_rope(a_k.reshape(bq, t, KV_A, HD_A), cos_a[:, None, :], sin_a[:, None, :])
    oa = window_attention(qa, ka, a_v.reshape(bq, t, KV_A, HD_A), kc_a, vc_a, lw['sink'])
    ob, _ = hgrn_mixer(b_q, b_i, b_ff, b_fb, b_g, lw['lb_fwd'], lw['lb_bwd'], lw['hgrn_norm'], st[:, 0], st[:, 1])
    qn, qr, ckv = mla_project(c_q, c_kv, lw['q_norm'], lw['kv_norm'], lw['w_uq'])
    qr = apply_rope(qr, cos_c[:, None, :], sin_c[:, None, :])
    k_lat, v_lat = mla_keys_values(ckv, apply_rope(c_kr, cos_c, sin_c), lw['w_ukv'])
    k_ctx, v_ctx = mla_keys_values(ckv_c, kr_c, lw['w_ukv'])
    oc = mla_attend(jnp.concatenate([qn, qr], axis=-1),
                    jnp.concatenate([k_lat, k_ctx], axis=1), jnp.concatenate([v_lat, v_ctx], axis=1))
    od = pool_mixer(d_h, lw['pool_w'], lw['pool_scale'])
    return jnp.concatenate([oa, ob, oc, od], axis=-1)


def setup_inputs(seed: int = 0) -> dict:
    key = jax.random.key(seed)
    keys = jax.random.split(key, 32)

    def nrm(i, shape, scale=1.0):
        return jax.random.normal(keys[i], shape, jnp.float32) * scale

    return {
        "x_prompt": nrm(0, (BATCH, SEQ, D_MODEL)),
        "x_sample": nrm(1, (DEC_BATCH, DEC_SEQ, D_MODEL)),
        "c": nrm(2, (DEC_BATCH, D_MODEL)),
        "c_ctx": nrm(3, (D_MODEL,)),
        "cache_attn_k": nrm(4, (DEC_BATCH, DEPTH, KV_A, PAST_LEN, HD_A)),
        "cache_attn_v": nrm(5, (DEC_BATCH, DEPTH, KV_A, PAST_LEN, HD_A)),
        "state_hgrn": nrm(6, (DEC_BATCH, DEPTH, 2, H_B, DK_B, DV_B), 0.5),
        "cache_mla_ckv": nrm(7, (DEC_BATCH, DEPTH, PAST_LEN, KV_LORA_C)),
        "cache_mla_krope": nrm(8, (DEC_BATCH, DEPTH, PAST_LEN, ROPE_C)),
        "w_ada": nrm(9, (DEPTH, D_MODEL, N_MOD * D_MODEL), 0.5 * D_MODEL ** -0.5),
        "b_ada": nrm(10, (DEPTH, N_MOD * D_MODEL), 0.02),
        "norm_sub": 1.0 + nrm(11, (DEPTH, 3, D_MODEL), 0.1),
        "w_ffn_gate": nrm(12, (DEPTH, 2, D_MODEL, D_FF), D_MODEL ** -0.5),
        "w_ffn_up": nrm(13, (DEPTH, 2, D_MODEL, D_FF), D_MODEL ** -0.5),
        "w_ffn_down": nrm(14, (DEPTH, 2, D_FF, D_MODEL), D_FF ** -0.5),
        "w_in": nrm(15, (DEPTH, D_MODEL, D_IN), D_MODEL ** -0.5),
        "w_out": nrm(16, (DEPTH, D_MIX, D_MODEL), D_MIX ** -0.5),
        "attn_sink": nrm(17, (DEPTH, H_A), 0.5),
        "hgrn_lb_logits": nrm(18, (DEPTH, 2, H_B * DK_B)),
        "hgrn_out_norm": 1.0 + nrm(19, (DEPTH, DV_B), 0.1),
        "mla_q_norm": 1.0 + nrm(20, (DEPTH, Q_LORA_C), 0.1),
        "mla_kv_norm": 1.0 + nrm(21, (DEPTH, KV_LORA_C), 0.1),
        "mla_w_uq": nrm(22, (DEPTH, Q_LORA_C, H_C * (NOPE_C + ROPE_C)), Q_LORA_C ** -0.5),
        "mla_w_ukv": nrm(23, (DEPTH, KV_LORA_C, H_C * (NOPE_C + V_C)), KV_LORA_C ** -0.5),
        "pool_w": nrm(24, (DEPTH, POOL_GROUPS, POOL_CH, POOL_CH), POOL_CH ** -0.5),
        "pool_scale": 1.0 + nrm(25, (DEPTH, POOL_GROUPS * POOL_CH), 0.1),
        "final_norm": 1.0 + nrm(26, (D_MODEL,), 0.1),
    }


def reference(x_prompt, x_sample, c, c_ctx, cache_attn_k, cache_attn_v, state_hgrn, cache_mla_ckv,
              cache_mla_krope, w_ada, b_ada, norm_sub, w_ffn_gate, w_ffn_up, w_ffn_down, w_in, w_out,
              attn_sink, hgrn_lb_logits, hgrn_out_norm, mla_q_norm, mla_kv_norm, mla_w_uq, mla_w_ukv,
              pool_w, pool_scale, final_norm):
    lb = hgrn_lower_bounds(hgrn_lb_logits)
    rows = x_sample.shape[1] // GRID_W
    rope_a = axial_rope_tables(rows, HD_A)
    rope_c = axial_rope_tables(rows, ROPE_C)
    xp, xs = x_prompt, x_sample
    new_k, new_v, new_st, new_ckv, new_kr = [], [], [], [], []
    for l in range(DEPTH):
        lw = {"sink": attn_sink[l].reshape(KV_A, G_A).astype(jnp.float32), "lb_fwd": lb[l, 0],
              "lb_bwd": lb[l, 1], "hgrn_norm": hgrn_out_norm[l], "q_norm": mla_q_norm[l],
              "kv_norm": mla_kv_norm[l], "w_uq": mla_w_uq[l], "w_ukv": mla_w_ukv[l],
              "pool_w": pool_w[l], "pool_scale": pool_scale[l]}
        mp = modulation(c_ctx[None, :], w_ada[l], b_ada[l])
        ms = modulation(c, w_ada[l], b_ada[l])
        xp = ffn_sublayer(xp, mp, 0, norm_sub[l, 0], w_ffn_gate[l, 0], w_ffn_up[l, 0], w_ffn_down[l, 0])
        xs = ffn_sublayer(xs, ms, 0, norm_sub[l, 0], w_ffn_gate[l, 0], w_ffn_up[l, 0], w_ffn_down[l, 0])
        mix_p, (ka, va, st, ckv, kr) = context_mixers(mixer_inputs(xp, mp, norm_sub[l, 1], w_in[l]), lw)
        xp = xp + mp[:, :, 5] * (mix_p @ w_out[l])
        new_k.append(ka)
        new_v.append(va)
        new_st.append(st)
        new_ckv.append(ckv)
        new_kr.append(kr)
        mix_s = latent_mixers(mixer_inputs(xs, ms, norm_sub[l, 1], w_in[l]), lw,
                              cache_attn_k[:, l], cache_attn_v[:, l], state_hgrn[:, l],
                              cache_mla_ckv[:, l], cache_mla_krope[:, l], rope_a, rope_c)
        xs = xs + ms[:, :, 5] * (mix_s @ w_out[l])
        xp = ffn_sublayer(xp, mp, 2, norm_sub[l, 2], w_ffn_gate[l, 1], w_ffn_up[l, 1], w_ffn_down[l, 1])
        xs = ffn_sublayer(xs, ms, 2, norm_sub[l, 2], w_ffn_gate[l, 1], w_ffn_up[l, 1], w_ffn_down[l, 1])
    y_prompt = rms_norm(xp, final_norm)
    y_sample = rms_norm(xs, final_norm)
    new_attn_k = jnp.stack(new_k, axis=1)
    new_attn_v = jnp.stack(new_v, axis=1)
    new_state_hgrn = jnp.stack(new_st, axis=1)
    new_mla_ckv = jnp.stack(new_ckv, axis=1)
    new_mla_krope = jnp.stack(new_kr, axis=1)
    return (y_prompt, y_sample, new_attn_k, new_attn_v, new_state_hgrn, new_mla_ckv, new_mla_krope)
```

```python
import functools

import numpy as np
import jax
import jax.numpy as jnp
from jax import lax
from jax.experimental import pallas as pl
from jax.experimental.pallas import tpu as pltpu

f32 = jnp.float32
bf16 = jnp.bfloat16

D_MODEL = 1024
DEPTH = 2
GRID_W = 64
EPS = 1e-6
ROPE_THETA = 10000.0
NEG_INF = -1e30
N_MOD = 9
D_FF = 2816
H_A, KV_A, HD_A = 4, 2, 64
G_A = H_A // KV_A
WINDOW_A = 128
BLOCK_A = 128
H_B, DK_B, DV_B = 4, 64, 64
H_C, Q_LORA_C, KV_LORA_C, NOPE_C, ROPE_C, V_C = 4, 256, 128, 64, 32, 64
POOL_WINDOWS = (2, 4, 8, 16)
POOL_CH = 64

LANES = 128
VMEM_LIMIT = 56 * 1024 * 1024
HG_TILE = LANES
HG_BLOCK = 16
HG_LEVELS = (16, 32, 64)
POOL_PAD = 16

PR_W = 1280
PT_W = 1280

_NT = (((1,), (1,)), ((), ()))
_TN = (((0,), (0,)), ((), ()))


def _params(*sem):
    return pltpu.CompilerParams(dimension_semantics=sem, vmem_limit_bytes=VMEM_LIMIT)


def _const_spec(shape):
    zeros = (0,) * len(shape)
    return pl.BlockSpec(shape, lambda *_: zeros, pipeline_mode=pl.Buffered(1))


def _dot(a, b):
    return jnp.dot(a, b, preferred_element_type=f32)


def _sigmoid(x):
    return 1.0 / (1.0 + jnp.exp(-x))


def _rms(x, g):
    return x * lax.rsqrt(jnp.mean(x * x, axis=-1, keepdims=True) + EPS) * g


def _mod_kernel(c_ref, w_ref, b_ref, o_ref):
    c = c_ref[...]
    a = (c * _sigmoid(c)).astype(bf16)
    o_ref[0] = _dot(a, w_ref[0].astype(bf16)) + b_ref[0]


def _modulation(cvec, w_ada, b_ada):
    tn = 1536
    n_out = N_MOD * D_MODEL
    return pl.pallas_call(
        _mod_kernel,
        out_shape=jax.ShapeDtypeStruct((DEPTH, cvec.shape[0], n_out), f32),
        grid=(DEPTH, n_out // tn),
        in_specs=[pl.BlockSpec(cvec.shape, lambda l, j: (0, 0)),
                  pl.BlockSpec((1, D_MODEL, tn), lambda l, j: (l, 0, j)),
                  pl.BlockSpec((1, 1, tn), lambda l, j: (l, 0, j))],
        out_specs=pl.BlockSpec((1, cvec.shape[0], tn), lambda l, j: (l, 0, j)),
        compiler_params=_params("arbitrary", "arbitrary"),
        name="modulation",
    )(cvec, w_ada, b_ada.reshape(DEPTH, 1, n_out))


def _ffn_kernel(*refs, j, final):
    if final:
        x_ref, m_ref, g_ref, wg_ref, wu_ref, wd_ref, fg_ref, o_ref = refs
    else:
        x_ref, m_ref, g_ref, wg_ref, wu_ref, wd_ref, o_ref = refs
    x = x_ref[...]
    m = m_ref[0]
    h = (_rms(x, g_ref[...]) * (1.0 + m[3 * j + 1:3 * j + 2]) + m[3 * j:3 * j + 1]).astype(bf16)
    gate = _dot(h, wg_ref[...])
    up = _dot(h, wu_ref[...])
    act = (gate * _sigmoid(gate) * up).astype(bf16)
    y = x + (0.5 * m[3 * j + 2:3 * j + 3]) * _dot(act, wd_ref[...])
    if final:
        y = _rms(y, fg_ref[...])
    o_ref[...] = y


def _mod_spec(base, rpm, tm):
    return pl.BlockSpec((1, N_MOD, D_MODEL), lambda i: (base + (i * tm) // rpm, 0, 0))


def _ffn(x, mods, g, wg, wu, wd, *, j, base, rpm, tm, final_g=None):
    n = x.shape[0]
    in_specs = [pl.BlockSpec((tm, D_MODEL), lambda i: (i, 0)), _mod_spec(base, rpm, tm),
                _const_spec((1, D_MODEL)), _const_spec((D_MODEL, D_FF)), _const_spec((D_MODEL, D_FF)),
                _const_spec((D_FF, D_MODEL))]
    args = [x, mods, g, wg, wu, wd]
    if final_g is not None:
        in_specs.append(_const_spec((1, D_MODEL)))
        args.append(final_g)
    return pl.pallas_call(
        functools.partial(_ffn_kernel, j=j, final=final_g is not None),
        out_shape=jax.ShapeDtypeStruct((n, D_MODEL), f32),
        grid=(n // tm,),
        in_specs=in_specs,
        out_specs=pl.BlockSpec((tm, D_MODEL), lambda i: (i, 0)),
        compiler_params=_params("parallel"),
        name="ffn",
    )(*args)


def _mixin_kernel(x_ref, m_ref, g_ref, wr_ref, wt_ref, pr_ref, pt_ref):
    m = m_ref[0]
    h = (_rms(x_ref[...], g_ref[...]) * (1.0 + m[4:5]) + m[3:4]).astype(bf16)
    pr_ref[...] = _dot(h, wr_ref[...])
    pt_ref[...] = lax.dot_general(wt_ref[...], h, _NT, preferred_element_type=f32)


def _mixer_in(x, mods, g, w_row, w_t, *, base, rpm, tm):
    n = x.shape[0]
    return pl.pallas_call(
        _mixin_kernel,
        out_shape=(jax.ShapeDtypeStruct((n, PR_W), f32), jax.ShapeDtypeStruct((PT_W, n), f32)),
        grid=(n // tm,),
        in_specs=[pl.BlockSpec((tm, D_MODEL), lambda i: (i, 0)), _mod_spec(base, rpm, tm),
                  _const_spec((1, D_MODEL)), _const_spec((D_MODEL, PR_W)), _const_spec((PT_W, D_MODEL))],
        out_specs=(pl.BlockSpec((tm, PR_W), lambda i: (i, 0)), pl.BlockSpec((PT_W, tm), lambda i: (0, i))),
        compiler_params=_params("parallel"),
        name="mixer_in",
    )(x, mods, g, w_row, w_t)


def _outproj_kernel(x_ref, m_ref, a_ref, b_ref, c_ref, d_ref, w_ref, o_ref):
    m = m_ref[0]
    mix = _dot(a_ref[...].astype(bf16), w_ref[0:256, :])
    mix += _dot(b_ref[...].astype(bf16), w_ref[256:512, :])
    mix += _dot(c_ref[...].astype(bf16), w_ref[512:768, :])
    mix += _dot(d_ref[...].astype(bf16), w_ref[768:1024, :])
    o_ref[...] = x_ref[...] + m[5:6] * mix


def _out_proj(x, mods, oa, ob, oc, od, w_out, *, base, rpm, tm):
    n = x.shape[0]
    part = pl.BlockSpec((tm, 256), lambda i: (i, 0))
    return pl.pallas_call(
        _outproj_kernel,
        out_shape=jax.ShapeDtypeStruct((n, D_MODEL), f32),
        grid=(n // tm,),
        in_specs=[pl.BlockSpec((tm, D_MODEL), lambda i: (i, 0)), _mod_spec(base, rpm, tm),
                  part, part, part, part, _const_spec((D_MODEL, D_MODEL))],
        out_specs=pl.BlockSpec((tm, D_MODEL), lambda i: (i, 0)),
        compiler_params=_params("parallel"),
        name="out_proj",
    )(x, mods, oa, ob, oc, od, w_out)


def _rope(x, tab_ref, rows, lanes, shift):
    width = x.shape[-1]
    c = tab_ref[0, rows, lanes]
    sa = tab_ref[1, rows, lanes]
    sb = tab_ref[2, rows, lanes]
    return x * c + pltpu.roll(x, shift, axis=1) * sa + pltpu.roll(x, width - shift, axis=1) * sb


def _softmax_rows(s_list, sink=None):
    mx = functools.reduce(jnp.maximum, [jnp.max(s, axis=-1, keepdims=True) for s in s_list])
    if sink is not None:
        mx = jnp.maximum(mx, sink)
    ps = [jnp.exp(s - mx) for s in s_list]
    den = functools.reduce(lambda a, b: a + b, [jnp.sum(p, axis=-1, keepdims=True) for p in ps])
    if sink is not None:
        den = den + jnp.exp(sink - mx)
    inv = 1.0 / den
    return [p * inv for p in ps]


def _attn_ctx_kernel(sink_ref, q_ref, kv_ref, o_ref, ko_ref, vo_ref):
    q = q_ref[...]
    kv = kv_ref[...]
    scale = HD_A ** -0.5
    outs = []
    for j in range(KV_A):
        k = kv[:, HD_A * j:HD_A * (j + 1)]
        v = kv[:, KV_A * HD_A + HD_A * j:KV_A * HD_A + HD_A * (j + 1)]
        ko_ref[0, j] = k
        vo_ref[0, j] = v
        kb = k.astype(bf16)
        vb = v.astype(bf16)
        for g in range(G_A):
            h = j * G_A + g
            qh = q[:, HD_A * h:HD_A * (h + 1)].astype(bf16)
            s = lax.dot_general(qh, kb, _NT, preferred_element_type=f32) * scale
            (p,) = _softmax_rows([s], sink_ref[h])
            outs.append(_dot(p.astype(bf16), vb))
    o_ref[...] = jnp.concatenate(outs, axis=-1)


def _attn_ctx(pr, sink, *, batch, t):
    kvshape = jax.ShapeDtypeStruct((batch, KV_A, t, HD_A), f32)
    kvspec = pl.BlockSpec((1, KV_A, t, HD_A), lambda b: (b, 0, 0, 0))
    return pl.pallas_call(
        _attn_ctx_kernel,
        out_shape=(jax.ShapeDtypeStruct((batch * t, 256), f32), kvshape, kvshape),
        grid=(batch,),
        in_specs=[pl.BlockSpec(memory_space=pltpu.SMEM),
                  pl.BlockSpec((t, 256), lambda b: (b, 0)), pl.BlockSpec((t, 256), lambda b: (b, 1))],
        out_specs=(pl.BlockSpec((t, 256), lambda b: (b, 0)), kvspec, kvspec),
        compiler_params=_params("parallel"),
        name="attn_ctx",
    )(sink, pr, pr)


def _attn_lat_kernel(sink_ref, q_ref, kv_ref, kc_ref, vc_ref, tab_ref, o_ref, kpad, vpad, *, t):
    n = pl.program_id(1)
    kvw = KV_A * HD_A

    @pl.when(n == 0)
    def _():
        zeros = jnp.zeros((BLOCK_A, kvw), bf16)
        kv = kv_ref[...]
        kpad[0:BLOCK_A, :] = zeros
        vpad[0:BLOCK_A, :] = zeros
        kpad[BLOCK_A:BLOCK_A + t, :] = _rope(kv[:, 0:kvw], tab_ref, slice(None), slice(0, kvw), HD_A // 2).astype(bf16)
        vpad[BLOCK_A:BLOCK_A + t, :] = kv[:, kvw:2 * kvw].astype(bf16)
        kpad[BLOCK_A + t:2 * BLOCK_A + t, :] = zeros
        vpad[BLOCK_A + t:2 * BLOCK_A + t, :] = zeros

    row0 = pl.multiple_of(n * BLOCK_A, BLOCK_A)
    q = _rope(q_ref[...], tab_ref, pl.ds(row0, BLOCK_A), slice(None), HD_A // 2)
    kband = kpad[pl.ds(row0, 3 * BLOCK_A), :]
    vband = vpad[pl.ds(row0, 3 * BLOCK_A), :]
    scale = HD_A ** -0.5
    rows = G_A * BLOCK_A
    r = lax.broadcasted_iota(jnp.int32, (rows, 3 * BLOCK_A), 0)
    cidx = lax.broadcasted_iota(jnp.int32, (rows, 3 * BLOCK_A), 1)
    q_pos = n * BLOCK_A + (r % BLOCK_A)
    k_pos = (n - 1) * BLOCK_A + cidx
    valid = (jnp.abs(k_pos - q_pos) <= WINDOW_A) & (k_pos >= 0) & (k_pos < t)
    rcol = lax.broadcasted_iota(jnp.int32, (rows, 1), 0)
    outs = []
    for j in range(KV_A):
        q2 = jnp.concatenate([q[:, HD_A * (j * G_A + g):HD_A * (j * G_A + g + 1)] for g in range(G_A)],
                             axis=0).astype(bf16)
        ksl = slice(HD_A * j, HD_A * (j + 1))
        s_loc = lax.dot_general(q2, kband[:, ksl], _NT, preferred_element_type=f32) * scale
        s_loc = jnp.where(valid, s_loc, NEG_INF)
        s_ctx = lax.dot_general(q2, kc_ref[0, j].astype(bf16), _NT, preferred_element_type=f32) * scale
        sink = jnp.where(rcol < BLOCK_A, sink_ref[j * G_A], sink_ref[j * G_A + 1])
        p_loc, p_ctx = _softmax_rows([s_loc, s_ctx], sink)
        o = _dot(p_loc.astype(bf16), vband[:, ksl]) + _dot(p_ctx.astype(bf16), vc_ref[0, j].astype(bf16))
        outs += [o[g * BLOCK_A:(g + 1) * BLOCK_A] for g in range(G_A)]
    o_ref[...] = jnp.concatenate(outs, axis=-1)


def _attn_lat(pr, sink, kc, vc, tab, *, batch, t):
    nb = t // BLOCK_A
    lc = kc.shape[2]
    cspec = pl.BlockSpec((1, KV_A, lc, HD_A), lambda b, n: (b, 0, 0, 0))
    return pl.pallas_call(
        functools.partial(_attn_lat_kernel, t=t),
        out_shape=jax.ShapeDtypeStruct((batch * t, 256), f32),
        grid=(batch, nb),
        in_specs=[pl.BlockSpec(memory_space=pltpu.SMEM),
                  pl.BlockSpec((BLOCK_A, 256), lambda b, n: (b * nb + n, 0)),
                  pl.BlockSpec((t, 256), lambda b, n: (b, 1)),
                  cspec, cspec,
                  pl.BlockSpec((3, t, 256), lambda b, n: (0, 0, 0))],
        out_specs=pl.BlockSpec((BLOCK_A, 256), lambda b, n: (b * nb + n, 0)),
        scratch_shapes=[pltpu.VMEM((t + 2 * BLOCK_A, KV_A * HD_A), bf16),
                        pltpu.VMEM((t + 2 * BLOCK_A, KV_A * HD_A), bf16)],
        compiler_params=_params("parallel", "arbitrary"),
        name="attn_lat",
    )(sink, pr, pr, kc, vc, tab)


def _mla_kernel(*refs, t, lc, tq, rope):
    it = iter(refs)
    cq_ref, ckv_ref, kr_ref = next(it), next(it), next(it)
    cc_ref = kc_ref = tab_ref = None
    if lc:
        cc_ref, kc_ref = next(it), next(it)
    if rope:
        tab_ref = next(it)
    qn_ref, kvn_ref, wq_ref, wk_ref, wv_ref = next(it), next(it), next(it), next(it), next(it)
    o_ref, ckv_out_ref, kbuf, vbuf = next(it), next(it), next(it), next(it)
    i = pl.program_id(1)
    hw = LANES

    @pl.when(i == 0)
    def _():
        ckv = _rms(ckv_ref[...], kvn_ref[...])
        ckv_out_ref[...] = ckv
        ckv_b = ckv.astype(bf16)
        kr = kr_ref[...]
        if rope:
            kr = _rope(kr, tab_ref, slice(None), slice(None), ROPE_C // 2)
        kk = _dot(ckv_b, wk_ref[...])
        for h in range(H_C):
            kbuf[h, 0:t, :] = (kk[:, hw * h:hw * (h + 1)] + kr).astype(bf16)
        vbuf[0:t, :] = _dot(ckv_b, wv_ref[...]).astype(bf16)
        if lc:
            cc_b = cc_ref[0].astype(bf16)
            kkc = _dot(cc_b, wk_ref[...])
            krc = kc_ref[0]
            for h in range(H_C):
                kbuf[h, t:t + lc, :] = (kkc[:, hw * h:hw * (h + 1)] + krc).astype(bf16)
            vbuf[t:t + lc, :] = _dot(cc_b, wv_ref[...]).astype(bf16)

    cqn = _rms(cq_ref[...], qn_ref[...]).astype(bf16)
    qf = _dot(cqn, wq_ref[...])
    scale = (NOPE_C + ROPE_C) ** -0.5
    row0 = pl.multiple_of(i * tq, tq)
    outs = []
    for h in range(H_C):
        qh = qf[:, hw * h:hw * (h + 1)]
        if rope:
            qh = _rope(qh, tab_ref, pl.ds(row0, tq), slice(None), ROPE_C // 2)
        s = lax.dot_general(qh.astype(bf16), kbuf[h], _NT, preferred_element_type=f32) * scale
        (p,) = _softmax_rows([s])
        outs.append(_dot(p.astype(bf16), vbuf[:, V_C * h:V_C * (h + 1)]))
    o_ref[...] = jnp.concatenate(outs, axis=-1)


def _mla(pr, qn, kvn, wq, wk, wv, *, batch, t, tq, ctx=None, tab=None):
    nq = t // tq
    lc = 0 if ctx is None else ctx[0].shape[1]
    in_specs = [pl.BlockSpec((tq, 256), lambda b, i: (b * nq + i, 2)),
                pl.BlockSpec((t, LANES), lambda b, i: (b, 6)),
                pl.BlockSpec((t, LANES), lambda b, i: (b, 7))]
    args = [pr, pr, pr]
    if lc:
        in_specs += [pl.BlockSpec((1, lc, LANES), lambda b, i: (b, 0, 0))] * 2
        args += list(ctx)
    if tab is not None:
        in_specs.append(pl.BlockSpec((3, t, LANES), lambda b, i: (0, 0, 0)))
        args.append(tab)
    in_specs += [_const_spec((1, Q_LORA_C)), _const_spec((1, KV_LORA_C)), _const_spec(wq.shape),
                 _const_spec(wk.shape), _const_spec(wv.shape)]
    args += [qn, kvn, wq, wk, wv]
    return pl.pallas_call(
        functools.partial(_mla_kernel, t=t, lc=lc, tq=tq, rope=tab is not None),
        out_shape=(jax.ShapeDtypeStruct((batch * t, 256), f32), jax.ShapeDtypeStruct((batch * t, LANES), f32)),
        grid=(batch, nq),
        in_specs=in_specs,
        out_specs=(pl.BlockSpec((tq, 256), lambda b, i: (b * nq + i, 0)),
                   pl.BlockSpec((t, LANES), lambda b, i: (b, 0))),
        scratch_shapes=[pltpu.VMEM((H_C, t + lc, LANES), bf16), pltpu.VMEM((t + lc, H_C * V_C), bf16)],
        compiler_params=_params("parallel", "arbitrary"),
        name="mla",
    )(*args)


def _pool_kernel(x_ref, w_ref, sc_ref, o_ref, xpad, *, t):
    x = x_ref[...]
    n = t + 2 * POOL_PAD
    zeros = jnp.zeros((POOL_PAD, 256), f32)
    xpad[0:POOL_PAD, :] = zeros
    xpad[POOL_PAD:POOL_PAD + t, :] = x
    xpad[POOL_PAD + t:n, :] = zeros
    xp = xpad[...]
    s2 = xp + pltpu.roll(xp, 1, axis=0)
    s4 = pltpu.roll(s2, 1, axis=0) + pltpu.roll(s2, n - 1, axis=0)
    s8 = pltpu.roll(s4, 2, axis=0) + pltpu.roll(s4, n - 2, axis=0)
    s16 = pltpu.roll(s8, 4, axis=0) + pltpu.roll(s8, n - 4, axis=0)
    sums = [s[POOL_PAD:POOL_PAD + t] for s in (s2, s4, s8, s16)]
    group = lax.broadcasted_iota(jnp.int32, (t, 256), 1) // POOL_CH
    pos = lax.broadcasted_iota(jnp.int32, (t, 256), 0)
    win_sum = sums[0]
    half = jnp.full((t, 256), POOL_WINDOWS[0] // 2, jnp.int32)
    for gi in range(1, len(POOL_WINDOWS)):
        win_sum = jnp.where(group == gi, sums[gi], win_sum)
        half = jnp.where(group == gi, POOL_WINDOWS[gi] // 2, half)
    cnt = jnp.minimum(pos + half, t) - jnp.maximum(pos - half, 0)
    mean = win_sum / cnt.astype(f32)
    o_ref[...] = _dot((mean - x).astype(bf16), w_ref[...]) * sc_ref[...]


def _pool(pr, w_bd, scale, *, batch, t):
    return pl.pallas_call(
        functools.partial(_pool_kernel, t=t),
        out_shape=jax.ShapeDtypeStruct((batch * t, 256), f32),
        grid=(batch,),
        in_specs=[pl.BlockSpec((t, 256), lambda b: (b, 4)), _const_spec((256, 256)), _const_spec((1, 256))],
        out_specs=pl.BlockSpec((t, 256), lambda b: (b, 0)),
        scratch_shapes=[pltpu.VMEM((t + 2 * POOL_PAD, 256), f32)],
        compiler_params=_params("parallel"),
        name="pool",
    )(pr, w_bd, scale)


def _hgrn_mats():
    n = HG_TILE
    s = np.arange(n)[:, None]
    t = np.arange(n)[None, :]
    mats = [(s // HG_BLOCK == t // HG_BLOCK) & (s <= t)]
    for m in HG_LEVELS:
        same_pair = s // (2 * m) == t // (2 * m)
        right_t = (t // m) % 2 == 1
        right_s = (s // m) % 2 == 1
        q_side = same_pair & right_t & right_s & (s <= t)
        k_side = same_pair & ~right_t & ~right_s & (s > t)
        mats.append(q_side | k_side)
    mats.append(s <= t)
    mats.append(s > t)
    fwd = np.concatenate([m.astype(np.float32) for m in mats], axis=1)
    bwd = np.concatenate([m[::-1, ::-1].astype(np.float32) for m in mats], axis=1)
    return np.stack([fwd, bwd])


def _hgrn_tile(q, v, fpre, lb, mats, st_ref, sidx, reverse):
    n = HG_TILE
    f = lb + (1.0 - lb) * _sigmoid(fpre)
    k = 1.0 - f
    logf = jnp.log(f)
    hi = logf.astype(bf16)
    lo = (logf - hi.astype(f32)).astype(bf16)
    e_all = _dot(hi, mats) + _dot(lo, mats)
    ones = jnp.ones((8, n), bf16)
    g_row = (lax.dot_general(ones, hi, _NT, preferred_element_type=f32)
             + lax.dot_general(ones, lo, _NT, preferred_element_type=f32))[0:1]
    beta = e_all[:, 0:n]
    lane = lax.broadcasted_iota(jnp.int32, (1, n), 1)
    pos = lane % HG_BLOCK
    q3 = q.reshape(H_B, DK_B, n)
    v3 = v.reshape(H_B, DV_B, n)

    o3 = jnp.sum((q * k).reshape(H_B, DK_B, n), axis=1, keepdims=True) * v3
    for d in range(1, HG_BLOCK):
        shift = (n - d) if reverse else d
        valid = (pos <= HG_BLOCK - 1 - d) if reverse else (pos >= d)
        kr = pltpu.roll(k, shift, axis=1)
        br = pltpu.roll(beta, shift, axis=1)
        vr = pltpu.roll(v, shift, axis=1)
        e = jnp.exp(jnp.where(valid, beta - br, NEG_INF))
        sc = jnp.sum((q * kr * e).reshape(H_B, DK_B, n), axis=1, keepdims=True)
        o3 = o3 + sc * vr.reshape(H_B, DV_B, n)

    srow = lax.broadcasted_iota(jnp.int32, (n, n), 0)
    tcol = lax.broadcasted_iota(jnp.int32, (n, n), 1)
    scores = [jnp.zeros((n, n), f32) for _ in range(H_B)]
    for li, m in enumerate(HG_LEVELS):
        ex = jnp.exp(e_all[:, (li + 1) * n:(li + 2) * n])
        is_q = ((lane // m) % 2) == (0 if reverse else 1)
        qs = jnp.where(is_q, q * ex, 0.0).astype(bf16)
        ks = jnp.where(is_q, 0.0, k * ex).astype(bf16)
        pair = (srow // (2 * m)) == (tcol // (2 * m))
        for h in range(H_B):
            a = lax.dot_general(ks[DK_B * h:DK_B * (h + 1)], qs[DK_B * h:DK_B * (h + 1)], _TN,
                                preferred_element_type=f32)
            scores[h] = scores[h] + jnp.where(pair, a, 0.0)

    q_in = (q * jnp.exp(e_all[:, 4 * n:5 * n])).astype(bf16)
    k_out = (k * jnp.exp(e_all[:, 5 * n:6 * n])).astype(bf16)
    vb = v.astype(bf16)
    outs = []
    for h in range(H_B):
        rows = slice(DK_B * h, DK_B * (h + 1))
        st = st_ref[sidx, rows, :]
        o_h = o3[h] + _dot(vb[rows], scores[h].astype(bf16)) + _dot(st.astype(bf16), q_in[rows])
        outs.append(o_h)
        upd = lax.dot_general(vb[rows], k_out[rows], _NT, preferred_element_type=f32)
        st_ref[sidx, rows, :] = st * jnp.exp(g_row[:, rows]) + upd
    return jnp.concatenate(outs, axis=0)


def _hgrn_kernel(q_ref, i_ref, ff_ref, fb_ref, g_ref, lb_ref, ng_ref, mats_ref, s0_ref,
                 o_ref, sfin_ref, st_ref, oacc, *, t):
    n = HG_TILE
    nt = t // n
    st_ref[...] = s0_ref[0]

    def fwd(i, c):
        off = pl.multiple_of(i * n, n)
        cols = pl.ds(off, n)
        oacc[:, cols] = _hgrn_tile(q_ref[:, cols], i_ref[:, cols], ff_ref[:, cols], lb_ref[0], mats_ref[0],
                                   st_ref, 0, False)
        return c

    lax.fori_loop(0, nt, fwd, 0)

    def bwd(ii, c):
        i = nt - 1 - ii
        off = pl.multiple_of(i * n, n)
        cols = pl.ds(off, n)
        o = oacc[:, cols] + _hgrn_tile(q_ref[:, cols], i_ref[:, cols], fb_ref[:, cols], lb_ref[1], mats_ref[1],
                                       st_ref, 1, True)
        o3 = o.reshape(H_B, DV_B, n)
        y = o3 * lax.rsqrt(jnp.mean(o3 * o3, axis=1, keepdims=True) + EPS)
        gate = g_ref[:, cols]
        y = y.reshape(H_B * DV_B, n) * ng_ref[...] * (gate * _sigmoid(gate))
        o_ref[pl.ds(off, n), :] = y.T
        return c

    lax.fori_loop(0, nt, bwd, 0)
    sfin_ref[0] = st_ref[...]


def _hgrn(pt, lb, ng, mats, s0, *, batch, t):
    def row(r):
        return pl.BlockSpec((256, t), lambda b: (r, b))
    sspec = pl.BlockSpec((1, 2, H_B * DV_B, DK_B), lambda b: (b, 0, 0, 0))
    return pl.pallas_call(
        functools.partial(_hgrn_kernel, t=t),
        out_shape=(jax.ShapeDtypeStruct((batch * t, 256), f32),
                   jax.ShapeDtypeStruct((batch, 2, H_B * DV_B, DK_B), f32)),
        grid=(batch,),
        in_specs=[row(0), row(1), row(2), row(3), row(4),
                  _const_spec((2, 256, LANES)), _const_spec((256, LANES)), _const_spec(mats.shape), sspec],
        out_specs=(pl.BlockSpec((t, 256), lambda b: (b, 0)), sspec),
        scratch_shapes=[pltpu.VMEM((2, H_B * DV_B, DK_B), f32), pltpu.VMEM((256, t), f32)],
        compiler_params=_params("parallel"),
        name="hgrn",
    )(pt, pt, pt, pt, pt, lb, ng, mats, s0)


def _rope_tables(t):
    def cos_sin(dim):
        n_freq = dim // 4
        inv = ROPE_THETA ** (-jnp.arange(n_freq, dtype=f32) / n_freq)
        rows = t // GRID_W
        row_id = jnp.repeat(jnp.arange(rows, dtype=f32), GRID_W)
        col_id = jnp.tile(jnp.arange(GRID_W, dtype=f32), rows)
        ang = jnp.concatenate([row_id[:, None] * inv, col_id[:, None] * inv], axis=-1)
        return jnp.cos(ang), jnp.sin(ang)

    cos, sin = cos_sin(HD_A)
    zero = jnp.zeros_like(sin)
    tab_a = jnp.stack([jnp.tile(jnp.concatenate([cos, cos], -1), (1, H_A)),
                       jnp.tile(jnp.concatenate([zero, sin], -1), (1, H_A)),
                       jnp.tile(jnp.concatenate([-sin, zero], -1), (1, H_A))])
    cos, sin = cos_sin(ROPE_C)
    zero = jnp.zeros_like(sin)
    one64 = jnp.ones((t, NOPE_C), f32)
    zero64 = jnp.zeros((t, NOPE_C), f32)
    pad32 = jnp.zeros((t, LANES - NOPE_C - ROPE_C), f32)
    tab_c = jnp.stack([jnp.concatenate([one64, cos, cos, pad32 + 1.0], -1),
                       jnp.concatenate([zero64, zero, sin, pad32], -1),
                       jnp.concatenate([zero64, -sin, zero, pad32], -1)])
    return tab_a, tab_c


def _pad_rope_cols(x):
    pads = [(0, 0)] * (x.ndim - 1) + [(NOPE_C, LANES - NOPE_C - ROPE_C)]
    return jnp.pad(x, pads)


def kernel(x_prompt, x_sample, c, c_ctx, cache_attn_k, cache_attn_v, state_hgrn, cache_mla_ckv, cache_mla_krope,
           w_ada, b_ada, norm_sub, w_ffn_gate, w_ffn_up, w_ffn_down, w_in, w_out, attn_sink, hgrn_lb_logits,
           hgrn_out_norm, mla_q_norm, mla_kv_norm, mla_w_uq, mla_w_ukv, pool_w, pool_scale, final_norm):
    bp, tp, _ = x_prompt.shape
    bs, ts, _ = x_sample.shape
    np_, ns = bp * tp, bs * ts
    xp = x_prompt.reshape(np_, D_MODEL)
    xs = x_sample.reshape(ns, D_MODEL)

    n_vec = 1 + bs
    cvec = jnp.zeros((-(-n_vec // 8) * 8, D_MODEL), f32).at[0].set(c_ctx).at[1:n_vec].set(c)
    mods = _modulation(cvec, w_ada, b_ada).reshape(DEPTH, cvec.shape[0], N_MOD, D_MODEL)

    p_lb = jax.nn.softmax(hgrn_lb_logits.astype(f32), axis=0)
    lb_all = jnp.cumsum(p_lb, axis=0) - p_lb[0:1]
    tab_a, tab_c = _rope_tables(ts)
    mats = jnp.asarray(_hgrn_mats(), bf16)
    grp_p = dict(base=0, rpm=np_)
    grp_s = dict(base=1, rpm=ts)
    tm = 512

    new_k, new_v, new_st, new_ckv, new_kr = [], [], [], [], []
    for l in range(DEPTH):
        m_l = mods[l]
        wb = lambda w: w.astype(bf16)
        wi = w_in[l]
        w_row = wb(jnp.concatenate([wi[:, 0:512], wi[:, 1792:2176], _pad_rope_cols(wi[:, 2176:2208]),
                                    wi[:, 2208:2464]], axis=1))
        w_t = wb(wi[:, 512:1792].T)
        wq = wb(jnp.pad(mla_w_uq[l].reshape(Q_LORA_C, H_C, NOPE_C + ROPE_C),
                        ((0, 0), (0, 0), (0, LANES - NOPE_C - ROPE_C))).reshape(Q_LORA_C, H_C * LANES))
        wkv = mla_w_ukv[l].reshape(KV_LORA_C, H_C, NOPE_C + V_C)
        wk = wb(jnp.pad(wkv[:, :, :NOPE_C], ((0, 0), (0, 0), (0, LANES - NOPE_C))).reshape(KV_LORA_C, H_C * LANES))
        wv = wb(wkv[:, :, NOPE_C:].reshape(KV_LORA_C, H_C * V_C))
        w_pool = wb(jax.scipy.linalg.block_diag(*[pool_w[l, gi] for gi in range(len(POOL_WINDOWS))]))
        lb = jnp.broadcast_to(lb_all[l][:, :, None], (2, 256, LANES))
        ng = jnp.broadcast_to(jnp.tile(hgrn_out_norm[l], H_B)[:, None], (256, LANES))
        sink = attn_sink[l].astype(f32)
        qn = mla_q_norm[l].reshape(1, Q_LORA_C)
        kvn = mla_kv_norm[l].reshape(1, KV_LORA_C)
        pscale = pool_scale[l].reshape(1, 256)
        wo = wb(w_out[l])

        def ffn(x, grp, j, idx, final_g=None):
            return _ffn(x, m_l, norm_sub[l, j].reshape(1, D_MODEL), wb(w_ffn_gate[l, idx]), wb(w_ffn_up[l, idx]),
                        wb(w_ffn_down[l, idx]), j=j, tm=tm, final_g=final_g, **grp)

        xp = ffn(xp, grp_p, 0, 0)
        xs = ffn(xs, grp_s, 0, 0)

        g1 = norm_sub[l, 1].reshape(1, D_MODEL)
        pr, pt = _mixer_in(xp, m_l, g1, w_row, w_t, tm=tm, **grp_p)
        oa, ka, va = _attn_ctx(pr, sink, batch=bp, t=tp)
        s0 = jnp.zeros((bp, 2, H_B * DV_B, DK_B), f32)
        ob, st = _hgrn(pt, lb, ng, mats, s0, batch=bp, t=tp)
        oc, ckv = _mla(pr, qn, kvn, wq, wk, wv, batch=bp, t=tp, tq=tp)
        od = _pool(pr, w_pool, pscale, batch=bp, t=tp)
        xp = _out_proj(xp, m_l, oa, ob, oc, od, wo, tm=tm, **grp_p)
        new_k.append(ka)
        new_v.append(va)
        new_st.append(jnp.swapaxes(st.reshape(bp, 2, H_B, DV_B, DK_B), -1, -2))
        new_ckv.append(ckv.reshape(bp, tp, KV_LORA_C))
        new_kr.append(pr[:, 896 + NOPE_C:896 + NOPE_C + ROPE_C].reshape(bp, tp, ROPE_C))

        pr, pt = _mixer_in(xs, m_l, g1, w_row, w_t, tm=tm, **grp_s)
        oa = _attn_lat(pr, sink, cache_attn_k[:, l], cache_attn_v[:, l], tab_a, batch=bs, t=ts)
        s0 = jnp.swapaxes(state_hgrn[:, l], -1, -2).reshape(bs, 2, H_B * DV_B, DK_B)
        ob, _ = _hgrn(pt, lb, ng, mats, s0, batch=bs, t=ts)
        oc, _ = _mla(pr, qn, kvn, wq, wk, wv, batch=bs, t=ts, tq=256,
                     ctx=(cache_mla_ckv[:, l], _pad_rope_cols(cache_mla_krope[:, l])), tab=tab_c)
        od = _pool(pr, w_pool, pscale, batch=bs, t=ts)
        xs = _out_proj(xs, m_l, oa, ob, oc, od, wo, tm=tm, **grp_s)

        fin = final_norm.reshape(1, D_MODEL) if l == DEPTH - 1 else None
        xp = ffn(xp, grp_p, 2, 1, fin)
        xs = ffn(xs, grp_s, 2, 1, fin)

    return (xp.reshape(bp, tp, D_MODEL), xs.reshape(bs, ts, D_MODEL),
            jnp.stack(new_k, axis=1), jnp.stack(new_v, axis=1), jnp.stack(new_st, axis=1),
            jnp.stack(new_ckv, axis=1), jnp.stack(new_kr, axis=1))
```

```python
import functools

import numpy as np
import jax
import jax.numpy as jnp
from jax import lax
from jax.experimental import pallas as pl
from jax.experimental.pallas import tpu as pltpu

f32 = jnp.float32
bf16 = jnp.bfloat16

D_MODEL = 1024
DEPTH = 2
GRID_W = 64
EPS = 1e-6
ROPE_THETA = 10000.0
NEG_INF = -1e30
N_MOD = 9
D_FF = 2816
H_A, KV_A, HD_A = 4, 2, 64
G_A = H_A // KV_A
WINDOW_A = 128
BLOCK_A = 128
H_B, DK_B, DV_B = 4, 64, 64
H_C, Q_LORA_C, KV_LORA_C, NOPE_C, ROPE_C, V_C = 4, 256, 128, 64, 32, 64
POOL_WINDOWS = (2, 4, 8, 16)
POOL_CH = 64

LANES = 128
VMEM_LIMIT = 56 * 1024 * 1024
HG_TILE = LANES
HG_HALF = HG_TILE // 2
HG_MAX_EXP = 80.0
HG_BLOCK = 16
HG_LEVELS = (16, 32, 64)
POOL_PAD = 16

PR_W = 1280
PT_W = 1280

_NT = (((1,), (1,)), ((), ()))
_TN = (((0,), (0,)), ((), ()))


def _params(*sem):
    return pltpu.CompilerParams(dimension_semantics=sem, vmem_limit_bytes=VMEM_LIMIT)


def _const_spec(shape):
    zeros = (0,) * len(shape)
    return pl.BlockSpec(shape, lambda *_: zeros, pipeline_mode=pl.Buffered(1))


def _dot(a, b):
    return jnp.dot(a, b, preferred_element_type=f32)


def _sigmoid(x):
    return 1.0 / (1.0 + jnp.exp(-x))


def _rms(x, g):
    return x * lax.rsqrt(jnp.mean(x * x, axis=-1, keepdims=True) + EPS) * g


def _mod_kernel(c_ref, w_ref, b_ref, o_ref):
    c = c_ref[...]
    a = (c * _sigmoid(c)).astype(bf16)
    o_ref[0] = _dot(a, w_ref[0].astype(bf16)) + b_ref[0]


def _modulation(cvec, w_ada, b_ada):
    tn = 1536
    n_out = N_MOD * D_MODEL
    return pl.pallas_call(
        _mod_kernel,
        out_shape=jax.ShapeDtypeStruct((DEPTH, cvec.shape[0], n_out), f32),
        grid=(DEPTH, n_out // tn),
        in_specs=[pl.BlockSpec(cvec.shape, lambda l, j: (0, 0)),
                  pl.BlockSpec((1, D_MODEL, tn), lambda l, j: (l, 0, j)),
                  pl.BlockSpec((1, 1, tn), lambda l, j: (l, 0, j))],
        out_specs=pl.BlockSpec((1, cvec.shape[0], tn), lambda l, j: (l, 0, j)),
        compiler_params=_params("arbitrary", "arbitrary"),
        name="modulation",
    )(cvec, w_ada, b_ada.reshape(DEPTH, 1, n_out))


def _cast_kernel(x_ref, o_ref):
    o_ref[...] = x_ref[...].astype(bf16)


def _to_bf16(x, rows):
    r, c = x.shape
    return pl.pallas_call(
        _cast_kernel,
        out_shape=jax.ShapeDtypeStruct((r, c), bf16),
        grid=(r // rows,),
        in_specs=[pl.BlockSpec((rows, c), lambda i: (i, 0))],
        out_specs=pl.BlockSpec((rows, c), lambda i: (i, 0)),
        compiler_params=_params("parallel"),
        name="cast_bf16",
    )(x)


def _ffn_kernel(*refs, j, final):
    if final:
        x_ref, m_ref, g_ref, wg_ref, wu_ref, wd_ref, fg_ref, o_ref = refs
    else:
        x_ref, m_ref, g_ref, wg_ref, wu_ref, wd_ref, o_ref = refs
    x = x_ref[...]
    m = m_ref[0]
    h = (_rms(x, g_ref[...]) * (1.0 + m[3 * j + 1:3 * j + 2]) + m[3 * j:3 * j + 1]).astype(bf16)
    gate = _dot(h, wg_ref[...])
    up = _dot(h, wu_ref[...])
    act = (gate * _sigmoid(gate) * up).astype(bf16)
    y = x + (0.5 * m[3 * j + 2:3 * j + 3]) * _dot(act, wd_ref[...])
    if final:
        y = _rms(y, fg_ref[...])
    o_ref[...] = y


def _mod_spec(base, rpm, tm):
    return pl.BlockSpec((1, N_MOD, D_MODEL), lambda i: (base + (i * tm) // rpm, 0, 0))


def _ffn(x, mods, g, wg, wu, wd, *, widx, j, base, rpm, tm, final_g=None):
    n = x.shape[0]

    def wspec(r, c):
        return pl.BlockSpec((None, r, c), lambda i: (widx, 0, 0), pipeline_mode=pl.Buffered(1))

    in_specs = [pl.BlockSpec((tm, D_MODEL), lambda i: (i, 0)), _mod_spec(base, rpm, tm),
                _const_spec((1, D_MODEL)), wspec(D_MODEL, D_FF), wspec(D_MODEL, D_FF), wspec(D_FF, D_MODEL)]
    args = [x, mods, g, wg, wu, wd]
    if final_g is not None:
        in_specs.append(_const_spec((1, D_MODEL)))
        args.append(final_g)
    return pl.pallas_call(
        functools.partial(_ffn_kernel, j=j, final=final_g is not None),
        out_shape=jax.ShapeDtypeStruct((n, D_MODEL), f32),
        grid=(n // tm,),
        in_specs=in_specs,
        out_specs=pl.BlockSpec((tm, D_MODEL), lambda i: (i, 0)),
        compiler_params=_params("parallel"),
        name="ffn",
    )(*args)


def _mixin_kernel(x_ref, m_ref, g_ref, wr_ref, wt_ref, pr_ref, pt_ref):
    m = m_ref[0]
    h = (_rms(x_ref[...], g_ref[...]) * (1.0 + m[4:5]) + m[3:4]).astype(bf16)
    pr_ref[...] = _dot(h, wr_ref[...])
    pt_ref[...] = lax.dot_general(wt_ref[...], h, _NT, preferred_element_type=f32)


def _mixer_in(x, mods, g, w_row, w_t, *, base, rpm, tm):
    n = x.shape[0]
    return pl.pallas_call(
        _mixin_kernel,
        out_shape=(jax.ShapeDtypeStruct((n, PR_W), f32), jax.ShapeDtypeStruct((PT_W, n), f32)),
        grid=(n // tm,),
        in_specs=[pl.BlockSpec((tm, D_MODEL), lambda i: (i, 0)), _mod_spec(base, rpm, tm),
                  _const_spec((1, D_MODEL)), _const_spec((D_MODEL, PR_W)), _const_spec((PT_W, D_MODEL))],
        out_specs=(pl.BlockSpec((tm, PR_W), lambda i: (i, 0)), pl.BlockSpec((PT_W, tm), lambda i: (0, i))),
        compiler_params=_params("parallel"),
        name="mixer_in",
    )(x, mods, g, w_row, w_t)


def _outproj_kernel(x_ref, m_ref, a_ref, b_ref, c_ref, d_ref, w_ref, o_ref):
    m = m_ref[0]
    mix = _dot(a_ref[...].astype(bf16), w_ref[0:256, :])
    mix += _dot(b_ref[...].astype(bf16), w_ref[256:512, :])
    mix += _dot(c_ref[...].astype(bf16), w_ref[512:768, :])
    mix += _dot(d_ref[...].astype(bf16), w_ref[768:1024, :])
    o_ref[...] = x_ref[...] + m[5:6] * mix


def _out_proj(x, mods, oa, ob, oc, od, w_out, *, base, rpm, tm):
    n = x.shape[0]
    part = pl.BlockSpec((tm, 256), lambda i: (i, 0))
    return pl.pallas_call(
        _outproj_kernel,
        out_shape=jax.ShapeDtypeStruct((n, D_MODEL), f32),
        grid=(n // tm,),
        in_specs=[pl.BlockSpec((tm, D_MODEL), lambda i: (i, 0)), _mod_spec(base, rpm, tm),
                  part, part, part, part, _const_spec((D_MODEL, D_MODEL))],
        out_specs=pl.BlockSpec((tm, D_MODEL), lambda i: (i, 0)),
        compiler_params=_params("parallel"),
        name="out_proj",
    )(x, mods, oa, ob, oc, od, w_out)


def _rope(x, tab_ref, rows, lanes, shift):
    width = x.shape[-1]
    c = tab_ref[0, rows, lanes]
    sa = tab_ref[1, rows, lanes]
    sb = tab_ref[2, rows, lanes]
    return x * c + pltpu.roll(x, shift, axis=1) * sa + pltpu.roll(x, width - shift, axis=1) * sb


def _softmax_rows(s_list, sink=None):
    mx = functools.reduce(jnp.maximum, [jnp.max(s, axis=-1, keepdims=True) for s in s_list])
    if sink is not None:
        mx = jnp.maximum(mx, sink)
    ps = [jnp.exp(s - mx) for s in s_list]
    den = functools.reduce(lambda a, b: a + b, [jnp.sum(p, axis=-1, keepdims=True) for p in ps])
    if sink is not None:
        den = den + jnp.exp(sink - mx)
    inv = 1.0 / den
    return [p * inv for p in ps]


def _attn_ctx_kernel(sink_ref, q_ref, kv_ref, o_ref, ko_ref, vo_ref):
    q = q_ref[...]
    kv = kv_ref[...]
    scale = HD_A ** -0.5
    outs = []
    for j in range(KV_A):
        k = kv[:, HD_A * j:HD_A * (j + 1)]
        v = kv[:, KV_A * HD_A + HD_A * j:KV_A * HD_A + HD_A * (j + 1)]
        ko_ref[0, j] = k
        vo_ref[0, j] = v
        kb = k.astype(bf16)
        vb = v.astype(bf16)
        for g in range(G_A):
            h = j * G_A + g
            qh = q[:, HD_A * h:HD_A * (h + 1)].astype(bf16)
            s = lax.dot_general(qh, kb, _NT, preferred_element_type=f32) * scale
            (p,) = _softmax_rows([s], sink_ref[h])
            outs.append(_dot(p.astype(bf16), vb))
    o_ref[...] = jnp.concatenate(outs, axis=-1)


def _attn_ctx(pr, sink, *, batch, t):
    kvshape = jax.ShapeDtypeStruct((batch, KV_A, t, HD_A), f32)
    kvspec = pl.BlockSpec((1, KV_A, t, HD_A), lambda b: (b, 0, 0, 0))
    return pl.pallas_call(
        _attn_ctx_kernel,
        out_shape=(jax.ShapeDtypeStruct((batch * t, 256), f32), kvshape, kvshape),
        grid=(batch,),
        in_specs=[pl.BlockSpec(memory_space=pltpu.SMEM),
                  pl.BlockSpec((t, 256), lambda b: (b, 0)), pl.BlockSpec((t, 256), lambda b: (b, 1))],
        out_specs=(pl.BlockSpec((t, 256), lambda b: (b, 0)), kvspec, kvspec),
        compiler_params=_params("parallel"),
        name="attn_ctx",
    )(sink, pr, pr)


def _attn_lat_kernel(sink_ref, q_ref, kv_ref, kc_ref, vc_ref, tab_ref, o_ref, kpad, vpad, *, t):
    n = pl.program_id(1)
    kvw = KV_A * HD_A

    @pl.when(n == 0)
    def _():
        zeros = jnp.zeros((BLOCK_A, kvw), bf16)
        kv = kv_ref[...]
        kpad[0:BLOCK_A, :] = zeros
        vpad[0:BLOCK_A, :] = zeros
        kpad[BLOCK_A:BLOCK_A + t, :] = _rope(kv[:, 0:kvw], tab_ref, slice(None), slice(0, kvw), HD_A // 2).astype(bf16)
        vpad[BLOCK_A:BLOCK_A + t, :] = kv[:, kvw:2 * kvw].astype(bf16)
        kpad[BLOCK_A + t:2 * BLOCK_A + t, :] = zeros
        vpad[BLOCK_A + t:2 * BLOCK_A + t, :] = zeros

    row0 = pl.multiple_of(n * BLOCK_A, BLOCK_A)
    q = _rope(q_ref[...], tab_ref, pl.ds(row0, BLOCK_A), slice(None), HD_A // 2)
    kband = kpad[pl.ds(row0, 3 * BLOCK_A), :]
    vband = vpad[pl.ds(row0, 3 * BLOCK_A), :]
    scale = HD_A ** -0.5
    rows = G_A * BLOCK_A
    r = lax.broadcasted_iota(jnp.int32, (rows, 3 * BLOCK_A), 0)
    cidx = lax.broadcasted_iota(jnp.int32, (rows, 3 * BLOCK_A), 1)
    q_pos = n * BLOCK_A + (r % BLOCK_A)
    k_pos = (n - 1) * BLOCK_A + cidx
    valid = (jnp.abs(k_pos - q_pos) <= WINDOW_A) & (k_pos >= 0) & (k_pos < t)
    rcol = lax.broadcasted_iota(jnp.int32, (rows, 1), 0)
    outs = []
    for j in range(KV_A):
        q2 = jnp.concatenate([q[:, HD_A * (j * G_A + g):HD_A * (j * G_A + g + 1)] for g in range(G_A)],
                             axis=0).astype(bf16)
        ksl = slice(HD_A * j, HD_A * (j + 1))
        s_loc = lax.dot_general(q2, kband[:, ksl], _NT, preferred_element_type=f32) * scale
        s_loc = jnp.where(valid, s_loc, NEG_INF)
        s_ctx = lax.dot_general(q2, kc_ref[0, j].astype(bf16), _NT, preferred_element_type=f32) * scale
        sink = jnp.where(rcol < BLOCK_A, sink_ref[j * G_A], sink_ref[j * G_A + 1])
        p_loc, p_ctx = _softmax_rows([s_loc, s_ctx], sink)
        o = _dot(p_loc.astype(bf16), vband[:, ksl]) + _dot(p_ctx.astype(bf16), vc_ref[0, j].astype(bf16))
        outs += [o[g * BLOCK_A:(g + 1) * BLOCK_A] for g in range(G_A)]
    o_ref[...] = jnp.concatenate(outs, axis=-1)


def _attn_lat(pr, sink, kc, vc, tab, *, batch, t):
    nb = t // BLOCK_A
    lc = kc.shape[2]
    cspec = pl.BlockSpec((1, KV_A, lc, HD_A), lambda b, n: (b, 0, 0, 0))
    return pl.pallas_call(
        functools.partial(_attn_lat_kernel, t=t),
        out_shape=jax.ShapeDtypeStruct((batch * t, 256), f32),
        grid=(batch, nb),
        in_specs=[pl.BlockSpec(memory_space=pltpu.SMEM),
                  pl.BlockSpec((BLOCK_A, 256), lambda b, n: (b * nb + n, 0)),
                  pl.BlockSpec((t, 256), lambda b, n: (b, 1)),
                  cspec, cspec,
                  pl.BlockSpec((3, t, 256), lambda b, n: (0, 0, 0))],
        out_specs=pl.BlockSpec((BLOCK_A, 256), lambda b, n: (b * nb + n, 0)),
        scratch_shapes=[pltpu.VMEM((t + 2 * BLOCK_A, KV_A * HD_A), bf16),
                        pltpu.VMEM((t + 2 * BLOCK_A, KV_A * HD_A), bf16)],
        compiler_params=_params("parallel", "arbitrary"),
        name="attn_lat",
    )(sink, pr, pr, kc, vc, tab)


def _mla_kernel(*refs, t, lc, tq, rope):
    it = iter(refs)
    cq_ref, ckv_ref, kr_ref = next(it), next(it), next(it)
    cc_ref = kc_ref = tab_ref = None
    if lc:
        cc_ref, kc_ref = next(it), next(it)
    if rope:
        tab_ref = next(it)
    qn_ref, kvn_ref, wq_ref, wk_ref, wv_ref = next(it), next(it), next(it), next(it), next(it)
    o_ref, ckv_out_ref, kbuf, vbuf = next(it), next(it), next(it), next(it)
    i = pl.program_id(1)
    hw = LANES

    @pl.when(i == 0)
    def _():
        ckv = _rms(ckv_ref[...], kvn_ref[...])
        ckv_out_ref[...] = ckv
        ckv_b = ckv.astype(bf16)
        kr = kr_ref[...]
        if rope:
            kr = _rope(kr, tab_ref, slice(None), slice(None), ROPE_C // 2)
        kk = _dot(ckv_b, wk_ref[...])
        for h in range(H_C):
            kbuf[h, 0:t, :] = (kk[:, hw * h:hw * (h + 1)] + kr).astype(bf16)
        vbuf[0:t, :] = _dot(ckv_b, wv_ref[...]).astype(bf16)
        if lc:
            cc_b = cc_ref[0].astype(bf16)
            kkc = _dot(cc_b, wk_ref[...])
            krc = kc_ref[0]
            for h in range(H_C):
                kbuf[h, t:t + lc, :] = (kkc[:, hw * h:hw * (h + 1)] + krc).astype(bf16)
            vbuf[t:t + lc, :] = _dot(cc_b, wv_ref[...]).astype(bf16)

    cqn = _rms(cq_ref[...], qn_ref[...]).astype(bf16)
    qf = _dot(cqn, wq_ref[...])
    scale = (NOPE_C + ROPE_C) ** -0.5
    row0 = pl.multiple_of(i * tq, tq)
    outs = []
    for h in range(H_C):
        qh = qf[:, hw * h:hw * (h + 1)]
        if rope:
            qh = _rope(qh, tab_ref, pl.ds(row0, tq), slice(None), ROPE_C // 2)
        s = lax.dot_general(qh.astype(bf16), kbuf[h], _NT, preferred_element_type=f32) * scale
        (p,) = _softmax_rows([s])
        outs.append(_dot(p.astype(bf16), vbuf[:, V_C * h:V_C * (h + 1)]))
    o_ref[...] = jnp.concatenate(outs, axis=-1)


def _mla(pr, qn, kvn, wq, wk, wv, *, batch, t, tq, ctx=None, tab=None):
    nq = t // tq
    lc = 0 if ctx is None else ctx[0].shape[1]
    in_specs = [pl.BlockSpec((tq, 256), lambda b, i: (b * nq + i, 2)),
                pl.BlockSpec((t, LANES), lambda b, i: (b, 6)),
                pl.BlockSpec((t, LANES), lambda b, i: (b, 7))]
    args = [pr, pr, pr]
    if lc:
        in_specs += [pl.BlockSpec((1, lc, LANES), lambda b, i: (b, 0, 0))] * 2
        args += list(ctx)
    if tab is not None:
        in_specs.append(pl.BlockSpec((3, t, LANES), lambda b, i: (0, 0, 0)))
        args.append(tab)
    in_specs += [_const_spec((1, Q_LORA_C)), _const_spec((1, KV_LORA_C)), _const_spec(wq.shape),
                 _const_spec(wk.shape), _const_spec(wv.shape)]
    args += [qn, kvn, wq, wk, wv]
    return pl.pallas_call(
        functools.partial(_mla_kernel, t=t, lc=lc, tq=tq, rope=tab is not None),
        out_shape=(jax.ShapeDtypeStruct((batch * t, 256), f32), jax.ShapeDtypeStruct((batch * t, LANES), f32)),
        grid=(batch, nq),
        in_specs=in_specs,
        out_specs=(pl.BlockSpec((tq, 256), lambda b, i: (b * nq + i, 0)),
                   pl.BlockSpec((t, LANES), lambda b, i: (b, 0))),
        scratch_shapes=[pltpu.VMEM((H_C, t + lc, LANES), bf16), pltpu.VMEM((t + lc, H_C * V_C), bf16)],
        compiler_params=_params("parallel", "arbitrary"),
        name="mla",
    )(*args)


def _pool_kernel(x_ref, w_ref, sc_ref, o_ref, xpad, *, t):
    x = x_ref[...]
    n = t + 2 * POOL_PAD
    zeros = jnp.zeros((POOL_PAD, 256), f32)
    xpad[0:POOL_PAD, :] = zeros
    xpad[POOL_PAD:POOL_PAD + t, :] = x
    xpad[POOL_PAD + t:n, :] = zeros
    xp = xpad[...]
    s2 = xp + pltpu.roll(xp, 1, axis=0)
    s4 = pltpu.roll(s2, 1, axis=0) + pltpu.roll(s2, n - 1, axis=0)
    s8 = pltpu.roll(s4, 2, axis=0) + pltpu.roll(s4, n - 2, axis=0)
    s16 = pltpu.roll(s8, 4, axis=0) + pltpu.roll(s8, n - 4, axis=0)
    sums = [s[POOL_PAD:POOL_PAD + t] for s in (s2, s4, s8, s16)]
    group = lax.broadcasted_iota(jnp.int32, (t, 256), 1) // POOL_CH
    pos = lax.broadcasted_iota(jnp.int32, (t, 256), 0)
    win_sum = sums[0]
    half = jnp.full((t, 256), POOL_WINDOWS[0] // 2, jnp.int32)
    for gi in range(1, len(POOL_WINDOWS)):
        win_sum = jnp.where(group == gi, sums[gi], win_sum)
        half = jnp.where(group == gi, POOL_WINDOWS[gi] // 2, half)
    cnt = jnp.minimum(pos + half, t) - jnp.maximum(pos - half, 0)
    mean = win_sum / cnt.astype(f32)
    o_ref[...] = _dot((mean - x).astype(bf16), w_ref[...]) * sc_ref[...]


def _pool(pr, w_bd, scale, *, batch, t):
    return pl.pallas_call(
        functools.partial(_pool_kernel, t=t),
        out_shape=jax.ShapeDtypeStruct((batch * t, 256), f32),
        grid=(batch,),
        in_specs=[pl.BlockSpec((t, 256), lambda b: (b, 4)), _const_spec((256, 256)), _const_spec((1, 256))],
        out_specs=pl.BlockSpec((t, 256), lambda b: (b, 0)),
        scratch_shapes=[pltpu.VMEM((t + 2 * POOL_PAD, 256), f32)],
        compiler_params=_params("parallel"),
        name="pool",
    )(pr, w_bd, scale)


def _hgrn_mats():
    n = HG_TILE
    s = np.arange(n)[:, None]
    t = np.arange(n)[None, :]
    same_half = s // HG_HALF == t // HG_HALF
    mid = (t // HG_HALF) * HG_HALF + HG_HALF // 2 - 1
    rel_mid = (same_half & (s > mid) & (s <= t)).astype(np.float32) - (same_half & (s <= mid) & (s > t))
    mats = [rel_mid, same_half & (s <= t), same_half & (s > t), s <= t, s > t,
            (s // HG_BLOCK == t // HG_BLOCK) & (s <= t)]
    for m in HG_LEVELS:
        same_pair = s // (2 * m) == t // (2 * m)
        right_t = (t // m) % 2 == 1
        right_s = (s // m) % 2 == 1
        q_side = same_pair & right_t & right_s & (s <= t)
        k_side = same_pair & ~right_t & ~right_s & (s > t)
        mats.append(q_side | k_side)
    fwd = np.concatenate([m.astype(np.float32) for m in mats], axis=1)
    bwd = np.concatenate([m[::-1, ::-1].astype(np.float32) for m in mats], axis=1)
    return np.stack([fwd, bwd])


def _hgrn_exact_intra(q, k, v, vb, hi, lo, mats, reverse):
    n = HG_TILE
    e_all = _dot(hi, mats) + _dot(lo, mats)
    beta = e_all[:, 0:n]
    lane = lax.broadcasted_iota(jnp.int32, (1, n), 1)
    pos = lane % HG_BLOCK
    v3 = v.reshape(H_B, DV_B, n)

    o3 = jnp.sum((q * k).reshape(H_B, DK_B, n), axis=1, keepdims=True) * v3
    for d in range(1, HG_BLOCK):
        shift = (n - d) if reverse else d
        valid = (pos <= HG_BLOCK - 1 - d) if reverse else (pos >= d)
        kr = pltpu.roll(k, shift, axis=1)
        br = pltpu.roll(beta, shift, axis=1)
        vr = pltpu.roll(v, shift, axis=1)
        e = jnp.exp(jnp.where(valid, beta - br, NEG_INF))
        sc = jnp.sum((q * kr * e).reshape(H_B, DK_B, n), axis=1, keepdims=True)
        o3 = o3 + sc * vr.reshape(H_B, DV_B, n)

    srow = lax.broadcasted_iota(jnp.int32, (n, n), 0)
    tcol = lax.broadcasted_iota(jnp.int32, (n, n), 1)
    scores = [jnp.zeros((n, n), f32) for _ in range(H_B)]
    for li, m in enumerate(HG_LEVELS):
        ex = jnp.exp(e_all[:, (li + 1) * n:(li + 2) * n])
        is_q = ((lane // m) % 2) == (0 if reverse else 1)
        qs = jnp.where(is_q, q * ex, 0.0).astype(bf16)
        ks = jnp.where(is_q, 0.0, k * ex).astype(bf16)
        pair = (srow // (2 * m)) == (tcol // (2 * m))
        for h in range(H_B):
            a = lax.dot_general(ks[DK_B * h:DK_B * (h + 1)], qs[DK_B * h:DK_B * (h + 1)], _TN,
                                preferred_element_type=f32)
            scores[h] = scores[h] + jnp.where(pair, a, 0.0)

    return [o3[h] + _dot(vb[DV_B * h:DV_B * (h + 1)], scores[h].astype(bf16)) for h in range(H_B)]


def _hgrn_common(q, v, fpre, lb, mats_ref, sidx, st_ref):
    n = HG_TILE
    f = lb + (1.0 - lb) * _sigmoid(fpre)
    k = 1.0 - f
    logf = jnp.log(f)
    hi = logf.astype(bf16)
    lo = (logf - hi.astype(f32)).astype(bf16)
    m5 = mats_ref[sidx, :, 0:5 * n]
    ex = _dot(hi, m5) + _dot(lo, m5)
    ones = jnp.ones((8, n), bf16)
    g_row = (lax.dot_general(ones, hi, _NT, preferred_element_type=f32)
             + lax.dot_general(ones, lo, _NT, preferred_element_type=f32))[0:1]
    d_mid = ex[:, 0:n]
    small = jnp.max(jnp.abs(d_mid)) <= HG_MAX_EXP
    q_in = (q * jnp.exp(ex[:, 3 * n:4 * n])).astype(bf16)
    k_out = (k * jnp.exp(ex[:, 4 * n:5 * n])).astype(bf16)
    vb = v.astype(bf16)
    o_state = []
    for h in range(H_B):
        rows = slice(DK_B * h, DK_B * (h + 1))
        st = st_ref[sidx, rows, :]
        o_state.append(_dot(st.astype(bf16), q_in[rows]))
        upd = lax.dot_general(vb[rows], k_out[rows], _NT, preferred_element_type=f32)
        st_ref[sidx, rows, :] = st * jnp.exp(g_row[:, rows]) + upd
    return dict(q=q, k=k, v=v, vb=vb, hi=hi, lo=lo, ex=ex, o_state=o_state, small=small)


def _hgrn_fast_intra(c, reverse):
    n = HG_TILE
    q, k, ex = c["q"], c["k"], c["ex"]
    lane = lax.broadcasted_iota(jnp.int32, (1, n), 1)
    in0 = (lane // HG_HALF) == 0
    q_half = in0 if reverse else jnp.logical_not(in0)
    d_mid = ex[:, 0:n]
    qd = q * jnp.exp(d_mid)
    kd = k * jnp.exp(-d_mid)
    ql = jnp.where(q_half, q * jnp.exp(ex[:, n:2 * n]), 0.0)
    kl = jnp.where(q_half, 0.0, k * jnp.exp(ex[:, 2 * n:3 * n]))
    parts_q = [jnp.where(in0, qd, 0.0).astype(bf16), jnp.where(in0, 0.0, qd).astype(bf16), ql.astype(bf16)]
    parts_k = [jnp.where(in0, kd, 0.0).astype(bf16), jnp.where(in0, 0.0, kd).astype(bf16), kl.astype(bf16)]
    srow = lax.broadcasted_iota(jnp.int32, (n, n), 0)
    tcol = lax.broadcasted_iota(jnp.int32, (n, n), 1)
    causal = (srow >= tcol) if reverse else (srow <= tcol)
    scores = []
    for h in range(H_B):
        rows = slice(DK_B * h, DK_B * (h + 1))
        k3 = jnp.concatenate([p[rows] for p in parts_k], axis=0)
        q3 = jnp.concatenate([p[rows] for p in parts_q], axis=0)
        scores.append(lax.dot_general(k3, q3, _TN, preferred_element_type=f32))
    scores = [jnp.where(causal, a, 0.0).astype(bf16) for a in scores]
    outs = [c["o_state"][h] + _dot(c["vb"][DV_B * h:DV_B * (h + 1)], scores[h]) for h in range(H_B)]
    return jnp.concatenate(outs, axis=0)


def _hgrn_exact(c, mats, reverse):
    intra = _hgrn_exact_intra(c["q"], c["k"], c["v"], c["vb"], c["hi"], c["lo"], mats, reverse)
    return jnp.concatenate([c["o_state"][h] + intra[h] for h in range(H_B)], axis=0)


def _hgrn_kernel(q_ref, i_ref, ff_ref, fb_ref, g_ref, lb_ref, ng_ref, mats_ref, s0_ref,
                 o_ref, sfin_ref, st_ref, oacc_f, oacc_b, *, t):
    n = HG_TILE
    nt = t // n
    st_ref[...] = s0_ref[0]

    def scan(i, c):
        cols_f = pl.ds(pl.multiple_of(i * n, n), n)
        cols_b = pl.ds(pl.multiple_of((nt - 1 - i) * n, n), n)
        cf = _hgrn_common(q_ref[:, cols_f], i_ref[:, cols_f], ff_ref[:, cols_f], lb_ref[0], mats_ref, 0, st_ref)
        cb = _hgrn_common(q_ref[:, cols_b], i_ref[:, cols_b], fb_ref[:, cols_b], lb_ref[1], mats_ref, 1, st_ref)
        small = jnp.logical_and(cf["small"], cb["small"])
        oacc_f[:, cols_f] = _hgrn_fast_intra(cf, False)
        oacc_b[:, cols_b] = _hgrn_fast_intra(cb, True)

        @pl.when(jnp.logical_not(small))
        def _():
            oacc_f[:, cols_f] = _hgrn_exact(cf, mats_ref[0, :, 5 * n:9 * n], False)
            oacc_b[:, cols_b] = _hgrn_exact(cb, mats_ref[1, :, 5 * n:9 * n], True)

        return c

    lax.fori_loop(0, nt, scan, 0)

    def finish(i, c):
        off = pl.multiple_of(i * n, n)
        cols = pl.ds(off, n)
        o = oacc_f[:, cols] + oacc_b[:, cols]
        o3 = o.reshape(H_B, DV_B, n)
        y = o3 * lax.rsqrt(jnp.mean(o3 * o3, axis=1, keepdims=True) + EPS)
        gate = g_ref[:, cols]
        y = y.reshape(H_B * DV_B, n) * ng_ref[...] * (gate * _sigmoid(gate))
        o_ref[pl.ds(off, n), :] = y.T
        return c

    lax.fori_loop(0, nt, finish, 0)
    sfin_ref[0] = st_ref[...]


def _hgrn(pt, lb, ng, mats, s0, *, batch, t):
    def row(r):
        return pl.BlockSpec((256, t), lambda b: (r, b))
    sspec = pl.BlockSpec((1, 2, H_B * DV_B, DK_B), lambda b: (b, 0, 0, 0))
    return pl.pallas_call(
        functools.partial(_hgrn_kernel, t=t),
        out_shape=(jax.ShapeDtypeStruct((batch * t, 256), f32),
                   jax.ShapeDtypeStruct((batch, 2, H_B * DV_B, DK_B), f32)),
        grid=(batch,),
        in_specs=[row(0), row(1), row(2), row(3), row(4),
                  _const_spec((2, 256, LANES)), _const_spec((256, LANES)), _const_spec(mats.shape), sspec],
        out_specs=(pl.BlockSpec((t, 256), lambda b: (b, 0)), sspec),
        scratch_shapes=[pltpu.VMEM((2, H_B * DV_B, DK_B), f32), pltpu.VMEM((256, t), f32),
                        pltpu.VMEM((256, t), f32)],
        compiler_params=_params("parallel"),
        name="hgrn",
    )(pt, pt, pt, pt, pt, lb, ng, mats, s0)


def _rope_tables(t):
    def cos_sin(dim):
        n_freq = dim // 4
        inv = ROPE_THETA ** (-jnp.arange(n_freq, dtype=f32) / n_freq)
        rows = t // GRID_W
        row_id = jnp.repeat(jnp.arange(rows, dtype=f32), GRID_W)
        col_id = jnp.tile(jnp.arange(GRID_W, dtype=f32), rows)
        ang = jnp.concatenate([row_id[:, None] * inv, col_id[:, None] * inv], axis=-1)
        return jnp.cos(ang), jnp.sin(ang)

    cos, sin = cos_sin(HD_A)
    zero = jnp.zeros_like(sin)
    tab_a = jnp.stack([jnp.tile(jnp.concatenate([cos, cos], -1), (1, H_A)),
                       jnp.tile(jnp.concatenate([zero, sin], -1), (1, H_A)),
                       jnp.tile(jnp.concatenate([-sin, zero], -1), (1, H_A))])
    cos, sin = cos_sin(ROPE_C)
    zero = jnp.zeros_like(sin)
    one64 = jnp.ones((t, NOPE_C), f32)
    zero64 = jnp.zeros((t, NOPE_C), f32)
    pad32 = jnp.zeros((t, LANES - NOPE_C - ROPE_C), f32)
    tab_c = jnp.stack([jnp.concatenate([one64, cos, cos, pad32 + 1.0], -1),
                       jnp.concatenate([zero64, zero, sin, pad32], -1),
                       jnp.concatenate([zero64, -sin, zero, pad32], -1)])
    return tab_a, tab_c


def _pad_rope_cols(x):
    pads = [(0, 0)] * (x.ndim - 1) + [(NOPE_C, LANES - NOPE_C - ROPE_C)]
    return jnp.pad(x, pads)


def kernel(x_prompt, x_sample, c, c_ctx, cache_attn_k, cache_attn_v, state_hgrn, cache_mla_ckv, cache_mla_krope,
           w_ada, b_ada, norm_sub, w_ffn_gate, w_ffn_up, w_ffn_down, w_in, w_out, attn_sink, hgrn_lb_logits,
           hgrn_out_norm, mla_q_norm, mla_kv_norm, mla_w_uq, mla_w_ukv, pool_w, pool_scale, final_norm):
    bp, tp, _ = x_prompt.shape
    bs, ts, _ = x_sample.shape
    np_, ns = bp * tp, bs * ts
    xp = x_prompt.reshape(np_, D_MODEL)
    xs = x_sample.reshape(ns, D_MODEL)

    n_vec = 1 + bs
    cvec = jnp.zeros((-(-n_vec // 8) * 8, D_MODEL), f32).at[0].set(c_ctx).at[1:n_vec].set(c)
    mods = _modulation(cvec, w_ada, b_ada).reshape(DEPTH, cvec.shape[0], N_MOD, D_MODEL)

    p_lb = jax.nn.softmax(hgrn_lb_logits.astype(f32), axis=0)
    lb_all = jnp.cumsum(p_lb, axis=0) - p_lb[0:1]
    tab_a, tab_c = _rope_tables(ts)
    mats = jnp.asarray(_hgrn_mats(), bf16)
    n_ffn = DEPTH * 2
    wg_all = _to_bf16(w_ffn_gate.reshape(n_ffn * D_MODEL, D_FF), 512).reshape(n_ffn, D_MODEL, D_FF)
    wu_all = _to_bf16(w_ffn_up.reshape(n_ffn * D_MODEL, D_FF), 512).reshape(n_ffn, D_MODEL, D_FF)
    wd_all = _to_bf16(w_ffn_down.reshape(n_ffn * D_FF, D_MODEL), D_FF // 2).reshape(n_ffn, D_FF, D_MODEL)
    grp_p = dict(base=0, rpm=np_)
    grp_s = dict(base=1, rpm=ts)
    tm = 512

    new_k, new_v, new_st, new_ckv, new_kr = [], [], [], [], []
    for l in range(DEPTH):
        m_l = mods[l]
        wb = lambda w: w.astype(bf16)
        wi = w_in[l]
        w_row = wb(jnp.concatenate([wi[:, 0:512], wi[:, 1792:2176], _pad_rope_cols(wi[:, 2176:2208]),
                                    wi[:, 2208:2464]], axis=1))
        w_t = wb(wi[:, 512:1792].T)
        wq = wb(jnp.pad(mla_w_uq[l].reshape(Q_LORA_C, H_C, NOPE_C + ROPE_C),
                        ((0, 0), (0, 0), (0, LANES - NOPE_C - ROPE_C))).reshape(Q_LORA_C, H_C * LANES))
        wkv = mla_w_ukv[l].reshape(KV_LORA_C, H_C, NOPE_C + V_C)
        wk = wb(jnp.pad(wkv[:, :, :NOPE_C], ((0, 0), (0, 0), (0, LANES - NOPE_C))).reshape(KV_LORA_C, H_C * LANES))
        wv = wb(wkv[:, :, NOPE_C:].reshape(KV_LORA_C, H_C * V_C))
        w_pool = wb(jax.scipy.linalg.block_diag(*[pool_w[l, gi] for gi in range(len(POOL_WINDOWS))]))
        lb = jnp.broadcast_to(lb_all[l][:, :, None], (2, 256, LANES))
        ng = jnp.broadcast_to(jnp.tile(hgrn_out_norm[l], H_B)[:, None], (256, LANES))
        sink = attn_sink[l].astype(f32)
        qn = mla_q_norm[l].reshape(1, Q_LORA_C)
        kvn = mla_kv_norm[l].reshape(1, KV_LORA_C)
        pscale = pool_scale[l].reshape(1, 256)
        wo = wb(w_out[l])

        def ffn(x, grp, j, idx, final_g=None):
            return _ffn(x, m_l, norm_sub[l, j].reshape(1, D_MODEL), wg_all, wu_all, wd_all, widx=2 * l + idx,
                        j=j, tm=tm, final_g=final_g, **grp)

        xp = ffn(xp, grp_p, 0, 0)
        xs = ffn(xs, grp_s, 0, 0)

        g1 = norm_sub[l, 1].reshape(1, D_MODEL)
        pr, pt = _mixer_in(xp, m_l, g1, w_row, w_t, tm=tm, **grp_p)
        oa, ka, va = _attn_ctx(pr, sink, batch=bp, t=tp)
        s0 = jnp.zeros((bp, 2, H_B * DV_B, DK_B), f32)
        ob, st = _hgrn(pt, lb, ng, mats, s0, batch=bp, t=tp)
        oc, ckv = _mla(pr, qn, kvn, wq, wk, wv, batch=bp, t=tp, tq=tp)
        od = _pool(pr, w_pool, pscale, batch=bp, t=tp)
        xp = _out_proj(xp, m_l, oa, ob, oc, od, wo, tm=tm, **grp_p)
        new_k.append(ka)
        new_v.append(va)
        new_st.append(jnp.swapaxes(st.reshape(bp, 2, H_B, DV_B, DK_B), -1, -2))
        new_ckv.append(ckv.reshape(bp, tp, KV_LORA_C))
        new_kr.append(pr[:, 896 + NOPE_C:896 + NOPE_C + ROPE_C].reshape(bp, tp, ROPE_C))

        pr, pt = _mixer_in(xs, m_l, g1, w_row, w_t, tm=tm, **grp_s)
        oa = _attn_lat(pr, sink, cache_attn_k[:, l], cache_attn_v[:, l], tab_a, batch=bs, t=ts)
        s0 = jnp.swapaxes(state_hgrn[:, l], -1, -2).reshape(bs, 2, H_B * DV_B, DK_B)
        ob, _ = _hgrn(pt, lb, ng, mats, s0, batch=bs, t=ts)
        oc, _ = _mla(pr, qn, kvn, wq, wk, wv, batch=bs, t=ts, tq=256,
                     ctx=(cache_mla_ckv[:, l], _pad_rope_cols(cache_mla_krope[:, l])), tab=tab_c)
        od = _pool(pr, w_pool, pscale, batch=bs, t=ts)
        xs = _out_proj(xs, m_l, oa, ob, oc, od, wo, tm=tm, **grp_s)

        fin = final_norm.reshape(1, D_MODEL) if l == DEPTH - 1 else None
        xp = ffn(xp, grp_p, 2, 1, fin)
        xs = ffn(xs, grp_s, 2, 1, fin)

    return (xp.reshape(bp, tp, D_MODEL), xs.reshape(bs, ts, D_MODEL),
            jnp.stack(new_k, axis=1), jnp.stack(new_v, axis=1), jnp.stack(new_st, axis=1),
            jnp.stack(new_ckv, axis=1), jnp.stack(new_kr, axis=1))
```

```python
import functools

import numpy as np
import jax
import jax.numpy as jnp
from jax import lax
from jax.experimental import pallas as pl
from jax.experimental.pallas import tpu as pltpu

f32 = jnp.float32
bf16 = jnp.bfloat16

D_MODEL = 1024
DEPTH = 2
GRID_W = 64
EPS = 1e-6
LOG2E = 1.4426950408889634
ROPE_THETA = 10000.0
NEG_INF = -1e30
N_MOD = 9
D_FF = 2816
H_A, KV_A, HD_A = 4, 2, 64
G_A = H_A // KV_A
WINDOW_A = 128
BLOCK_A = 128
H_B, DK_B, DV_B = 4, 64, 64
H_C, Q_LORA_C, KV_LORA_C, NOPE_C, ROPE_C, V_C = 4, 256, 128, 64, 32, 64
POOL_WINDOWS = (2, 4, 8, 16)
POOL_CH = 64

LANES = 128
VMEM_LIMIT = 56 * 1024 * 1024
HG_TILE = LANES
HG_HALF = HG_TILE // 2
HG_MAX_EXP = 80.0
HG_BLOCK = 16
HG_LEVELS = (16, 32, 64)
ATTN_LAT_BLOCKS = 4
POOL_PAD = 16

PR_W = 1280
PT_W = 1280

_NT = (((1,), (1,)), ((), ()))
_TN = (((0,), (0,)), ((), ()))


def _params(*sem):
    return pltpu.CompilerParams(dimension_semantics=sem, vmem_limit_bytes=VMEM_LIMIT)


def _const_spec(shape):
    zeros = (0,) * len(shape)
    return pl.BlockSpec(shape, lambda *_: zeros, pipeline_mode=pl.Buffered(1))


def _dot(a, b):
    return jnp.dot(a, b, preferred_element_type=f32)


def _sigmoid(x):
    return 1.0 / (1.0 + jnp.exp(-x))


def _rms(x, g):
    return x * lax.rsqrt(jnp.mean(x * x, axis=-1, keepdims=True) + EPS) * g


def _mod_kernel(c_ref, w_ref, b_ref, o_ref):
    c = c_ref[...]
    a = (c * _sigmoid(c)).astype(bf16)
    o_ref[0] = _dot(a, w_ref[0].astype(bf16)) + b_ref[0]


def _modulation(cvec, w_ada, b_ada):
    tn = 1536
    n_out = N_MOD * D_MODEL
    return pl.pallas_call(
        _mod_kernel,
        out_shape=jax.ShapeDtypeStruct((DEPTH, cvec.shape[0], n_out), f32),
        grid=(DEPTH, n_out // tn),
        in_specs=[pl.BlockSpec(cvec.shape, lambda l, j: (0, 0)),
                  pl.BlockSpec((1, D_MODEL, tn), lambda l, j: (l, 0, j)),
                  pl.BlockSpec((1, 1, tn), lambda l, j: (l, 0, j))],
        out_specs=pl.BlockSpec((1, cvec.shape[0], tn), lambda l, j: (l, 0, j)),
        compiler_params=_params("arbitrary", "arbitrary"),
        name="modulation",
    )(cvec, w_ada, b_ada.reshape(DEPTH, 1, n_out))


def _cast_kernel(x_ref, o_ref):
    o_ref[...] = x_ref[...].astype(bf16)


def _to_bf16(x, rows):
    r, c = x.shape
    return pl.pallas_call(
        _cast_kernel,
        out_shape=jax.ShapeDtypeStruct((r, c), bf16),
        grid=(r // rows,),
        in_specs=[pl.BlockSpec((rows, c), lambda i: (i, 0))],
        out_specs=pl.BlockSpec((rows, c), lambda i: (i, 0)),
        compiler_params=_params("parallel"),
        name="cast_bf16",
    )(x)


def _ffn_kernel(*refs, j, mix, final):
    it = iter(refs)
    x_ref, m_ref = next(it), next(it)
    mix_refs = [next(it) for _ in range(5)] if mix else None
    g_ref, wg_ref, wu_ref, wd_ref = next(it), next(it), next(it), next(it)
    fg_ref = next(it) if final else None
    o_ref = next(it)
    x = x_ref[...]
    m = m_ref[0]
    if mix:
        wo_ref = mix_refs[4]
        acc = _dot(mix_refs[0][...], wo_ref[0:256, :])
        for p in range(1, 4):
            acc += _dot(mix_refs[p][...], wo_ref[256 * p:256 * (p + 1), :])
        x = x + m[5:6] * acc
    h = (_rms(x, g_ref[...]) * (1.0 + m[3 * j + 1:3 * j + 2]) + m[3 * j:3 * j + 1]).astype(bf16)
    gate = _dot(h, wg_ref[...])
    up = _dot(h, wu_ref[...])
    act = (gate * _sigmoid(gate) * up).astype(bf16)
    y = x + (0.5 * m[3 * j + 2:3 * j + 3]) * _dot(act, wd_ref[...])
    if final:
        y = _rms(y, fg_ref[...])
    o_ref[...] = y


def _mod_spec(base, rpm, tm):
    return pl.BlockSpec((1, N_MOD, D_MODEL), lambda i: (base + (i * tm) // rpm, 0, 0))


def _ffn(x, mods, g, wg, wu, wd, *, widx, j, base, rpm, tm, mix=None, final_g=None):
    n = x.shape[0]

    def wspec(r, c):
        return pl.BlockSpec((None, r, c), lambda i: (widx, 0, 0), pipeline_mode=pl.Buffered(1))

    in_specs = [pl.BlockSpec((tm, D_MODEL), lambda i: (i, 0)), _mod_spec(base, rpm, tm)]
    args = [x, mods]
    if mix is not None:
        in_specs += [pl.BlockSpec((tm, 256), lambda i: (i, 0))] * 4 + [_const_spec((D_MODEL, D_MODEL))]
        args += list(mix)
    in_specs += [_const_spec((1, D_MODEL)), wspec(D_MODEL, D_FF), wspec(D_MODEL, D_FF), wspec(D_FF, D_MODEL)]
    args += [g, wg, wu, wd]
    if final_g is not None:
        in_specs.append(_const_spec((1, D_MODEL)))
        args.append(final_g)
    return pl.pallas_call(
        functools.partial(_ffn_kernel, j=j, mix=mix is not None, final=final_g is not None),
        out_shape=jax.ShapeDtypeStruct((n, D_MODEL), f32),
        grid=(n // tm,),
        in_specs=in_specs,
        out_specs=pl.BlockSpec((tm, D_MODEL), lambda i: (i, 0)),
        compiler_params=_params("parallel"),
        name="ffn",
    )(*args)


def _mixin_kernel(x_ref, m_ref, g_ref, wr_ref, wt_ref, pr_ref, pt_ref):
    m = m_ref[0]
    h = (_rms(x_ref[...], g_ref[...]) * (1.0 + m[4:5]) + m[3:4]).astype(bf16)
    pr_ref[...] = _dot(h, wr_ref[...])
    pt_ref[...] = lax.dot_general(wt_ref[...], h, _NT, preferred_element_type=f32)


def _mixer_in(x, mods, g, w_row, w_t, *, base, rpm, tm):
    n = x.shape[0]
    return pl.pallas_call(
        _mixin_kernel,
        out_shape=(jax.ShapeDtypeStruct((n, PR_W), f32), jax.ShapeDtypeStruct((PT_W, n), f32)),
        grid=(n // tm,),
        in_specs=[pl.BlockSpec((tm, D_MODEL), lambda i: (i, 0)), _mod_spec(base, rpm, tm),
                  _const_spec((1, D_MODEL)), _const_spec((D_MODEL, PR_W)), _const_spec((PT_W, D_MODEL))],
        out_specs=(pl.BlockSpec((tm, PR_W), lambda i: (i, 0)), pl.BlockSpec((PT_W, tm), lambda i: (0, i))),
        compiler_params=_params("parallel"),
        name="mixer_in",
    )(x, mods, g, w_row, w_t)


def _rope(x, tab_ref, rows, lanes, shift):
    width = x.shape[-1]
    c = tab_ref[0, rows, lanes]
    sa = tab_ref[1, rows, lanes]
    sb = tab_ref[2, rows, lanes]
    return x * c + pltpu.roll(x, shift, axis=1) * sa + pltpu.roll(x, width - shift, axis=1) * sb


def _softmax_parts(s_list, scale, sink=None):
    mx = functools.reduce(jnp.maximum, [jnp.max(s, axis=-1, keepdims=True) for s in s_list]) * scale
    if sink is not None:
        mx = jnp.maximum(mx, sink)
    off = mx * LOG2E
    ps = [jnp.exp2(s * (scale * LOG2E) - off) for s in s_list]
    den = functools.reduce(lambda a, b: a + b, [jnp.sum(p, axis=-1, keepdims=True) for p in ps])
    if sink is not None:
        den = den + jnp.exp(sink - mx)
    return ps, 1.0 / den


def _kv_per_query_head(x):
    return jnp.concatenate([x[:, HD_A * (h // G_A):HD_A * (h // G_A + 1)] for h in range(H_A)], axis=1)


def _stack_heads(q):
    head = lax.broadcasted_iota(jnp.int32, (1, H_A * HD_A), 1) // HD_A
    return jnp.concatenate([jnp.where(head == h, q, 0.0) for h in range(H_A)], axis=0)


def _pick_heads(o, r):
    head = lax.broadcasted_iota(jnp.int32, (1, H_A * HD_A), 1) // HD_A
    out = o[0:r]
    for h in range(1, H_A):
        out = jnp.where(head == h, o[h * r:(h + 1) * r], out)
    return out


def _sink_rows(sink_ref, r):
    row = lax.broadcasted_iota(jnp.int32, (H_A * r, 1), 0)
    out = jnp.full((H_A * r, 1), sink_ref[0], f32)
    for h in range(1, H_A):
        out = jnp.where(row >= h * r, sink_ref[h], out)
    return out


def _attn_ctx_kernel(sink_ref, q_ref, kv_ref, o_ref, ko_ref, vo_ref, *, t):
    kv = kv_ref[...]
    kvw = KV_A * HD_A
    for j in range(KV_A):
        ko_ref[0, j] = kv[:, HD_A * j:HD_A * (j + 1)]
        vo_ref[0, j] = kv[:, kvw + HD_A * j:kvw + HD_A * (j + 1)]
    kx = _kv_per_query_head(kv[:, 0:kvw]).astype(bf16)
    vx = _kv_per_query_head(kv[:, kvw:2 * kvw]).astype(bf16)
    q4 = _stack_heads(q_ref[...]).astype(bf16)
    s = lax.dot_general(q4, kx, _NT, preferred_element_type=f32)
    (p,), inv = _softmax_parts([s], HD_A ** -0.5, _sink_rows(sink_ref, t))
    o_ref[...] = _pick_heads(_dot(p.astype(bf16), vx) * inv, t).astype(o_ref.dtype)


def _attn_ctx(pr, sink, *, batch, t):
    kvshape = jax.ShapeDtypeStruct((batch, KV_A, t, HD_A), f32)
    kvspec = pl.BlockSpec((1, KV_A, t, HD_A), lambda b: (b, 0, 0, 0))
    return pl.pallas_call(
        functools.partial(_attn_ctx_kernel, t=t),
        out_shape=(jax.ShapeDtypeStruct((batch * t, 256), bf16), kvshape, kvshape),
        grid=(batch,),
        in_specs=[pl.BlockSpec(memory_space=pltpu.SMEM),
                  pl.BlockSpec((t, 256), lambda b: (b, 0)), pl.BlockSpec((t, 256), lambda b: (b, 1))],
        out_specs=(pl.BlockSpec((t, 256), lambda b: (b, 0)), kvspec, kvspec),
        compiler_params=_params("parallel"),
        name="attn_ctx",
    )(sink, pr, pr)


def _attn_lat_kernel(sink_ref, q_ref, kv_ref, kc_ref, vc_ref, tab_ref, o_ref, kpad, vpad, kctx, vctx, *, t):
    n = pl.program_id(1)
    kvw = KV_A * HD_A
    hw = H_A * HD_A

    @pl.when(n == 0)
    def _():
        zeros = jnp.zeros((BLOCK_A, hw), bf16)
        kv = kv_ref[...]
        k = _rope(kv[:, 0:kvw], tab_ref, slice(None), slice(0, kvw), HD_A // 2)
        kpad[0:BLOCK_A, :] = zeros
        vpad[0:BLOCK_A, :] = zeros
        kpad[BLOCK_A:BLOCK_A + t, :] = _kv_per_query_head(k).astype(bf16)
        vpad[BLOCK_A:BLOCK_A + t, :] = _kv_per_query_head(kv[:, kvw:2 * kvw]).astype(bf16)
        kpad[BLOCK_A + t:2 * BLOCK_A + t, :] = zeros
        vpad[BLOCK_A + t:2 * BLOCK_A + t, :] = zeros
        kctx[...] = jnp.concatenate([kc_ref[0, h // G_A] for h in range(H_A)], axis=1).astype(bf16)
        vctx[...] = jnp.concatenate([vc_ref[0, h // G_A] for h in range(H_A)], axis=1).astype(bf16)

    rows = H_A * BLOCK_A
    rr = lax.broadcasted_iota(jnp.int32, (rows, 1), 0) % BLOCK_A
    cidx = lax.broadcasted_iota(jnp.int32, (rows, 3 * BLOCK_A), 1)
    sink = _sink_rows(sink_ref, BLOCK_A)
    for u in range(ATTN_LAT_BLOCKS):
        blk = n * ATTN_LAT_BLOCKS + u
        row0 = pl.multiple_of(blk * BLOCK_A, BLOCK_A)
        q = _rope(q_ref[u * BLOCK_A:(u + 1) * BLOCK_A, :], tab_ref, pl.ds(row0, BLOCK_A), slice(None), HD_A // 2)
        q4 = _stack_heads(q).astype(bf16)
        kband = kpad[pl.ds(row0, 3 * BLOCK_A), :]
        vband = vpad[pl.ds(row0, 3 * BLOCK_A), :]
        s_loc = lax.dot_general(q4, kband, _NT, preferred_element_type=f32)
        s_ctx = lax.dot_general(q4, kctx[...], _NT, preferred_element_type=f32)
        lo = jnp.maximum(rr + (BLOCK_A - WINDOW_A), jnp.maximum(0, (1 - blk) * BLOCK_A))
        hi = jnp.minimum(rr + (BLOCK_A + WINDOW_A), jnp.minimum(3 * BLOCK_A, t - (blk - 1) * BLOCK_A) - 1)
        s_loc = jnp.where((cidx >= lo) & (cidx <= hi), s_loc, NEG_INF)
        (p_loc, p_ctx), inv = _softmax_parts([s_loc, s_ctx], HD_A ** -0.5, sink)
        o = (_dot(p_loc.astype(bf16), vband) + _dot(p_ctx.astype(bf16), vctx[...])) * inv
        o_ref[u * BLOCK_A:(u + 1) * BLOCK_A, :] = _pick_heads(o, BLOCK_A).astype(o_ref.dtype)


def _attn_lat(pr, sink, kc, vc, tab, *, batch, t):
    rows = ATTN_LAT_BLOCKS * BLOCK_A
    nb = t // rows
    lc = kc.shape[2]
    cspec = pl.BlockSpec((1, KV_A, lc, HD_A), lambda b, n: (b, 0, 0, 0))
    return pl.pallas_call(
        functools.partial(_attn_lat_kernel, t=t),
        out_shape=jax.ShapeDtypeStruct((batch * t, 256), bf16),
        grid=(batch, nb),
        in_specs=[pl.BlockSpec(memory_space=pltpu.SMEM),
                  pl.BlockSpec((rows, 256), lambda b, n: (b * nb + n, 0)),
                  pl.BlockSpec((t, 256), lambda b, n: (b, 1)),
                  cspec, cspec,
                  pl.BlockSpec((3, t, 256), lambda b, n: (0, 0, 0))],
        out_specs=pl.BlockSpec((rows, 256), lambda b, n: (b * nb + n, 0)),
        scratch_shapes=[pltpu.VMEM((t + 2 * BLOCK_A, H_A * HD_A), bf16),
                        pltpu.VMEM((t + 2 * BLOCK_A, H_A * HD_A), bf16),
                        pltpu.VMEM((lc, H_A * HD_A), bf16), pltpu.VMEM((lc, H_A * HD_A), bf16)],
        compiler_params=_params("parallel", "arbitrary"),
        name="attn_lat",
    )(sink, pr, pr, kc, vc, tab)


def _mla_kernel(*refs, t, lc, tq, rope):
    it = iter(refs)
    cq_ref, ckv_ref, kr_ref = next(it), next(it), next(it)
    cc_ref = kc_ref = tab_ref = None
    if lc:
        cc_ref, kc_ref = next(it), next(it)
    if rope:
        tab_ref = next(it)
    qn_ref, kvn_ref, wq_ref, wk_ref, wv_ref = next(it), next(it), next(it), next(it), next(it)
    o_ref, ckv_out_ref, kbuf, vbuf = next(it), next(it), next(it), next(it)
    i = pl.program_id(1)
    hw = LANES

    @pl.when(i == 0)
    def _():
        ckv = _rms(ckv_ref[...], kvn_ref[...])
        ckv_out_ref[...] = ckv
        ckv_b = ckv.astype(bf16)
        kr = kr_ref[...]
        if rope:
            kr = _rope(kr, tab_ref, slice(None), slice(None), ROPE_C // 2)
        kk = _dot(ckv_b, wk_ref[...])
        for h in range(H_C):
            kbuf[h, 0:t, :] = (kk[:, hw * h:hw * (h + 1)] + kr).astype(bf16)
        vbuf[0:t, :] = _dot(ckv_b, wv_ref[...]).astype(bf16)
        if lc:
            cc_b = cc_ref[0].astype(bf16)
            kkc = _dot(cc_b, wk_ref[...])
            krc = kc_ref[0]
            for h in range(H_C):
                kbuf[h, t:t + lc, :] = (kkc[:, hw * h:hw * (h + 1)] + krc).astype(bf16)
            vbuf[t:t + lc, :] = _dot(cc_b, wv_ref[...]).astype(bf16)

    cqn = _rms(cq_ref[...], qn_ref[...]).astype(bf16)
    qf = _dot(cqn, wq_ref[...])
    scale = (NOPE_C + ROPE_C) ** -0.5
    row0 = pl.multiple_of(i * tq, tq)
    head = lax.broadcasted_iota(jnp.int32, (1, H_C * V_C), 1) // V_C
    out = None
    for h in range(H_C):
        qh = qf[:, hw * h:hw * (h + 1)]
        if rope:
            qh = _rope(qh, tab_ref, pl.ds(row0, tq), slice(None), ROPE_C // 2)
        s = lax.dot_general(qh.astype(bf16), kbuf[h], _NT, preferred_element_type=f32)
        (p,), inv = _softmax_parts([s], scale)
        o_h = _dot(p.astype(bf16), vbuf[...]) * inv
        out = o_h if out is None else jnp.where(head == h, o_h, out)
    o_ref[...] = out.astype(o_ref.dtype)


def _mla(pr, qn, kvn, wq, wk, wv, *, batch, t, tq, ctx=None, tab=None):
    nq = t // tq
    lc = 0 if ctx is None else ctx[0].shape[1]
    in_specs = [pl.BlockSpec((tq, 256), lambda b, i: (b * nq + i, 2)),
                pl.BlockSpec((t, LANES), lambda b, i: (b, 6)),
                pl.BlockSpec((t, LANES), lambda b, i: (b, 7))]
    args = [pr, pr, pr]
    if lc:
        in_specs += [pl.BlockSpec((1, lc, LANES), lambda b, i: (b, 0, 0))] * 2
        args += list(ctx)
    if tab is not None:
        in_specs.append(pl.BlockSpec((3, t, LANES), lambda b, i: (0, 0, 0)))
        args.append(tab)
    in_specs += [_const_spec((1, Q_LORA_C)), _const_spec((1, KV_LORA_C)), _const_spec(wq.shape),
                 _const_spec(wk.shape), _const_spec(wv.shape)]
    args += [qn, kvn, wq, wk, wv]
    return pl.pallas_call(
        functools.partial(_mla_kernel, t=t, lc=lc, tq=tq, rope=tab is not None),
        out_shape=(jax.ShapeDtypeStruct((batch * t, 256), bf16), jax.ShapeDtypeStruct((batch * t, LANES), f32)),
        grid=(batch, nq),
        in_specs=in_specs,
        out_specs=(pl.BlockSpec((tq, 256), lambda b, i: (b * nq + i, 0)),
                   pl.BlockSpec((t, LANES), lambda b, i: (b, 0))),
        scratch_shapes=[pltpu.VMEM((H_C, t + lc, LANES), bf16), pltpu.VMEM((t + lc, H_C * V_C), bf16)],
        compiler_params=_params("parallel", "arbitrary"),
        name="mla",
    )(*args)


def _pool_kernel(x_ref, w_ref, sc_ref, o_ref, xpad, *, t):
    x = x_ref[...]
    n = t + 2 * POOL_PAD
    zeros = jnp.zeros((POOL_PAD, 256), f32)
    xpad[0:POOL_PAD, :] = zeros
    xpad[POOL_PAD:POOL_PAD + t, :] = x
    xpad[POOL_PAD + t:n, :] = zeros
    xp = xpad[...]
    s2 = xp + pltpu.roll(xp, 1, axis=0)
    s4 = pltpu.roll(s2, 1, axis=0) + pltpu.roll(s2, n - 1, axis=0)
    s8 = pltpu.roll(s4, 2, axis=0) + pltpu.roll(s4, n - 2, axis=0)
    s16 = pltpu.roll(s8, 4, axis=0) + pltpu.roll(s8, n - 4, axis=0)
    sums = [s[POOL_PAD:POOL_PAD + t] for s in (s2, s4, s8, s16)]
    group = lax.broadcasted_iota(jnp.int32, (t, 256), 1) // POOL_CH
    pos = lax.broadcasted_iota(jnp.int32, (t, 256), 0)
    win_sum = sums[0]
    half = jnp.full((t, 256), POOL_WINDOWS[0] // 2, jnp.int32)
    for gi in range(1, len(POOL_WINDOWS)):
        win_sum = jnp.where(group == gi, sums[gi], win_sum)
        half = jnp.where(group == gi, POOL_WINDOWS[gi] // 2, half)
    cnt = jnp.minimum(pos + half, t) - jnp.maximum(pos - half, 0)
    mean = win_sum / cnt.astype(f32)
    o_ref[...] = (_dot((mean - x).astype(bf16), w_ref[...]) * sc_ref[...]).astype(o_ref.dtype)


def _pool(pr, w_bd, scale, *, batch, t):
    return pl.pallas_call(
        functools.partial(_pool_kernel, t=t),
        out_shape=jax.ShapeDtypeStruct((batch * t, 256), bf16),
        grid=(batch,),
        in_specs=[pl.BlockSpec((t, 256), lambda b: (b, 4)), _const_spec((256, 256)), _const_spec((1, 256))],
        out_specs=pl.BlockSpec((t, 256), lambda b: (b, 0)),
        scratch_shapes=[pltpu.VMEM((t + 2 * POOL_PAD, 256), f32)],
        compiler_params=_params("parallel"),
        name="pool",
    )(pr, w_bd, scale)


def _hgrn_mats():
    n = HG_TILE
    s = np.arange(n)[:, None]
    t = np.arange(n)[None, :]
    same_half = s // HG_HALF == t // HG_HALF
    mid = (t // HG_HALF) * HG_HALF + HG_HALF // 2 - 1
    rel_mid = (same_half & (s > mid) & (s <= t)).astype(np.float32) - (same_half & (s <= mid) & (s > t))
    mats = [rel_mid, same_half & (s <= t), same_half & (s > t), s <= t, s > t,
            (s // HG_BLOCK == t // HG_BLOCK) & (s <= t)]
    for m in HG_LEVELS:
        same_pair = s // (2 * m) == t // (2 * m)
        right_t = (t // m) % 2 == 1
        right_s = (s // m) % 2 == 1
        q_side = same_pair & right_t & right_s & (s <= t)
        k_side = same_pair & ~right_t & ~right_s & (s > t)
        mats.append(q_side | k_side)
    fwd = np.concatenate([m.astype(np.float32) for m in mats], axis=1)
    bwd = np.concatenate([m[::-1, ::-1].astype(np.float32) for m in mats], axis=1)
    return np.stack([fwd, bwd])


def _hgrn_exact_intra(q, k, v, vb, hi, lo, mats, reverse):
    n = HG_TILE
    e_all = _dot(hi, mats) + _dot(lo, mats)
    beta = e_all[:, 0:n]
    lane = lax.broadcasted_iota(jnp.int32, (1, n), 1)
    pos = lane % HG_BLOCK
    v3 = v.reshape(H_B, DV_B, n)

    o3 = jnp.sum((q * k).reshape(H_B, DK_B, n), axis=1, keepdims=True) * v3
    for d in range(1, HG_BLOCK):
        shift = (n - d) if reverse else d
        valid = (pos <= HG_BLOCK - 1 - d) if reverse else (pos >= d)
        kr = pltpu.roll(k, shift, axis=1)
        br = pltpu.roll(beta, shift, axis=1)
        vr = pltpu.roll(v, shift, axis=1)
        e = jnp.exp(jnp.where(valid, beta - br, NEG_INF))
        sc = jnp.sum((q * kr * e).reshape(H_B, DK_B, n), axis=1, keepdims=True)
        o3 = o3 + sc * vr.reshape(H_B, DV_B, n)

    srow = lax.broadcasted_iota(jnp.int32, (n, n), 0)
    tcol = lax.broadcasted_iota(jnp.int32, (n, n), 1)
    scores = [jnp.zeros((n, n), f32) for _ in range(H_B)]
    for li, m in enumerate(HG_LEVELS):
        ex = jnp.exp(e_all[:, (li + 1) * n:(li + 2) * n])
        is_q = ((lane // m) % 2) == (0 if reverse else 1)
        qs = jnp.where(is_q, q * ex, 0.0).astype(bf16)
        ks = jnp.where(is_q, 0.0, k * ex).astype(bf16)
        pair = (srow // (2 * m)) == (tcol // (2 * m))
        for h in range(H_B):
            a = lax.dot_general(ks[DK_B * h:DK_B * (h + 1)], qs[DK_B * h:DK_B * (h + 1)], _TN,
                                preferred_element_type=f32)
            scores[h] = scores[h] + jnp.where(pair, a, 0.0)

    return [o3[h] + _dot(vb[DV_B * h:DV_B * (h + 1)], scores[h].astype(bf16)) for h in range(H_B)]


def _hgrn_common(q, v, fpre, lb, mats_ref, sidx, st_ref):
    n = HG_TILE
    f = lb + (1.0 - lb) * _sigmoid(fpre)
    k = 1.0 - f
    logf = jnp.log(f)
    hi = logf.astype(bf16)
    lo = (logf - hi.astype(f32)).astype(bf16)
    m5 = mats_ref[sidx, :, 0:5 * n]
    ex = _dot(hi, m5) + _dot(lo, m5)
    ones = jnp.ones((8, n), bf16)
    g_row = (lax.dot_general(ones, hi, _NT, preferred_element_type=f32)
             + lax.dot_general(ones, lo, _NT, preferred_element_type=f32))[0:1]
    d_mid = ex[:, 0:n]
    small = jnp.max(jnp.abs(d_mid)) <= HG_MAX_EXP
    q_in = (q * jnp.exp(ex[:, 3 * n:4 * n])).astype(bf16)
    k_out = (k * jnp.exp(ex[:, 4 * n:5 * n])).astype(bf16)
    vb = v.astype(bf16)
    o_state = []
    for h in range(H_B):
        rows = slice(DK_B * h, DK_B * (h + 1))
        st = st_ref[sidx, rows, :]
        o_state.append(_dot(st.astype(bf16), q_in[rows]))
        upd = lax.dot_general(vb[rows], k_out[rows], _NT, preferred_element_type=f32)
        st_ref[sidx, rows, :] = st * jnp.exp(g_row[:, rows]) + upd
    return dict(q=q, k=k, v=v, vb=vb, hi=hi, lo=lo, ex=ex, o_state=o_state, small=small)


def _hgrn_fast_intra(c, reverse):
    n = HG_TILE
    q, k, ex = c["q"], c["k"], c["ex"]
    lane = lax.broadcasted_iota(jnp.int32, (1, n), 1)
    in0 = (lane // HG_HALF) == 0
    q_half = in0 if reverse else jnp.logical_not(in0)
    d_mid = ex[:, 0:n]
    qd = q * jnp.exp(d_mid)
    kd = k * jnp.exp(-d_mid)
    ql = jnp.where(q_half, q * jnp.exp(ex[:, n:2 * n]), 0.0)
    kl = jnp.where(q_half, 0.0, k * jnp.exp(ex[:, 2 * n:3 * n]))
    parts_q = [jnp.where(in0, qd, 0.0).astype(bf16), jnp.where(in0, 0.0, qd).astype(bf16), ql.astype(bf16)]
    parts_k = [jnp.where(in0, kd, 0.0).astype(bf16), jnp.where(in0, 0.0, kd).astype(bf16), kl.astype(bf16)]
    srow = lax.broadcasted_iota(jnp.int32, (n, n), 0)
    tcol = lax.broadcasted_iota(jnp.int32, (n, n), 1)
    causal = (srow >= tcol) if reverse else (srow <= tcol)
    scores = []
    for h in range(H_B):
        rows = slice(DK_B * h, DK_B * (h + 1))
        k3 = jnp.concatenate([p[rows] for p in parts_k], axis=0)
        q3 = jnp.concatenate([p[rows] for p in parts_q], axis=0)
        scores.append(lax.dot_general(k3, q3, _TN, preferred_element_type=f32))
    scores = [jnp.where(causal, a, 0.0).astype(bf16) for a in scores]
    outs = [c["o_state"][h] + _dot(c["vb"][DV_B * h:DV_B * (h + 1)], scores[h]) for h in range(H_B)]
    return jnp.concatenate(outs, axis=0)


def _hgrn_exact(c, mats, reverse):
    intra = _hgrn_exact_intra(c["q"], c["k"], c["v"], c["vb"], c["hi"], c["lo"], mats, reverse)
    return jnp.concatenate([c["o_state"][h] + intra[h] for h in range(H_B)], axis=0)


def _hgrn_kernel(q_ref, i_ref, ff_ref, fb_ref, g_ref, lb_ref, ng_ref, mats_ref, s0_ref,
                 o_ref, sfin_ref, st_ref, oacc_f, oacc_b, *, t):
    n = HG_TILE
    nt = t // n
    st_ref[...] = s0_ref[0]

    def scan(i, c):
        cols_f = pl.ds(pl.multiple_of(i * n, n), n)
        cols_b = pl.ds(pl.multiple_of((nt - 1 - i) * n, n), n)
        cf = _hgrn_common(q_ref[:, cols_f], i_ref[:, cols_f], ff_ref[:, cols_f], lb_ref[0], mats_ref, 0, st_ref)
        cb = _hgrn_common(q_ref[:, cols_b], i_ref[:, cols_b], fb_ref[:, cols_b], lb_ref[1], mats_ref, 1, st_ref)
        small = jnp.logical_and(cf["small"], cb["small"])
        oacc_f[:, cols_f] = _hgrn_fast_intra(cf, False)
        oacc_b[:, cols_b] = _hgrn_fast_intra(cb, True)

        @pl.when(jnp.logical_not(small))
        def _():
            oacc_f[:, cols_f] = _hgrn_exact(cf, mats_ref[0, :, 5 * n:9 * n], False)
            oacc_b[:, cols_b] = _hgrn_exact(cb, mats_ref[1, :, 5 * n:9 * n], True)

        return c

    lax.fori_loop(0, nt, scan, 0)

    def finish(i, c):
        off = pl.multiple_of(i * n, n)
        cols = pl.ds(off, n)
        o = oacc_f[:, cols] + oacc_b[:, cols]
        o3 = o.reshape(H_B, DV_B, n)
        y = o3 * lax.rsqrt(jnp.mean(o3 * o3, axis=1, keepdims=True) + EPS)
        gate = g_ref[:, cols]
        y = y.reshape(H_B * DV_B, n) * ng_ref[...] * (gate * _sigmoid(gate))
        o_ref[pl.ds(off, n), :] = y.T.astype(o_ref.dtype)
        return c

    lax.fori_loop(0, nt, finish, 0)
    sfin_ref[0] = st_ref[...]


def _hgrn(pt, lb, ng, mats, s0, *, batch, t):
    def row(r):
        return pl.BlockSpec((256, t), lambda b: (r, b))
    sspec = pl.BlockSpec((1, 2, H_B * DV_B, DK_B), lambda b: (b, 0, 0, 0))
    return pl.pallas_call(
        functools.partial(_hgrn_kernel, t=t),
        out_shape=(jax.ShapeDtypeStruct((batch * t, 256), bf16),
                   jax.ShapeDtypeStruct((batch, 2, H_B * DV_B, DK_B), f32)),
        grid=(batch,),
        in_specs=[row(0), row(1), row(2), row(3), row(4),
                  _const_spec((2, 256, LANES)), _const_spec((256, LANES)), _const_spec(mats.shape), sspec],
        out_specs=(pl.BlockSpec((t, 256), lambda b: (b, 0)), sspec),
        scratch_shapes=[pltpu.VMEM((2, H_B * DV_B, DK_B), f32), pltpu.VMEM((256, t), f32),
                        pltpu.VMEM((256, t), f32)],
        compiler_params=_params("parallel"),
        name="hgrn",
    )(pt, pt, pt, pt, pt, lb, ng, mats, s0)


def _rope_tables(t):
    def cos_sin(dim):
        n_freq = dim // 4
        inv = ROPE_THETA ** (-jnp.arange(n_freq, dtype=f32) / n_freq)
        rows = t // GRID_W
        row_id = jnp.repeat(jnp.arange(rows, dtype=f32), GRID_W)
        col_id = jnp.tile(jnp.arange(GRID_W, dtype=f32), rows)
        ang = jnp.concatenate([row_id[:, None] * inv, col_id[:, None] * inv], axis=-1)
        return jnp.cos(ang), jnp.sin(ang)

    cos, sin = cos_sin(HD_A)
    zero = jnp.zeros_like(sin)
    tab_a = jnp.stack([jnp.tile(jnp.concatenate([cos, cos], -1), (1, H_A)),
                       jnp.tile(jnp.concatenate([zero, sin], -1), (1, H_A)),
                       jnp.tile(jnp.concatenate([-sin, zero], -1), (1, H_A))])
    cos, sin = cos_sin(ROPE_C)
    zero = jnp.zeros_like(sin)
    one64 = jnp.ones((t, NOPE_C), f32)
    zero64 = jnp.zeros((t, NOPE_C), f32)
    pad32 = jnp.zeros((t, LANES - NOPE_C - ROPE_C), f32)
    tab_c = jnp.stack([jnp.concatenate([one64, cos, cos, pad32 + 1.0], -1),
                       jnp.concatenate([zero64, zero, sin, pad32], -1),
                       jnp.concatenate([zero64, -sin, zero, pad32], -1)])
    return tab_a, tab_c


def _pad_rope_cols(x):
    pads = [(0, 0)] * (x.ndim - 1) + [(NOPE_C, LANES - NOPE_C - ROPE_C)]
    return jnp.pad(x, pads)


def kernel(x_prompt, x_sample, c, c_ctx, cache_attn_k, cache_attn_v, state_hgrn, cache_mla_ckv, cache_mla_krope,
           w_ada, b_ada, norm_sub, w_ffn_gate, w_ffn_up, w_ffn_down, w_in, w_out, attn_sink, hgrn_lb_logits,
           hgrn_out_norm, mla_q_norm, mla_kv_norm, mla_w_uq, mla_w_ukv, pool_w, pool_scale, final_norm):
    bp, tp, _ = x_prompt.shape
    bs, ts, _ = x_sample.shape
    np_, ns = bp * tp, bs * ts
    xp = x_prompt.reshape(np_, D_MODEL)
    xs = x_sample.reshape(ns, D_MODEL)

    n_vec = 1 + bs
    cvec = jnp.zeros((-(-n_vec // 8) * 8, D_MODEL), f32).at[0].set(c_ctx).at[1:n_vec].set(c)
    mods = _modulation(cvec, w_ada, b_ada).reshape(DEPTH, cvec.shape[0], N_MOD, D_MODEL)

    p_lb = jax.nn.softmax(hgrn_lb_logits.astype(f32), axis=0)
    lb_all = jnp.cumsum(p_lb, axis=0) - p_lb[0:1]
    tab_a, tab_c = _rope_tables(ts)
    mats = jnp.asarray(_hgrn_mats(), bf16)
    n_ffn = DEPTH * 2
    wg_all = _to_bf16(w_ffn_gate.reshape(n_ffn * D_MODEL, D_FF), 512).reshape(n_ffn, D_MODEL, D_FF)
    wu_all = _to_bf16(w_ffn_up.reshape(n_ffn * D_MODEL, D_FF), 512).reshape(n_ffn, D_MODEL, D_FF)
    wd_all = _to_bf16(w_ffn_down.reshape(n_ffn * D_FF, D_MODEL), D_FF // 2).reshape(n_ffn, D_FF, D_MODEL)
    grp_p = dict(base=0, rpm=np_)
    grp_s = dict(base=1, rpm=ts)
    tm = 512

    new_k, new_v, new_st, new_ckv, new_kr = [], [], [], [], []
    for l in range(DEPTH):
        m_l = mods[l]
        wb = lambda w: w.astype(bf16)
        wi = w_in[l]
        w_row = wb(jnp.concatenate([wi[:, 0:512], wi[:, 1792:2176], _pad_rope_cols(wi[:, 2176:2208]),
                                    wi[:, 2208:2464]], axis=1))
        w_t = wb(wi[:, 512:1792].T)
        wq = wb(jnp.pad(mla_w_uq[l].reshape(Q_LORA_C, H_C, NOPE_C + ROPE_C),
                        ((0, 0), (0, 0), (0, LANES - NOPE_C - ROPE_C))).reshape(Q_LORA_C, H_C * LANES))
        wkv = mla_w_ukv[l].reshape(KV_LORA_C, H_C, NOPE_C + V_C)
        wk = wb(jnp.pad(wkv[:, :, :NOPE_C], ((0, 0), (0, 0), (0, LANES - NOPE_C))).reshape(KV_LORA_C, H_C * LANES))
        wv = wb(wkv[:, :, NOPE_C:].reshape(KV_LORA_C, H_C * V_C))
        w_pool = wb(jax.scipy.linalg.block_diag(*[pool_w[l, gi] for gi in range(len(POOL_WINDOWS))]))
        lb = jnp.broadcast_to(lb_all[l][:, :, None], (2, 256, LANES))
        ng = jnp.broadcast_to(jnp.tile(hgrn_out_norm[l], H_B)[:, None], (256, LANES))
        sink = attn_sink[l].astype(f32)
        qn = mla_q_norm[l].reshape(1, Q_LORA_C)
        kvn = mla_kv_norm[l].reshape(1, KV_LORA_C)
        pscale = pool_scale[l].reshape(1, 256)
        wo = wb(w_out[l])

        def ffn(x, grp, j, idx, mix=None, final_g=None):
            return _ffn(x, m_l, norm_sub[l, j].reshape(1, D_MODEL), wg_all, wu_all, wd_all, widx=2 * l + idx,
                        j=j, tm=tm, mix=mix, final_g=final_g, **grp)

        xp = ffn(xp, grp_p, 0, 0)
        xs = ffn(xs, grp_s, 0, 0)

        g1 = norm_sub[l, 1].reshape(1, D_MODEL)
        pr, pt = _mixer_in(xp, m_l, g1, w_row, w_t, tm=tm, **grp_p)
        oa, ka, va = _attn_ctx(pr, sink, batch=bp, t=tp)
        s0 = jnp.zeros((bp, 2, H_B * DV_B, DK_B), f32)
        ob, st = _hgrn(pt, lb, ng, mats, s0, batch=bp, t=tp)
        oc, ckv = _mla(pr, qn, kvn, wq, wk, wv, batch=bp, t=tp, tq=tp)
        od = _pool(pr, w_pool, pscale, batch=bp, t=tp)
        mix_p = (oa, ob, oc, od, wo)
        new_k.append(ka)
        new_v.append(va)
        new_st.append(jnp.swapaxes(st.reshape(bp, 2, H_B, DV_B, DK_B), -1, -2))
        new_ckv.append(ckv.reshape(bp, tp, KV_LORA_C))
        new_kr.append(pr[:, 896 + NOPE_C:896 + NOPE_C + ROPE_C].reshape(bp, tp, ROPE_C))

        pr, pt = _mixer_in(xs, m_l, g1, w_row, w_t, tm=tm, **grp_s)
        oa = _attn_lat(pr, sink, cache_attn_k[:, l], cache_attn_v[:, l], tab_a, batch=bs, t=ts)
        s0 = jnp.swapaxes(state_hgrn[:, l], -1, -2).reshape(bs, 2, H_B * DV_B, DK_B)
        ob, _ = _hgrn(pt, lb, ng, mats, s0, batch=bs, t=ts)
        oc, _ = _mla(pr, qn, kvn, wq, wk, wv, batch=bs, t=ts, tq=256,
                     ctx=(cache_mla_ckv[:, l], _pad_rope_cols(cache_mla_krope[:, l])), tab=tab_c)
        od = _pool(pr, w_pool, pscale, batch=bs, t=ts)
        mix_s = (oa, ob, oc, od, wo)

        fin = final_norm.reshape(1, D_MODEL) if l == DEPTH - 1 else None
        xp = ffn(xp, grp_p, 2, 1, mix_p, fin)
        xs = ffn(xs, grp_s, 2, 1, mix_s, fin)

    return (xp.reshape(bp, tp, D_MODEL), xs.reshape(bs, ts, D_MODEL),
            jnp.stack(new_k, axis=1), jnp.stack(new_v, axis=1), jnp.stack(new_st, axis=1),
            jnp.stack(new_ckv, axis=1), jnp.stack(new_kr, axis=1))
```

```python
import functools

import numpy as np
import jax
import jax.numpy as jnp
from jax import lax
from jax.experimental import pallas as pl
from jax.experimental.pallas import tpu as pltpu

f32 = jnp.float32
bf16 = jnp.bfloat16

D_MODEL = 1024
DEPTH = 2
GRID_W = 64
EPS = 1e-6
LOG2E = 1.4426950408889634
ROPE_THETA = 10000.0
NEG_INF = -1e30
N_MOD = 9
D_FF = 2816
H_A, KV_A, HD_A = 4, 2, 64
G_A = H_A // KV_A
WINDOW_A = 128
BLOCK_A = 128
H_B, DK_B, DV_B = 4, 64, 64
H_C, Q_LORA_C, KV_LORA_C, NOPE_C, ROPE_C, V_C = 4, 256, 128, 64, 32, 64
POOL_WINDOWS = (2, 4, 8, 16)
POOL_CH = 64

LANES = 128
SUBLANES = 8
ROW_REDUCE_CHAINS = 8
VMEM_LIMIT = 56 * 1024 * 1024
HG_TILE = LANES
HG_UNROLL = 2
HG_HALF = HG_TILE // 2
HG_MAX_EXP = 80.0
HG_BLOCK = 16
HG_LEVELS = (16, 32, 64)
ATTN_LAT_BLOCKS = 4
POOL_PAD = 16

PR_W = 1280
PT_W = 1280

_NT = (((1,), (1,)), ((), ()))
_TN = (((0,), (0,)), ((), ()))


def _params(*sem):
    return pltpu.CompilerParams(dimension_semantics=sem, vmem_limit_bytes=VMEM_LIMIT)


def _const_spec(shape):
    zeros = (0,) * len(shape)
    return pl.BlockSpec(shape, lambda *_: zeros, pipeline_mode=pl.Buffered(1))


def _dot(a, b):
    return jnp.dot(a, b, preferred_element_type=f32)


def _sigmoid(x):
    return 1.0 / (1.0 + jnp.exp(-x))


def _rms(x, g):
    return x * lax.rsqrt(jnp.mean(x * x, axis=-1, keepdims=True) + EPS) * g


def _mod_kernel(c_ref, w_ref, b_ref, o_ref):
    c = c_ref[...]
    a = (c * _sigmoid(c)).astype(bf16)
    o_ref[0] = _dot(a, w_ref[0].astype(bf16)) + b_ref[0]


def _modulation(cvec, w_ada, b_ada):
    tn = 1536
    n_out = N_MOD * D_MODEL
    return pl.pallas_call(
        _mod_kernel,
        out_shape=jax.ShapeDtypeStruct((DEPTH, cvec.shape[0], n_out), f32),
        grid=(DEPTH, n_out // tn),
        in_specs=[pl.BlockSpec(cvec.shape, lambda l, j: (0, 0)),
                  pl.BlockSpec((1, D_MODEL, tn), lambda l, j: (l, 0, j)),
                  pl.BlockSpec((1, 1, tn), lambda l, j: (l, 0, j))],
        out_specs=pl.BlockSpec((1, cvec.shape[0], tn), lambda l, j: (l, 0, j)),
        compiler_params=_params("arbitrary", "arbitrary"),
        name="modulation",
    )(cvec, w_ada, b_ada.reshape(DEPTH, 1, n_out))


def _cast_kernel(x_ref, o_ref):
    o_ref[...] = x_ref[...].astype(bf16)


def _to_bf16(x, rows):
    r, c = x.shape
    return pl.pallas_call(
        _cast_kernel,
        out_shape=jax.ShapeDtypeStruct((r, c), bf16),
        grid=(r // rows,),
        in_specs=[pl.BlockSpec((rows, c), lambda i: (i, 0))],
        out_specs=pl.BlockSpec((rows, c), lambda i: (i, 0)),
        compiler_params=_params("parallel"),
        name="cast_bf16",
    )(x)


def _ffn_kernel(*refs, j, mix, final):
    it = iter(refs)
    x_ref, m_ref = next(it), next(it)
    mix_refs = [next(it) for _ in range(5)] if mix else None
    g_ref, wg_ref, wu_ref, wd_ref = next(it), next(it), next(it), next(it)
    fg_ref = next(it) if final else None
    o_ref = next(it)
    x = x_ref[...]
    m = m_ref[0]
    if mix:
        wo_ref = mix_refs[4]
        acc = _dot(mix_refs[0][...], wo_ref[0:256, :])
        for p in range(1, 4):
            acc += _dot(mix_refs[p][...], wo_ref[256 * p:256 * (p + 1), :])
        x = x + m[5:6] * acc
    h = (_rms(x, g_ref[...]) * (1.0 + m[3 * j + 1:3 * j + 2]) + m[3 * j:3 * j + 1]).astype(bf16)
    gate = _dot(h, wg_ref[...])
    up = _dot(h, wu_ref[...])
    act = (gate * _sigmoid(gate) * up).astype(bf16)
    y = x + (0.5 * m[3 * j + 2:3 * j + 3]) * _dot(act, wd_ref[...])
    if final:
        y = _rms(y, fg_ref[...])
    o_ref[...] = y


def _mod_spec(base, rpm, tm):
    return pl.BlockSpec((1, N_MOD, D_MODEL), lambda i: (base + (i * tm) // rpm, 0, 0))


def _ffn(x, mods, g, wg, wu, wd, *, widx, j, base, rpm, tm, mix=None, final_g=None):
    n = x.shape[0]

    def wspec(r, c):
        return pl.BlockSpec((None, r, c), lambda i: (widx, 0, 0), pipeline_mode=pl.Buffered(1))

    in_specs = [pl.BlockSpec((tm, D_MODEL), lambda i: (i, 0)), _mod_spec(base, rpm, tm)]
    args = [x, mods]
    if mix is not None:
        in_specs += [pl.BlockSpec((tm, 256), lambda i: (i, 0))] * 4 + [_const_spec((D_MODEL, D_MODEL))]
        args += list(mix)
    in_specs += [_const_spec((1, D_MODEL)), wspec(D_MODEL, D_FF), wspec(D_MODEL, D_FF), wspec(D_FF, D_MODEL)]
    args += [g, wg, wu, wd]
    if final_g is not None:
        in_specs.append(_const_spec((1, D_MODEL)))
        args.append(final_g)
    return pl.pallas_call(
        functools.partial(_ffn_kernel, j=j, mix=mix is not None, final=final_g is not None),
        out_shape=jax.ShapeDtypeStruct((n, D_MODEL), f32),
        grid=(n // tm,),
        in_specs=in_specs,
        out_specs=pl.BlockSpec((tm, D_MODEL), lambda i: (i, 0)),
        compiler_params=_params("parallel"),
        name="ffn",
    )(*args)


def _mixin_kernel(x_ref, m_ref, g_ref, wr_ref, wt_ref, pr_ref, pt_ref):
    m = m_ref[0]
    h = (_rms(x_ref[...], g_ref[...]) * (1.0 + m[4:5]) + m[3:4]).astype(bf16)
    pr_ref[...] = _dot(h, wr_ref[...])
    pt_ref[...] = lax.dot_general(wt_ref[...], h, _NT, preferred_element_type=f32)


def _mixer_in(x, mods, g, w_row, w_t, *, base, rpm, tm):
    n = x.shape[0]
    return pl.pallas_call(
        _mixin_kernel,
        out_shape=(jax.ShapeDtypeStruct((n, PR_W), f32), jax.ShapeDtypeStruct((PT_W, n), f32)),
        grid=(n // tm,),
        in_specs=[pl.BlockSpec((tm, D_MODEL), lambda i: (i, 0)), _mod_spec(base, rpm, tm),
                  _const_spec((1, D_MODEL)), _const_spec((D_MODEL, PR_W)), _const_spec((PT_W, D_MODEL))],
        out_specs=(pl.BlockSpec((tm, PR_W), lambda i: (i, 0)), pl.BlockSpec((PT_W, tm), lambda i: (0, i))),
        compiler_params=_params("parallel"),
        name="mixer_in",
    )(x, mods, g, w_row, w_t)


def _rope(x, tab_ref, rows, lanes, shift):
    width = x.shape[-1]
    c = tab_ref[0, rows, lanes]
    sa = tab_ref[1, rows, lanes]
    sb = tab_ref[2, rows, lanes]
    return x * c + pltpu.roll(x, shift, axis=1) * sa + pltpu.roll(x, width - shift, axis=1) * sb


def _softmax_parts(s_list, scale, sink=None):
    mx = functools.reduce(jnp.maximum, [jnp.max(s, axis=-1, keepdims=True) for s in s_list]) * scale
    if sink is not None:
        mx = jnp.maximum(mx, sink)
    off = mx * LOG2E
    ps = [jnp.exp2(s * (scale * LOG2E) - off) for s in s_list]
    den = functools.reduce(lambda a, b: a + b, [jnp.sum(p, axis=-1, keepdims=True) for p in ps])
    if sink is not None:
        den = den + jnp.exp(sink - mx)
    return ps, 1.0 / den


def _kv_per_query_head(x):
    return jnp.concatenate([x[:, HD_A * (h // G_A):HD_A * (h // G_A + 1)] for h in range(H_A)], axis=1)


def _stack_heads(q):
    head = lax.broadcasted_iota(jnp.int32, (1, H_A * HD_A), 1) // HD_A
    return jnp.concatenate([jnp.where(head == h, q, 0.0) for h in range(H_A)], axis=0)


def _pick_heads(o, r):
    head = lax.broadcasted_iota(jnp.int32, (1, H_A * HD_A), 1) // HD_A
    out = o[0:r]
    for h in range(1, H_A):
        out = jnp.where(head == h, o[h * r:(h + 1) * r], out)
    return out


def _sink_rows(sink_ref, r):
    row = lax.broadcasted_iota(jnp.int32, (H_A * r, 1), 0)
    out = jnp.full((H_A * r, 1), sink_ref[0], f32)
    for h in range(1, H_A):
        out = jnp.where(row >= h * r, sink_ref[h], out)
    return out


def _attn_ctx_kernel(sink_ref, q_ref, kv_ref, o_ref, ko_ref, vo_ref, *, t):
    kv = kv_ref[...]
    kvw = KV_A * HD_A
    for j in range(KV_A):
        ko_ref[0, j] = kv[:, HD_A * j:HD_A * (j + 1)]
        vo_ref[0, j] = kv[:, kvw + HD_A * j:kvw + HD_A * (j + 1)]
    kx = _kv_per_query_head(kv[:, 0:kvw]).astype(bf16)
    vx = _kv_per_query_head(kv[:, kvw:2 * kvw]).astype(bf16)
    q4 = _stack_heads(q_ref[...]).astype(bf16)
    s = lax.dot_general(q4, kx, _NT, preferred_element_type=f32)
    (p,), inv = _softmax_parts([s], HD_A ** -0.5, _sink_rows(sink_ref, t))
    o_ref[...] = _pick_heads(_dot(p.astype(bf16), vx) * inv, t).astype(o_ref.dtype)


def _attn_ctx(pr, sink, *, batch, t):
    kvshape = jax.ShapeDtypeStruct((batch, KV_A, t, HD_A), f32)
    kvspec = pl.BlockSpec((1, KV_A, t, HD_A), lambda b: (b, 0, 0, 0))
    return pl.pallas_call(
        functools.partial(_attn_ctx_kernel, t=t),
        out_shape=(jax.ShapeDtypeStruct((batch * t, 256), bf16), kvshape, kvshape),
        grid=(batch,),
        in_specs=[pl.BlockSpec(memory_space=pltpu.SMEM),
                  pl.BlockSpec((t, 256), lambda b: (b, 0)), pl.BlockSpec((t, 256), lambda b: (b, 1))],
        out_specs=(pl.BlockSpec((t, 256), lambda b: (b, 0)), kvspec, kvspec),
        compiler_params=_params("parallel"),
        name="attn_ctx",
    )(sink, pr, pr)


def _attn_lat_kernel(sink_ref, q_ref, kv_ref, kc_ref, vc_ref, tab_ref, o_ref, kpad, vpad, kctx, vctx, *, t):
    n = pl.program_id(1)
    kvw = KV_A * HD_A
    hw = H_A * HD_A

    @pl.when(n == 0)
    def _():
        zeros = jnp.zeros((BLOCK_A, hw), bf16)
        kv = kv_ref[...]
        k = _rope(kv[:, 0:kvw], tab_ref, slice(None), slice(0, kvw), HD_A // 2)
        kpad[0:BLOCK_A, :] = zeros
        vpad[0:BLOCK_A, :] = zeros
        kpad[BLOCK_A:BLOCK_A + t, :] = _kv_per_query_head(k).astype(bf16)
        vpad[BLOCK_A:BLOCK_A + t, :] = _kv_per_query_head(kv[:, kvw:2 * kvw]).astype(bf16)
        kpad[BLOCK_A + t:2 * BLOCK_A + t, :] = zeros
        vpad[BLOCK_A + t:2 * BLOCK_A + t, :] = zeros
        kctx[...] = jnp.concatenate([kc_ref[0, h // G_A] for h in range(H_A)], axis=1).astype(bf16)
        vctx[...] = jnp.concatenate([vc_ref[0, h // G_A] for h in range(H_A)], axis=1).astype(bf16)

    rows = H_A * BLOCK_A
    rr = lax.broadcasted_iota(jnp.int32, (rows, 1), 0) % BLOCK_A
    cidx = lax.broadcasted_iota(jnp.int32, (rows, 3 * BLOCK_A), 1)
    sink = _sink_rows(sink_ref, BLOCK_A)
    for u in range(ATTN_LAT_BLOCKS):
        blk = n * ATTN_LAT_BLOCKS + u
        row0 = pl.multiple_of(blk * BLOCK_A, BLOCK_A)
        q = _rope(q_ref[u * BLOCK_A:(u + 1) * BLOCK_A, :], tab_ref, pl.ds(row0, BLOCK_A), slice(None), HD_A // 2)
        q4 = _stack_heads(q).astype(bf16)
        kband = kpad[pl.ds(row0, 3 * BLOCK_A), :]
        vband = vpad[pl.ds(row0, 3 * BLOCK_A), :]
        s_loc = lax.dot_general(q4, kband, _NT, preferred_element_type=f32)
        s_ctx = lax.dot_general(q4, kctx[...], _NT, preferred_element_type=f32)
        lo = jnp.maximum(rr + (BLOCK_A - WINDOW_A), jnp.maximum(0, (1 - blk) * BLOCK_A))
        hi = jnp.minimum(rr + (BLOCK_A + WINDOW_A), jnp.minimum(3 * BLOCK_A, t - (blk - 1) * BLOCK_A) - 1)
        s_loc = jnp.where((cidx >= lo) & (cidx <= hi), s_loc, NEG_INF)
        (p_loc, p_ctx), inv = _softmax_parts([s_loc, s_ctx], HD_A ** -0.5, sink)
        o = (_dot(p_loc.astype(bf16), vband) + _dot(p_ctx.astype(bf16), vctx[...])) * inv
        o_ref[u * BLOCK_A:(u + 1) * BLOCK_A, :] = _pick_heads(o, BLOCK_A).astype(o_ref.dtype)


def _attn_lat(pr, sink, kc, vc, tab, *, batch, t):
    rows = ATTN_LAT_BLOCKS * BLOCK_A
    nb = t // rows
    lc = kc.shape[2]
    cspec = pl.BlockSpec((1, KV_A, lc, HD_A), lambda b, n: (b, 0, 0, 0))
    return pl.pallas_call(
        functools.partial(_attn_lat_kernel, t=t),
        out_shape=jax.ShapeDtypeStruct((batch * t, 256), bf16),
        grid=(batch, nb),
        in_specs=[pl.BlockSpec(memory_space=pltpu.SMEM),
                  pl.BlockSpec((rows, 256), lambda b, n: (b * nb + n, 0)),
                  pl.BlockSpec((t, 256), lambda b, n: (b, 1)),
                  cspec, cspec,
                  pl.BlockSpec((3, t, 256), lambda b, n: (0, 0, 0))],
        out_specs=pl.BlockSpec((rows, 256), lambda b, n: (b * nb + n, 0)),
        scratch_shapes=[pltpu.VMEM((t + 2 * BLOCK_A, H_A * HD_A), bf16),
                        pltpu.VMEM((t + 2 * BLOCK_A, H_A * HD_A), bf16),
                        pltpu.VMEM((lc, H_A * HD_A), bf16), pltpu.VMEM((lc, H_A * HD_A), bf16)],
        compiler_params=_params("parallel", "arbitrary"),
        name="attn_lat",
    )(sink, pr, pr, kc, vc, tab)


def _reduce_rows(x, reduce_fn):
    r, c = x.shape
    g = ROW_REDUCE_CHAINS
    if r % (g * SUBLANES):
        return reduce_fn(x, axis=0, keepdims=True)
    part = reduce_fn(x.reshape(g, r // (g * SUBLANES), SUBLANES, c), axis=1)
    return reduce_fn(reduce_fn(part, axis=0), axis=0, keepdims=True)


def _mla_kernel(*refs, t, lc, tq, rope):
    it = iter(refs)
    cq_ref, ckv_ref, kr_ref = next(it), next(it), next(it)
    cc_ref = kc_ref = tab_ref = None
    if lc:
        cc_ref, kc_ref = next(it), next(it)
    if rope:
        tab_ref = next(it)
    qn_ref, kvn_ref, wq_ref, wk_ref, wv_ref = next(it), next(it), next(it), next(it), next(it)
    o_ref, ckv_out_ref, kbuf, vbuf = next(it), next(it), next(it), next(it)
    i = pl.program_id(1)
    hw = LANES

    @pl.when(i == 0)
    def _():
        ckv = _rms(ckv_ref[...], kvn_ref[...])
        ckv_out_ref[...] = ckv
        ckv_b = ckv.astype(bf16)
        kr = kr_ref[...]
        if rope:
            kr = _rope(kr, tab_ref, slice(None), slice(None), ROPE_C // 2)
        kk = _dot(ckv_b, wk_ref[...])
        for h in range(H_C):
            kbuf[h, 0:t, :] = (kk[:, hw * h:hw * (h + 1)] + kr).astype(bf16)
        vbuf[:, 0:t] = lax.dot_general(wv_ref[...], ckv_b, _NT, preferred_element_type=f32).astype(bf16)
        if lc:
            cc_b = cc_ref[0].astype(bf16)
            kkc = _dot(cc_b, wk_ref[...])
            krc = kc_ref[0]
            for h in range(H_C):
                kbuf[h, t:t + lc, :] = (kkc[:, hw * h:hw * (h + 1)] + krc).astype(bf16)
            vbuf[:, t:t + lc] = lax.dot_general(wv_ref[...], cc_b, _NT, preferred_element_type=f32).astype(bf16)

    cqn = _rms(cq_ref[...], qn_ref[...]).astype(bf16)
    qf = _dot(cqn, wq_ref[...])
    c2 = (NOPE_C + ROPE_C) ** -0.5 * LOG2E
    row0 = pl.multiple_of(i * tq, tq)
    n_keys = t + lc
    half = n_keys // 2

    def scores(h):
        qh = qf[:, hw * h:hw * (h + 1)]
        if rope:
            qh = _rope(qh, tab_ref, pl.ds(row0, tq), slice(None), ROPE_C // 2)
        qb = (qh * c2).astype(bf16)
        return [lax.dot_general(kbuf[h, c * half:(c + 1) * half, :], qb, _NT, preferred_element_type=f32)
                for c in range(2)]

    def probs(s):
        m = jnp.maximum(_reduce_rows(s[0], jnp.max), _reduce_rows(s[1], jnp.max))
        p = [jnp.exp2(x - m) for x in s]
        inv = 1.0 / (_reduce_rows(p[0], jnp.sum) + _reduce_rows(p[1], jnp.sum))
        return [x.astype(bf16) for x in p], inv

    def values(h, p, inv):
        rows = slice(V_C * h, V_C * (h + 1))
        return (_dot(vbuf[rows, 0:half], p[0]) + _dot(vbuf[rows, half:2 * half], p[1])) * inv

    s_next = scores(0)
    outs = []
    for h in range(H_C):
        s_cur = s_next
        if h + 1 < H_C:
            s_next = scores(h + 1)
        p, inv = probs(s_cur)
        outs.append(values(h, p, inv))
    o_ref[...] = jnp.concatenate(outs, axis=0).T.astype(o_ref.dtype)


def _mla(pr, qn, kvn, wq, wk, wv, *, batch, t, tq, ctx=None, tab=None):
    nq = t // tq
    lc = 0 if ctx is None else ctx[0].shape[1]
    in_specs = [pl.BlockSpec((tq, 256), lambda b, i: (b * nq + i, 2)),
                pl.BlockSpec((t, LANES), lambda b, i: (b, 6)),
                pl.BlockSpec((t, LANES), lambda b, i: (b, 7))]
    args = [pr, pr, pr]
    if lc:
        in_specs += [pl.BlockSpec((1, lc, LANES), lambda b, i: (b, 0, 0))] * 2
        args += list(ctx)
    if tab is not None:
        in_specs.append(pl.BlockSpec((3, t, LANES), lambda b, i: (0, 0, 0)))
        args.append(tab)
    in_specs += [_const_spec((1, Q_LORA_C)), _const_spec((1, KV_LORA_C)), _const_spec(wq.shape),
                 _const_spec(wk.shape), _const_spec(wv.shape)]
    args += [qn, kvn, wq, wk, wv]
    return pl.pallas_call(
        functools.partial(_mla_kernel, t=t, lc=lc, tq=tq, rope=tab is not None),
        out_shape=(jax.ShapeDtypeStruct((batch * t, 256), bf16), jax.ShapeDtypeStruct((batch * t, LANES), f32)),
        grid=(batch, nq),
        in_specs=in_specs,
        out_specs=(pl.BlockSpec((tq, 256), lambda b, i: (b * nq + i, 0)),
                   pl.BlockSpec((t, LANES), lambda b, i: (b, 0))),
        scratch_shapes=[pltpu.VMEM((H_C, t + lc, LANES), bf16), pltpu.VMEM((H_C * V_C, t + lc), bf16)],
        compiler_params=_params("parallel", "arbitrary"),
        name="mla",
    )(*args)


def _pool_kernel(x_ref, w_ref, sc_ref, o_ref, xpad, *, t):
    x = x_ref[...]
    n = t + 2 * POOL_PAD
    zeros = jnp.zeros((POOL_PAD, 256), f32)
    xpad[0:POOL_PAD, :] = zeros
    xpad[POOL_PAD:POOL_PAD + t, :] = x
    xpad[POOL_PAD + t:n, :] = zeros
    xp = xpad[...]
    s2 = xp + pltpu.roll(xp, 1, axis=0)
    s4 = pltpu.roll(s2, 1, axis=0) + pltpu.roll(s2, n - 1, axis=0)
    s8 = pltpu.roll(s4, 2, axis=0) + pltpu.roll(s4, n - 2, axis=0)
    s16 = pltpu.roll(s8, 4, axis=0) + pltpu.roll(s8, n - 4, axis=0)
    sums = [s[POOL_PAD:POOL_PAD + t] for s in (s2, s4, s8, s16)]
    group = lax.broadcasted_iota(jnp.int32, (t, 256), 1) // POOL_CH
    pos = lax.broadcasted_iota(jnp.int32, (t, 256), 0)
    win_sum = sums[0]
    half = jnp.full((t, 256), POOL_WINDOWS[0] // 2, jnp.int32)
    for gi in range(1, len(POOL_WINDOWS)):
        win_sum = jnp.where(group == gi, sums[gi], win_sum)
        half = jnp.where(group == gi, POOL_WINDOWS[gi] // 2, half)
    cnt = jnp.minimum(pos + half, t) - jnp.maximum(pos - half, 0)
    mean = win_sum / cnt.astype(f32)
    o_ref[...] = (_dot((mean - x).astype(bf16), w_ref[...]) * sc_ref[...]).astype(o_ref.dtype)


def _pool(pr, w_bd, scale, *, batch, t):
    return pl.pallas_call(
        functools.partial(_pool_kernel, t=t),
        out_shape=jax.ShapeDtypeStruct((batch * t, 256), bf16),
        grid=(batch,),
        in_specs=[pl.BlockSpec((t, 256), lambda b: (b, 4)), _const_spec((256, 256)), _const_spec((1, 256))],
        out_specs=pl.BlockSpec((t, 256), lambda b: (b, 0)),
        scratch_shapes=[pltpu.VMEM((t + 2 * POOL_PAD, 256), f32)],
        compiler_params=_params("parallel"),
        name="pool",
    )(pr, w_bd, scale)


def _hgrn_mats():
    n = HG_TILE
    s = np.arange(n)[:, None]
    t = np.arange(n)[None, :]
    same_half = s // HG_HALF == t // HG_HALF
    mid = (t // HG_HALF) * HG_HALF + HG_HALF // 2 - 1
    rel_mid = (same_half & (s > mid) & (s <= t)).astype(np.float32) - (same_half & (s <= mid) & (s > t))
    mats = [rel_mid, same_half & (s <= t), same_half & (s > t), s <= t, s > t,
            (s // HG_BLOCK == t // HG_BLOCK) & (s <= t)]
    for m in HG_LEVELS:
        same_pair = s // (2 * m) == t // (2 * m)
        right_t = (t // m) % 2 == 1
        right_s = (s // m) % 2 == 1
        q_side = same_pair & right_t & right_s & (s <= t)
        k_side = same_pair & ~right_t & ~right_s & (s > t)
        mats.append(q_side | k_side)
    fwd = np.concatenate([m.astype(np.float32) for m in mats], axis=1)
    bwd = np.concatenate([m[::-1, ::-1].astype(np.float32) for m in mats], axis=1)
    both = np.stack([fwd, bwd])
    return np.concatenate([both, both], axis=1)


def _hgrn_exact_intra(q, k, v, vb, hl, mats, reverse):
    n = HG_TILE
    e_all = _dot(hl, mats)
    beta = e_all[:, 0:n]
    lane = lax.broadcasted_iota(jnp.int32, (1, n), 1)
    pos = lane % HG_BLOCK
    v3 = v.reshape(H_B, DV_B, n)

    o3 = jnp.sum((q * k).reshape(H_B, DK_B, n), axis=1, keepdims=True) * v3
    for d in range(1, HG_BLOCK):
        shift = (n - d) if reverse else d
        valid = (pos <= HG_BLOCK - 1 - d) if reverse else (pos >= d)
        kr = pltpu.roll(k, shift, axis=1)
        br = pltpu.roll(beta, shift, axis=1)
        vr = pltpu.roll(v, shift, axis=1)
        e = jnp.exp(jnp.where(valid, beta - br, NEG_INF))
        sc = jnp.sum((q * kr * e).reshape(H_B, DK_B, n), axis=1, keepdims=True)
        o3 = o3 + sc * vr.reshape(H_B, DV_B, n)

    srow = lax.broadcasted_iota(jnp.int32, (n, n), 0)
    tcol = lax.broadcasted_iota(jnp.int32, (n, n), 1)
    scores = [jnp.zeros((n, n), f32) for _ in range(H_B)]
    for li, m in enumerate(HG_LEVELS):
        ex = jnp.exp(e_all[:, (li + 1) * n:(li + 2) * n])
        is_q = ((lane // m) % 2) == (0 if reverse else 1)
        qs = jnp.where(is_q, q * ex, 0.0).astype(bf16)
        ks = jnp.where(is_q, 0.0, k * ex).astype(bf16)
        pair = (srow // (2 * m)) == (tcol // (2 * m))
        for h in range(H_B):
            a = lax.dot_general(ks[DK_B * h:DK_B * (h + 1)], qs[DK_B * h:DK_B * (h + 1)], _TN,
                                preferred_element_type=f32)
            scores[h] = scores[h] + jnp.where(pair, a, 0.0)

    return [o3[h] + _dot(vb[DV_B * h:DV_B * (h + 1)], scores[h].astype(bf16)) for h in range(H_B)]


def _hgrn_common(q, v, fpre, lb, mats_ref, sidx, st_ref):
    n = HG_TILE
    f = lb + (1.0 - lb) * _sigmoid(fpre)
    k = 1.0 - f
    logf = jnp.log(f)
    hi = logf.astype(bf16)
    hl = jnp.concatenate([hi, (logf - hi.astype(f32)).astype(bf16)], axis=1)
    ex = _dot(hl, mats_ref[sidx, :, 0:5 * n])
    ones = jnp.ones((8, 2 * n), bf16)
    g_row = lax.dot_general(ones, hl, _NT, preferred_element_type=f32)[0:1]
    d_mid = ex[:, 0:n]
    small = jnp.max(jnp.abs(d_mid)) <= HG_MAX_EXP
    q_in = (q * jnp.exp(ex[:, 3 * n:4 * n])).astype(bf16)
    k_out = (k * jnp.exp(ex[:, 4 * n:5 * n])).astype(bf16)
    vb = v.astype(bf16)
    o_state = []
    for h in range(H_B):
        rows = slice(DK_B * h, DK_B * (h + 1))
        st = st_ref[sidx, rows, :]
        o_state.append(_dot(st.astype(bf16), q_in[rows]))
        upd = lax.dot_general(vb[rows], k_out[rows], _NT, preferred_element_type=f32)
        st_ref[sidx, rows, :] = st * jnp.exp(g_row[:, rows]) + upd
    return dict(q=q, k=k, v=v, vb=vb, hl=hl, ex=ex, o_state=o_state, small=small)


def _hgrn_fast_intra(c, reverse):
    n = HG_TILE
    q, k, ex = c["q"], c["k"], c["ex"]
    lane = lax.broadcasted_iota(jnp.int32, (1, n), 1)
    in0 = (lane // HG_HALF) == 0
    q_half = in0 if reverse else jnp.logical_not(in0)
    d_mid = ex[:, 0:n]
    qd = q * jnp.exp(d_mid)
    kd = k * jnp.exp(-d_mid)
    ql = jnp.where(q_half, q * jnp.exp(ex[:, n:2 * n]), 0.0)
    kl = jnp.where(q_half, 0.0, k * jnp.exp(ex[:, 2 * n:3 * n]))
    parts_q = [jnp.where(in0, qd, 0.0).astype(bf16), jnp.where(in0, 0.0, qd).astype(bf16), ql.astype(bf16)]
    parts_k = [jnp.where(in0, kd, 0.0).astype(bf16), jnp.where(in0, 0.0, kd).astype(bf16), kl.astype(bf16)]
    srow = lax.broadcasted_iota(jnp.int32, (n, n), 0)
    tcol = lax.broadcasted_iota(jnp.int32, (n, n), 1)
    causal = (srow >= tcol) if reverse else (srow <= tcol)
    scores = []
    for h in range(H_B):
        rows = slice(DK_B * h, DK_B * (h + 1))
        k3 = jnp.concatenate([p[rows] for p in parts_k], axis=0)
        q3 = jnp.concatenate([p[rows] for p in parts_q], axis=0)
        scores.append(lax.dot_general(k3, q3, _TN, preferred_element_type=f32))
    scores = [jnp.where(causal, a, 0.0).astype(bf16) for a in scores]
    outs = [c["o_state"][h] + _dot(c["vb"][DV_B * h:DV_B * (h + 1)], scores[h]) for h in range(H_B)]
    return jnp.concatenate(outs, axis=0)


def _hgrn_exact(c, mats, reverse):
    intra = _hgrn_exact_intra(c["q"], c["k"], c["v"], c["vb"], c["hl"], mats, reverse)
    return jnp.concatenate([c["o_state"][h] + intra[h] for h in range(H_B)], axis=0)


def _hgrn_kernel(q_ref, i_ref, ff_ref, fb_ref, g_ref, lb_ref, ng_ref, mats_ref, s0_ref,
                 o_ref, sfin_ref, st_ref, oacc_f, oacc_b, *, t):
    n = HG_TILE
    nt = t // n
    st_ref[...] = s0_ref[0]

    def scan(i, c):
        work = []
        small = None
        for u in range(HG_UNROLL):
            j = i * HG_UNROLL + u
            cols_f = pl.ds(pl.multiple_of(j * n, n), n)
            cols_b = pl.ds(pl.multiple_of((nt - 1 - j) * n, n), n)
            cf = _hgrn_common(q_ref[:, cols_f], i_ref[:, cols_f], ff_ref[:, cols_f], lb_ref[0], mats_ref, 0, st_ref)
            cb = _hgrn_common(q_ref[:, cols_b], i_ref[:, cols_b], fb_ref[:, cols_b], lb_ref[1], mats_ref, 1, st_ref)
            work.append((cols_f, cf, cols_b, cb))
            both = jnp.logical_and(cf["small"], cb["small"])
            small = both if small is None else jnp.logical_and(small, both)
        for cols_f, cf, cols_b, cb in work:
            oacc_f[:, cols_f] = _hgrn_fast_intra(cf, False)
            oacc_b[:, cols_b] = _hgrn_fast_intra(cb, True)

        @pl.when(jnp.logical_not(small))
        def _():
            for cols_f, cf, cols_b, cb in work:
                oacc_f[:, cols_f] = _hgrn_exact(cf, mats_ref[0, :, 5 * n:9 * n], False)
                oacc_b[:, cols_b] = _hgrn_exact(cb, mats_ref[1, :, 5 * n:9 * n], True)

        return c

    lax.fori_loop(0, nt // HG_UNROLL, scan, 0)

    def finish(i, c):
        off = pl.multiple_of(i * n, n)
        cols = pl.ds(off, n)
        o = oacc_f[:, cols] + oacc_b[:, cols]
        o3 = o.reshape(H_B, DV_B, n)
        y = o3 * lax.rsqrt(jnp.mean(o3 * o3, axis=1, keepdims=True) + EPS)
        gate = g_ref[:, cols]
        y = y.reshape(H_B * DV_B, n) * ng_ref[...] * (gate * _sigmoid(gate))
        o_ref[pl.ds(off, n), :] = y.T.astype(o_ref.dtype)
        return c

    lax.fori_loop(0, nt, finish, 0)
    sfin_ref[0] = st_ref[...]


def _hgrn(pt, lb, ng, mats, s0, *, batch, t):
    def row(r):
        return pl.BlockSpec((256, t), lambda b: (r, b))
    sspec = pl.BlockSpec((1, 2, H_B * DV_B, DK_B), lambda b: (b, 0, 0, 0))
    return pl.pallas_call(
        functools.partial(_hgrn_kernel, t=t),
        out_shape=(jax.ShapeDtypeStruct((batch * t, 256), bf16),
                   jax.ShapeDtypeStruct((batch, 2, H_B * DV_B, DK_B), f32)),
        grid=(batch,),
        in_specs=[row(0), row(1), row(2), row(3), row(4),
                  _const_spec((2, 256, LANES)), _const_spec((256, LANES)), _const_spec(mats.shape), sspec],
        out_specs=(pl.BlockSpec((t, 256), lambda b: (b, 0)), sspec),
        scratch_shapes=[pltpu.VMEM((2, H_B * DV_B, DK_B), f32), pltpu.VMEM((256, t), f32),
                        pltpu.VMEM((256, t), f32)],
        compiler_params=_params("parallel"),
        name="hgrn",
    )(pt, pt, pt, pt, pt, lb, ng, mats, s0)


def _rope_tables(t):
    def cos_sin(dim):
        n_freq = dim // 4
        inv = ROPE_THETA ** (-jnp.arange(n_freq, dtype=f32) / n_freq)
        rows = t // GRID_W
        row_id = jnp.repeat(jnp.arange(rows, dtype=f32), GRID_W)
        col_id = jnp.tile(jnp.arange(GRID_W, dtype=f32), rows)
        ang = jnp.concatenate([row_id[:, None] * inv, col_id[:, None] * inv], axis=-1)
        return jnp.cos(ang), jnp.sin(ang)

    cos, sin = cos_sin(HD_A)
    zero = jnp.zeros_like(sin)
    tab_a = jnp.stack([jnp.tile(jnp.concatenate([cos, cos], -1), (1, H_A)),
                       jnp.tile(jnp.concatenate([zero, sin], -1), (1, H_A)),
                       jnp.tile(jnp.concatenate([-sin, zero], -1), (1, H_A))])
    cos, sin = cos_sin(ROPE_C)
    zero = jnp.zeros_like(sin)
    one64 = jnp.ones((t, NOPE_C), f32)
    zero64 = jnp.zeros((t, NOPE_C), f32)
    pad32 = jnp.zeros((t, LANES - NOPE_C - ROPE_C), f32)
    tab_c = jnp.stack([jnp.concatenate([one64, cos, cos, pad32 + 1.0], -1),
                       jnp.concatenate([zero64, zero, sin, pad32], -1),
                       jnp.concatenate([zero64, -sin, zero, pad32], -1)])
    return tab_a, tab_c


def _pad_rope_cols(x):
    pads = [(0, 0)] * (x.ndim - 1) + [(NOPE_C, LANES - NOPE_C - ROPE_C)]
    return jnp.pad(x, pads)


def kernel(x_prompt, x_sample, c, c_ctx, cache_attn_k, cache_attn_v, state_hgrn, cache_mla_ckv, cache_mla_krope,
           w_ada, b_ada, norm_sub, w_ffn_gate, w_ffn_up, w_ffn_down, w_in, w_out, attn_sink, hgrn_lb_logits,
           hgrn_out_norm, mla_q_norm, mla_kv_norm, mla_w_uq, mla_w_ukv, pool_w, pool_scale, final_norm):
    bp, tp, _ = x_prompt.shape
    bs, ts, _ = x_sample.shape
    np_, ns = bp * tp, bs * ts
    xp = x_prompt.reshape(np_, D_MODEL)
    xs = x_sample.reshape(ns, D_MODEL)

    n_vec = 1 + bs
    cvec = jnp.zeros((-(-n_vec // 8) * 8, D_MODEL), f32).at[0].set(c_ctx).at[1:n_vec].set(c)
    mods = _modulation(cvec, w_ada, b_ada).reshape(DEPTH, cvec.shape[0], N_MOD, D_MODEL)

    p_lb = jax.nn.softmax(hgrn_lb_logits.astype(f32), axis=0)
    lb_all = jnp.cumsum(p_lb, axis=0) - p_lb[0:1]
    tab_a, tab_c = _rope_tables(ts)
    mats = jnp.asarray(_hgrn_mats(), bf16)
    n_ffn = DEPTH * 2
    wg_all = _to_bf16(w_ffn_gate.reshape(n_ffn * D_MODEL, D_FF), 512).reshape(n_ffn, D_MODEL, D_FF)
    wu_all = _to_bf16(w_ffn_up.reshape(n_ffn * D_MODEL, D_FF), 512).reshape(n_ffn, D_MODEL, D_FF)
    wd_all = _to_bf16(w_ffn_down.reshape(n_ffn * D_FF, D_MODEL), D_FF // 2).reshape(n_ffn, D_FF, D_MODEL)
    grp_p = dict(base=0, rpm=np_)
    grp_s = dict(base=1, rpm=ts)
    tm = 512

    new_k, new_v, new_st, new_ckv, new_kr = [], [], [], [], []
    for l in range(DEPTH):
        m_l = mods[l]
        wb = lambda w: w.astype(bf16)
        wi = w_in[l]
        w_row = wb(jnp.concatenate([wi[:, 0:512], wi[:, 1792:2176], _pad_rope_cols(wi[:, 2176:2208]),
                                    wi[:, 2208:2464]], axis=1))
        w_t = wb(wi[:, 512:1792].T)
        wq = wb(jnp.pad(mla_w_uq[l].reshape(Q_LORA_C, H_C, NOPE_C + ROPE_C),
                        ((0, 0), (0, 0), (0, LANES - NOPE_C - ROPE_C))).reshape(Q_LORA_C, H_C * LANES))
        wkv = mla_w_ukv[l].reshape(KV_LORA_C, H_C, NOPE_C + V_C)
        wk = wb(jnp.pad(wkv[:, :, :NOPE_C], ((0, 0), (0, 0), (0, LANES - NOPE_C))).reshape(KV_LORA_C, H_C * LANES))
        wv = wb(wkv[:, :, NOPE_C:].reshape(KV_LORA_C, H_C * V_C).T)
        w_pool = wb(jax.scipy.linalg.block_diag(*[pool_w[l, gi] for gi in range(len(POOL_WINDOWS))]))
        lb = jnp.broadcast_to(lb_all[l][:, :, None], (2, 256, LANES))
        ng = jnp.broadcast_to(jnp.tile(hgrn_out_norm[l], H_B)[:, None], (256, LANES))
        sink = attn_sink[l].astype(f32)
        qn = mla_q_norm[l].reshape(1, Q_LORA_C)
        kvn = mla_kv_norm[l].reshape(1, KV_LORA_C)
        pscale = pool_scale[l].reshape(1, 256)
        wo = wb(w_out[l])

        def ffn(x, grp, j, idx, mix=None, final_g=None):
            return _ffn(x, m_l, norm_sub[l, j].reshape(1, D_MODEL), wg_all, wu_all, wd_all, widx=2 * l + idx,
                        j=j, tm=tm, mix=mix, final_g=final_g, **grp)

        xp = ffn(xp, grp_p, 0, 0)
        xs = ffn(xs, grp_s, 0, 0)

        g1 = norm_sub[l, 1].reshape(1, D_MODEL)
        pr, pt = _mixer_in(xp, m_l, g1, w_row, w_t, tm=tm, **grp_p)
        oa, ka, va = _attn_ctx(pr, sink, batch=bp, t=tp)
        s0 = jnp.zeros((bp, 2, H_B * DV_B, DK_B), f32)
        ob, st = _hgrn(pt, lb, ng, mats, s0, batch=bp, t=tp)
        oc, ckv = _mla(pr, qn, kvn, wq, wk, wv, batch=bp, t=tp, tq=tp)
        od = _pool(pr, w_pool, pscale, batch=bp, t=tp)
        mix_p = (oa, ob, oc, od, wo)
        new_k.append(ka)
        new_v.append(va)
        new_st.append(jnp.swapaxes(st.reshape(bp, 2, H_B, DV_B, DK_B), -1, -2))
        new_ckv.append(ckv.reshape(bp, tp, KV_LORA_C))
        new_kr.append(pr[:, 896 + NOPE_C:896 + NOPE_C + ROPE_C].reshape(bp, tp, ROPE_C))

        pr, pt = _mixer_in(xs, m_l, g1, w_row, w_t, tm=tm, **grp_s)
        oa = _attn_lat(pr, sink, cache_attn_k[:, l], cache_attn_v[:, l], tab_a, batch=bs, t=ts)
        s0 = jnp.swapaxes(state_hgrn[:, l], -1, -2).reshape(bs, 2, H_B * DV_B, DK_B)
        ob, _ = _hgrn(pt, lb, ng, mats, s0, batch=bs, t=ts)
        oc, _ = _mla(pr, qn, kvn, wq, wk, wv, batch=bs, t=ts, tq=256,
                     ctx=(cache_mla_ckv[:, l], _pad_rope_cols(cache_mla_krope[:, l])), tab=tab_c)
        od = _pool(pr, w_pool, pscale, batch=bs, t=ts)
        mix_s = (oa, ob, oc, od, wo)

        fin = final_norm.reshape(1, D_MODEL) if l == DEPTH - 1 else None
        xp = ffn(xp, grp_p, 2, 1, mix_p, fin)
        xs = ffn(xs, grp_s, 2, 1, mix_s, fin)

    return (xp.reshape(bp, tp, D_MODEL), xs.reshape(bs, ts, D_MODEL),
            jnp.stack(new_k, axis=1), jnp.stack(new_v, axis=1), jnp.stack(new_st, axis=1),
            jnp.stack(new_ckv, axis=1), jnp.stack(new_kr, axis=1))
```

```python
import functools

import numpy as np
import jax
import jax.numpy as jnp
from jax import lax
from jax.experimental import pallas as pl
from jax.experimental.pallas import tpu as pltpu

f32 = jnp.float32
bf16 = jnp.bfloat16

D_MODEL = 1024
DEPTH = 2
GRID_W = 64
EPS = 1e-6
LOG2E = 1.4426950408889634
ROPE_THETA = 10000.0
NEG_INF = -1e30
N_MOD = 9
D_FF = 2816
H_A, KV_A, HD_A = 4, 2, 64
G_A = H_A // KV_A
WINDOW_A = 128
BLOCK_A = 128
H_B, DK_B, DV_B = 4, 64, 64
H_C, Q_LORA_C, KV_LORA_C, NOPE_C, ROPE_C, V_C = 4, 256, 128, 64, 32, 64
POOL_WINDOWS = (2, 4, 8, 16)
POOL_CH = 64

LANES = 128
SUBLANES = 8
ROW_REDUCE_CHAINS = 8
VMEM_LIMIT = 56 * 1024 * 1024
HG_TILE = LANES
HG_UNROLL = 2
HG_HALF = HG_TILE // 2
HG_MAX_EXP = 80.0
HG_BLOCK = 16
HG_LEVELS = (16, 32, 64)
ATTN_LAT_BLOCKS = 4
POOL_PAD = 16

PR_W = 1280
PT_W = 1280

_NT = (((1,), (1,)), ((), ()))
_TN = (((0,), (0,)), ((), ()))


def _params(*sem):
    return pltpu.CompilerParams(dimension_semantics=sem, vmem_limit_bytes=VMEM_LIMIT)


def _const_spec(shape):
    zeros = (0,) * len(shape)
    return pl.BlockSpec(shape, lambda *_: zeros, pipeline_mode=pl.Buffered(1))


def _dot(a, b):
    return jnp.dot(a, b, preferred_element_type=f32)


def _sigmoid(x):
    return 1.0 / (1.0 + jnp.exp(-x))


def _rms(x, g):
    return x * lax.rsqrt(jnp.mean(x * x, axis=-1, keepdims=True) + EPS) * g


def _mod_kernel(c_ref, w_ref, b_ref, o_ref):
    c = c_ref[...]
    a = (c * _sigmoid(c)).astype(bf16)
    o_ref[0] = _dot(a, w_ref[0].astype(bf16)) + b_ref[0]


def _modulation(cvec, w_ada, b_ada):
    tn = 1536
    n_out = N_MOD * D_MODEL
    return pl.pallas_call(
        _mod_kernel,
        out_shape=jax.ShapeDtypeStruct((DEPTH, cvec.shape[0], n_out), f32),
        grid=(DEPTH, n_out // tn),
        in_specs=[pl.BlockSpec(cvec.shape, lambda l, j: (0, 0)),
                  pl.BlockSpec((1, D_MODEL, tn), lambda l, j: (l, 0, j)),
                  pl.BlockSpec((1, 1, tn), lambda l, j: (l, 0, j))],
        out_specs=pl.BlockSpec((1, cvec.shape[0], tn), lambda l, j: (l, 0, j)),
        compiler_params=_params("arbitrary", "arbitrary"),
        name="modulation",
    )(cvec, w_ada, b_ada.reshape(DEPTH, 1, n_out))


def _cast_kernel(x_ref, o_ref):
    o_ref[...] = x_ref[...].astype(bf16)


def _to_bf16(x, rows):
    r, c = x.shape
    return pl.pallas_call(
        _cast_kernel,
        out_shape=jax.ShapeDtypeStruct((r, c), bf16),
        grid=(r // rows,),
        in_specs=[pl.BlockSpec((rows, c), lambda i: (i, 0))],
        out_specs=pl.BlockSpec((rows, c), lambda i: (i, 0)),
        compiler_params=_params("parallel"),
        name="cast_bf16",
    )(x)


def _ffn_kernel(*refs, j, mix, final):
    it = iter(refs)
    x_ref, m_ref = next(it), next(it)
    mix_refs = [next(it) for _ in range(5)] if mix else None
    g_ref, wg_ref, wu_ref, wd_ref = next(it), next(it), next(it), next(it)
    fg_ref = next(it) if final else None
    o_ref = next(it)
    x = x_ref[...]
    m = m_ref[0]
    if mix:
        wo_ref = mix_refs[4]
        acc = _dot(mix_refs[0][...], wo_ref[0:256, :])
        for p in range(1, 4):
            acc += _dot(mix_refs[p][...], wo_ref[256 * p:256 * (p + 1), :])
        x = x + m[5:6] * acc
    h = (_rms(x, g_ref[...]) * (1.0 + m[3 * j + 1:3 * j + 2]) + m[3 * j:3 * j + 1]).astype(bf16)
    gate = _dot(h, wg_ref[...])
    up = _dot(h, wu_ref[...])
    act = (gate * _sigmoid(gate) * up).astype(bf16)
    y = x + (0.5 * m[3 * j + 2:3 * j + 3]) * _dot(act, wd_ref[...])
    if final:
        y = _rms(y, fg_ref[...])
    o_ref[...] = y


def _mod_spec(base, rpm, tm):
    return pl.BlockSpec((1, N_MOD, D_MODEL), lambda i: (base + (i * tm) // rpm, 0, 0))


def _ffn(x, mods, g, wg, wu, wd, *, widx, j, base, rpm, tm, mix=None, final_g=None):
    n = x.shape[0]

    def wspec(r, c):
        return pl.BlockSpec((None, r, c), lambda i: (widx, 0, 0), pipeline_mode=pl.Buffered(1))

    in_specs = [pl.BlockSpec((tm, D_MODEL), lambda i: (i, 0)), _mod_spec(base, rpm, tm)]
    args = [x, mods]
    if mix is not None:
        in_specs += [pl.BlockSpec((tm, 256), lambda i: (i, 0))] * 4 + [_const_spec((D_MODEL, D_MODEL))]
        args += list(mix)
    in_specs += [_const_spec((1, D_MODEL)), wspec(D_MODEL, D_FF), wspec(D_MODEL, D_FF), wspec(D_FF, D_MODEL)]
    args += [g, wg, wu, wd]
    if final_g is not None:
        in_specs.append(_const_spec((1, D_MODEL)))
        args.append(final_g)
    return pl.pallas_call(
        functools.partial(_ffn_kernel, j=j, mix=mix is not None, final=final_g is not None),
        out_shape=jax.ShapeDtypeStruct((n, D_MODEL), f32),
        grid=(n // tm,),
        in_specs=in_specs,
        out_specs=pl.BlockSpec((tm, D_MODEL), lambda i: (i, 0)),
        compiler_params=_params("parallel"),
        name="ffn",
    )(*args)


def _mixin_kernel(x_ref, m_ref, g_ref, wr_ref, wt_ref, pr_ref, pt_ref):
    m = m_ref[0]
    h = (_rms(x_ref[...], g_ref[...]) * (1.0 + m[4:5]) + m[3:4]).astype(bf16)
    pr_ref[...] = _dot(h, wr_ref[...])
    pt_ref[...] = lax.dot_general(wt_ref[...], h, _NT, preferred_element_type=f32)


def _mixer_in(x, mods, g, w_row, w_t, *, base, rpm, tm):
    n = x.shape[0]
    return pl.pallas_call(
        _mixin_kernel,
        out_shape=(jax.ShapeDtypeStruct((n, PR_W), f32), jax.ShapeDtypeStruct((PT_W, n), f32)),
        grid=(n // tm,),
        in_specs=[pl.BlockSpec((tm, D_MODEL), lambda i: (i, 0)), _mod_spec(base, rpm, tm),
                  _const_spec((1, D_MODEL)), _const_spec((D_MODEL, PR_W)), _const_spec((PT_W, D_MODEL))],
        out_specs=(pl.BlockSpec((tm, PR_W), lambda i: (i, 0)), pl.BlockSpec((PT_W, tm), lambda i: (0, i))),
        compiler_params=_params("parallel"),
        name="mixer_in",
    )(x, mods, g, w_row, w_t)


def _rope(x, tab_ref, rows, lanes, shift):
    width = x.shape[-1]
    c = tab_ref[0, rows, lanes]
    sa = tab_ref[1, rows, lanes]
    sb = tab_ref[2, rows, lanes]
    return x * c + pltpu.roll(x, shift, axis=1) * sa + pltpu.roll(x, width - shift, axis=1) * sb


def _softmax_parts(s_list, sink):
    sink2 = sink * LOG2E
    mx = functools.reduce(jnp.maximum, [jnp.max(s, axis=-1, keepdims=True) for s in s_list])
    mx = jnp.maximum(mx, sink2)
    ps = [jnp.exp2(s - mx) for s in s_list]
    den = functools.reduce(lambda a, b: a + b, [jnp.sum(p, axis=-1, keepdims=True) for p in ps])
    return ps, 1.0 / (den + jnp.exp2(sink2 - mx))


def _kv_per_query_head(x):
    return jnp.concatenate([x[:, HD_A * (h // G_A):HD_A * (h // G_A + 1)] for h in range(H_A)], axis=1)


def _stack_heads(q):
    head = lax.broadcasted_iota(jnp.int32, (1, H_A * HD_A), 1) // HD_A
    return jnp.concatenate([jnp.where(head == h, q, 0.0) for h in range(H_A)], axis=0)


def _pick_heads(o, r):
    head = lax.broadcasted_iota(jnp.int32, (1, H_A * HD_A), 1) // HD_A
    out = o[0:r]
    for h in range(1, H_A):
        out = jnp.where(head == h, o[h * r:(h + 1) * r], out)
    return out


def _sink_rows(sink_ref, r):
    row = lax.broadcasted_iota(jnp.int32, (H_A * r, 1), 0)
    out = jnp.full((H_A * r, 1), sink_ref[0], f32)
    for h in range(1, H_A):
        out = jnp.where(row >= h * r, sink_ref[h], out)
    return out


def _attn_ctx_kernel(*refs, t, n_prev):
    if n_prev:
        sink_ref, q_ref, kv_ref, kp_ref, vp_ref, o_ref, ko_ref, vo_ref = refs
        ko_ref[0, 0:n_prev] = kp_ref[0]
        vo_ref[0, 0:n_prev] = vp_ref[0]
    else:
        sink_ref, q_ref, kv_ref, o_ref, ko_ref, vo_ref = refs
    kv = kv_ref[...]
    kvw = KV_A * HD_A
    for j in range(KV_A):
        ko_ref[0, n_prev, j] = kv[:, HD_A * j:HD_A * (j + 1)]
        vo_ref[0, n_prev, j] = kv[:, kvw + HD_A * j:kvw + HD_A * (j + 1)]
    kx = _kv_per_query_head(kv[:, 0:kvw]).astype(bf16)
    vx = _kv_per_query_head(kv[:, kvw:2 * kvw]).astype(bf16)
    q4 = _stack_heads(q_ref[...] * (HD_A ** -0.5 * LOG2E)).astype(bf16)
    s = lax.dot_general(q4, kx, _NT, preferred_element_type=f32)
    (p,), inv = _softmax_parts([s], _sink_rows(sink_ref, t))
    o_ref[...] = _pick_heads(_dot(p.astype(bf16), vx) * inv, t).astype(o_ref.dtype)


def _cache_spec(n_layers, *dims):
    return pl.BlockSpec((1, n_layers) + dims, lambda b: (b,) + (0,) * (1 + len(dims)))


def _attn_ctx(pr, sink, prev, *, batch, t):
    n_prev = 0 if prev is None else prev[0].shape[1]
    kvshape = jax.ShapeDtypeStruct((batch, n_prev + 1, KV_A, t, HD_A), f32)
    in_specs = [pl.BlockSpec(memory_space=pltpu.SMEM),
                pl.BlockSpec((t, 256), lambda b: (b, 0)), pl.BlockSpec((t, 256), lambda b: (b, 1))]
    args = [sink, pr, pr]
    if n_prev:
        in_specs += [_cache_spec(n_prev, KV_A, t, HD_A)] * 2
        args += list(prev)
    return pl.pallas_call(
        functools.partial(_attn_ctx_kernel, t=t, n_prev=n_prev),
        out_shape=(jax.ShapeDtypeStruct((batch * t, 256), bf16), kvshape, kvshape),
        grid=(batch,),
        in_specs=in_specs,
        out_specs=(pl.BlockSpec((t, 256), lambda b: (b, 0)),
                   _cache_spec(n_prev + 1, KV_A, t, HD_A), _cache_spec(n_prev + 1, KV_A, t, HD_A)),
        compiler_params=_params("parallel"),
        name="attn_ctx",
    )(*args)


def _attn_lat_kernel(sink_ref, q_ref, kv_ref, kc_ref, vc_ref, tab_ref, o_ref, kpad, vpad, kctx, vctx, *, t):
    n = pl.program_id(1)
    kvw = KV_A * HD_A
    hw = H_A * HD_A

    @pl.when(n == 0)
    def _():
        zeros = jnp.zeros((BLOCK_A, hw), bf16)
        kv = kv_ref[...]
        k = _rope(kv[:, 0:kvw], tab_ref, slice(None), slice(0, kvw), HD_A // 2)
        kpad[0:BLOCK_A, :] = zeros
        vpad[0:BLOCK_A, :] = zeros
        kpad[BLOCK_A:BLOCK_A + t, :] = _kv_per_query_head(k).astype(bf16)
        vpad[BLOCK_A:BLOCK_A + t, :] = _kv_per_query_head(kv[:, kvw:2 * kvw]).astype(bf16)
        kpad[BLOCK_A + t:2 * BLOCK_A + t, :] = zeros
        vpad[BLOCK_A + t:2 * BLOCK_A + t, :] = zeros
        kctx[...] = jnp.concatenate([kc_ref[0, h // G_A] for h in range(H_A)], axis=1).astype(bf16)
        vctx[...] = jnp.concatenate([vc_ref[0, h // G_A] for h in range(H_A)], axis=1).astype(bf16)

    rows = H_A * BLOCK_A
    rr = lax.broadcasted_iota(jnp.int32, (rows, 1), 0) % BLOCK_A
    cidx = lax.broadcasted_iota(jnp.int32, (rows, 3 * BLOCK_A), 1)
    sink = _sink_rows(sink_ref, BLOCK_A)
    for u in range(ATTN_LAT_BLOCKS):
        blk = n * ATTN_LAT_BLOCKS + u
        row0 = pl.multiple_of(blk * BLOCK_A, BLOCK_A)
        q = _rope(q_ref[u * BLOCK_A:(u + 1) * BLOCK_A, :], tab_ref, pl.ds(row0, BLOCK_A), slice(None), HD_A // 2)
        q4 = _stack_heads(q * (HD_A ** -0.5 * LOG2E)).astype(bf16)
        kband = kpad[pl.ds(row0, 3 * BLOCK_A), :]
        vband = vpad[pl.ds(row0, 3 * BLOCK_A), :]
        s_loc = lax.dot_general(q4, kband, _NT, preferred_element_type=f32)
        s_ctx = lax.dot_general(q4, kctx[...], _NT, preferred_element_type=f32)
        lo = jnp.maximum(rr + (BLOCK_A - WINDOW_A), jnp.maximum(0, (1 - blk) * BLOCK_A))
        hi = jnp.minimum(rr + (BLOCK_A + WINDOW_A), jnp.minimum(3 * BLOCK_A, t - (blk - 1) * BLOCK_A) - 1)
        s_loc = jnp.where((cidx >= lo) & (cidx <= hi), s_loc, NEG_INF)
        (p_loc, p_ctx), inv = _softmax_parts([s_loc, s_ctx], sink)
        o = (_dot(p_loc.astype(bf16), vband) + _dot(p_ctx.astype(bf16), vctx[...])) * inv
        o_ref[u * BLOCK_A:(u + 1) * BLOCK_A, :] = _pick_heads(o, BLOCK_A).astype(o_ref.dtype)


def _attn_lat(pr, sink, kc, vc, tab, *, batch, t):
    rows = ATTN_LAT_BLOCKS * BLOCK_A
    nb = t // rows
    lc = kc.shape[2]
    cspec = pl.BlockSpec((1, KV_A, lc, HD_A), lambda b, n: (b, 0, 0, 0))
    return pl.pallas_call(
        functools.partial(_attn_lat_kernel, t=t),
        out_shape=jax.ShapeDtypeStruct((batch * t, 256), bf16),
        grid=(batch, nb),
        in_specs=[pl.BlockSpec(memory_space=pltpu.SMEM),
                  pl.BlockSpec((rows, 256), lambda b, n: (b * nb + n, 0)),
                  pl.BlockSpec((t, 256), lambda b, n: (b, 1)),
                  cspec, cspec,
                  pl.BlockSpec((3, t, 256), lambda b, n: (0, 0, 0))],
        out_specs=pl.BlockSpec((rows, 256), lambda b, n: (b * nb + n, 0)),
        scratch_shapes=[pltpu.VMEM((t + 2 * BLOCK_A, H_A * HD_A), bf16),
                        pltpu.VMEM((t + 2 * BLOCK_A, H_A * HD_A), bf16),
                        pltpu.VMEM((lc, H_A * HD_A), bf16), pltpu.VMEM((lc, H_A * HD_A), bf16)],
        compiler_params=_params("parallel", "arbitrary"),
        name="attn_lat",
    )(sink, pr, pr, kc, vc, tab)


def _reduce_rows(x, reduce_fn):
    r, c = x.shape
    g = ROW_REDUCE_CHAINS
    if r % (g * SUBLANES):
        return reduce_fn(x, axis=0, keepdims=True)
    part = reduce_fn(x.reshape(g, r // (g * SUBLANES), SUBLANES, c), axis=1)
    return reduce_fn(reduce_fn(part, axis=0), axis=0, keepdims=True)


def _mla_kernel(*refs, t, lc, tq, rope, emit_cache, n_prev):
    it = iter(refs)
    cq_ref, ckv_ref, kr_ref = next(it), next(it), next(it)
    cc_ref = kc_ref = tab_ref = None
    if lc:
        cc_ref, kc_ref = next(it), next(it)
    if rope:
        tab_ref = next(it)
    qn_ref, kvn_ref, wq_ref, wk_ref, wv_ref = next(it), next(it), next(it), next(it), next(it)
    ckvp_ref, krp_ref = (next(it), next(it)) if n_prev else (None, None)
    o_ref = next(it)
    ckv_out_ref, kr_out_ref = (next(it), next(it)) if emit_cache else (None, None)
    kbuf, vbuf = next(it), next(it)
    i = pl.program_id(1)
    hw = LANES

    @pl.when(i == 0)
    def _():
        ckv = _rms(ckv_ref[...], kvn_ref[...])
        kr = kr_ref[...]
        if emit_cache:
            if n_prev:
                ckv_out_ref[0, 0:n_prev] = ckvp_ref[0]
                kr_out_ref[0, 0:n_prev] = krp_ref[0]
            ckv_out_ref[0, n_prev] = ckv
            kr_out_ref[0, n_prev] = kr[:, NOPE_C:NOPE_C + ROPE_C]
        ckv_b = ckv.astype(bf16)
        if rope:
            kr = _rope(kr, tab_ref, slice(None), slice(None), ROPE_C // 2)
        kk = _dot(ckv_b, wk_ref[...])
        for h in range(H_C):
            kbuf[h, 0:t, :] = (kk[:, hw * h:hw * (h + 1)] + kr).astype(bf16)
        vbuf[:, 0:t] = lax.dot_general(wv_ref[...], ckv_b, _NT, preferred_element_type=f32).astype(bf16)
        if lc:
            cc_b = cc_ref[0].astype(bf16)
            kkc = _dot(cc_b, wk_ref[...])
            krc = kc_ref[0]
            for h in range(H_C):
                kbuf[h, t:t + lc, :] = (kkc[:, hw * h:hw * (h + 1)] + krc).astype(bf16)
            vbuf[:, t:t + lc] = lax.dot_general(wv_ref[...], cc_b, _NT, preferred_element_type=f32).astype(bf16)

    cqn = _rms(cq_ref[...], qn_ref[...]).astype(bf16)
    qf = _dot(cqn, wq_ref[...])
    c2 = (NOPE_C + ROPE_C) ** -0.5 * LOG2E
    row0 = pl.multiple_of(i * tq, tq)
    n_keys = t + lc
    half = n_keys // 2

    def scores(h):
        qh = qf[:, hw * h:hw * (h + 1)]
        if rope:
            qh = _rope(qh, tab_ref, pl.ds(row0, tq), slice(None), ROPE_C // 2)
        qb = (qh * c2).astype(bf16)
        return [lax.dot_general(kbuf[h, c * half:(c + 1) * half, :], qb, _NT, preferred_element_type=f32)
                for c in range(2)]

    def probs(s):
        m = jnp.maximum(_reduce_rows(s[0], jnp.max), _reduce_rows(s[1], jnp.max))
        p = [jnp.exp2(x - m) for x in s]
        inv = 1.0 / (_reduce_rows(p[0], jnp.sum) + _reduce_rows(p[1], jnp.sum))
        return [x.astype(bf16) for x in p], inv

    def values(h, p, inv):
        rows = slice(V_C * h, V_C * (h + 1))
        return (_dot(vbuf[rows, 0:half], p[0]) + _dot(vbuf[rows, half:2 * half], p[1])) * inv

    s_next = scores(0)
    outs = []
    for h in range(H_C):
        s_cur = s_next
        if h + 1 < H_C:
            s_next = scores(h + 1)
        p, inv = probs(s_cur)
        outs.append(values(h, p, inv))
    o_ref[...] = jnp.concatenate(outs, axis=0).T.astype(o_ref.dtype)


def _mla(pr, qn, kvn, wq, wk, wv, *, batch, t, tq, ctx=None, tab=None, emit_cache=False, cache_prev=None):
    nq = t // tq
    lc = 0 if ctx is None else ctx[0].shape[1]
    n_prev = 0 if cache_prev is None else cache_prev[0].shape[1]
    in_specs = [pl.BlockSpec((tq, 256), lambda b, i: (b * nq + i, 2)),
                pl.BlockSpec((t, LANES), lambda b, i: (b, 6)),
                pl.BlockSpec((t, LANES), lambda b, i: (b, 7))]
    args = [pr, pr, pr]
    if lc:
        in_specs += [pl.BlockSpec((1, lc, LANES), lambda b, i: (b, 0, 0))] * 2
        args += list(ctx)
    if tab is not None:
        in_specs.append(pl.BlockSpec((3, t, LANES), lambda b, i: (0, 0, 0)))
        args.append(tab)
    in_specs += [_const_spec((1, Q_LORA_C)), _const_spec((1, KV_LORA_C)), _const_spec(wq.shape),
                 _const_spec(wk.shape), _const_spec(wv.shape)]
    args += [qn, kvn, wq, wk, wv]

    def cspec(n_layers, width):
        return pl.BlockSpec((1, n_layers, t, width), lambda b, i: (b, 0, 0, 0))

    if n_prev:
        in_specs += [cspec(n_prev, KV_LORA_C), cspec(n_prev, ROPE_C)]
        args += list(cache_prev)
    out_shape = [jax.ShapeDtypeStruct((batch * t, 256), bf16)]
    out_specs = [pl.BlockSpec((tq, 256), lambda b, i: (b * nq + i, 0))]
    if emit_cache:
        out_shape += [jax.ShapeDtypeStruct((batch, n_prev + 1, t, KV_LORA_C), f32),
                      jax.ShapeDtypeStruct((batch, n_prev + 1, t, ROPE_C), f32)]
        out_specs += [cspec(n_prev + 1, KV_LORA_C), cspec(n_prev + 1, ROPE_C)]
    return pl.pallas_call(
        functools.partial(_mla_kernel, t=t, lc=lc, tq=tq, rope=tab is not None, emit_cache=emit_cache,
                          n_prev=n_prev),
        out_shape=tuple(out_shape),
        grid=(batch, nq),
        in_specs=in_specs,
        out_specs=tuple(out_specs),
        scratch_shapes=[pltpu.VMEM((H_C, t + lc, LANES), bf16), pltpu.VMEM((H_C * V_C, t + lc), bf16)],
        compiler_params=_params("parallel", "arbitrary"),
        name="mla",
    )(*args)


def _pool_kernel(x_ref, w_ref, sc_ref, o_ref, xpad, *, t):
    x = x_ref[...]
    n = t + 2 * POOL_PAD
    zeros = jnp.zeros((POOL_PAD, 256), f32)
    xpad[0:POOL_PAD, :] = zeros
    xpad[POOL_PAD:POOL_PAD + t, :] = x
    xpad[POOL_PAD + t:n, :] = zeros
    xp = xpad[...]
    s2 = xp + pltpu.roll(xp, 1, axis=0)
    s4 = pltpu.roll(s2, 1, axis=0) + pltpu.roll(s2, n - 1, axis=0)
    s8 = pltpu.roll(s4, 2, axis=0) + pltpu.roll(s4, n - 2, axis=0)
    s16 = pltpu.roll(s8, 4, axis=0) + pltpu.roll(s8, n - 4, axis=0)
    sums = [s[POOL_PAD:POOL_PAD + t] for s in (s2, s4, s8, s16)]
    group = lax.broadcasted_iota(jnp.int32, (t, 256), 1) // POOL_CH
    pos = lax.broadcasted_iota(jnp.int32, (t, 256), 0)
    win_sum = sums[0]
    half = jnp.full((t, 256), POOL_WINDOWS[0] // 2, jnp.int32)
    for gi in range(1, len(POOL_WINDOWS)):
        win_sum = jnp.where(group == gi, sums[gi], win_sum)
        half = jnp.where(group == gi, POOL_WINDOWS[gi] // 2, half)
    cnt = jnp.minimum(pos + half, t) - jnp.maximum(pos - half, 0)
    mean = win_sum / cnt.astype(f32)
    o_ref[...] = (_dot((mean - x).astype(bf16), w_ref[...]) * sc_ref[...]).astype(o_ref.dtype)


def _pool(pr, w_bd, scale, *, batch, t):
    return pl.pallas_call(
        functools.partial(_pool_kernel, t=t),
        out_shape=jax.ShapeDtypeStruct((batch * t, 256), bf16),
        grid=(batch,),
        in_specs=[pl.BlockSpec((t, 256), lambda b: (b, 4)), _const_spec((256, 256)), _const_spec((1, 256))],
        out_specs=pl.BlockSpec((t, 256), lambda b: (b, 0)),
        scratch_shapes=[pltpu.VMEM((t + 2 * POOL_PAD, 256), f32)],
        compiler_params=_params("parallel"),
        name="pool",
    )(pr, w_bd, scale)


def _hgrn_mats():
    n = HG_TILE
    s = np.arange(n)[:, None]
    t = np.arange(n)[None, :]
    same_half = s // HG_HALF == t // HG_HALF
    mid = (t // HG_HALF) * HG_HALF + HG_HALF // 2 - 1
    rel_mid = (same_half & (s > mid) & (s <= t)).astype(np.float32) - (same_half & (s <= mid) & (s > t))
    mats = [rel_mid, same_half & (s <= t), same_half & (s > t), s <= t, s > t,
            (s // HG_BLOCK == t // HG_BLOCK) & (s <= t)]
    for m in HG_LEVELS:
        same_pair = s // (2 * m) == t // (2 * m)
        right_t = (t // m) % 2 == 1
        right_s = (s // m) % 2 == 1
        q_side = same_pair & right_t & right_s & (s <= t)
        k_side = same_pair & ~right_t & ~right_s & (s > t)
        mats.append(q_side | k_side)
    fwd = np.concatenate([m.astype(np.float32) for m in mats], axis=1)
    bwd = np.concatenate([m[::-1, ::-1].astype(np.float32) for m in mats], axis=1)
    both = np.stack([fwd, bwd])
    return np.concatenate([both, both], axis=1)


def _hgrn_exact_intra(q, k, v, vb, hl, mats, reverse):
    n = HG_TILE
    e_all = _dot(hl, mats)
    beta = e_all[:, 0:n]
    lane = lax.broadcasted_iota(jnp.int32, (1, n), 1)
    pos = lane % HG_BLOCK
    v3 = v.reshape(H_B, DV_B, n)

    o3 = jnp.sum((q * k).reshape(H_B, DK_B, n), axis=1, keepdims=True) * v3
    for d in range(1, HG_BLOCK):
        shift = (n - d) if reverse else d
        valid = (pos <= HG_BLOCK - 1 - d) if reverse else (pos >= d)
        kr = pltpu.roll(k, shift, axis=1)
        br = pltpu.roll(beta, shift, axis=1)
        vr = pltpu.roll(v, shift, axis=1)
        e = jnp.exp(jnp.where(valid, beta - br, NEG_INF))
        sc = jnp.sum((q * kr * e).reshape(H_B, DK_B, n), axis=1, keepdims=True)
        o3 = o3 + sc * vr.reshape(H_B, DV_B, n)

    srow = lax.broadcasted_iota(jnp.int32, (n, n), 0)
    tcol = lax.broadcasted_iota(jnp.int32, (n, n), 1)
    scores = [jnp.zeros((n, n), f32) for _ in range(H_B)]
    for li, m in enumerate(HG_LEVELS):
        ex = jnp.exp(e_all[:, (li + 1) * n:(li + 2) * n])
        is_q = ((lane // m) % 2) == (0 if reverse else 1)
        qs = jnp.where(is_q, q * ex, 0.0).astype(bf16)
        ks = jnp.where(is_q, 0.0, k * ex).astype(bf16)
        pair = (srow // (2 * m)) == (tcol // (2 * m))
        for h in range(H_B):
            a = lax.dot_general(ks[DK_B * h:DK_B * (h + 1)], qs[DK_B * h:DK_B * (h + 1)], _TN,
                                preferred_element_type=f32)
            scores[h] = scores[h] + jnp.where(pair, a, 0.0)

    return [o3[h] + _dot(vb[DV_B * h:DV_B * (h + 1)], scores[h].astype(bf16)) for h in range(H_B)]


def _hgrn_common(q, v, fpre, lb, mats_ref, sidx, st_ref):
    n = HG_TILE
    f = lb + (1.0 - lb) * _sigmoid(fpre)
    k = 1.0 - f
    logf = jnp.log(f)
    hi = logf.astype(bf16)
    hl = jnp.concatenate([hi, (logf - hi.astype(f32)).astype(bf16)], axis=1)
    ex = _dot(hl, mats_ref[sidx, :, 0:5 * n])
    ones = jnp.ones((8, 2 * n), bf16)
    g_row = lax.dot_general(ones, hl, _NT, preferred_element_type=f32)[0:1]
    d_mid = ex[:, 0:n]
    small = jnp.max(jnp.abs(d_mid)) <= HG_MAX_EXP
    q_in = (q * jnp.exp(ex[:, 3 * n:4 * n])).astype(bf16)
    k_out = (k * jnp.exp(ex[:, 4 * n:5 * n])).astype(bf16)
    vb = v.astype(bf16)
    o_state = []
    for h in range(H_B):
        rows = slice(DK_B * h, DK_B * (h + 1))
        st = st_ref[sidx, rows, :]
        o_state.append(_dot(st.astype(bf16), q_in[rows]))
        upd = lax.dot_general(vb[rows], k_out[rows], _NT, preferred_element_type=f32)
        st_ref[sidx, rows, :] = st * jnp.exp(g_row[:, rows]) + upd
    return dict(q=q, k=k, v=v, vb=vb, hl=hl, ex=ex, o_state=o_state, small=small)


def _hgrn_fast_intra(c, reverse):
    n = HG_TILE
    q, k, ex = c["q"], c["k"], c["ex"]
    lane = lax.broadcasted_iota(jnp.int32, (1, n), 1)
    in0 = (lane // HG_HALF) == 0
    q_half = in0 if reverse else jnp.logical_not(in0)
    d_mid = ex[:, 0:n]
    qd = q * jnp.exp(d_mid)
    kd = k * jnp.exp(-d_mid)
    ql = jnp.where(q_half, q * jnp.exp(ex[:, n:2 * n]), 0.0)
    kl = jnp.where(q_half, 0.0, k * jnp.exp(ex[:, 2 * n:3 * n]))
    parts_q = [jnp.where(in0, qd, 0.0).astype(bf16), jnp.where(in0, 0.0, qd).astype(bf16), ql.astype(bf16)]
    parts_k = [jnp.where(in0, kd, 0.0).astype(bf16), jnp.where(in0, 0.0, kd).astype(bf16), kl.astype(bf16)]
    srow = lax.broadcasted_iota(jnp.int32, (n, n), 0)
    tcol = lax.broadcasted_iota(jnp.int32, (n, n), 1)
    causal = (srow >= tcol) if reverse else (srow <= tcol)
    scores = []
    for h in range(H_B):
        rows = slice(DK_B * h, DK_B * (h + 1))
        k3 = jnp.concatenate([p[rows] for p in parts_k], axis=0)
        q3 = jnp.concatenate([p[rows] for p in parts_q], axis=0)
        scores.append(lax.dot_general(k3, q3, _TN, preferred_element_type=f32))
    scores = [jnp.where(causal, a, 0.0).astype(bf16) for a in scores]
    outs = [c["o_state"][h] + _dot(c["vb"][DV_B * h:DV_B * (h + 1)], scores[h]) for h in range(H_B)]
    return jnp.concatenate(outs, axis=0)


def _hgrn_exact(c, mats, reverse):
    intra = _hgrn_exact_intra(c["q"], c["k"], c["v"], c["vb"], c["hl"], mats, reverse)
    return jnp.concatenate([c["o_state"][h] + intra[h] for h in range(H_B)], axis=0)


def _hgrn_kernel(*refs, t, has_s0, emit_state, n_prev):
    it = iter(refs)
    q_ref, i_ref, ff_ref, fb_ref, g_ref, lb_ref, ng_ref, mats_ref = [next(it) for _ in range(8)]
    s0_ref = next(it) if has_s0 else None
    sprev_ref = next(it) if n_prev else None
    o_ref = next(it)
    sfin_ref = next(it) if emit_state else None
    st_ref, oacc_f, oacc_b = next(it), next(it), next(it)
    n = HG_TILE
    nt = t // n
    for d in range(2):
        for h in range(H_B):
            rows = slice(DV_B * h, DV_B * (h + 1))
            st_ref[d, rows, :] = s0_ref[0, d, h].T if has_s0 else jnp.zeros((DV_B, DK_B), f32)

    def scan(i, c):
        work = []
        small = None
        for u in range(HG_UNROLL):
            j = i * HG_UNROLL + u
            cols_f = pl.ds(pl.multiple_of(j * n, n), n)
            cols_b = pl.ds(pl.multiple_of((nt - 1 - j) * n, n), n)
            cf = _hgrn_common(q_ref[:, cols_f], i_ref[:, cols_f], ff_ref[:, cols_f], lb_ref[0], mats_ref, 0, st_ref)
            cb = _hgrn_common(q_ref[:, cols_b], i_ref[:, cols_b], fb_ref[:, cols_b], lb_ref[1], mats_ref, 1, st_ref)
            work.append((cols_f, cf, cols_b, cb))
            both = jnp.logical_and(cf["small"], cb["small"])
            small = both if small is None else jnp.logical_and(small, both)
        for cols_f, cf, cols_b, cb in work:
            oacc_f[:, cols_f] = _hgrn_fast_intra(cf, False)
            oacc_b[:, cols_b] = _hgrn_fast_intra(cb, True)

        @pl.when(jnp.logical_not(small))
        def _():
            for cols_f, cf, cols_b, cb in work:
                oacc_f[:, cols_f] = _hgrn_exact(cf, mats_ref[0, :, 5 * n:9 * n], False)
                oacc_b[:, cols_b] = _hgrn_exact(cb, mats_ref[1, :, 5 * n:9 * n], True)

        return c

    lax.fori_loop(0, nt // HG_UNROLL, scan, 0)

    def finish(i, c):
        off = pl.multiple_of(i * n, n)
        cols = pl.ds(off, n)
        o = oacc_f[:, cols] + oacc_b[:, cols]
        o3 = o.reshape(H_B, DV_B, n)
        y = o3 * lax.rsqrt(jnp.mean(o3 * o3, axis=1, keepdims=True) + EPS)
        gate = g_ref[:, cols]
        y = y.reshape(H_B * DV_B, n) * ng_ref[...] * (gate * _sigmoid(gate))
        o_ref[pl.ds(off, n), :] = y.T.astype(o_ref.dtype)
        return c

    lax.fori_loop(0, nt, finish, 0)
    if emit_state:
        if n_prev:
            sfin_ref[0, 0:n_prev] = sprev_ref[0]
        for d in range(2):
            for h in range(H_B):
                sfin_ref[0, n_prev, d, h] = st_ref[d, DV_B * h:DV_B * (h + 1), :].T


def _hgrn(pt, lb, ng, mats, *, batch, t, s0=None, layer=0, emit_state=False, state_prev=None):
    def row(r):
        return pl.BlockSpec((256, t), lambda b: (r, b))

    def sspec(n_layers):
        return pl.BlockSpec((1, n_layers, 2, H_B, DK_B, DV_B), lambda b: (b, 0, 0, 0, 0, 0))

    n_prev = 0 if state_prev is None else state_prev.shape[1]
    in_specs = [row(0), row(1), row(2), row(3), row(4),
                _const_spec((2, 256, LANES)), _const_spec((256, LANES)), _const_spec(mats.shape)]
    args = [pt, pt, pt, pt, pt, lb, ng, mats]
    if s0 is not None:
        in_specs.append(pl.BlockSpec((1, None, 2, H_B, DK_B, DV_B), lambda b: (b, layer, 0, 0, 0, 0)))
        args.append(s0)
    if n_prev:
        in_specs.append(sspec(n_prev))
        args.append(state_prev)
    out_shape = [jax.ShapeDtypeStruct((batch * t, 256), bf16)]
    out_specs = [pl.BlockSpec((t, 256), lambda b: (b, 0))]
    if emit_state:
        out_shape.append(jax.ShapeDtypeStruct((batch, n_prev + 1, 2, H_B, DK_B, DV_B), f32))
        out_specs.append(sspec(n_prev + 1))
    return pl.pallas_call(
        functools.partial(_hgrn_kernel, t=t, has_s0=s0 is not None, emit_state=emit_state, n_prev=n_prev),
        out_shape=tuple(out_shape),
        grid=(batch,),
        in_specs=in_specs,
        out_specs=tuple(out_specs),
        scratch_shapes=[pltpu.VMEM((2, H_B * DV_B, DK_B), f32), pltpu.VMEM((256, t), f32),
                        pltpu.VMEM((256, t), f32)],
        compiler_params=_params("parallel"),
        name="hgrn",
    )(*args)


def _rope_tables(t):
    def cos_sin(dim):
        n_freq = dim // 4
        inv = ROPE_THETA ** (-jnp.arange(n_freq, dtype=f32) / n_freq)
        rows = t // GRID_W
        row_id = jnp.repeat(jnp.arange(rows, dtype=f32), GRID_W)
        col_id = jnp.tile(jnp.arange(GRID_W, dtype=f32), rows)
        ang = jnp.concatenate([row_id[:, None] * inv, col_id[:, None] * inv], axis=-1)
        return jnp.cos(ang), jnp.sin(ang)

    cos, sin = cos_sin(HD_A)
    zero = jnp.zeros_like(sin)
    tab_a = jnp.stack([jnp.tile(jnp.concatenate([cos, cos], -1), (1, H_A)),
                       jnp.tile(jnp.concatenate([zero, sin], -1), (1, H_A)),
                       jnp.tile(jnp.concatenate([-sin, zero], -1), (1, H_A))])
    cos, sin = cos_sin(ROPE_C)
    zero = jnp.zeros_like(sin)
    one64 = jnp.ones((t, NOPE_C), f32)
    zero64 = jnp.zeros((t, NOPE_C), f32)
    pad32 = jnp.zeros((t, LANES - NOPE_C - ROPE_C), f32)
    tab_c = jnp.stack([jnp.concatenate([one64, cos, cos, pad32 + 1.0], -1),
                       jnp.concatenate([zero64, zero, sin, pad32], -1),
                       jnp.concatenate([zero64, -sin, zero, pad32], -1)])
    return tab_a, tab_c


def _pad_rope_cols(x):
    pads = [(0, 0)] * (x.ndim - 1) + [(NOPE_C, LANES - NOPE_C - ROPE_C)]
    return jnp.pad(x, pads)


def kernel(x_prompt, x_sample, c, c_ctx, cache_attn_k, cache_attn_v, state_hgrn, cache_mla_ckv, cache_mla_krope,
           w_ada, b_ada, norm_sub, w_ffn_gate, w_ffn_up, w_ffn_down, w_in, w_out, attn_sink, hgrn_lb_logits,
           hgrn_out_norm, mla_q_norm, mla_kv_norm, mla_w_uq, mla_w_ukv, pool_w, pool_scale, final_norm):
    bp, tp, _ = x_prompt.shape
    bs, ts, _ = x_sample.shape
    np_, ns = bp * tp, bs * ts
    xp = x_prompt.reshape(np_, D_MODEL)
    xs = x_sample.reshape(ns, D_MODEL)

    n_vec = 1 + bs
    cvec = jnp.zeros((-(-n_vec // 8) * 8, D_MODEL), f32).at[0].set(c_ctx).at[1:n_vec].set(c)
    mods = _modulation(cvec, w_ada, b_ada).reshape(DEPTH, cvec.shape[0], N_MOD, D_MODEL)

    p_lb = jax.nn.softmax(hgrn_lb_logits.astype(f32), axis=0)
    lb_all = jnp.cumsum(p_lb, axis=0) - p_lb[0:1]
    tab_a, tab_c = _rope_tables(ts)
    mats = jnp.asarray(_hgrn_mats(), bf16)
    n_ffn = DEPTH * 2
    wg_all = _to_bf16(w_ffn_gate.reshape(n_ffn * D_MODEL, D_FF), 512).reshape(n_ffn, D_MODEL, D_FF)
    wu_all = _to_bf16(w_ffn_up.reshape(n_ffn * D_MODEL, D_FF), 512).reshape(n_ffn, D_MODEL, D_FF)
    wd_all = _to_bf16(w_ffn_down.reshape(n_ffn * D_FF, D_MODEL), D_FF // 2).reshape(n_ffn, D_FF, D_MODEL)
    grp_p = dict(base=0, rpm=np_)
    grp_s = dict(base=1, rpm=ts)
    tm = 512

    new_k = new_v = new_st = new_ckv = new_kr = None
    for l in range(DEPTH):
        m_l = mods[l]
        wb = lambda w: w.astype(bf16)
        wi = w_in[l]
        w_row = wb(jnp.concatenate([wi[:, 0:512], wi[:, 1792:2176], _pad_rope_cols(wi[:, 2176:2208]),
                                    wi[:, 2208:2464]], axis=1))
        w_t = wb(wi[:, 512:1792].T)
        wq = wb(jnp.pad(mla_w_uq[l].reshape(Q_LORA_C, H_C, NOPE_C + ROPE_C),
                        ((0, 0), (0, 0), (0, LANES - NOPE_C - ROPE_C))).reshape(Q_LORA_C, H_C * LANES))
        wkv = mla_w_ukv[l].reshape(KV_LORA_C, H_C, NOPE_C + V_C)
        wk = wb(jnp.pad(wkv[:, :, :NOPE_C], ((0, 0), (0, 0), (0, LANES - NOPE_C))).reshape(KV_LORA_C, H_C * LANES))
        wv = wb(wkv[:, :, NOPE_C:].reshape(KV_LORA_C, H_C * V_C).T)
        w_pool = wb(jax.scipy.linalg.block_diag(*[pool_w[l, gi] for gi in range(len(POOL_WINDOWS))]))
        lb = jnp.broadcast_to(lb_all[l][:, :, None], (2, 256, LANES))
        ng = jnp.broadcast_to(jnp.tile(hgrn_out_norm[l], H_B)[:, None], (256, LANES))
        sink = attn_sink[l].astype(f32)
        qn = mla_q_norm[l].reshape(1, Q_LORA_C)
        kvn = mla_kv_norm[l].reshape(1, KV_LORA_C)
        pscale = pool_scale[l].reshape(1, 256)
        wo = wb(w_out[l])

        def ffn(x, grp, j, idx, mix=None, final_g=None):
            return _ffn(x, m_l, norm_sub[l, j].reshape(1, D_MODEL), wg_all, wu_all, wd_all, widx=2 * l + idx,
                        j=j, tm=tm, mix=mix, final_g=final_g, **grp)

        xp = ffn(xp, grp_p, 0, 0)
        xs = ffn(xs, grp_s, 0, 0)

        g1 = norm_sub[l, 1].reshape(1, D_MODEL)
        pr, pt = _mixer_in(xp, m_l, g1, w_row, w_t, tm=tm, **grp_p)
        oa, new_k, new_v = _attn_ctx(pr, sink, None if l == 0 else (new_k, new_v), batch=bp, t=tp)
        ob, new_st = _hgrn(pt, lb, ng, mats, batch=bp, t=tp, emit_state=True, state_prev=new_st)
        oc, new_ckv, new_kr = _mla(pr, qn, kvn, wq, wk, wv, batch=bp, t=tp, tq=tp, emit_cache=True,
                                   cache_prev=None if l == 0 else (new_ckv, new_kr))
        od = _pool(pr, w_pool, pscale, batch=bp, t=tp)
        mix_p = (oa, ob, oc, od, wo)

        pr, pt = _mixer_in(xs, m_l, g1, w_row, w_t, tm=tm, **grp_s)
        oa = _attn_lat(pr, sink, cache_attn_k[:, l], cache_attn_v[:, l], tab_a, batch=bs, t=ts)
        (ob,) = _hgrn(pt, lb, ng, mats, batch=bs, t=ts, s0=state_hgrn, layer=l)
        (oc,) = _mla(pr, qn, kvn, wq, wk, wv, batch=bs, t=ts, tq=512,
                     ctx=(cache_mla_ckv[:, l], _pad_rope_cols(cache_mla_krope[:, l])), tab=tab_c)
        od = _pool(pr, w_pool, pscale, batch=bs, t=ts)
        mix_s = (oa, ob, oc, od, wo)

        fin = final_norm.reshape(1, D_MODEL) if l == DEPTH - 1 else None
        xp = ffn(xp, grp_p, 2, 1, mix_p, fin)
        xs = ffn(xs, grp_s, 2, 1, mix_s, fin)

    return (xp.reshape(bp, tp, D_MODEL), xs.reshape(bs, ts, D_MODEL), new_k, new_v, new_st, new_ckv, new_kr)
```

```python
import functools

import numpy as np
import jax
import jax.numpy as jnp
from jax import lax
from jax.experimental import pallas as pl
from jax.experimental.pallas import tpu as pltpu

f32 = jnp.float32
bf16 = jnp.bfloat16

D_MODEL = 1024
DEPTH = 2
GRID_W = 64
EPS = 1e-6
LOG2E = 1.4426950408889634
ROPE_THETA = 10000.0
NEG_INF = -1e30
N_MOD = 9
D_FF = 2816
H_A, KV_A, HD_A = 4, 2, 64
G_A = H_A // KV_A
WINDOW_A = 128
BLOCK_A = 128
H_B, DK_B, DV_B = 4, 64, 64
H_C, Q_LORA_C, KV_LORA_C, NOPE_C, ROPE_C, V_C = 4, 256, 128, 64, 32, 64
POOL_WINDOWS = (2, 4, 8, 16)
POOL_CH = 64

LANES = 128
SUBLANES = 8
ROW_REDUCE_CHAINS = 8
VMEM_LIMIT = 56 * 1024 * 1024
HG_TILE = LANES
HG_UNROLL = 2
HG_HALF = HG_TILE // 2
HG_MAX_EXP = 80.0
HG_BLOCK = 16
HG_LEVELS = (16, 32, 64)
ATTN_LAT_BLOCKS = 4
POOL_PAD = 16

PR_W = 1280
PT_W = 1280

_NT = (((1,), (1,)), ((), ()))
_TN = (((0,), (0,)), ((), ()))


def _params(*sem):
    return pltpu.CompilerParams(dimension_semantics=sem, vmem_limit_bytes=VMEM_LIMIT)


def _const_spec(shape, lead=None):
    zeros = (0,) * len(shape)
    if lead is None:
        return pl.BlockSpec(shape, lambda *_: zeros, pipeline_mode=pl.Buffered(1))
    return pl.BlockSpec((None,) + tuple(shape), lambda *_: (lead,) + zeros, pipeline_mode=pl.Buffered(1))


def _dot(a, b):
    return jnp.dot(a, b, preferred_element_type=f32)


def _sigmoid(x):
    return 1.0 / (1.0 + jnp.exp(-x))


def _rms(x, g):
    return x * lax.rsqrt(jnp.mean(x * x, axis=-1, keepdims=True) + EPS) * g


def _mod_kernel(c_ref, w_ref, b_ref, o_ref):
    c = c_ref[...]
    a = (c * _sigmoid(c)).astype(bf16)
    o_ref[0] = _dot(a, w_ref[0].astype(bf16)) + b_ref[0]


def _modulation(cvec, w_ada, b_ada):
    tn = 1536
    n_out = N_MOD * D_MODEL
    return pl.pallas_call(
        _mod_kernel,
        out_shape=jax.ShapeDtypeStruct((DEPTH, cvec.shape[0], n_out), f32),
        grid=(DEPTH, n_out // tn),
        in_specs=[pl.BlockSpec(cvec.shape, lambda l, j: (0, 0)),
                  pl.BlockSpec((1, D_MODEL, tn), lambda l, j: (l, 0, j)),
                  pl.BlockSpec((1, 1, tn), lambda l, j: (l, 0, j))],
        out_specs=pl.BlockSpec((1, cvec.shape[0], tn), lambda l, j: (l, 0, j)),
        compiler_params=_params("arbitrary", "arbitrary"),
        name="modulation",
    )(cvec, w_ada, b_ada.reshape(DEPTH, 1, n_out))


def _cast_kernel(x_ref, o_ref):
    o_ref[...] = x_ref[...].astype(bf16)


def _to_bf16(x, rows):
    r, c = x.shape
    return pl.pallas_call(
        _cast_kernel,
        out_shape=jax.ShapeDtypeStruct((r, c), bf16),
        grid=(r // rows,),
        in_specs=[pl.BlockSpec((rows, c), lambda i: (i, 0))],
        out_specs=pl.BlockSpec((rows, c), lambda i: (i, 0)),
        compiler_params=_params("parallel"),
        name="cast_bf16",
    )(x)


def _ffn_kernel(*refs, j, mix, final):
    it = iter(refs)
    x_ref, m_ref = next(it), next(it)
    mix_refs = [next(it) for _ in range(5)] if mix else None
    g_ref, wg_ref, wu_ref, wd_ref = next(it), next(it), next(it), next(it)
    fg_ref = next(it) if final else None
    o_ref = next(it)
    x = x_ref[...]
    m = m_ref[0]
    if mix:
        wo_ref = mix_refs[4]
        acc = _dot(mix_refs[0][...], wo_ref[0:256, :])
        for p in range(1, 4):
            acc += _dot(mix_refs[p][...], wo_ref[256 * p:256 * (p + 1), :])
        x = x + m[5:6] * acc
    h = (_rms(x, g_ref[...]) * (1.0 + m[3 * j + 1:3 * j + 2]) + m[3 * j:3 * j + 1]).astype(bf16)
    gate = _dot(h, wg_ref[...])
    up = _dot(h, wu_ref[...])
    act = (gate * _sigmoid(gate) * up).astype(bf16)
    y = x + (0.5 * m[3 * j + 2:3 * j + 3]) * _dot(act, wd_ref[...])
    if final:
        y = _rms(y, fg_ref[...])
    o_ref[...] = y


def _mod_spec(layer, base, rpm, tm):
    return pl.BlockSpec((None, 1, N_MOD, D_MODEL), lambda i: (layer, base + (i * tm) // rpm, 0, 0))


def _ffn(x, mods, norms, wg, wu, wd, *, layer, widx, j, base, rpm, tm, mix=None, final_g=None):
    n = x.shape[0]

    def wspec(r, c):
        return _const_spec((r, c), widx)

    in_specs = [pl.BlockSpec((tm, D_MODEL), lambda i: (i, 0)), _mod_spec(layer, base, rpm, tm)]
    args = [x, mods]
    if mix is not None:
        in_specs += [pl.BlockSpec((tm, 256), lambda i: (i, 0))] * 4 + [_const_spec((D_MODEL, D_MODEL), layer)]
        args += list(mix)
    in_specs += [_const_spec((1, D_MODEL), 3 * layer + j), wspec(D_MODEL, D_FF), wspec(D_MODEL, D_FF),
                 wspec(D_FF, D_MODEL)]
    args += [norms, wg, wu, wd]
    if final_g is not None:
        in_specs.append(_const_spec((1, D_MODEL)))
        args.append(final_g)
    return pl.pallas_call(
        functools.partial(_ffn_kernel, j=j, mix=mix is not None, final=final_g is not None),
        out_shape=jax.ShapeDtypeStruct((n, D_MODEL), f32),
        grid=(n // tm,),
        in_specs=in_specs,
        out_specs=pl.BlockSpec((tm, D_MODEL), lambda i: (i, 0)),
        compiler_params=_params("parallel"),
        name="ffn",
    )(*args)


def _mixin_kernel(x_ref, m_ref, g_ref, wr_ref, wt_ref, pr_ref, pt_ref):
    m = m_ref[0]
    h = (_rms(x_ref[...], g_ref[...]) * (1.0 + m[4:5]) + m[3:4]).astype(bf16)
    pr_ref[...] = _dot(h, wr_ref[...])
    pt_ref[...] = lax.dot_general(wt_ref[...], h, _NT, preferred_element_type=f32)


def _mixer_in(x, mods, norms, w_row, w_t, *, layer, base, rpm, tm):
    n = x.shape[0]
    return pl.pallas_call(
        _mixin_kernel,
        out_shape=(jax.ShapeDtypeStruct((n, PR_W), f32), jax.ShapeDtypeStruct((PT_W, n), f32)),
        grid=(n // tm,),
        in_specs=[pl.BlockSpec((tm, D_MODEL), lambda i: (i, 0)), _mod_spec(layer, base, rpm, tm),
                  _const_spec((1, D_MODEL), 3 * layer + 1), _const_spec((D_MODEL, PR_W), layer),
                  _const_spec((PT_W, D_MODEL), layer)],
        out_specs=(pl.BlockSpec((tm, PR_W), lambda i: (i, 0)), pl.BlockSpec((PT_W, tm), lambda i: (0, i))),
        compiler_params=_params("parallel"),
        name="mixer_in",
    )(x, mods, norms, w_row, w_t)


def _rope(x, tab_ref, rows, lanes, shift):
    width = x.shape[-1]
    c = tab_ref[0, rows, lanes]
    sa = tab_ref[1, rows, lanes]
    sb = tab_ref[2, rows, lanes]
    return x * c + pltpu.roll(x, shift, axis=1) * sa + pltpu.roll(x, width - shift, axis=1) * sb


def _softmax_parts(s_list, sink):
    sink2 = sink * LOG2E
    mx = functools.reduce(jnp.maximum, [jnp.max(s, axis=-1, keepdims=True) for s in s_list])
    mx = jnp.maximum(mx, sink2)
    ps = [jnp.exp2(s - mx) for s in s_list]
    den = functools.reduce(lambda a, b: a + b, [jnp.sum(p, axis=-1, keepdims=True) for p in ps])
    return ps, 1.0 / (den + jnp.exp2(sink2 - mx))


def _kv_per_query_head(x):
    return jnp.concatenate([x[:, HD_A * (h // G_A):HD_A * (h // G_A + 1)] for h in range(H_A)], axis=1)


def _stack_heads(q):
    head = lax.broadcasted_iota(jnp.int32, (1, H_A * HD_A), 1) // HD_A
    return jnp.concatenate([jnp.where(head == h, q, 0.0) for h in range(H_A)], axis=0)


def _pick_heads(o, r):
    head = lax.broadcasted_iota(jnp.int32, (1, H_A * HD_A), 1) // HD_A
    out = o[0:r]
    for h in range(1, H_A):
        out = jnp.where(head == h, o[h * r:(h + 1) * r], out)
    return out


def _sink_rows(sink_ref, layer, r):
    row = lax.broadcasted_iota(jnp.int32, (H_A * r, 1), 0)
    out = jnp.full((H_A * r, 1), sink_ref[layer, 0], f32)
    for h in range(1, H_A):
        out = jnp.where(row >= h * r, sink_ref[layer, h], out)
    return out


def _attn_ctx_kernel(*refs, t, n_prev, layer):
    if n_prev:
        sink_ref, q_ref, kv_ref, kp_ref, vp_ref, o_ref, ko_ref, vo_ref = refs
        ko_ref[0, 0:n_prev] = kp_ref[0]
        vo_ref[0, 0:n_prev] = vp_ref[0]
    else:
        sink_ref, q_ref, kv_ref, o_ref, ko_ref, vo_ref = refs
    kv = kv_ref[...]
    kvw = KV_A * HD_A
    for j in range(KV_A):
        ko_ref[0, n_prev, j] = kv[:, HD_A * j:HD_A * (j + 1)]
        vo_ref[0, n_prev, j] = kv[:, kvw + HD_A * j:kvw + HD_A * (j + 1)]
    kx = _kv_per_query_head(kv[:, 0:kvw]).astype(bf16)
    vx = _kv_per_query_head(kv[:, kvw:2 * kvw]).astype(bf16)
    q4 = _stack_heads(q_ref[...] * (HD_A ** -0.5 * LOG2E)).astype(bf16)
    s = lax.dot_general(q4, kx, _NT, preferred_element_type=f32)
    (p,), inv = _softmax_parts([s], _sink_rows(sink_ref, layer, t))
    o_ref[...] = _pick_heads(_dot(p.astype(bf16), vx) * inv, t).astype(o_ref.dtype)


def _cache_spec(n_layers, *dims):
    return pl.BlockSpec((1, n_layers) + dims, lambda b: (b,) + (0,) * (1 + len(dims)))


def _attn_ctx(pr, sink, prev, *, layer, batch, t):
    n_prev = 0 if prev is None else prev[0].shape[1]
    kvshape = jax.ShapeDtypeStruct((batch, n_prev + 1, KV_A, t, HD_A), f32)
    in_specs = [pl.BlockSpec(memory_space=pltpu.SMEM),
                pl.BlockSpec((t, 256), lambda b: (b, 0)), pl.BlockSpec((t, 256), lambda b: (b, 1))]
    args = [sink, pr, pr]
    if n_prev:
        in_specs += [_cache_spec(n_prev, KV_A, t, HD_A)] * 2
        args += list(prev)
    return pl.pallas_call(
        functools.partial(_attn_ctx_kernel, t=t, n_prev=n_prev, layer=layer),
        out_shape=(jax.ShapeDtypeStruct((batch * t, 256), bf16), kvshape, kvshape),
        grid=(batch,),
        in_specs=in_specs,
        out_specs=(pl.BlockSpec((t, 256), lambda b: (b, 0)),
                   _cache_spec(n_prev + 1, KV_A, t, HD_A), _cache_spec(n_prev + 1, KV_A, t, HD_A)),
        compiler_params=_params("parallel"),
        name="attn_ctx",
    )(*args)


def _attn_lat_kernel(sink_ref, q_ref, kv_ref, kc_ref, vc_ref, tab_ref, o_ref, kpad, vpad, kctx, vctx, *, t, layer):
    n = pl.program_id(1)
    kvw = KV_A * HD_A
    hw = H_A * HD_A

    @pl.when(n == 0)
    def _():
        zeros = jnp.zeros((BLOCK_A, hw), bf16)
        kv = kv_ref[...]
        k = _rope(kv[:, 0:kvw], tab_ref, slice(None), slice(0, kvw), HD_A // 2)
        kpad[0:BLOCK_A, :] = zeros
        vpad[0:BLOCK_A, :] = zeros
        kpad[BLOCK_A:BLOCK_A + t, :] = _kv_per_query_head(k).astype(bf16)
        vpad[BLOCK_A:BLOCK_A + t, :] = _kv_per_query_head(kv[:, kvw:2 * kvw]).astype(bf16)
        kpad[BLOCK_A + t:2 * BLOCK_A + t, :] = zeros
        vpad[BLOCK_A + t:2 * BLOCK_A + t, :] = zeros
        kctx[...] = jnp.concatenate([kc_ref[0, h // G_A] for h in range(H_A)], axis=1).astype(bf16)
        vctx[...] = jnp.concatenate([vc_ref[0, h // G_A] for h in range(H_A)], axis=1).astype(bf16)

    rows = H_A * BLOCK_A
    rr = lax.broadcasted_iota(jnp.int32, (rows, 1), 0) % BLOCK_A
    cidx = lax.broadcasted_iota(jnp.int32, (rows, 3 * BLOCK_A), 1)
    sink = _sink_rows(sink_ref, layer, BLOCK_A)
    for u in range(ATTN_LAT_BLOCKS):
        blk = n * ATTN_LAT_BLOCKS + u
        row0 = pl.multiple_of(blk * BLOCK_A, BLOCK_A)
        q = _rope(q_ref[u * BLOCK_A:(u + 1) * BLOCK_A, :], tab_ref, pl.ds(row0, BLOCK_A), slice(None), HD_A // 2)
        q4 = _stack_heads(q * (HD_A ** -0.5 * LOG2E)).astype(bf16)
        kband = kpad[pl.ds(row0, 3 * BLOCK_A), :]
        vband = vpad[pl.ds(row0, 3 * BLOCK_A), :]
        s_loc = lax.dot_general(q4, kband, _NT, preferred_element_type=f32)
        s_ctx = lax.dot_general(q4, kctx[...], _NT, preferred_element_type=f32)
        lo = jnp.maximum(rr + (BLOCK_A - WINDOW_A), jnp.maximum(0, (1 - blk) * BLOCK_A))
        hi = jnp.minimum(rr + (BLOCK_A + WINDOW_A), jnp.minimum(3 * BLOCK_A, t - (blk - 1) * BLOCK_A) - 1)
        s_loc = jnp.where((cidx >= lo) & (cidx <= hi), s_loc, NEG_INF)
        (p_loc, p_ctx), inv = _softmax_parts([s_loc, s_ctx], sink)
        o = (_dot(p_loc.astype(bf16), vband) + _dot(p_ctx.astype(bf16), vctx[...])) * inv
        o_ref[u * BLOCK_A:(u + 1) * BLOCK_A, :] = _pick_heads(o, BLOCK_A).astype(o_ref.dtype)


def _attn_lat(pr, sink, kc, vc, tab, *, layer, batch, t):
    rows = ATTN_LAT_BLOCKS * BLOCK_A
    nb = t // rows
    lc = kc.shape[3]
    cspec = pl.BlockSpec((1, None, KV_A, lc, HD_A), lambda b, n: (b, layer, 0, 0, 0))
    return pl.pallas_call(
        functools.partial(_attn_lat_kernel, t=t, layer=layer),
        out_shape=jax.ShapeDtypeStruct((batch * t, 256), bf16),
        grid=(batch, nb),
        in_specs=[pl.BlockSpec(memory_space=pltpu.SMEM),
                  pl.BlockSpec((rows, 256), lambda b, n: (b * nb + n, 0)),
                  pl.BlockSpec((t, 256), lambda b, n: (b, 1)),
                  cspec, cspec,
                  pl.BlockSpec((3, t, 256), lambda b, n: (0, 0, 0))],
        out_specs=pl.BlockSpec((rows, 256), lambda b, n: (b * nb + n, 0)),
        scratch_shapes=[pltpu.VMEM((t + 2 * BLOCK_A, H_A * HD_A), bf16),
                        pltpu.VMEM((t + 2 * BLOCK_A, H_A * HD_A), bf16),
                        pltpu.VMEM((lc, H_A * HD_A), bf16), pltpu.VMEM((lc, H_A * HD_A), bf16)],
        compiler_params=_params("parallel", "arbitrary"),
        name="attn_lat",
    )(sink, pr, pr, kc, vc, tab)


def _reduce_rows(x, reduce_fn):
    r, c = x.shape
    g = ROW_REDUCE_CHAINS
    if r % (g * SUBLANES):
        return reduce_fn(x, axis=0, keepdims=True)
    part = reduce_fn(x.reshape(g, r // (g * SUBLANES), SUBLANES, c), axis=1)
    return reduce_fn(reduce_fn(part, axis=0), axis=0, keepdims=True)


def _mla_kernel(*refs, t, lc, tq, rope, emit_cache, n_prev):
    it = iter(refs)
    cq_ref, ckv_ref, kr_ref = next(it), next(it), next(it)
    cc_ref = kc_ref = tab_ref = None
    if lc:
        cc_ref, kc_ref = next(it), next(it)
    if rope:
        tab_ref = next(it)
    qn_ref, kvn_ref, wq_ref, wk_ref, wv_ref = next(it), next(it), next(it), next(it), next(it)
    ckvp_ref, krp_ref = (next(it), next(it)) if n_prev else (None, None)
    o_ref = next(it)
    ckv_out_ref, kr_out_ref = (next(it), next(it)) if emit_cache else (None, None)
    kbuf, vbuf = next(it), next(it)
    i = pl.program_id(1)
    hw = LANES

    @pl.when(i == 0)
    def _():
        ckv = _rms(ckv_ref[...], kvn_ref[...])
        kr = kr_ref[...]
        if emit_cache:
            if n_prev:
                ckv_out_ref[0, 0:n_prev] = ckvp_ref[0]
                kr_out_ref[0, 0:n_prev] = krp_ref[0]
            ckv_out_ref[0, n_prev] = ckv
            kr_out_ref[0, n_prev] = kr[:, NOPE_C:NOPE_C + ROPE_C]
        ckv_b = ckv.astype(bf16)
        if rope:
            kr = _rope(kr, tab_ref, slice(None), slice(None), ROPE_C // 2)
        kk = _dot(ckv_b, wk_ref[...])
        for h in range(H_C):
            kbuf[h, 0:t, :] = (kk[:, hw * h:hw * (h + 1)] + kr).astype(bf16)
        vbuf[:, 0:t] = lax.dot_general(wv_ref[...], ckv_b, _NT, preferred_element_type=f32).astype(bf16)
        if lc:
            cc_b = cc_ref[0].astype(bf16)
            kkc = _dot(cc_b, wk_ref[...])
            krc = kc_ref[0]
            for h in range(H_C):
                kbuf[h, t:t + lc, :] = (kkc[:, hw * h:hw * (h + 1)] + krc).astype(bf16)
            vbuf[:, t:t + lc] = lax.dot_general(wv_ref[...], cc_b, _NT, preferred_element_type=f32).astype(bf16)

    cqn = _rms(cq_ref[...], qn_ref[...]).astype(bf16)
    qf = _dot(cqn, wq_ref[...])
    c2 = (NOPE_C + ROPE_C) ** -0.5 * LOG2E
    row0 = pl.multiple_of(i * tq, tq)
    n_keys = t + lc
    half = n_keys // 2

    def scores(h):
        qh = qf[:, hw * h:hw * (h + 1)]
        if rope:
            qh = _rope(qh, tab_ref, pl.ds(row0, tq), slice(None), ROPE_C // 2)
        qb = (qh * c2).astype(bf16)
        return [lax.dot_general(kbuf[h, c * half:(c + 1) * half, :], qb, _NT, preferred_element_type=f32)
                for c in range(2)]

    def probs(s):
        m = jnp.maximum(_reduce_rows(s[0], jnp.max), _reduce_rows(s[1], jnp.max))
        p = [jnp.exp2(x - m) for x in s]
        inv = 1.0 / (_reduce_rows(p[0], jnp.sum) + _reduce_rows(p[1], jnp.sum))
        return [x.astype(bf16) for x in p], inv

    def values(h, p, inv):
        rows = slice(V_C * h, V_C * (h + 1))
        return (_dot(vbuf[rows, 0:half], p[0]) + _dot(vbuf[rows, half:2 * half], p[1])) * inv

    s_next = scores(0)
    outs = []
    for h in range(H_C):
        s_cur = s_next
        if h + 1 < H_C:
            s_next = scores(h + 1)
        p, inv = probs(s_cur)
        outs.append(values(h, p, inv))
    o_ref[...] = jnp.concatenate(outs, axis=0).T.astype(o_ref.dtype)


def _mla(pr, qn, kvn, wq, wk, wv, *, layer, batch, t, tq, ctx=None, tab=None, emit_cache=False, cache_prev=None):
    nq = t // tq
    lc = 0 if ctx is None else ctx[0].shape[2]
    n_prev = 0 if cache_prev is None else cache_prev[0].shape[1]
    in_specs = [pl.BlockSpec((tq, 256), lambda b, i: (b * nq + i, 2)),
                pl.BlockSpec((t, LANES), lambda b, i: (b, 6)),
                pl.BlockSpec((t, LANES), lambda b, i: (b, 7))]
    args = [pr, pr, pr]
    if lc:
        in_specs += [pl.BlockSpec((1, None, lc, LANES), lambda b, i: (b, layer, 0, 0))] * 2
        args += list(ctx)
    if tab is not None:
        in_specs.append(pl.BlockSpec((3, t, LANES), lambda b, i: (0, 0, 0)))
        args.append(tab)
    in_specs += [_const_spec(a.shape[1:], layer) for a in (qn, kvn, wq, wk, wv)]
    args += [qn, kvn, wq, wk, wv]

    def cspec(n_layers, width):
        return pl.BlockSpec((1, n_layers, t, width), lambda b, i: (b, 0, 0, 0))

    if n_prev:
        in_specs += [cspec(n_prev, KV_LORA_C), cspec(n_prev, ROPE_C)]
        args += list(cache_prev)
    out_shape = [jax.ShapeDtypeStruct((batch * t, 256), bf16)]
    out_specs = [pl.BlockSpec((tq, 256), lambda b, i: (b * nq + i, 0))]
    if emit_cache:
        out_shape += [jax.ShapeDtypeStruct((batch, n_prev + 1, t, KV_LORA_C), f32),
                      jax.ShapeDtypeStruct((batch, n_prev + 1, t, ROPE_C), f32)]
        out_specs += [cspec(n_prev + 1, KV_LORA_C), cspec(n_prev + 1, ROPE_C)]
    return pl.pallas_call(
        functools.partial(_mla_kernel, t=t, lc=lc, tq=tq, rope=tab is not None, emit_cache=emit_cache,
                          n_prev=n_prev),
        out_shape=tuple(out_shape),
        grid=(batch, nq),
        in_specs=in_specs,
        out_specs=tuple(out_specs),
        scratch_shapes=[pltpu.VMEM((H_C, t + lc, LANES), bf16), pltpu.VMEM((H_C * V_C, t + lc), bf16)],
        compiler_params=_params("parallel", "arbitrary"),
        name="mla",
    )(*args)


def _pool_counts(t):
    pos = np.arange(t)[:, None]
    half = np.repeat(np.asarray(POOL_WINDOWS) // 2, POOL_CH)[None, :]
    return (np.minimum(pos + half, t) - np.maximum(pos - half, 0)).astype(np.float32)


def _pool_kernel(x_ref, cnt_ref, w_ref, sc_ref, o_ref, buf_a, buf_b, buf_c, *, t):
    x = x_ref[...]
    pad, edge = POOL_PAD, SUBLANES
    n = t + 2 * pad
    m = n - 2 * edge
    zeros = jnp.zeros((pad, 256), f32)
    buf_a[0:pad, :] = zeros
    buf_a[pad:pad + t, :] = x
    buf_a[pad + t:n, :] = zeros
    for buf in (buf_b, buf_c):
        buf[0:edge, :] = zeros[0:edge]
        buf[n - edge:n, :] = zeros[0:edge]
    inner = slice(edge, edge + m)
    buf_b[inner, :] = buf_a[pl.ds(edge - 1, m), :] + buf_a[pl.ds(edge, m), :]
    buf_c[inner, :] = buf_b[pl.ds(edge - 1, m), :] + buf_b[pl.ds(edge + 1, m), :]
    lo, hi = slice(0, LANES), slice(LANES, 2 * LANES)
    buf_a[inner, hi] = buf_c[pl.ds(edge - 2, m), hi] + buf_c[pl.ds(edge + 2, m), hi]
    s16 = buf_a[pl.ds(pad - 4, t), hi] + buf_a[pl.ds(pad + 4, t), hi]
    first = lax.broadcasted_iota(jnp.int32, (1, LANES), 1) < POOL_CH
    rows = slice(pad, pad + t)
    win_sum = jnp.concatenate([jnp.where(first, buf_b[rows, lo], buf_c[rows, lo]),
                               jnp.where(first, buf_a[rows, hi], s16)], axis=1)
    mean = win_sum / cnt_ref[...]
    o_ref[...] = (_dot((mean - x).astype(bf16), w_ref[...]) * sc_ref[...]).astype(o_ref.dtype)


def _pool(pr, w_bd, scale, *, layer, batch, t):
    return pl.pallas_call(
        functools.partial(_pool_kernel, t=t),
        out_shape=jax.ShapeDtypeStruct((batch * t, 256), bf16),
        grid=(batch,),
        in_specs=[pl.BlockSpec((t, 256), lambda b: (b, 4)), _const_spec((t, 256)), _const_spec((256, 256), layer),
                  _const_spec((1, 256), layer)],
        out_specs=pl.BlockSpec((t, 256), lambda b: (b, 0)),
        scratch_shapes=[pltpu.VMEM((t + 2 * POOL_PAD, 256), f32)] * 3,
        compiler_params=_params("parallel"),
        name="pool",
    )(pr, jnp.asarray(_pool_counts(t)), w_bd, scale)


def _hgrn_mats():
    n = HG_TILE
    s = np.arange(n)[:, None]
    t = np.arange(n)[None, :]
    same_half = s // HG_HALF == t // HG_HALF
    mid = (t // HG_HALF) * HG_HALF + HG_HALF // 2 - 1
    rel_mid = (same_half & (s > mid) & (s <= t)).astype(np.float32) - (same_half & (s <= mid) & (s > t))
    mats = [rel_mid, same_half & (s <= t), same_half & (s > t), s <= t, s > t,
            (s // HG_BLOCK == t // HG_BLOCK) & (s <= t)]
    for m in HG_LEVELS:
        same_pair = s // (2 * m) == t // (2 * m)
        right_t = (t // m) % 2 == 1
        right_s = (s // m) % 2 == 1
        q_side = same_pair & right_t & right_s & (s <= t)
        k_side = same_pair & ~right_t & ~right_s & (s > t)
        mats.append(q_side | k_side)
    fwd = np.concatenate([m.astype(np.float32) for m in mats], axis=1)
    bwd = np.concatenate([m[::-1, ::-1].astype(np.float32) for m in mats], axis=1)
    both = np.stack([fwd, bwd])
    return np.concatenate([both, both], axis=1)


def _hgrn_exact_intra(q, k, v, vb, hl, mats, reverse):
    n = HG_TILE
    e_all = _dot(hl, mats)
    beta = e_all[:, 0:n]
    lane = lax.broadcasted_iota(jnp.int32, (1, n), 1)
    pos = lane % HG_BLOCK
    v3 = v.reshape(H_B, DV_B, n)

    o3 = jnp.sum((q * k).reshape(H_B, DK_B, n), axis=1, keepdims=True) * v3
    for d in range(1, HG_BLOCK):
        shift = (n - d) if reverse else d
        valid = (pos <= HG_BLOCK - 1 - d) if reverse else (pos >= d)
        kr = pltpu.roll(k, shift, axis=1)
        br = pltpu.roll(beta, shift, axis=1)
        vr = pltpu.roll(v, shift, axis=1)
        e = jnp.exp(jnp.where(valid, beta - br, NEG_INF))
        sc = jnp.sum((q * kr * e).reshape(H_B, DK_B, n), axis=1, keepdims=True)
        o3 = o3 + sc * vr.reshape(H_B, DV_B, n)

    srow = lax.broadcasted_iota(jnp.int32, (n, n), 0)
    tcol = lax.broadcasted_iota(jnp.int32, (n, n), 1)
    scores = [jnp.zeros((n, n), f32) for _ in range(H_B)]
    for li, m in enumerate(HG_LEVELS):
        ex = jnp.exp(e_all[:, (li + 1) * n:(li + 2) * n])
        is_q = ((lane // m) % 2) == (0 if reverse else 1)
        qs = jnp.where(is_q, q * ex, 0.0).astype(bf16)
        ks = jnp.where(is_q, 0.0, k * ex).astype(bf16)
        pair = (srow // (2 * m)) == (tcol // (2 * m))
        for h in range(H_B):
            a = lax.dot_general(ks[DK_B * h:DK_B * (h + 1)], qs[DK_B * h:DK_B * (h + 1)], _TN,
                                preferred_element_type=f32)
            scores[h] = scores[h] + jnp.where(pair, a, 0.0)

    return [o3[h] + _dot(vb[DV_B * h:DV_B * (h + 1)], scores[h].astype(bf16)) for h in range(H_B)]


def _hgrn_common(q, v, fpre, lb, mats_ref, sidx, st_ref):
    n = HG_TILE
    f = lb + (1.0 - lb) * _sigmoid(fpre)
    k = 1.0 - f
    logf = jnp.log(f)
    hi = logf.astype(bf16)
    hl = jnp.concatenate([hi, (logf - hi.astype(f32)).astype(bf16)], axis=1)
    ex = _dot(hl, mats_ref[sidx, :, 0:5 * n])
    ones = jnp.ones((8, 2 * n), bf16)
    g_row = lax.dot_general(ones, hl, _NT, preferred_element_type=f32)[0:1]
    d_mid = ex[:, 0:n]
    small = jnp.max(jnp.abs(d_mid)) <= HG_MAX_EXP
    q_in = (q * jnp.exp(ex[:, 3 * n:4 * n])).astype(bf16)
    k_out = (k * jnp.exp(ex[:, 4 * n:5 * n])).astype(bf16)
    vb = v.astype(bf16)
    o_state = []
    for h in range(H_B):
        rows = slice(DK_B * h, DK_B * (h + 1))
        st = st_ref[sidx, rows, :]
        o_state.append(_dot(st.astype(bf16), q_in[rows]))
        upd = lax.dot_general(vb[rows], k_out[rows], _NT, preferred_element_type=f32)
        st_ref[sidx, rows, :] = st * jnp.exp(g_row[:, rows]) + upd
    return dict(q=q, k=k, v=v, vb=vb, hl=hl, ex=ex, o_state=o_state, small=small)


def _hgrn_fast_intra(c, reverse):
    n = HG_TILE
    q, k, ex = c["q"], c["k"], c["ex"]
    lane = lax.broadcasted_iota(jnp.int32, (1, n), 1)
    in0 = (lane // HG_HALF) == 0
    q_half = in0 if reverse else jnp.logical_not(in0)
    d_mid = ex[:, 0:n]
    qd = q * jnp.exp(d_mid)
    kd = k * jnp.exp(-d_mid)
    ql = jnp.where(q_half, q * jnp.exp(ex[:, n:2 * n]), 0.0)
    kl = jnp.where(q_half, 0.0, k * jnp.exp(ex[:, 2 * n:3 * n]))
    parts_q = [jnp.where(in0, qd, 0.0).astype(bf16), jnp.where(in0, 0.0, qd).astype(bf16), ql.astype(bf16)]
    parts_k = [jnp.where(in0, kd, 0.0).astype(bf16), jnp.where(in0, 0.0, kd).astype(bf16), kl.astype(bf16)]
    srow = lax.broadcasted_iota(jnp.int32, (n, n), 0)
    tcol = lax.broadcasted_iota(jnp.int32, (n, n), 1)
    causal = (srow >= tcol) if reverse else (srow <= tcol)
    scores = []
    for h in range(H_B):
        rows = slice(DK_B * h, DK_B * (h + 1))
        k3 = jnp.concatenate([p[rows] for p in parts_k], axis=0)
        q3 = jnp.concatenate([p[rows] for p in parts_q], axis=0)
        scores.append(lax.dot_general(k3, q3, _TN, preferred_element_type=f32))
    scores = [jnp.where(causal, a, 0.0).astype(bf16) for a in scores]
    outs = [c["o_state"][h] + _dot(c["vb"][DV_B * h:DV_B * (h + 1)], scores[h]) for h in range(H_B)]
    return jnp.concatenate(outs, axis=0)


def _hgrn_exact(c, mats, reverse):
    intra = _hgrn_exact_intra(c["q"], c["k"], c["v"], c["vb"], c["hl"], mats, reverse)
    return jnp.concatenate([c["o_state"][h] + intra[h] for h in range(H_B)], axis=0)


def _hgrn_kernel(*refs, t, has_s0, emit_state, n_prev):
    it = iter(refs)
    q_ref, i_ref, ff_ref, fb_ref, g_ref, lb_ref, ng_ref, mats_ref = [next(it) for _ in range(8)]
    s0_ref = next(it) if has_s0 else None
    sprev_ref = next(it) if n_prev else None
    o_ref = next(it)
    sfin_ref = next(it) if emit_state else None
    st_ref, oacc_f, oacc_b = next(it), next(it), next(it)
    n = HG_TILE
    nt = t // n
    for d in range(2):
        for h in range(H_B):
            rows = slice(DV_B * h, DV_B * (h + 1))
            st_ref[d, rows, :] = s0_ref[0, d, h].T if has_s0 else jnp.zeros((DV_B, DK_B), f32)

    def scan(i, c):
        work = []
        small = None
        for u in range(HG_UNROLL):
            j = i * HG_UNROLL + u
            cols_f = pl.ds(pl.multiple_of(j * n, n), n)
            cols_b = pl.ds(pl.multiple_of((nt - 1 - j) * n, n), n)
            cf = _hgrn_common(q_ref[:, cols_f], i_ref[:, cols_f], ff_ref[:, cols_f], lb_ref[0], mats_ref, 0, st_ref)
            cb = _hgrn_common(q_ref[:, cols_b], i_ref[:, cols_b], fb_ref[:, cols_b], lb_ref[1], mats_ref, 1, st_ref)
            work.append((cols_f, cf, cols_b, cb))
            both = jnp.logical_and(cf["small"], cb["small"])
            small = both if small is None else jnp.logical_and(small, both)
        for cols_f, cf, cols_b, cb in work:
            oacc_f[:, cols_f] = _hgrn_fast_intra(cf, False)
            oacc_b[:, cols_b] = _hgrn_fast_intra(cb, True)

        @pl.when(jnp.logical_not(small))
        def _():
            for cols_f, cf, cols_b, cb in work:
                oacc_f[:, cols_f] = _hgrn_exact(cf, mats_ref[0, :, 5 * n:9 * n], False)
                oacc_b[:, cols_b] = _hgrn_exact(cb, mats_ref[1, :, 5 * n:9 * n], True)

        return c

    lax.fori_loop(0, nt // HG_UNROLL, scan, 0)

    def finish(i, c):
        off = pl.multiple_of(i * n, n)
        cols = pl.ds(off, n)
        o = oacc_f[:, cols] + oacc_b[:, cols]
        o3 = o.reshape(H_B, DV_B, n)
        y = o3 * lax.rsqrt(jnp.mean(o3 * o3, axis=1, keepdims=True) + EPS)
        gate = g_ref[:, cols]
        y = y.reshape(H_B * DV_B, n) * ng_ref[...] * (gate * _sigmoid(gate))
        o_ref[pl.ds(off, n), :] = y.T.astype(o_ref.dtype)
        return c

    lax.fori_loop(0, nt, finish, 0)
    if emit_state:
        if n_prev:
            sfin_ref[0, 0:n_prev] = sprev_ref[0]
        for d in range(2):
            for h in range(H_B):
                sfin_ref[0, n_prev, d, h] = st_ref[d, DV_B * h:DV_B * (h + 1), :].T


def _hgrn(pt, lb, ng, mats, *, batch, t, s0=None, layer=0, emit_state=False, state_prev=None):
    def row(r):
        return pl.BlockSpec((256, t), lambda b: (r, b))

    def sspec(n_layers):
        return pl.BlockSpec((1, n_layers, 2, H_B, DK_B, DV_B), lambda b: (b, 0, 0, 0, 0, 0))

    n_prev = 0 if state_prev is None else state_prev.shape[1]
    in_specs = [row(0), row(1), row(2), row(3), row(4),
                _const_spec((2, 256, LANES), layer), _const_spec((256, LANES), layer), _const_spec(mats.shape)]
    args = [pt, pt, pt, pt, pt, lb, ng, mats]
    if s0 is not None:
        in_specs.append(pl.BlockSpec((1, None, 2, H_B, DK_B, DV_B), lambda b: (b, layer, 0, 0, 0, 0)))
        args.append(s0)
    if n_prev:
        in_specs.append(sspec(n_prev))
        args.append(state_prev)
    out_shape = [jax.ShapeDtypeStruct((batch * t, 256), bf16)]
    out_specs = [pl.BlockSpec((t, 256), lambda b: (b, 0))]
    if emit_state:
        out_shape.append(jax.ShapeDtypeStruct((batch, n_prev + 1, 2, H_B, DK_B, DV_B), f32))
        out_specs.append(sspec(n_prev + 1))
    return pl.pallas_call(
        functools.partial(_hgrn_kernel, t=t, has_s0=s0 is not None, emit_state=emit_state, n_prev=n_prev),
        out_shape=tuple(out_shape),
        grid=(batch,),
        in_specs=in_specs,
        out_specs=tuple(out_specs),
        scratch_shapes=[pltpu.VMEM((2, H_B * DV_B, DK_B), f32), pltpu.VMEM((256, t), f32),
                        pltpu.VMEM((256, t), f32)],
        compiler_params=_params("parallel"),
        name="hgrn",
    )(*args)


def _rope_tables(t):
    def cos_sin(dim):
        n_freq = dim // 4
        inv = ROPE_THETA ** (-jnp.arange(n_freq, dtype=f32) / n_freq)
        rows = t // GRID_W
        row_id = jnp.repeat(jnp.arange(rows, dtype=f32), GRID_W)
        col_id = jnp.tile(jnp.arange(GRID_W, dtype=f32), rows)
        ang = jnp.concatenate([row_id[:, None] * inv, col_id[:, None] * inv], axis=-1)
        return jnp.cos(ang), jnp.sin(ang)

    cos, sin = cos_sin(HD_A)
    zero = jnp.zeros_like(sin)
    tab_a = jnp.stack([jnp.tile(jnp.concatenate([cos, cos], -1), (1, H_A)),
                       jnp.tile(jnp.concatenate([zero, sin], -1), (1, H_A)),
                       jnp.tile(jnp.concatenate([-sin, zero], -1), (1, H_A))])
    cos, sin = cos_sin(ROPE_C)
    zero = jnp.zeros_like(sin)
    one64 = jnp.ones((t, NOPE_C), f32)
    zero64 = jnp.zeros((t, NOPE_C), f32)
    pad32 = jnp.zeros((t, LANES - NOPE_C - ROPE_C), f32)
    tab_c = jnp.stack([jnp.concatenate([one64, cos, cos, pad32 + 1.0], -1),
                       jnp.concatenate([zero64, zero, sin, pad32], -1),
                       jnp.concatenate([zero64, -sin, zero, pad32], -1)])
    return tab_a, tab_c


def _pad_rope_cols(x):
    pads = [(0, 0)] * (x.ndim - 1) + [(NOPE_C, LANES - NOPE_C - ROPE_C)]
    return jnp.pad(x, pads)


def kernel(x_prompt, x_sample, c, c_ctx, cache_attn_k, cache_attn_v, state_hgrn, cache_mla_ckv, cache_mla_krope,
           w_ada, b_ada, norm_sub, w_ffn_gate, w_ffn_up, w_ffn_down, w_in, w_out, attn_sink, hgrn_lb_logits,
           hgrn_out_norm, mla_q_norm, mla_kv_norm, mla_w_uq, mla_w_ukv, pool_w, pool_scale, final_norm):
    bp, tp, _ = x_prompt.shape
    bs, ts, _ = x_sample.shape
    np_, ns = bp * tp, bs * ts
    xp = x_prompt.reshape(np_, D_MODEL)
    xs = x_sample.reshape(ns, D_MODEL)

    n_vec = 1 + bs
    cvec = jnp.zeros((-(-n_vec // 8) * 8, D_MODEL), f32).at[0].set(c_ctx).at[1:n_vec].set(c)
    mods = _modulation(cvec, w_ada, b_ada).reshape(DEPTH, cvec.shape[0], N_MOD, D_MODEL)

    p_lb = jax.nn.softmax(hgrn_lb_logits.astype(f32), axis=0)
    lb_all = jnp.cumsum(p_lb, axis=0) - p_lb[0:1]
    tab_a, tab_c = _rope_tables(ts)
    mats = jnp.asarray(_hgrn_mats(), bf16)
    n_ffn = DEPTH * 2
    wg_all = _to_bf16(w_ffn_gate.reshape(n_ffn * D_MODEL, D_FF), 512).reshape(n_ffn, D_MODEL, D_FF)
    wu_all = _to_bf16(w_ffn_up.reshape(n_ffn * D_MODEL, D_FF), 512).reshape(n_ffn, D_MODEL, D_FF)
    wd_all = _to_bf16(w_ffn_down.reshape(n_ffn * D_FF, D_MODEL), D_FF // 2).reshape(n_ffn, D_FF, D_MODEL)
    grp_p = dict(base=0, rpm=np_)
    grp_s = dict(base=1, rpm=ts)
    tm, tm_mix = 512, 1024

    wb = lambda w: w.astype(bf16)
    w_row = wb(jnp.concatenate([w_in[:, :, 0:512], w_in[:, :, 1792:2176], _pad_rope_cols(w_in[:, :, 2176:2208]),
                                w_in[:, :, 2208:2464]], axis=2))
    w_t = wb(jnp.swapaxes(w_in[:, :, 512:1792], 1, 2))
    wq = wb(jnp.pad(mla_w_uq.reshape(DEPTH, Q_LORA_C, H_C, NOPE_C + ROPE_C),
                    ((0, 0), (0, 0), (0, 0), (0, LANES - NOPE_C - ROPE_C))).reshape(DEPTH, Q_LORA_C, H_C * LANES))
    wkv = mla_w_ukv.reshape(DEPTH, KV_LORA_C, H_C, NOPE_C + V_C)
    wk = wb(jnp.pad(wkv[..., :NOPE_C], ((0, 0), (0, 0), (0, 0), (0, LANES - NOPE_C)))
            .reshape(DEPTH, KV_LORA_C, H_C * LANES))
    wv = wb(jnp.swapaxes(wkv[..., NOPE_C:].reshape(DEPTH, KV_LORA_C, H_C * V_C), 1, 2))
    n_grp = len(POOL_WINDOWS)
    w_pool = wb(jnp.einsum('lgij,gh->lgihj', pool_w, jnp.eye(n_grp, dtype=f32))
                .reshape(DEPTH, n_grp * POOL_CH, n_grp * POOL_CH))
    lb = jnp.broadcast_to(lb_all[..., None], (DEPTH, 2, 256, LANES))
    ng = jnp.broadcast_to(jnp.tile(hgrn_out_norm, (1, H_B))[..., None], (DEPTH, 256, LANES))
    sink = attn_sink.astype(f32)
    qn = mla_q_norm.reshape(DEPTH, 1, Q_LORA_C)
    kvn = mla_kv_norm.reshape(DEPTH, 1, KV_LORA_C)
    pscale = pool_scale.reshape(DEPTH, 1, 256)
    wo = wb(w_out)
    norms = norm_sub.reshape(DEPTH * 3, 1, D_MODEL)
    ctx_c = (cache_mla_ckv, _pad_rope_cols(cache_mla_krope))

    new_k = new_v = new_st = new_ckv = new_kr = None
    for l in range(DEPTH):
        def ffn(x, grp, j, idx, mix=None, final_g=None):
            return _ffn(x, mods, norms, wg_all, wu_all, wd_all, layer=l, widx=2 * l + idx,
                        j=j, tm=tm, mix=mix, final_g=final_g, **grp)

        xp = ffn(xp, grp_p, 0, 0)
        xs = ffn(xs, grp_s, 0, 0)

        pr, pt = _mixer_in(xp, mods, norms, w_row, w_t, layer=l, tm=tm_mix, **grp_p)
        oa, new_k, new_v = _attn_ctx(pr, sink, None if l == 0 else (new_k, new_v), layer=l, batch=bp, t=tp)
        ob, new_st = _hgrn(pt, lb, ng, mats, layer=l, batch=bp, t=tp, emit_state=True, state_prev=new_st)
        oc, new_ckv, new_kr = _mla(pr, qn, kvn, wq, wk, wv, layer=l, batch=bp, t=tp, tq=tp, emit_cache=True,
                                   cache_prev=None if l == 0 else (new_ckv, new_kr))
        od = _pool(pr, w_pool, pscale, layer=l, batch=bp, t=tp)
        mix_p = (oa, ob, oc, od, wo)

        pr, pt = _mixer_in(xs, mods, norms, w_row, w_t, layer=l, tm=tm_mix, **grp_s)
        oa = _attn_lat(pr, sink, cache_attn_k, cache_attn_v, tab_a, layer=l, batch=bs, t=ts)
        (ob,) = _hgrn(pt, lb, ng, mats, layer=l, batch=bs, t=ts, s0=state_hgrn)
        (oc,) = _mla(pr, qn, kvn, wq, wk, wv, layer=l, batch=bs, t=ts, tq=512, ctx=ctx_c, tab=tab_c)
        od = _pool(pr, w_pool, pscale, layer=l, batch=bs, t=ts)
        mix_s = (oa, ob, oc, od, wo)

        fin = final_norm.reshape(1, D_MODEL) if l == DEPTH - 1 else None
        xp = ffn(xp, grp_p, 2, 1, mix_p, fin)
        xs = ffn(xs, grp_s, 2, 1, mix_s, fin)

    return (xp.reshape(bp, tp, D_MODEL), xs.reshape(bs, ts, D_MODEL), new_k, new_v, new_st, new_ckv, new_kr)
```

```python
import functools

import numpy as np
import jax
import jax.numpy as jnp
from jax import lax
from jax.experimental import pallas as pl
from jax.experimental.pallas import tpu as pltpu

f32 = jnp.float32
bf16 = jnp.bfloat16

D_MODEL = 1024
DEPTH = 2
GRID_W = 64
EPS = 1e-6
LOG2E = 1.4426950408889634
ROPE_THETA = 10000.0
NEG_INF = -1e30
N_MOD = 9
D_FF = 2816
H_A, KV_A, HD_A = 4, 2, 64
G_A = H_A // KV_A
WINDOW_A = 128
BLOCK_A = 128
H_B, DK_B, DV_B = 4, 64, 64
H_C, Q_LORA_C, KV_LORA_C, NOPE_C, ROPE_C, V_C = 4, 256, 128, 64, 32, 64
POOL_WINDOWS = (2, 4, 8, 16)
POOL_CH = 64

LANES = 128
SUBLANES = 8
ROW_REDUCE_CHAINS = 8
VMEM_LIMIT = 56 * 1024 * 1024
HG_TILE = LANES
HG_UNROLL = 2
HG_HALF = HG_TILE // 2
HG_MAX_EXP = 80.0
HG_BLOCK = 16
HG_LEVELS = (16, 32, 64)
ATTN_CTX_SEQS = 2
POOL_CTX_SEQS = 8
ATTN_LAT_BLOCKS = 4
POOL_PAD = 16

PR_W = 1280
PT_W = 1280

_NT = (((1,), (1,)), ((), ()))
_TN = (((0,), (0,)), ((), ()))


def _params(*sem):
    return pltpu.CompilerParams(dimension_semantics=sem, vmem_limit_bytes=VMEM_LIMIT)


def _const_spec(shape, lead=None):
    zeros = (0,) * len(shape)
    if lead is None:
        return pl.BlockSpec(shape, lambda *_: zeros, pipeline_mode=pl.Buffered(1))
    return pl.BlockSpec((None,) + tuple(shape), lambda *_: (lead,) + zeros, pipeline_mode=pl.Buffered(1))


def _dot(a, b):
    return jnp.dot(a, b, preferred_element_type=f32)


def _sigmoid(x):
    return 1.0 / (1.0 + jnp.exp(-x))


def _rms(x, g):
    return x * lax.rsqrt(jnp.mean(x * x, axis=-1, keepdims=True) + EPS) * g


def _mod_kernel(c_ref, w_ref, b_ref, o_ref):
    c = c_ref[...]
    a = (c * _sigmoid(c)).astype(bf16)
    o_ref[0] = _dot(a, w_ref[0].astype(bf16)) + b_ref[0]


def _modulation(cvec, w_ada, b_ada):
    tn = 1536
    n_out = N_MOD * D_MODEL
    return pl.pallas_call(
        _mod_kernel,
        out_shape=jax.ShapeDtypeStruct((DEPTH, cvec.shape[0], n_out), f32),
        grid=(DEPTH, n_out // tn),
        in_specs=[pl.BlockSpec(cvec.shape, lambda l, j: (0, 0)),
                  pl.BlockSpec((1, D_MODEL, tn), lambda l, j: (l, 0, j)),
                  pl.BlockSpec((1, 1, tn), lambda l, j: (l, 0, j))],
        out_specs=pl.BlockSpec((1, cvec.shape[0], tn), lambda l, j: (l, 0, j)),
        compiler_params=_params("arbitrary", "arbitrary"),
        name="modulation",
    )(cvec, w_ada, b_ada.reshape(DEPTH, 1, n_out))


def _cast_kernel(x_ref, o_ref):
    o_ref[...] = x_ref[...].astype(bf16)


def _to_bf16(x, rows):
    r, c = x.shape
    return pl.pallas_call(
        _cast_kernel,
        out_shape=jax.ShapeDtypeStruct((r, c), bf16),
        grid=(r // rows,),
        in_specs=[pl.BlockSpec((rows, c), lambda i: (i, 0))],
        out_specs=pl.BlockSpec((rows, c), lambda i: (i, 0)),
        compiler_params=_params("parallel"),
        name="cast_bf16",
    )(x)


def _ffn_kernel(*refs, j, mix, final):
    it = iter(refs)
    x_ref, m_ref = next(it), next(it)
    mix_refs = [next(it) for _ in range(5)] if mix else None
    g_ref, wg_ref, wu_ref, wd_ref = next(it), next(it), next(it), next(it)
    fg_ref = next(it) if final else None
    o_ref = next(it)
    x = x_ref[...]
    m = m_ref[0]
    if mix:
        wo_ref = mix_refs[4]
        acc = _dot(mix_refs[0][...], wo_ref[0:256, :])
        for p in range(1, 4):
            acc += _dot(mix_refs[p][...], wo_ref[256 * p:256 * (p + 1), :])
        x = x + m[5:6] * acc
    h = (_rms(x, g_ref[...]) * (1.0 + m[3 * j + 1:3 * j + 2]) + m[3 * j:3 * j + 1]).astype(bf16)
    gate = _dot(h, wg_ref[...])
    up = _dot(h, wu_ref[...])
    act = (gate * _sigmoid(gate) * up).astype(bf16)
    y = x + (0.5 * m[3 * j + 2:3 * j + 3]) * _dot(act, wd_ref[...])
    if final:
        y = _rms(y, fg_ref[...])
    o_ref[...] = y


def _mod_spec(layer, base, rpm, tm):
    return pl.BlockSpec((None, 1, N_MOD, D_MODEL), lambda i: (layer, base + (i * tm) // rpm, 0, 0))


def _ffn(x, mods, norms, wg, wu, wd, *, layer, widx, j, base, rpm, tm, mix=None, final_g=None):
    n = x.shape[0]

    def wspec(r, c):
        return _const_spec((r, c), widx)

    in_specs = [pl.BlockSpec((tm, D_MODEL), lambda i: (i, 0)), _mod_spec(layer, base, rpm, tm)]
    args = [x, mods]
    if mix is not None:
        in_specs += [pl.BlockSpec((tm, 256), lambda i: (i, 0))] * 4 + [_const_spec((D_MODEL, D_MODEL), layer)]
        args += list(mix)
    in_specs += [_const_spec((1, D_MODEL), 3 * layer + j), wspec(D_MODEL, D_FF), wspec(D_MODEL, D_FF),
                 wspec(D_FF, D_MODEL)]
    args += [norms, wg, wu, wd]
    if final_g is not None:
        in_specs.append(_const_spec((1, D_MODEL)))
        args.append(final_g)
    return pl.pallas_call(
        functools.partial(_ffn_kernel, j=j, mix=mix is not None, final=final_g is not None),
        out_shape=jax.ShapeDtypeStruct((n, D_MODEL), f32),
        grid=(n // tm,),
        in_specs=in_specs,
        out_specs=pl.BlockSpec((tm, D_MODEL), lambda i: (i, 0)),
        compiler_params=_params("parallel"),
        name="ffn",
    )(*args)


def _mixin_kernel(x_ref, m_ref, g_ref, wr_ref, wt_ref, pr_ref, pt_ref):
    m = m_ref[0]
    h = (_rms(x_ref[...], g_ref[...]) * (1.0 + m[4:5]) + m[3:4]).astype(bf16)
    pr_ref[...] = _dot(h, wr_ref[...])
    pt_ref[...] = lax.dot_general(wt_ref[...], h, _NT, preferred_element_type=f32)


def _mixer_in(x, mods, norms, w_row, w_t, *, layer, base, rpm, tm):
    n = x.shape[0]
    return pl.pallas_call(
        _mixin_kernel,
        out_shape=(jax.ShapeDtypeStruct((n, PR_W), f32), jax.ShapeDtypeStruct((PT_W, n), f32)),
        grid=(n // tm,),
        in_specs=[pl.BlockSpec((tm, D_MODEL), lambda i: (i, 0)), _mod_spec(layer, base, rpm, tm),
                  _const_spec((1, D_MODEL), 3 * layer + 1), _const_spec((D_MODEL, PR_W), layer),
                  _const_spec((PT_W, D_MODEL), layer)],
        out_specs=(pl.BlockSpec((tm, PR_W), lambda i: (i, 0)), pl.BlockSpec((PT_W, tm), lambda i: (0, i))),
        compiler_params=_params("parallel"),
        name="mixer_in",
    )(x, mods, norms, w_row, w_t)


def _rope(x, tab_ref, rows, lanes, shift):
    width = x.shape[-1]
    c = tab_ref[0, rows, lanes]
    sa = tab_ref[1, rows, lanes]
    sb = tab_ref[2, rows, lanes]
    return x * c + pltpu.roll(x, shift, axis=1) * sa + pltpu.roll(x, width - shift, axis=1) * sb


def _softmax_parts(s_list, sink):
    sink2 = sink * LOG2E
    mx = functools.reduce(jnp.maximum, [jnp.max(s, axis=-1, keepdims=True) for s in s_list])
    mx = jnp.maximum(mx, sink2)
    ps = [jnp.exp2(s - mx) for s in s_list]
    den = functools.reduce(lambda a, b: a + b, [jnp.sum(p, axis=-1, keepdims=True) for p in ps])
    return ps, 1.0 / (den + jnp.exp2(sink2 - mx))


def _kv_per_query_head(x):
    return jnp.concatenate([x[:, HD_A * (h // G_A):HD_A * (h // G_A + 1)] for h in range(H_A)], axis=1)


def _stack_heads(q):
    head = lax.broadcasted_iota(jnp.int32, (1, H_A * HD_A), 1) // HD_A
    return jnp.concatenate([jnp.where(head == h, q, 0.0) for h in range(H_A)], axis=0)


def _pick_heads(o, r):
    head = lax.broadcasted_iota(jnp.int32, (1, H_A * HD_A), 1) // HD_A
    out = o[0:r]
    for h in range(1, H_A):
        out = jnp.where(head == h, o[h * r:(h + 1) * r], out)
    return out


def _sink_rows(sink_ref, layer, r):
    row = lax.broadcasted_iota(jnp.int32, (H_A * r, 1), 0)
    out = jnp.full((H_A * r, 1), sink_ref[layer, 0], f32)
    for h in range(1, H_A):
        out = jnp.where(row >= h * r, sink_ref[layer, h], out)
    return out


def _attn_ctx_kernel(*refs, t, n_prev, layer):
    if n_prev:
        sink_ref, q_ref, kv_ref, kp_ref, vp_ref, o_ref, ko_ref, vo_ref = refs
        ko_ref[:, 0:n_prev] = kp_ref[...]
        vo_ref[:, 0:n_prev] = vp_ref[...]
    else:
        sink_ref, q_ref, kv_ref, o_ref, ko_ref, vo_ref = refs
    kvw = KV_A * HD_A
    sink = _sink_rows(sink_ref, layer, t)
    for u in range(ATTN_CTX_SEQS):
        rows = slice(u * t, (u + 1) * t)
        kv = kv_ref[rows, :]
        for j in range(KV_A):
            ko_ref[u, n_prev, j] = kv[:, HD_A * j:HD_A * (j + 1)]
            vo_ref[u, n_prev, j] = kv[:, kvw + HD_A * j:kvw + HD_A * (j + 1)]
        kx = _kv_per_query_head(kv[:, 0:kvw]).astype(bf16)
        vx = _kv_per_query_head(kv[:, kvw:2 * kvw]).astype(bf16)
        q4 = _stack_heads(q_ref[rows, :] * (HD_A ** -0.5 * LOG2E)).astype(bf16)
        s = lax.dot_general(q4, kx, _NT, preferred_element_type=f32)
        (p,), inv = _softmax_parts([s], sink)
        o_ref[rows, :] = _pick_heads(_dot(p.astype(bf16), vx) * inv, t).astype(o_ref.dtype)


def _cache_spec(n_seq, n_layers, *dims):
    return pl.BlockSpec((n_seq, n_layers) + dims, lambda b: (b,) + (0,) * (1 + len(dims)))


def _attn_ctx(pr, sink, prev, *, layer, batch, t):
    n_prev = 0 if prev is None else prev[0].shape[1]
    ns = ATTN_CTX_SEQS
    kvshape = jax.ShapeDtypeStruct((batch, n_prev + 1, KV_A, t, HD_A), f32)
    in_specs = [pl.BlockSpec(memory_space=pltpu.SMEM),
                pl.BlockSpec((ns * t, 256), lambda b: (b, 0)), pl.BlockSpec((ns * t, 256), lambda b: (b, 1))]
    args = [sink, pr, pr]
    if n_prev:
        in_specs += [_cache_spec(ns, n_prev, KV_A, t, HD_A)] * 2
        args += list(prev)
    return pl.pallas_call(
        functools.partial(_attn_ctx_kernel, t=t, n_prev=n_prev, layer=layer),
        out_shape=(jax.ShapeDtypeStruct((batch * t, 256), bf16), kvshape, kvshape),
        grid=(batch // ns,),
        in_specs=in_specs,
        out_specs=(pl.BlockSpec((ns * t, 256), lambda b: (b, 0)),
                   _cache_spec(ns, n_prev + 1, KV_A, t, HD_A), _cache_spec(ns, n_prev + 1, KV_A, t, HD_A)),
        compiler_params=_params("parallel"),
        name="attn_ctx",
    )(*args)


def _attn_lat_kernel(sink_ref, q_ref, kv_ref, kc_ref, vc_ref, tab_ref, o_ref, kpad, vpad, kctx, vctx, *, t, layer):
    n = pl.program_id(1)
    kvw = KV_A * HD_A
    hw = H_A * HD_A

    @pl.when(n == 0)
    def _():
        zeros = jnp.zeros((BLOCK_A, hw), bf16)
        kv = kv_ref[...]
        k = _rope(kv[:, 0:kvw], tab_ref, slice(None), slice(0, kvw), HD_A // 2)
        kpad[0:BLOCK_A, :] = zeros
        vpad[0:BLOCK_A, :] = zeros
        kpad[BLOCK_A:BLOCK_A + t, :] = _kv_per_query_head(k).astype(bf16)
        vpad[BLOCK_A:BLOCK_A + t, :] = _kv_per_query_head(kv[:, kvw:2 * kvw]).astype(bf16)
        kpad[BLOCK_A + t:2 * BLOCK_A + t, :] = zeros
        vpad[BLOCK_A + t:2 * BLOCK_A + t, :] = zeros
        kctx[...] = jnp.concatenate([kc_ref[0, h // G_A] for h in range(H_A)], axis=1).astype(bf16)
        vctx[...] = jnp.concatenate([vc_ref[0, h // G_A] for h in range(H_A)], axis=1).astype(bf16)

    rows = H_A * BLOCK_A
    rr = lax.broadcasted_iota(jnp.int32, (rows, 1), 0) % BLOCK_A
    cidx = lax.broadcasted_iota(jnp.int32, (rows, 3 * BLOCK_A), 1)
    sink = _sink_rows(sink_ref, layer, BLOCK_A)
    for u in range(ATTN_LAT_BLOCKS):
        blk = n * ATTN_LAT_BLOCKS + u
        row0 = pl.multiple_of(blk * BLOCK_A, BLOCK_A)
        q = _rope(q_ref[u * BLOCK_A:(u + 1) * BLOCK_A, :], tab_ref, pl.ds(row0, BLOCK_A), slice(None), HD_A // 2)
        q4 = _stack_heads(q * (HD_A ** -0.5 * LOG2E)).astype(bf16)
        kband = kpad[pl.ds(row0, 3 * BLOCK_A), :]
        vband = vpad[pl.ds(row0, 3 * BLOCK_A), :]
        s_loc = lax.dot_general(q4, kband, _NT, preferred_element_type=f32)
        s_ctx = lax.dot_general(q4, kctx[...], _NT, preferred_element_type=f32)
        lo = jnp.maximum(rr + (BLOCK_A - WINDOW_A), jnp.maximum(0, (1 - blk) * BLOCK_A))
        hi = jnp.minimum(rr + (BLOCK_A + WINDOW_A), jnp.minimum(3 * BLOCK_A, t - (blk - 1) * BLOCK_A) - 1)
        s_loc = jnp.where((cidx >= lo) & (cidx <= hi), s_loc, NEG_INF)
        (p_loc, p_ctx), inv = _softmax_parts([s_loc, s_ctx], sink)
        o = (_dot(p_loc.astype(bf16), vband) + _dot(p_ctx.astype(bf16), vctx[...])) * inv
        o_ref[u * BLOCK_A:(u + 1) * BLOCK_A, :] = _pick_heads(o, BLOCK_A).astype(o_ref.dtype)


def _attn_lat(pr, sink, kc, vc, tab, *, layer, batch, t):
    rows = ATTN_LAT_BLOCKS * BLOCK_A
    nb = t // rows
    lc = kc.shape[3]
    cspec = pl.BlockSpec((1, None, KV_A, lc, HD_A), lambda b, n: (b, layer, 0, 0, 0))
    return pl.pallas_call(
        functools.partial(_attn_lat_kernel, t=t, layer=layer),
        out_shape=jax.ShapeDtypeStruct((batch * t, 256), bf16),
        grid=(batch, nb),
        in_specs=[pl.BlockSpec(memory_space=pltpu.SMEM),
                  pl.BlockSpec((rows, 256), lambda b, n: (b * nb + n, 0)),
                  pl.BlockSpec((t, 256), lambda b, n: (b, 1)),
                  cspec, cspec,
                  pl.BlockSpec((3, t, 256), lambda b, n: (0, 0, 0))],
        out_specs=pl.BlockSpec((rows, 256), lambda b, n: (b * nb + n, 0)),
        scratch_shapes=[pltpu.VMEM((t + 2 * BLOCK_A, H_A * HD_A), bf16),
                        pltpu.VMEM((t + 2 * BLOCK_A, H_A * HD_A), bf16),
                        pltpu.VMEM((lc, H_A * HD_A), bf16), pltpu.VMEM((lc, H_A * HD_A), bf16)],
        compiler_params=_params("parallel", "arbitrary"),
        name="attn_lat",
    )(sink, pr, pr, kc, vc, tab)


def _reduce_rows(x, reduce_fn):
    r, c = x.shape
    g = ROW_REDUCE_CHAINS
    if r % (g * SUBLANES):
        return reduce_fn(x, axis=0, keepdims=True)
    part = reduce_fn(x.reshape(g, r // (g * SUBLANES), SUBLANES, c), axis=1)
    return reduce_fn(reduce_fn(part, axis=0), axis=0, keepdims=True)


def _mla_kernel(*refs, t, lc, tq, rope, emit_cache, n_prev):
    it = iter(refs)
    cq_ref, ckv_ref, kr_ref = next(it), next(it), next(it)
    cc_ref = kc_ref = tab_ref = None
    if lc:
        cc_ref, kc_ref = next(it), next(it)
    if rope:
        tab_ref = next(it)
    qn_ref, kvn_ref, wq_ref, wk_ref, wv_ref = next(it), next(it), next(it), next(it), next(it)
    ckvp_ref, krp_ref = (next(it), next(it)) if n_prev else (None, None)
    o_ref = next(it)
    ckv_out_ref, kr_out_ref = (next(it), next(it)) if emit_cache else (None, None)
    kbuf, vbuf = next(it), next(it)
    i = pl.program_id(1)
    hw = LANES

    @pl.when(i == 0)
    def _():
        ckv = _rms(ckv_ref[...], kvn_ref[...])
        kr = kr_ref[...]
        if emit_cache:
            if n_prev:
                ckv_out_ref[0, 0:n_prev] = ckvp_ref[0]
                kr_out_ref[0, 0:n_prev] = krp_ref[0]
            ckv_out_ref[0, n_prev] = ckv
            kr_out_ref[0, n_prev] = kr[:, NOPE_C:NOPE_C + ROPE_C]
        ckv_b = ckv.astype(bf16)
        if rope:
            kr = _rope(kr, tab_ref, slice(None), slice(None), ROPE_C // 2)
        kk = _dot(ckv_b, wk_ref[...])
        for h in range(H_C):
            kbuf[h, 0:t, :] = (kk[:, hw * h:hw * (h + 1)] + kr).astype(bf16)
        vbuf[:, 0:t] = lax.dot_general(wv_ref[...], ckv_b, _NT, preferred_element_type=f32).astype(bf16)
        if lc:
            cc_b = cc_ref[0].astype(bf16)
            kkc = _dot(cc_b, wk_ref[...])
            krc = kc_ref[0]
            for h in range(H_C):
                kbuf[h, t:t + lc, :] = (kkc[:, hw * h:hw * (h + 1)] + krc).astype(bf16)
            vbuf[:, t:t + lc] = lax.dot_general(wv_ref[...], cc_b, _NT, preferred_element_type=f32).astype(bf16)

    cqn = _rms(cq_ref[...], qn_ref[...]).astype(bf16)
    qf = _dot(cqn, wq_ref[...])
    c2 = (NOPE_C + ROPE_C) ** -0.5 * LOG2E
    row0 = pl.multiple_of(i * tq, tq)
    n_keys = t + lc
    half = n_keys // 2

    def scores(h):
        qh = qf[:, hw * h:hw * (h + 1)]
        if rope:
            qh = _rope(qh, tab_ref, pl.ds(row0, tq), slice(None), ROPE_C // 2)
        qb = (qh * c2).astype(bf16)
        return [lax.dot_general(kbuf[h, c * half:(c + 1) * half, :], qb, _NT, preferred_element_type=f32)
                for c in range(2)]

    def probs(s):
        m = jnp.maximum(_reduce_rows(s[0], jnp.max), _reduce_rows(s[1], jnp.max))
        p = [jnp.exp2(x - m) for x in s]
        inv = 1.0 / (_reduce_rows(p[0], jnp.sum) + _reduce_rows(p[1], jnp.sum))
        return [x.astype(bf16) for x in p], inv

    def values(h, p, inv):
        rows = slice(V_C * h, V_C * (h + 1))
        return (_dot(vbuf[rows, 0:half], p[0]) + _dot(vbuf[rows, half:2 * half], p[1])) * inv

    s_next = scores(0)
    outs = []
    for h in range(H_C):
        s_cur = s_next
        if h + 1 < H_C:
            s_next = scores(h + 1)
        p, inv = probs(s_cur)
        outs.append(values(h, p, inv))
    o_ref[...] = jnp.concatenate(outs, axis=0).T.astype(o_ref.dtype)


def _mla(pr, qn, kvn, wq, wk, wv, *, layer, batch, t, tq, ctx=None, tab=None, emit_cache=False, cache_prev=None):
    nq = t // tq
    lc = 0 if ctx is None else ctx[0].shape[2]
    n_prev = 0 if cache_prev is None else cache_prev[0].shape[1]
    in_specs = [pl.BlockSpec((tq, 256), lambda b, i: (b * nq + i, 2)),
                pl.BlockSpec((t, LANES), lambda b, i: (b, 6)),
                pl.BlockSpec((t, LANES), lambda b, i: (b, 7))]
    args = [pr, pr, pr]
    if lc:
        in_specs += [pl.BlockSpec((1, None, lc, LANES), lambda b, i: (b, layer, 0, 0))] * 2
        args += list(ctx)
    if tab is not None:
        in_specs.append(pl.BlockSpec((3, t, LANES), lambda b, i: (0, 0, 0)))
        args.append(tab)
    in_specs += [_const_spec(a.shape[1:], layer) for a in (qn, kvn, wq, wk, wv)]
    args += [qn, kvn, wq, wk, wv]

    def cspec(n_layers, width):
        return pl.BlockSpec((1, n_layers, t, width), lambda b, i: (b, 0, 0, 0))

    if n_prev:
        in_specs += [cspec(n_prev, KV_LORA_C), cspec(n_prev, ROPE_C)]
        args += list(cache_prev)
    out_shape = [jax.ShapeDtypeStruct((batch * t, 256), bf16)]
    out_specs = [pl.BlockSpec((tq, 256), lambda b, i: (b * nq + i, 0))]
    if emit_cache:
        out_shape += [jax.ShapeDtypeStruct((batch, n_prev + 1, t, KV_LORA_C), f32),
                      jax.ShapeDtypeStruct((batch, n_prev + 1, t, ROPE_C), f32)]
        out_specs += [cspec(n_prev + 1, KV_LORA_C), cspec(n_prev + 1, ROPE_C)]
    return pl.pallas_call(
        functools.partial(_mla_kernel, t=t, lc=lc, tq=tq, rope=tab is not None, emit_cache=emit_cache,
                          n_prev=n_prev),
        out_shape=tuple(out_shape),
        grid=(batch, nq),
        in_specs=in_specs,
        out_specs=tuple(out_specs),
        scratch_shapes=[pltpu.VMEM((H_C, t + lc, LANES), bf16), pltpu.VMEM((H_C * V_C, t + lc), bf16)],
        compiler_params=_params("parallel", "arbitrary"),
        name="mla",
    )(*args)


def _pool_counts(t):
    pos = np.arange(t)[:, None]
    half = np.repeat(np.asarray(POOL_WINDOWS) // 2, POOL_CH)[None, :]
    return (np.minimum(pos + half, t) - np.maximum(pos - half, 0)).astype(np.float32)


def _pool_kernel(x_ref, cnt_ref, w_ref, sc_ref, o_ref, buf_a, buf_b, buf_c, *, t, seqs):
    for u in range(seqs):
        _pool_sequence(x_ref, cnt_ref, w_ref, sc_ref, o_ref, buf_a, buf_b, buf_c, slice(u * t, (u + 1) * t), t)


def _pool_sequence(x_ref, cnt_ref, w_ref, sc_ref, o_ref, buf_a, buf_b, buf_c, seq, t):
    x = x_ref[seq, :]
    pad, edge = POOL_PAD, SUBLANES
    n = t + 2 * pad
    m = n - 2 * edge
    zeros = jnp.zeros((pad, 256), f32)
    buf_a[0:pad, :] = zeros
    buf_a[pad:pad + t, :] = x
    buf_a[pad + t:n, :] = zeros
    for buf in (buf_b, buf_c):
        buf[0:edge, :] = zeros[0:edge]
        buf[n - edge:n, :] = zeros[0:edge]
    inner = slice(edge, edge + m)
    buf_b[inner, :] = buf_a[pl.ds(edge - 1, m), :] + buf_a[pl.ds(edge, m), :]
    buf_c[inner, :] = buf_b[pl.ds(edge - 1, m), :] + buf_b[pl.ds(edge + 1, m), :]
    lo, hi = slice(0, LANES), slice(LANES, 2 * LANES)
    buf_a[inner, hi] = buf_c[pl.ds(edge - 2, m), hi] + buf_c[pl.ds(edge + 2, m), hi]
    s16 = buf_a[pl.ds(pad - 4, t), hi] + buf_a[pl.ds(pad + 4, t), hi]
    first = lax.broadcasted_iota(jnp.int32, (1, LANES), 1) < POOL_CH
    rows = slice(pad, pad + t)
    win_sum = jnp.concatenate([jnp.where(first, buf_b[rows, lo], buf_c[rows, lo]),
                               jnp.where(first, buf_a[rows, hi], s16)], axis=1)
    mean = win_sum / cnt_ref[...]
    o_ref[seq, :] = (_dot((mean - x).astype(bf16), w_ref[...]) * sc_ref[...]).astype(o_ref.dtype)


def _pool(pr, w_bd, scale, *, layer, batch, t, seqs=1):
    return pl.pallas_call(
        functools.partial(_pool_kernel, t=t, seqs=seqs),
        out_shape=jax.ShapeDtypeStruct((batch * t, 256), bf16),
        grid=(batch // seqs,),
        in_specs=[pl.BlockSpec((seqs * t, 256), lambda b: (b, 4)), _const_spec((t, 256)),
                  _const_spec((256, 256), layer), _const_spec((1, 256), layer)],
        out_specs=pl.BlockSpec((seqs * t, 256), lambda b: (b, 0)),
        scratch_shapes=[pltpu.VMEM((t + 2 * POOL_PAD, 256), f32)] * 3,
        compiler_params=_params("parallel"),
        name="pool",
    )(pr, jnp.asarray(_pool_counts(t)), w_bd, scale)


def _hgrn_mats():
    n = HG_TILE
    s = np.arange(n)[:, None]
    t = np.arange(n)[None, :]
    same_half = s // HG_HALF == t // HG_HALF
    mid = (t // HG_HALF) * HG_HALF + HG_HALF // 2 - 1
    rel_mid = (same_half & (s > mid) & (s <= t)).astype(np.float32) - (same_half & (s <= mid) & (s > t))
    mats = [rel_mid, same_half & (s <= t), same_half & (s > t), s <= t, s > t,
            (s // HG_BLOCK == t // HG_BLOCK) & (s <= t)]
    for m in HG_LEVELS:
        same_pair = s // (2 * m) == t // (2 * m)
        right_t = (t // m) % 2 == 1
        right_s = (s // m) % 2 == 1
        q_side = same_pair & right_t & right_s & (s <= t)
        k_side = same_pair & ~right_t & ~right_s & (s > t)
        mats.append(q_side | k_side)
    fwd = np.concatenate([m.astype(np.float32) for m in mats], axis=1)
    bwd = np.concatenate([m[::-1, ::-1].astype(np.float32) for m in mats], axis=1)
    both = np.stack([fwd, bwd])
    return np.concatenate([both, both], axis=1)


def _hgrn_exact_intra(q, k, v, vb, hl, mats, reverse):
    n = HG_TILE
    e_all = _dot(hl, mats)
    beta = e_all[:, 0:n]
    lane = lax.broadcasted_iota(jnp.int32, (1, n), 1)
    pos = lane % HG_BLOCK
    v3 = v.reshape(H_B, DV_B, n)

    o3 = jnp.sum((q * k).reshape(H_B, DK_B, n), axis=1, keepdims=True) * v3
    for d in range(1, HG_BLOCK):
        shift = (n - d) if reverse else d
        valid = (pos <= HG_BLOCK - 1 - d) if reverse else (pos >= d)
        kr = pltpu.roll(k, shift, axis=1)
        br = pltpu.roll(beta, shift, axis=1)
        vr = pltpu.roll(v, shift, axis=1)
        e = jnp.exp(jnp.where(valid, beta - br, NEG_INF))
        sc = jnp.sum((q * kr * e).reshape(H_B, DK_B, n), axis=1, keepdims=True)
        o3 = o3 + sc * vr.reshape(H_B, DV_B, n)

    srow = lax.broadcasted_iota(jnp.int32, (n, n), 0)
    tcol = lax.broadcasted_iota(jnp.int32, (n, n), 1)
    scores = [jnp.zeros((n, n), f32) for _ in range(H_B)]
    for li, m in enumerate(HG_LEVELS):
        ex = jnp.exp(e_all[:, (li + 1) * n:(li + 2) * n])
        is_q = ((lane // m) % 2) == (0 if reverse else 1)
        qs = jnp.where(is_q, q * ex, 0.0).astype(bf16)
        ks = jnp.where(is_q, 0.0, k * ex).astype(bf16)
        pair = (srow // (2 * m)) == (tcol // (2 * m))
        for h in range(H_B):
            a = lax.dot_general(ks[DK_B * h:DK_B * (h + 1)], qs[DK_B * h:DK_B * (h + 1)], _TN,
                                preferred_element_type=f32)
            scores[h] = scores[h] + jnp.where(pair, a, 0.0)

    return [o3[h] + _dot(vb[DV_B * h:DV_B * (h + 1)], scores[h].astype(bf16)) for h in range(H_B)]


def _hgrn_common(q, v, fpre, lb, mats_ref, sidx, st_ref):
    n = HG_TILE
    f = lb + (1.0 - lb) * _sigmoid(fpre)
    k = 1.0 - f
    logf = jnp.log(f)
    hi = logf.astype(bf16)
    hl = jnp.concatenate([hi, (logf - hi.astype(f32)).astype(bf16)], axis=1)
    ex = _dot(hl, mats_ref[sidx, :, 0:5 * n])
    ones = jnp.ones((8, 2 * n), bf16)
    g_row = lax.dot_general(ones, hl, _NT, preferred_element_type=f32)[0:1]
    d_mid = ex[:, 0:n]
    small = jnp.max(jnp.abs(d_mid)) <= HG_MAX_EXP
    q_in = (q * jnp.exp(ex[:, 3 * n:4 * n])).astype(bf16)
    k_out = (k * jnp.exp(ex[:, 4 * n:5 * n])).astype(bf16)
    vb = v.astype(bf16)
    o_state = []
    for h in range(H_B):
        rows = slice(DK_B * h, DK_B * (h + 1))
        st = st_ref[sidx, rows, :]
        o_state.append(_dot(st.astype(bf16), q_in[rows]))
        upd = lax.dot_general(vb[rows], k_out[rows], _NT, preferred_element_type=f32)
        st_ref[sidx, rows, :] = st * jnp.exp(g_row[:, rows]) + upd
    return dict(q=q, k=k, v=v, vb=vb, hl=hl, ex=ex, o_state=o_state, small=small)


def _hgrn_fast_intra(c, reverse):
    n = HG_TILE
    q, k, ex = c["q"], c["k"], c["ex"]
    lane = lax.broadcasted_iota(jnp.int32, (1, n), 1)
    in0 = (lane // HG_HALF) == 0
    q_half = in0 if reverse else jnp.logical_not(in0)
    d_mid = ex[:, 0:n]
    qd = q * jnp.exp(d_mid)
    kd = k * jnp.exp(-d_mid)
    ql = jnp.where(q_half, q * jnp.exp(ex[:, n:2 * n]), 0.0)
    kl = jnp.where(q_half, 0.0, k * jnp.exp(ex[:, 2 * n:3 * n]))
    parts_q = [jnp.where(in0, qd, 0.0).astype(bf16), jnp.where(in0, 0.0, qd).astype(bf16), ql.astype(bf16)]
    parts_k = [jnp.where(in0, kd, 0.0).astype(bf16), jnp.where(in0, 0.0, kd).astype(bf16), kl.astype(bf16)]
    srow = lax.broadcasted_iota(jnp.int32, (n, n), 0)
    tcol = lax.broadcasted_iota(jnp.int32, (n, n), 1)
    causal = (srow >= tcol) if reverse else (srow <= tcol)
    scores = []
    for h in range(H_B):
        rows = slice(DK_B * h, DK_B * (h + 1))
        k3 = jnp.concatenate([p[rows] for p in parts_k], axis=0)
        q3 = jnp.concatenate([p[rows] for p in parts_q], axis=0)
        scores.append(lax.dot_general(k3, q3, _TN, preferred_element_type=f32))
    scores = [jnp.where(causal, a, 0.0).astype(bf16) for a in scores]
    outs = [c["o_state"][h] + _dot(c["vb"][DV_B * h:DV_B * (h + 1)], scores[h]) for h in range(H_B)]
    return jnp.concatenate(outs, axis=0)


def _hgrn_exact(c, mats, reverse):
    intra = _hgrn_exact_intra(c["q"], c["k"], c["v"], c["vb"], c["hl"], mats, reverse)
    return jnp.concatenate([c["o_state"][h] + intra[h] for h in range(H_B)], axis=0)


def _hgrn_kernel(*refs, t, has_s0, emit_state, n_prev):
    it = iter(refs)
    q_ref, i_ref, ff_ref, fb_ref, g_ref, lb_ref, ng_ref, mats_ref = [next(it) for _ in range(8)]
    s0_ref = next(it) if has_s0 else None
    sprev_ref = next(it) if n_prev else None
    o_ref = next(it)
    sfin_ref = next(it) if emit_state else None
    st_ref, oacc_f, oacc_b = next(it), next(it), next(it)
    n = HG_TILE
    nt = t // n
    for d in range(2):
        for h in range(H_B):
            rows = slice(DV_B * h, DV_B * (h + 1))
            st_ref[d, rows, :] = s0_ref[0, d, h].T if has_s0 else jnp.zeros((DV_B, DK_B), f32)

    def scan(i, c):
        work = []
        small = None
        for u in range(HG_UNROLL):
            j = i * HG_UNROLL + u
            cols_f = pl.ds(pl.multiple_of(j * n, n), n)
            cols_b = pl.ds(pl.multiple_of((nt - 1 - j) * n, n), n)
            cf = _hgrn_common(q_ref[:, cols_f], i_ref[:, cols_f], ff_ref[:, cols_f], lb_ref[0], mats_ref, 0, st_ref)
            cb = _hgrn_common(q_ref[:, cols_b], i_ref[:, cols_b], fb_ref[:, cols_b], lb_ref[1], mats_ref, 1, st_ref)
            work.append((cols_f, cf, cols_b, cb))
            both = jnp.logical_and(cf["small"], cb["small"])
            small = both if small is None else jnp.logical_and(small, both)
        for cols_f, cf, cols_b, cb in work:
            oacc_f[:, cols_f] = _hgrn_fast_intra(cf, False)
            oacc_b[:, cols_b] = _hgrn_fast_intra(cb, True)

        @pl.when(jnp.logical_not(small))
        def _():
            for cols_f, cf, cols_b, cb in work:
                oacc_f[:, cols_f] = _hgrn_exact(cf, mats_ref[0, :, 5 * n:9 * n], False)
                oacc_b[:, cols_b] = _hgrn_exact(cb, mats_ref[1, :, 5 * n:9 * n], True)

        return c

    lax.fori_loop(0, nt // HG_UNROLL, scan, 0)

    def finish(i, c):
        off = pl.multiple_of(i * n, n)
        cols = pl.ds(off, n)
        o = oacc_f[:, cols] + oacc_b[:, cols]
        o3 = o.reshape(H_B, DV_B, n)
        y = o3 * lax.rsqrt(jnp.mean(o3 * o3, axis=1, keepdims=True) + EPS)
        gate = g_ref[:, cols]
        y = y.reshape(H_B * DV_B, n) * ng_ref[...] * (gate * _sigmoid(gate))
        o_ref[pl.ds(off, n), :] = y.T.astype(o_ref.dtype)
        return c

    lax.fori_loop(0, nt, finish, 0)
    if emit_state:
        if n_prev:
            sfin_ref[0, 0:n_prev] = sprev_ref[0]
        for d in range(2):
            for h in range(H_B):
                sfin_ref[0, n_prev, d, h] = st_ref[d, DV_B * h:DV_B * (h + 1), :].T


def _hgrn(pt, lb, ng, mats, *, batch, t, s0=None, layer=0, emit_state=False, state_prev=None):
    def row(r):
        return pl.BlockSpec((256, t), lambda b: (r, b))

    def sspec(n_layers):
        return pl.BlockSpec((1, n_layers, 2, H_B, DK_B, DV_B), lambda b: (b, 0, 0, 0, 0, 0))

    n_prev = 0 if state_prev is None else state_prev.shape[1]
    in_specs = [row(0), row(1), row(2), row(3), row(4),
                _const_spec((2, 256, LANES), layer), _const_spec((256, LANES), layer), _const_spec(mats.shape)]
    args = [pt, pt, pt, pt, pt, lb, ng, mats]
    if s0 is not None:
        in_specs.append(pl.BlockSpec((1, None, 2, H_B, DK_B, DV_B), lambda b: (b, layer, 0, 0, 0, 0)))
        args.append(s0)
    if n_prev:
        in_specs.append(sspec(n_prev))
        args.append(state_prev)
    out_shape = [jax.ShapeDtypeStruct((batch * t, 256), bf16)]
    out_specs = [pl.BlockSpec((t, 256), lambda b: (b, 0))]
    if emit_state:
        out_shape.append(jax.ShapeDtypeStruct((batch, n_prev + 1, 2, H_B, DK_B, DV_B), f32))
        out_specs.append(sspec(n_prev + 1))
    return pl.pallas_call(
        functools.partial(_hgrn_kernel, t=t, has_s0=s0 is not None, emit_state=emit_state, n_prev=n_prev),
        out_shape=tuple(out_shape),
        grid=(batch,),
        in_specs=in_specs,
        out_specs=tuple(out_specs),
        scratch_shapes=[pltpu.VMEM((2, H_B * DV_B, DK_B), f32), pltpu.VMEM((256, t), f32),
                        pltpu.VMEM((256, t), f32)],
        compiler_params=_params("parallel"),
        name="hgrn",
    )(*args)


def _rope_tables(t):
    def cos_sin(dim):
        n_freq = dim // 4
        inv = ROPE_THETA ** (-jnp.arange(n_freq, dtype=f32) / n_freq)
        rows = t // GRID_W
        row_id = jnp.repeat(jnp.arange(rows, dtype=f32), GRID_W)
        col_id = jnp.tile(jnp.arange(GRID_W, dtype=f32), rows)
        ang = jnp.concatenate([row_id[:, None] * inv, col_id[:, None] * inv], axis=-1)
        return jnp.cos(ang), jnp.sin(ang)

    cos, sin = cos_sin(HD_A)
    zero = jnp.zeros_like(sin)
    tab_a = jnp.stack([jnp.tile(jnp.concatenate([cos, cos], -1), (1, H_A)),
                       jnp.tile(jnp.concatenate([zero, sin], -1), (1, H_A)),
                       jnp.tile(jnp.concatenate([-sin, zero], -1), (1, H_A))])
    cos, sin = cos_sin(ROPE_C)
    zero = jnp.zeros_like(sin)
    one64 = jnp.ones((t, NOPE_C), f32)
    zero64 = jnp.zeros((t, NOPE_C), f32)
    pad32 = jnp.zeros((t, LANES - NOPE_C - ROPE_C), f32)
    tab_c = jnp.stack([jnp.concatenate([one64, cos, cos, pad32 + 1.0], -1),
                       jnp.concatenate([zero64, zero, sin, pad32], -1),
                       jnp.concatenate([zero64, -sin, zero, pad32], -1)])
    return tab_a, tab_c


def _pad_rope_cols(x):
    pads = [(0, 0)] * (x.ndim - 1) + [(NOPE_C, LANES - NOPE_C - ROPE_C)]
    return jnp.pad(x, pads)


def kernel(x_prompt, x_sample, c, c_ctx, cache_attn_k, cache_attn_v, state_hgrn, cache_mla_ckv, cache_mla_krope,
           w_ada, b_ada, norm_sub, w_ffn_gate, w_ffn_up, w_ffn_down, w_in, w_out, attn_sink, hgrn_lb_logits,
           hgrn_out_norm, mla_q_norm, mla_kv_norm, mla_w_uq, mla_w_ukv, pool_w, pool_scale, final_norm):
    bp, tp, _ = x_prompt.shape
    bs, ts, _ = x_sample.shape
    np_, ns = bp * tp, bs * ts
    xp = x_prompt.reshape(np_, D_MODEL)
    xs = x_sample.reshape(ns, D_MODEL)

    n_vec = 1 + bs
    cvec = jnp.zeros((-(-n_vec // 8) * 8, D_MODEL), f32).at[0].set(c_ctx).at[1:n_vec].set(c)
    mods = _modulation(cvec, w_ada, b_ada).reshape(DEPTH, cvec.shape[0], N_MOD, D_MODEL)

    p_lb = jax.nn.softmax(hgrn_lb_logits.astype(f32), axis=0)
    lb_all = jnp.cumsum(p_lb, axis=0) - p_lb[0:1]
    tab_a, tab_c = _rope_tables(ts)
    mats = jnp.asarray(_hgrn_mats(), bf16)
    n_ffn = DEPTH * 2
    wg_all = _to_bf16(w_ffn_gate.reshape(n_ffn * D_MODEL, D_FF), 512).reshape(n_ffn, D_MODEL, D_FF)
    wu_all = _to_bf16(w_ffn_up.reshape(n_ffn * D_MODEL, D_FF), 512).reshape(n_ffn, D_MODEL, D_FF)
    wd_all = _to_bf16(w_ffn_down.reshape(n_ffn * D_FF, D_MODEL), D_FF // 2).reshape(n_ffn, D_FF, D_MODEL)
    grp_p = dict(base=0, rpm=np_)
    grp_s = dict(base=1, rpm=ts)
    tm, tm_mix = 512, 1024

    wb = lambda w: w.astype(bf16)
    w_row = wb(jnp.concatenate([w_in[:, :, 0:512], w_in[:, :, 1792:2176], _pad_rope_cols(w_in[:, :, 2176:2208]),
                                w_in[:, :, 2208:2464]], axis=2))
    w_t = wb(jnp.swapaxes(w_in[:, :, 512:1792], 1, 2))
    wq = wb(jnp.pad(mla_w_uq.reshape(DEPTH, Q_LORA_C, H_C, NOPE_C + ROPE_C),
                    ((0, 0), (0, 0), (0, 0), (0, LANES - NOPE_C - ROPE_C))).reshape(DEPTH, Q_LORA_C, H_C * LANES))
    wkv = mla_w_ukv.reshape(DEPTH, KV_LORA_C, H_C, NOPE_C + V_C)
    wk = wb(jnp.pad(wkv[..., :NOPE_C], ((0, 0), (0, 0), (0, 0), (0, LANES - NOPE_C)))
            .reshape(DEPTH, KV_LORA_C, H_C * LANES))
    wv = wb(jnp.swapaxes(wkv[..., NOPE_C:].reshape(DEPTH, KV_LORA_C, H_C * V_C), 1, 2))
    n_grp = len(POOL_WINDOWS)
    w_pool = wb(jnp.einsum('lgij,gh->lgihj', pool_w, jnp.eye(n_grp, dtype=f32))
                .reshape(DEPTH, n_grp * POOL_CH, n_grp * POOL_CH))
    lb = jnp.broadcast_to(lb_all[..., None], (DEPTH, 2, 256, LANES))
    ng = jnp.broadcast_to(jnp.tile(hgrn_out_norm, (1, H_B))[..., None], (DEPTH, 256, LANES))
    sink = attn_sink.astype(f32)
    qn = mla_q_norm.reshape(DEPTH, 1, Q_LORA_C)
    kvn = mla_kv_norm.reshape(DEPTH, 1, KV_LORA_C)
    pscale = pool_scale.reshape(DEPTH, 1, 256)
    wo = wb(w_out)
    norms = norm_sub.reshape(DEPTH * 3, 1, D_MODEL)
    ctx_c = (cache_mla_ckv, _pad_rope_cols(cache_mla_krope))

    new_k = new_v = new_st = new_ckv = new_kr = None
    for l in range(DEPTH):
        def ffn(x, grp, j, idx, mix=None, final_g=None):
            return _ffn(x, mods, norms, wg_all, wu_all, wd_all, layer=l, widx=2 * l + idx,
                        j=j, tm=tm, mix=mix, final_g=final_g, **grp)

        xp = ffn(xp, grp_p, 0, 0)
        xs = ffn(xs, grp_s, 0, 0)

        pr, pt = _mixer_in(xp, mods, norms, w_row, w_t, layer=l, tm=tm_mix, **grp_p)
        oa, new_k, new_v = _attn_ctx(pr, sink, None if l == 0 else (new_k, new_v), layer=l, batch=bp, t=tp)
        ob, new_st = _hgrn(pt, lb, ng, mats, layer=l, batch=bp, t=tp, emit_state=True, state_prev=new_st)
        oc, new_ckv, new_kr = _mla(pr, qn, kvn, wq, wk, wv, layer=l, batch=bp, t=tp, tq=tp, emit_cache=True,
                                   cache_prev=None if l == 0 else (new_ckv, new_kr))
        od = _pool(pr, w_pool, pscale, layer=l, batch=bp, t=tp, seqs=POOL_CTX_SEQS)
        mix_p = (oa, ob, oc, od, wo)

        pr, pt = _mixer_in(xs, mods, norms, w_row, w_t, layer=l, tm=tm_mix, **grp_s)
        oa = _attn_lat(pr, sink, cache_attn_k, cache_attn_v, tab_a, layer=l, batch=bs, t=ts)
        (ob,) = _hgrn(pt, lb, ng, mats, layer=l, batch=bs, t=ts, s0=state_hgrn)
        (oc,) = _mla(pr, qn, kvn, wq, wk, wv, layer=l, batch=bs, t=ts, tq=512, ctx=ctx_c, tab=tab_c)
        od = _pool(pr, w_pool, pscale, layer=l, batch=bs, t=ts)
        mix_s = (oa, ob, oc, od, wo)

        fin = final_norm.reshape(1, D_MODEL) if l == DEPTH - 1 else None
        xp = ffn(xp, grp_p, 2, 1, mix_p, fin)
        xs = ffn(xs, grp_s, 2, 1, mix_s, fin)

    return (xp.reshape(bp, tp, D_MODEL), xs.reshape(bs, ts, D_MODEL), new_k, new_v, new_st, new_ckv, new_kr)
```

```python
import functools

import numpy as np
import jax
import jax.numpy as jnp
from jax import lax
from jax.experimental import pallas as pl
from jax.experimental.pallas import tpu as pltpu

f32 = jnp.float32
bf16 = jnp.bfloat16

D_MODEL = 1024
DEPTH = 2
GRID_W = 64
EPS = 1e-6
LOG2E = 1.4426950408889634
ROPE_THETA = 10000.0
NEG_INF = -1e30
N_MOD = 9
D_FF = 2816
H_A, KV_A, HD_A = 4, 2, 64
G_A = H_A // KV_A
WINDOW_A = 128
BLOCK_A = 128
H_B, DK_B, DV_B = 4, 64, 64
H_C, Q_LORA_C, KV_LORA_C, NOPE_C, ROPE_C, V_C = 4, 256, 128, 64, 32, 64
POOL_WINDOWS = (2, 4, 8, 16)
POOL_CH = 64

LANES = 128
SUBLANES = 8
ROW_REDUCE_CHAINS = 8
MLA_ONES = 2 * SUBLANES
VMEM_LIMIT = 56 * 1024 * 1024
HG_TILE = LANES
HG_UNROLL = 2
HG_HALF = HG_TILE // 2
HG_MAX_EXP = 80.0
HG_BLOCK = 16
HG_LEVELS = (16, 32, 64)
ATTN_CTX_SEQS = 2
POOL_CTX_SEQS = 8
ATTN_LAT_BLOCKS = 4
POOL_PAD = 16

PR_W = 1280
PT_W = 1280

_NT = (((1,), (1,)), ((), ()))
_TN = (((0,), (0,)), ((), ()))


def _params(*sem):
    return pltpu.CompilerParams(dimension_semantics=sem, vmem_limit_bytes=VMEM_LIMIT)


def _const_spec(shape, lead=None):
    zeros = (0,) * len(shape)
    if lead is None:
        return pl.BlockSpec(shape, lambda *_: zeros, pipeline_mode=pl.Buffered(1))
    return pl.BlockSpec((None,) + tuple(shape), lambda *_: (lead,) + zeros, pipeline_mode=pl.Buffered(1))


def _dot(a, b):
    return jnp.dot(a, b, preferred_element_type=f32)


def _sigmoid(x):
    return 1.0 / (1.0 + jnp.exp(-x))


def _rms(x, g):
    return x * lax.rsqrt(jnp.mean(x * x, axis=-1, keepdims=True) + EPS) * g


def _mod_kernel(c_ref, w_ref, b_ref, o_ref):
    c = c_ref[...]
    a = (c * _sigmoid(c)).astype(bf16)
    o_ref[0] = _dot(a, w_ref[0].astype(bf16)) + b_ref[0]


def _modulation(cvec, w_ada, b_ada):
    tn = 1536
    n_out = N_MOD * D_MODEL
    return pl.pallas_call(
        _mod_kernel,
        out_shape=jax.ShapeDtypeStruct((DEPTH, cvec.shape[0], n_out), f32),
        grid=(DEPTH, n_out // tn),
        in_specs=[pl.BlockSpec(cvec.shape, lambda l, j: (0, 0)),
                  pl.BlockSpec((1, D_MODEL, tn), lambda l, j: (l, 0, j)),
                  pl.BlockSpec((1, 1, tn), lambda l, j: (l, 0, j))],
        out_specs=pl.BlockSpec((1, cvec.shape[0], tn), lambda l, j: (l, 0, j)),
        compiler_params=_params("arbitrary", "arbitrary"),
        name="modulation",
    )(cvec, w_ada, b_ada.reshape(DEPTH, 1, n_out))


def _cast_kernel(x_ref, o_ref):
    o_ref[...] = x_ref[...].astype(bf16)


def _to_bf16(x, rows):
    r, c = x.shape
    return pl.pallas_call(
        _cast_kernel,
        out_shape=jax.ShapeDtypeStruct((r, c), bf16),
        grid=(r // rows,),
        in_specs=[pl.BlockSpec((rows, c), lambda i: (i, 0))],
        out_specs=pl.BlockSpec((rows, c), lambda i: (i, 0)),
        compiler_params=_params("parallel"),
        name="cast_bf16",
    )(x)


def _ffn_kernel(*refs, j, mix, final):
    it = iter(refs)
    x_ref, m_ref = next(it), next(it)
    mix_refs = [next(it) for _ in range(5)] if mix else None
    g_ref, wg_ref, wu_ref, wd_ref = next(it), next(it), next(it), next(it)
    fg_ref = next(it) if final else None
    o_ref = next(it)
    x = x_ref[...]
    m = m_ref[0]
    if mix:
        wo_ref = mix_refs[4]
        acc = _dot(mix_refs[0][...], wo_ref[0:256, :])
        for p in range(1, 4):
            acc += _dot(mix_refs[p][...], wo_ref[256 * p:256 * (p + 1), :])
        x = x + m[5:6] * acc
    h = (_rms(x, g_ref[...]) * (1.0 + m[3 * j + 1:3 * j + 2]) + m[3 * j:3 * j + 1]).astype(bf16)
    gate = _dot(h, wg_ref[...])
    up = _dot(h, wu_ref[...])
    act = (gate * _sigmoid(gate) * up).astype(bf16)
    y = x + (0.5 * m[3 * j + 2:3 * j + 3]) * _dot(act, wd_ref[...])
    if final:
        y = _rms(y, fg_ref[...])
    o_ref[...] = y


def _mod_spec(layer, base, rpm, tm):
    return pl.BlockSpec((None, 1, N_MOD, D_MODEL), lambda i: (layer, base + (i * tm) // rpm, 0, 0))


def _ffn(x, mods, norms, wg, wu, wd, *, layer, widx, j, base, rpm, tm, mix=None, final_g=None):
    n = x.shape[0]

    def wspec(r, c):
        return _const_spec((r, c), widx)

    in_specs = [pl.BlockSpec((tm, D_MODEL), lambda i: (i, 0)), _mod_spec(layer, base, rpm, tm)]
    args = [x, mods]
    if mix is not None:
        in_specs += [pl.BlockSpec((tm, 256), lambda i: (i, 0))] * 4 + [_const_spec((D_MODEL, D_MODEL), layer)]
        args += list(mix)
    in_specs += [_const_spec((1, D_MODEL), 3 * layer + j), wspec(D_MODEL, D_FF), wspec(D_MODEL, D_FF),
                 wspec(D_FF, D_MODEL)]
    args += [norms, wg, wu, wd]
    if final_g is not None:
        in_specs.append(_const_spec((1, D_MODEL)))
        args.append(final_g)
    return pl.pallas_call(
        functools.partial(_ffn_kernel, j=j, mix=mix is not None, final=final_g is not None),
        out_shape=jax.ShapeDtypeStruct((n, D_MODEL), f32),
        grid=(n // tm,),
        in_specs=in_specs,
        out_specs=pl.BlockSpec((tm, D_MODEL), lambda i: (i, 0)),
        compiler_params=_params("parallel"),
        name="ffn",
    )(*args)


def _mixin_kernel(x_ref, m_ref, g_ref, wr_ref, wt_ref, pr_ref, pt_ref):
    m = m_ref[0]
    h = (_rms(x_ref[...], g_ref[...]) * (1.0 + m[4:5]) + m[3:4]).astype(bf16)
    pr_ref[...] = _dot(h, wr_ref[...])
    pt_ref[...] = lax.dot_general(wt_ref[...], h, _NT, preferred_element_type=f32)


def _mixer_in(x, mods, norms, w_row, w_t, *, layer, base, rpm, tm):
    n = x.shape[0]
    return pl.pallas_call(
        _mixin_kernel,
        out_shape=(jax.ShapeDtypeStruct((n, PR_W), f32), jax.ShapeDtypeStruct((PT_W, n), f32)),
        grid=(n // tm,),
        in_specs=[pl.BlockSpec((tm, D_MODEL), lambda i: (i, 0)), _mod_spec(layer, base, rpm, tm),
                  _const_spec((1, D_MODEL), 3 * layer + 1), _const_spec((D_MODEL, PR_W), layer),
                  _const_spec((PT_W, D_MODEL), layer)],
        out_specs=(pl.BlockSpec((tm, PR_W), lambda i: (i, 0)), pl.BlockSpec((PT_W, tm), lambda i: (0, i))),
        compiler_params=_params("parallel"),
        name="mixer_in",
    )(x, mods, norms, w_row, w_t)


def _rope(x, tab_ref, rows, lanes, shift):
    width = x.shape[-1]
    c = tab_ref[0, rows, lanes]
    sa = tab_ref[1, rows, lanes]
    sb = tab_ref[2, rows, lanes]
    return x * c + pltpu.roll(x, shift, axis=1) * sa + pltpu.roll(x, width - shift, axis=1) * sb


def _softmax_parts(s_list, sink):
    sink2 = sink * LOG2E
    mx = functools.reduce(jnp.maximum, [jnp.max(s, axis=-1, keepdims=True) for s in s_list])
    mx = jnp.maximum(mx, sink2)
    ps = [jnp.exp2(s - mx) for s in s_list]
    den = functools.reduce(lambda a, b: a + b, [jnp.sum(p, axis=-1, keepdims=True) for p in ps])
    return ps, 1.0 / (den + jnp.exp2(sink2 - mx))


def _kv_per_query_head(x):
    return jnp.concatenate([x[:, HD_A * (h // G_A):HD_A * (h // G_A + 1)] for h in range(H_A)], axis=1)


def _stack_heads(q):
    head = lax.broadcasted_iota(jnp.int32, (1, H_A * HD_A), 1) // HD_A
    return jnp.concatenate([jnp.where(head == h, q, 0.0) for h in range(H_A)], axis=0)


def _pick_heads(o, r):
    head = lax.broadcasted_iota(jnp.int32, (1, H_A * HD_A), 1) // HD_A
    out = o[0:r]
    for h in range(1, H_A):
        out = jnp.where(head == h, o[h * r:(h + 1) * r], out)
    return out


def _sink_rows(sink_ref, layer, r):
    row = lax.broadcasted_iota(jnp.int32, (H_A * r, 1), 0)
    out = jnp.full((H_A * r, 1), sink_ref[layer, 0], f32)
    for h in range(1, H_A):
        out = jnp.where(row >= h * r, sink_ref[layer, h], out)
    return out


def _attn_ctx_kernel(*refs, t, n_prev, layer):
    if n_prev:
        sink_ref, q_ref, kv_ref, kp_ref, vp_ref, o_ref, ko_ref, vo_ref = refs
        ko_ref[:, 0:n_prev] = kp_ref[...]
        vo_ref[:, 0:n_prev] = vp_ref[...]
    else:
        sink_ref, q_ref, kv_ref, o_ref, ko_ref, vo_ref = refs
    kvw = KV_A * HD_A
    sink = _sink_rows(sink_ref, layer, t)
    for u in range(ATTN_CTX_SEQS):
        rows = slice(u * t, (u + 1) * t)
        kv = kv_ref[rows, :]
        for j in range(KV_A):
            ko_ref[u, n_prev, j] = kv[:, HD_A * j:HD_A * (j + 1)]
            vo_ref[u, n_prev, j] = kv[:, kvw + HD_A * j:kvw + HD_A * (j + 1)]
        kx = _kv_per_query_head(kv[:, 0:kvw]).astype(bf16)
        vx = _kv_per_query_head(kv[:, kvw:2 * kvw]).astype(bf16)
        q4 = _stack_heads(q_ref[rows, :] * (HD_A ** -0.5 * LOG2E)).astype(bf16)
        s = lax.dot_general(q4, kx, _NT, preferred_element_type=f32)
        (p,), inv = _softmax_parts([s], sink)
        o_ref[rows, :] = _pick_heads(_dot(p.astype(bf16), vx) * inv, t).astype(o_ref.dtype)


def _cache_spec(n_seq, n_layers, *dims):
    return pl.BlockSpec((n_seq, n_layers) + dims, lambda b: (b,) + (0,) * (1 + len(dims)))


def _attn_ctx(pr, sink, prev, *, layer, batch, t):
    n_prev = 0 if prev is None else prev[0].shape[1]
    ns = ATTN_CTX_SEQS
    kvshape = jax.ShapeDtypeStruct((batch, n_prev + 1, KV_A, t, HD_A), f32)
    in_specs = [pl.BlockSpec(memory_space=pltpu.SMEM),
                pl.BlockSpec((ns * t, 256), lambda b: (b, 0)), pl.BlockSpec((ns * t, 256), lambda b: (b, 1))]
    args = [sink, pr, pr]
    if n_prev:
        in_specs += [_cache_spec(ns, n_prev, KV_A, t, HD_A)] * 2
        args += list(prev)
    return pl.pallas_call(
        functools.partial(_attn_ctx_kernel, t=t, n_prev=n_prev, layer=layer),
        out_shape=(jax.ShapeDtypeStruct((batch * t, 256), bf16), kvshape, kvshape),
        grid=(batch // ns,),
        in_specs=in_specs,
        out_specs=(pl.BlockSpec((ns * t, 256), lambda b: (b, 0)),
                   _cache_spec(ns, n_prev + 1, KV_A, t, HD_A), _cache_spec(ns, n_prev + 1, KV_A, t, HD_A)),
        compiler_params=_params("parallel"),
        name="attn_ctx",
    )(*args)


def _attn_lat_kernel(sink_ref, q_ref, kv_ref, kc_ref, vc_ref, tab_ref, o_ref, kpad, vpad, kctx, vctx, *, t, layer):
    n = pl.program_id(1)
    kvw = KV_A * HD_A
    hw = H_A * HD_A

    @pl.when(n == 0)
    def _():
        zeros = jnp.zeros((BLOCK_A, hw), bf16)
        kv = kv_ref[...]
        k = _rope(kv[:, 0:kvw], tab_ref, slice(None), slice(0, kvw), HD_A // 2)
        kpad[0:BLOCK_A, :] = zeros
        vpad[0:BLOCK_A, :] = zeros
        kpad[BLOCK_A:BLOCK_A + t, :] = _kv_per_query_head(k).astype(bf16)
        vpad[BLOCK_A:BLOCK_A + t, :] = _kv_per_query_head(kv[:, kvw:2 * kvw]).astype(bf16)
        kpad[BLOCK_A + t:2 * BLOCK_A + t, :] = zeros
        vpad[BLOCK_A + t:2 * BLOCK_A + t, :] = zeros
        kctx[...] = jnp.concatenate([kc_ref[0, h // G_A] for h in range(H_A)], axis=1).astype(bf16)
        vctx[...] = jnp.concatenate([vc_ref[0, h // G_A] for h in range(H_A)], axis=1).astype(bf16)

    rows = H_A * BLOCK_A
    rr = lax.broadcasted_iota(jnp.int32, (rows, 1), 0) % BLOCK_A
    cidx = lax.broadcasted_iota(jnp.int32, (rows, 3 * BLOCK_A), 1)
    sink = _sink_rows(sink_ref, layer, BLOCK_A)
    for u in range(ATTN_LAT_BLOCKS):
        blk = n * ATTN_LAT_BLOCKS + u
        row0 = pl.multiple_of(blk * BLOCK_A, BLOCK_A)
        q = _rope(q_ref[u * BLOCK_A:(u + 1) * BLOCK_A, :], tab_ref, pl.ds(row0, BLOCK_A), slice(None), HD_A // 2)
        q4 = _stack_heads(q * (HD_A ** -0.5 * LOG2E)).astype(bf16)
        kband = kpad[pl.ds(row0, 3 * BLOCK_A), :]
        vband = vpad[pl.ds(row0, 3 * BLOCK_A), :]
        s_loc = lax.dot_general(q4, kband, _NT, preferred_element_type=f32)
        s_ctx = lax.dot_general(q4, kctx[...], _NT, preferred_element_type=f32)
        lo = jnp.maximum(rr + (BLOCK_A - WINDOW_A), jnp.maximum(0, (1 - blk) * BLOCK_A))
        hi = jnp.minimum(rr + (BLOCK_A + WINDOW_A), jnp.minimum(3 * BLOCK_A, t - (blk - 1) * BLOCK_A) - 1)
        s_loc = jnp.where((cidx >= lo) & (cidx <= hi), s_loc, NEG_INF)
        (p_loc, p_ctx), inv = _softmax_parts([s_loc, s_ctx], sink)
        o = (_dot(p_loc.astype(bf16), vband) + _dot(p_ctx.astype(bf16), vctx[...])) * inv
        o_ref[u * BLOCK_A:(u + 1) * BLOCK_A, :] = _pick_heads(o, BLOCK_A).astype(o_ref.dtype)


def _attn_lat(pr, sink, kc, vc, tab, *, layer, batch, t):
    rows = ATTN_LAT_BLOCKS * BLOCK_A
    nb = t // rows
    lc = kc.shape[3]
    cspec = pl.BlockSpec((1, None, KV_A, lc, HD_A), lambda b, n: (b, layer, 0, 0, 0))
    return pl.pallas_call(
        functools.partial(_attn_lat_kernel, t=t, layer=layer),
        out_shape=jax.ShapeDtypeStruct((batch * t, 256), bf16),
        grid=(batch, nb),
        in_specs=[pl.BlockSpec(memory_space=pltpu.SMEM),
                  pl.BlockSpec((rows, 256), lambda b, n: (b * nb + n, 0)),
                  pl.BlockSpec((t, 256), lambda b, n: (b, 1)),
                  cspec, cspec,
                  pl.BlockSpec((3, t, 256), lambda b, n: (0, 0, 0))],
        out_specs=pl.BlockSpec((rows, 256), lambda b, n: (b * nb + n, 0)),
        scratch_shapes=[pltpu.VMEM((t + 2 * BLOCK_A, H_A * HD_A), bf16),
                        pltpu.VMEM((t + 2 * BLOCK_A, H_A * HD_A), bf16),
                        pltpu.VMEM((lc, H_A * HD_A), bf16), pltpu.VMEM((lc, H_A * HD_A), bf16)],
        compiler_params=_params("parallel", "arbitrary"),
        name="attn_lat",
    )(sink, pr, pr, kc, vc, tab)


def _reduce_rows(x, reduce_fn):
    r, c = x.shape
    g = ROW_REDUCE_CHAINS
    if r % (g * SUBLANES):
        return reduce_fn(x, axis=0, keepdims=True)
    part = reduce_fn(x.reshape(g, r // (g * SUBLANES), SUBLANES, c), axis=1)
    return reduce_fn(reduce_fn(part, axis=0), axis=0, keepdims=True)


def _mla_kernel(*refs, t, lc, tq, rope, emit_cache, n_prev):
    it = iter(refs)
    cq_ref, ckv_ref, kr_ref = next(it), next(it), next(it)
    cc_ref = kc_ref = tab_ref = None
    if lc:
        cc_ref, kc_ref = next(it), next(it)
    if rope:
        tab_ref = next(it)
    qn_ref, kvn_ref, wq_ref, wk_ref, wv_ref = next(it), next(it), next(it), next(it), next(it)
    ckvp_ref, krp_ref = (next(it), next(it)) if n_prev else (None, None)
    o_ref = next(it)
    ckv_out_ref, kr_out_ref = (next(it), next(it)) if emit_cache else (None, None)
    kbuf, vbuf = next(it), next(it)
    i = pl.program_id(1)
    hw = LANES
    va = V_C + MLA_ONES

    @pl.when(i == 0)
    def _():
        ckv = _rms(ckv_ref[...], kvn_ref[...])
        kr = kr_ref[...]
        if emit_cache:
            if n_prev:
                ckv_out_ref[0, 0:n_prev] = ckvp_ref[0]
                kr_out_ref[0, 0:n_prev] = krp_ref[0]
            ckv_out_ref[0, n_prev] = ckv
            kr_out_ref[0, n_prev] = kr[:, NOPE_C:NOPE_C + ROPE_C]
        ckv_b = ckv.astype(bf16)
        if rope:
            kr = _rope(kr, tab_ref, slice(None), slice(None), ROPE_C // 2)
        kk = _dot(ckv_b, wk_ref[...])
        for h in range(H_C):
            kbuf[h, 0:t, :] = (kk[:, hw * h:hw * (h + 1)] + kr).astype(bf16)
        def put_values(vt, cols):
            for h in range(H_C):
                vbuf[va * h:va * h + V_C, cols] = vt[V_C * h:V_C * (h + 1)]

        for h in range(H_C):
            vbuf[va * h + V_C:va * (h + 1), :] = jnp.ones((MLA_ONES, t + lc), bf16)
        put_values(lax.dot_general(wv_ref[...], ckv_b, _NT, preferred_element_type=f32).astype(bf16), slice(0, t))
        if lc:
            cc_b = cc_ref[0].astype(bf16)
            kkc = _dot(cc_b, wk_ref[...])
            krc = kc_ref[0]
            for h in range(H_C):
                kbuf[h, t:t + lc, :] = (kkc[:, hw * h:hw * (h + 1)] + krc).astype(bf16)
            put_values(lax.dot_general(wv_ref[...], cc_b, _NT, preferred_element_type=f32).astype(bf16),
                       slice(t, t + lc))

    cqn = _rms(cq_ref[...], qn_ref[...]).astype(bf16)
    qf = _dot(cqn, wq_ref[...])
    c2 = (NOPE_C + ROPE_C) ** -0.5 * LOG2E
    row0 = pl.multiple_of(i * tq, tq)
    n_keys = t + lc
    half = n_keys // 2

    def scores(h):
        qh = qf[:, hw * h:hw * (h + 1)]
        if rope:
            qh = _rope(qh, tab_ref, pl.ds(row0, tq), slice(None), ROPE_C // 2)
        qb = (qh * c2).astype(bf16)
        return [lax.dot_general(kbuf[h, c * half:(c + 1) * half, :], qb, _NT, preferred_element_type=f32)
                for c in range(2)]

    def probs(s):
        m = jnp.maximum(_reduce_rows(s[0], jnp.max), _reduce_rows(s[1], jnp.max))
        return [jnp.exp2(x - m).astype(bf16) for x in s]

    def values(h, p):
        rows = slice(va * h, va * (h + 1))
        o = _dot(vbuf[rows, 0:half], p[0]) + _dot(vbuf[rows, half:2 * half], p[1])
        return o[0:V_C] * (1.0 / o[V_C:V_C + 1])

    s_next = scores(0)
    outs = []
    for h in range(H_C):
        s_cur = s_next
        if h + 1 < H_C:
            s_next = scores(h + 1)
        outs.append(values(h, probs(s_cur)))
    o_ref[...] = jnp.concatenate(outs, axis=0).T.astype(o_ref.dtype)


def _mla(pr, qn, kvn, wq, wk, wv, *, layer, batch, t, tq, ctx=None, tab=None, emit_cache=False, cache_prev=None):
    nq = t // tq
    lc = 0 if ctx is None else ctx[0].shape[2]
    n_prev = 0 if cache_prev is None else cache_prev[0].shape[1]
    in_specs = [pl.BlockSpec((tq, 256), lambda b, i: (b * nq + i, 2)),
                pl.BlockSpec((t, LANES), lambda b, i: (b, 6)),
                pl.BlockSpec((t, LANES), lambda b, i: (b, 7))]
    args = [pr, pr, pr]
    if lc:
        in_specs += [pl.BlockSpec((1, None, lc, LANES), lambda b, i: (b, layer, 0, 0))] * 2
        args += list(ctx)
    if tab is not None:
        in_specs.append(pl.BlockSpec((3, t, LANES), lambda b, i: (0, 0, 0)))
        args.append(tab)
    in_specs += [_const_spec(a.shape[1:], layer) for a in (qn, kvn, wq, wk, wv)]
    args += [qn, kvn, wq, wk, wv]

    def cspec(n_layers, width):
        return pl.BlockSpec((1, n_layers, t, width), lambda b, i: (b, 0, 0, 0))

    if n_prev:
        in_specs += [cspec(n_prev, KV_LORA_C), cspec(n_prev, ROPE_C)]
        args += list(cache_prev)
    out_shape = [jax.ShapeDtypeStruct((batch * t, 256), bf16)]
    out_specs = [pl.BlockSpec((tq, 256), lambda b, i: (b * nq + i, 0))]
    if emit_cache:
        out_shape += [jax.ShapeDtypeStruct((batch, n_prev + 1, t, KV_LORA_C), f32),
                      jax.ShapeDtypeStruct((batch, n_prev + 1, t, ROPE_C), f32)]
        out_specs += [cspec(n_prev + 1, KV_LORA_C), cspec(n_prev + 1, ROPE_C)]
    return pl.pallas_call(
        functools.partial(_mla_kernel, t=t, lc=lc, tq=tq, rope=tab is not None, emit_cache=emit_cache,
                          n_prev=n_prev),
        out_shape=tuple(out_shape),
        grid=(batch, nq),
        in_specs=in_specs,
        out_specs=tuple(out_specs),
        scratch_shapes=[pltpu.VMEM((H_C, t + lc, LANES), bf16),
                        pltpu.VMEM((H_C * (V_C + MLA_ONES), t + lc), bf16)],
        compiler_params=_params("parallel", "arbitrary"),
        name="mla",
    )(*args)


def _pool_counts(t):
    pos = np.arange(t)[:, None]
    half = np.repeat(np.asarray(POOL_WINDOWS) // 2, POOL_CH)[None, :]
    return (np.minimum(pos + half, t) - np.maximum(pos - half, 0)).astype(np.float32)


def _pool_kernel(x_ref, cnt_ref, w_ref, sc_ref, o_ref, buf_a, buf_b, buf_c, *, t, seqs):
    for u in range(seqs):
        _pool_sequence(x_ref, cnt_ref, w_ref, sc_ref, o_ref, buf_a, buf_b, buf_c, slice(u * t, (u + 1) * t), t)


def _pool_sequence(x_ref, cnt_ref, w_ref, sc_ref, o_ref, buf_a, buf_b, buf_c, seq, t):
    x = x_ref[seq, :]
    pad, edge = POOL_PAD, SUBLANES
    n = t + 2 * pad
    m = n - 2 * edge
    zeros = jnp.zeros((pad, 256), f32)
    buf_a[0:pad, :] = zeros
    buf_a[pad:pad + t, :] = x
    buf_a[pad + t:n, :] = zeros
    for buf in (buf_b, buf_c):
        buf[0:edge, :] = zeros[0:edge]
        buf[n - edge:n, :] = zeros[0:edge]
    inner = slice(edge, edge + m)
    buf_b[inner, :] = buf_a[pl.ds(edge - 1, m), :] + buf_a[pl.ds(edge, m), :]
    buf_c[inner, :] = buf_b[pl.ds(edge - 1, m), :] + buf_b[pl.ds(edge + 1, m), :]
    lo, hi = slice(0, LANES), slice(LANES, 2 * LANES)
    buf_a[inner, hi] = buf_c[pl.ds(edge - 2, m), hi] + buf_c[pl.ds(edge + 2, m), hi]
    s16 = buf_a[pl.ds(pad - 4, t), hi] + buf_a[pl.ds(pad + 4, t), hi]
    first = lax.broadcasted_iota(jnp.int32, (1, LANES), 1) < POOL_CH
    rows = slice(pad, pad + t)
    win_sum = jnp.concatenate([jnp.where(first, buf_b[rows, lo], buf_c[rows, lo]),
                               jnp.where(first, buf_a[rows, hi], s16)], axis=1)
    mean = win_sum / cnt_ref[...]
    o_ref[seq, :] = (_dot((mean - x).astype(bf16), w_ref[...]) * sc_ref[...]).astype(o_ref.dtype)


def _pool(pr, w_bd, scale, *, layer, batch, t, seqs=1):
    return pl.pallas_call(
        functools.partial(_pool_kernel, t=t, seqs=seqs),
        out_shape=jax.ShapeDtypeStruct((batch * t, 256), bf16),
        grid=(batch // seqs,),
        in_specs=[pl.BlockSpec((seqs * t, 256), lambda b: (b, 4)), _const_spec((t, 256)),
                  _const_spec((256, 256), layer), _const_spec((1, 256), layer)],
        out_specs=pl.BlockSpec((seqs * t, 256), lambda b: (b, 0)),
        scratch_shapes=[pltpu.VMEM((t + 2 * POOL_PAD, 256), f32)] * 3,
        compiler_params=_params("parallel"),
        name="pool",
    )(pr, jnp.asarray(_pool_counts(t)), w_bd, scale)


def _hgrn_mats():
    n = HG_TILE
    s = np.arange(n)[:, None]
    t = np.arange(n)[None, :]
    same_half = s // HG_HALF == t // HG_HALF
    mid = (t // HG_HALF) * HG_HALF + HG_HALF // 2 - 1
    rel_mid = (same_half & (s > mid) & (s <= t)).astype(np.float32) - (same_half & (s <= mid) & (s > t))
    mats = [rel_mid, same_half & (s <= t), same_half & (s > t), s <= t, s > t,
            (s // HG_BLOCK == t // HG_BLOCK) & (s <= t)]
    for m in HG_LEVELS:
        same_pair = s // (2 * m) == t // (2 * m)
        right_t = (t // m) % 2 == 1
        right_s = (s // m) % 2 == 1
        q_side = same_pair & right_t & right_s & (s <= t)
        k_side = same_pair & ~right_t & ~right_s & (s > t)
        mats.append(q_side | k_side)
    fwd = np.concatenate([m.astype(np.float32) for m in mats], axis=1)
    bwd = np.concatenate([m[::-1, ::-1].astype(np.float32) for m in mats], axis=1)
    both = np.stack([fwd, bwd])
    return np.concatenate([both, both], axis=1)


def _hgrn_exact_intra(q, k, v, vb, hl, mats, reverse):
    n = HG_TILE
    e_all = _dot(hl, mats)
    beta = e_all[:, 0:n]
    lane = lax.broadcasted_iota(jnp.int32, (1, n), 1)
    pos = lane % HG_BLOCK
    v3 = v.reshape(H_B, DV_B, n)

    o3 = jnp.sum((q * k).reshape(H_B, DK_B, n), axis=1, keepdims=True) * v3
    for d in range(1, HG_BLOCK):
        shift = (n - d) if reverse else d
        valid = (pos <= HG_BLOCK - 1 - d) if reverse else (pos >= d)
        kr = pltpu.roll(k, shift, axis=1)
        br = pltpu.roll(beta, shift, axis=1)
        vr = pltpu.roll(v, shift, axis=1)
        e = jnp.exp(jnp.where(valid, beta - br, NEG_INF))
        sc = jnp.sum((q * kr * e).reshape(H_B, DK_B, n), axis=1, keepdims=True)
        o3 = o3 + sc * vr.reshape(H_B, DV_B, n)

    srow = lax.broadcasted_iota(jnp.int32, (n, n), 0)
    tcol = lax.broadcasted_iota(jnp.int32, (n, n), 1)
    scores = [jnp.zeros((n, n), f32) for _ in range(H_B)]
    for li, m in enumerate(HG_LEVELS):
        ex = jnp.exp(e_all[:, (li + 1) * n:(li + 2) * n])
        is_q = ((lane // m) % 2) == (0 if reverse else 1)
        qs = jnp.where(is_q, q * ex, 0.0).astype(bf16)
        ks = jnp.where(is_q, 0.0, k * ex).astype(bf16)
        pair = (srow // (2 * m)) == (tcol // (2 * m))
        for h in range(H_B):
            a = lax.dot_general(ks[DK_B * h:DK_B * (h + 1)], qs[DK_B * h:DK_B * (h + 1)], _TN,
                                preferred_element_type=f32)
            scores[h] = scores[h] + jnp.where(pair, a, 0.0)

    return [o3[h] + _dot(vb[DV_B * h:DV_B * (h + 1)], scores[h].astype(bf16)) for h in range(H_B)]


def _hgrn_common(q, v, fpre, lb, mats_ref, sidx, st_ref):
    n = HG_TILE
    f = lb + (1.0 - lb) * _sigmoid(fpre)
    k = 1.0 - f
    logf = jnp.log(f)
    hi = logf.astype(bf16)
    hl = jnp.concatenate([hi, (logf - hi.astype(f32)).astype(bf16)], axis=1)
    ex = _dot(hl, mats_ref[sidx, :, 0:5 * n])
    ones = jnp.ones((8, 2 * n), bf16)
    g_row = lax.dot_general(ones, hl, _NT, preferred_element_type=f32)[0:1]
    d_mid = ex[:, 0:n]
    small = jnp.max(jnp.abs(d_mid)) <= HG_MAX_EXP
    q_in = (q * jnp.exp(ex[:, 3 * n:4 * n])).astype(bf16)
    k_out = (k * jnp.exp(ex[:, 4 * n:5 * n])).astype(bf16)
    vb = v.astype(bf16)
    o_state = []
    for h in range(H_B):
        rows = slice(DK_B * h, DK_B * (h + 1))
        st = st_ref[sidx, rows, :]
        o_state.append(_dot(st.astype(bf16), q_in[rows]))
        upd = lax.dot_general(vb[rows], k_out[rows], _NT, preferred_element_type=f32)
        st_ref[sidx, rows, :] = st * jnp.exp(g_row[:, rows]) + upd
    return dict(q=q, k=k, v=v, vb=vb, hl=hl, ex=ex, o_state=o_state, small=small)


def _hgrn_fast_intra(c, reverse):
    n = HG_TILE
    q, k, ex = c["q"], c["k"], c["ex"]
    lane = lax.broadcasted_iota(jnp.int32, (1, n), 1)
    in0 = (lane // HG_HALF) == 0
    q_half = in0 if reverse else jnp.logical_not(in0)
    d_mid = ex[:, 0:n]
    qd = q * jnp.exp(d_mid)
    kd = k * jnp.exp(-d_mid)
    ql = jnp.where(q_half, q * jnp.exp(ex[:, n:2 * n]), 0.0)
    kl = jnp.where(q_half, 0.0, k * jnp.exp(ex[:, 2 * n:3 * n]))
    parts_q = [jnp.where(in0, qd, 0.0).astype(bf16), jnp.where(in0, 0.0, qd).astype(bf16), ql.astype(bf16)]
    parts_k = [jnp.where(in0, kd, 0.0).astype(bf16), jnp.where(in0, 0.0, kd).astype(bf16), kl.astype(bf16)]
    srow = lax.broadcasted_iota(jnp.int32, (n, n), 0)
    tcol = lax.broadcasted_iota(jnp.int32, (n, n), 1)
    causal = (srow >= tcol) if reverse else (srow <= tcol)
    scores = []
    for h in range(H_B):
        rows = slice(DK_B * h, DK_B * (h + 1))
        k3 = jnp.concatenate([p[rows] for p in parts_k], axis=0)
        q3 = jnp.concatenate([p[rows] for p in parts_q], axis=0)
        scores.append(lax.dot_general(k3, q3, _TN, preferred_element_type=f32))
    scores = [jnp.where(causal, a, 0.0).astype(bf16) for a in scores]
    outs = [c["o_state"][h] + _dot(c["vb"][DV_B * h:DV_B * (h + 1)], scores[h]) for h in range(H_B)]
    return jnp.concatenate(outs, axis=0)


def _hgrn_exact(c, mats, reverse):
    intra = _hgrn_exact_intra(c["q"], c["k"], c["v"], c["vb"], c["hl"], mats, reverse)
    return jnp.concatenate([c["o_state"][h] + intra[h] for h in range(H_B)], axis=0)


def _hgrn_kernel(*refs, t, has_s0, emit_state, n_prev):
    it = iter(refs)
    q_ref, i_ref, ff_ref, fb_ref, g_ref, lb_ref, ng_ref, mats_ref = [next(it) for _ in range(8)]
    s0_ref = next(it) if has_s0 else None
    sprev_ref = next(it) if n_prev else None
    o_ref = next(it)
    sfin_ref = next(it) if emit_state else None
    st_ref, oacc_f, oacc_b = next(it), next(it), next(it)
    n = HG_TILE
    nt = t // n
    for d in range(2):
        for h in range(H_B):
            rows = slice(DV_B * h, DV_B * (h + 1))
            st_ref[d, rows, :] = s0_ref[0, d, h].T if has_s0 else jnp.zeros((DV_B, DK_B), f32)

    def scan(i, c):
        work = []
        small = None
        for u in range(HG_UNROLL):
            j = i * HG_UNROLL + u
            cols_f = pl.ds(pl.multiple_of(j * n, n), n)
            cols_b = pl.ds(pl.multiple_of((nt - 1 - j) * n, n), n)
            cf = _hgrn_common(q_ref[:, cols_f], i_ref[:, cols_f], ff_ref[:, cols_f], lb_ref[0], mats_ref, 0, st_ref)
            cb = _hgrn_common(q_ref[:, cols_b], i_ref[:, cols_b], fb_ref[:, cols_b], lb_ref[1], mats_ref, 1, st_ref)
            work.append((cols_f, cf, cols_b, cb))
            both = jnp.logical_and(cf["small"], cb["small"])
            small = both if small is None else jnp.logical_and(small, both)
        for cols_f, cf, cols_b, cb in work:
            oacc_f[:, cols_f] = _hgrn_fast_intra(cf, False)
            oacc_b[:, cols_b] = _hgrn_fast_intra(cb, True)

        @pl.when(jnp.logical_not(small))
        def _():
            for cols_f, cf, cols_b, cb in work:
                oacc_f[:, cols_f] = _hgrn_exact(cf, mats_ref[0, :, 5 * n:9 * n], False)
                oacc_b[:, cols_b] = _hgrn_exact(cb, mats_ref[1, :, 5 * n:9 * n], True)

        return c

    lax.fori_loop(0, nt // HG_UNROLL, scan, 0)

    def finish(i, c):
        off = pl.multiple_of(i * n, n)
        cols = pl.ds(off, n)
        o = oacc_f[:, cols] + oacc_b[:, cols]
        o3 = o.reshape(H_B, DV_B, n)
        y = o3 * lax.rsqrt(jnp.mean(o3 * o3, axis=1, keepdims=True) + EPS)
        gate = g_ref[:, cols]
        y = y.reshape(H_B * DV_B, n) * ng_ref[...] * (gate * _sigmoid(gate))
        o_ref[pl.ds(off, n), :] = y.T.astype(o_ref.dtype)
        return c

    lax.fori_loop(0, nt, finish, 0)
    if emit_state:
        if n_prev:
            sfin_ref[0, 0:n_prev] = sprev_ref[0]
        for d in range(2):
            for h in range(H_B):
                sfin_ref[0, n_prev, d, h] = st_ref[d, DV_B * h:DV_B * (h + 1), :].T


def _hgrn(pt, lb, ng, mats, *, batch, t, s0=None, layer=0, emit_state=False, state_prev=None):
    def row(r):
        return pl.BlockSpec((256, t), lambda b: (r, b))

    def sspec(n_layers):
        return pl.BlockSpec((1, n_layers, 2, H_B, DK_B, DV_B), lambda b: (b, 0, 0, 0, 0, 0))

    n_prev = 0 if state_prev is None else state_prev.shape[1]
    in_specs = [row(0), row(1), row(2), row(3), row(4),
                _const_spec((2, 256, LANES), layer), _const_spec((256, LANES), layer), _const_spec(mats.shape)]
    args = [pt, pt, pt, pt, pt, lb, ng, mats]
    if s0 is not None:
        in_specs.append(pl.BlockSpec((1, None, 2, H_B, DK_B, DV_B), lambda b: (b, layer, 0, 0, 0, 0)))
        args.append(s0)
    if n_prev:
        in_specs.append(sspec(n_prev))
        args.append(state_prev)
    out_shape = [jax.ShapeDtypeStruct((batch * t, 256), bf16)]
    out_specs = [pl.BlockSpec((t, 256), lambda b: (b, 0))]
    if emit_state:
        out_shape.append(jax.ShapeDtypeStruct((batch, n_prev + 1, 2, H_B, DK_B, DV_B), f32))
        out_specs.append(sspec(n_prev + 1))
    return pl.pallas_call(
        functools.partial(_hgrn_kernel, t=t, has_s0=s0 is not None, emit_state=emit_state, n_prev=n_prev),
        out_shape=tuple(out_shape),
        grid=(batch,),
        in_specs=in_specs,
        out_specs=tuple(out_specs),
        scratch_shapes=[pltpu.VMEM((2, H_B * DV_B, DK_B), f32), pltpu.VMEM((256, t), f32),
                        pltpu.VMEM((256, t), f32)],
        compiler_params=_params("parallel"),
        name="hgrn",
    )(*args)


def _rope_tables(t):
    def cos_sin(dim):
        n_freq = dim // 4
        inv = ROPE_THETA ** (-jnp.arange(n_freq, dtype=f32) / n_freq)
        rows = t // GRID_W
        row_id = jnp.repeat(jnp.arange(rows, dtype=f32), GRID_W)
        col_id = jnp.tile(jnp.arange(GRID_W, dtype=f32), rows)
        ang = jnp.concatenate([row_id[:, None] * inv, col_id[:, None] * inv], axis=-1)
        return jnp.cos(ang), jnp.sin(ang)

    cos, sin = cos_sin(HD_A)
    zero = jnp.zeros_like(sin)
    tab_a = jnp.stack([jnp.tile(jnp.concatenate([cos, cos], -1), (1, H_A)),
                       jnp.tile(jnp.concatenate([zero, sin], -1), (1, H_A)),
                       jnp.tile(jnp.concatenate([-sin, zero], -1), (1, H_A))])
    cos, sin = cos_sin(ROPE_C)
    zero = jnp.zeros_like(sin)
    one64 = jnp.ones((t, NOPE_C), f32)
    zero64 = jnp.zeros((t, NOPE_C), f32)
    pad32 = jnp.zeros((t, LANES - NOPE_C - ROPE_C), f32)
    tab_c = jnp.stack([jnp.concatenate([one64, cos, cos, pad32 + 1.0], -1),
                       jnp.concatenate([zero64, zero, sin, pad32], -1),
                       jnp.concatenate([zero64, -sin, zero, pad32], -1)])
    return tab_a, tab_c


def _pad_rope_cols(x):
    pads = [(0, 0)] * (x.ndim - 1) + [(NOPE_C, LANES - NOPE_C - ROPE_C)]
    return jnp.pad(x, pads)


def kernel(x_prompt, x_sample, c, c_ctx, cache_attn_k, cache_attn_v, state_hgrn, cache_mla_ckv, cache_mla_krope,
           w_ada, b_ada, norm_sub, w_ffn_gate, w_ffn_up, w_ffn_down, w_in, w_out, attn_sink, hgrn_lb_logits,
           hgrn_out_norm, mla_q_norm, mla_kv_norm, mla_w_uq, mla_w_ukv, pool_w, pool_scale, final_norm):
    bp, tp, _ = x_prompt.shape
    bs, ts, _ = x_sample.shape
    np_, ns = bp * tp, bs * ts
    xp = x_prompt.reshape(np_, D_MODEL)
    xs = x_sample.reshape(ns, D_MODEL)

    n_vec = 1 + bs
    cvec = jnp.zeros((-(-n_vec // 8) * 8, D_MODEL), f32).at[0].set(c_ctx).at[1:n_vec].set(c)
    mods = _modulation(cvec, w_ada, b_ada).reshape(DEPTH, cvec.shape[0], N_MOD, D_MODEL)

    p_lb = jax.nn.softmax(hgrn_lb_logits.astype(f32), axis=0)
    lb_all = jnp.cumsum(p_lb, axis=0) - p_lb[0:1]
    tab_a, tab_c = _rope_tables(ts)
    mats = jnp.asarray(_hgrn_mats(), bf16)
    n_ffn = DEPTH * 2
    wg_all = _to_bf16(w_ffn_gate.reshape(n_ffn * D_MODEL, D_FF), 512).reshape(n_ffn, D_MODEL, D_FF)
    wu_all = _to_bf16(w_ffn_up.reshape(n_ffn * D_MODEL, D_FF), 512).reshape(n_ffn, D_MODEL, D_FF)
    wd_all = _to_bf16(w_ffn_down.reshape(n_ffn * D_FF, D_MODEL), D_FF // 2).reshape(n_ffn, D_FF, D_MODEL)
    grp_p = dict(base=0, rpm=np_)
    grp_s = dict(base=1, rpm=ts)
    tm, tm_mix = 512, 1024

    wb = lambda w: w.astype(bf16)
    w_row = wb(jnp.concatenate([w_in[:, :, 0:512], w_in[:, :, 1792:2176], _pad_rope_cols(w_in[:, :, 2176:2208]),
                                w_in[:, :, 2208:2464]], axis=2))
    w_t = wb(jnp.swapaxes(w_in[:, :, 512:1792], 1, 2))
    wq = wb(jnp.pad(mla_w_uq.reshape(DEPTH, Q_LORA_C, H_C, NOPE_C + ROPE_C),
                    ((0, 0), (0, 0), (0, 0), (0, LANES - NOPE_C - ROPE_C))).reshape(DEPTH, Q_LORA_C, H_C * LANES))
    wkv = mla_w_ukv.reshape(DEPTH, KV_LORA_C, H_C, NOPE_C + V_C)
    wk = wb(jnp.pad(wkv[..., :NOPE_C], ((0, 0), (0, 0), (0, 0), (0, LANES - NOPE_C)))
            .reshape(DEPTH, KV_LORA_C, H_C * LANES))
    wv = wb(jnp.swapaxes(wkv[..., NOPE_C:].reshape(DEPTH, KV_LORA_C, H_C * V_C), 1, 2))
    n_grp = len(POOL_WINDOWS)
    w_pool = wb(jnp.einsum('lgij,gh->lgihj', pool_w, jnp.eye(n_grp, dtype=f32))
                .reshape(DEPTH, n_grp * POOL_CH, n_grp * POOL_CH))
    lb = jnp.broadcast_to(lb_all[..., None], (DEPTH, 2, 256, LANES))
    ng = jnp.broadcast_to(jnp.tile(hgrn_out_norm, (1, H_B))[..., None], (DEPTH, 256, LANES))
    sink = attn_sink.astype(f32)
    qn = mla_q_norm.reshape(DEPTH, 1, Q_LORA_C)
    kvn = mla_kv_norm.reshape(DEPTH, 1, KV_LORA_C)
    pscale = pool_scale.reshape(DEPTH, 1, 256)
    wo = wb(w_out)
    norms = norm_sub.reshape(DEPTH * 3, 1, D_MODEL)
    ctx_c = (cache_mla_ckv, _pad_rope_cols(cache_mla_krope))

    new_k = new_v = new_st = new_ckv = new_kr = None
    for l in range(DEPTH):
        def ffn(x, grp, j, idx, mix=None, final_g=None):
            return _ffn(x, mods, norms, wg_all, wu_all, wd_all, layer=l, widx=2 * l + idx,
                        j=j, tm=tm, mix=mix, final_g=final_g, **grp)

        xp = ffn(xp, grp_p, 0, 0)
        xs = ffn(xs, grp_s, 0, 0)

        pr, pt = _mixer_in(xp, mods, norms, w_row, w_t, layer=l, tm=tm_mix, **grp_p)
        oa, new_k, new_v = _attn_ctx(pr, sink, None if l == 0 else (new_k, new_v), layer=l, batch=bp, t=tp)
        ob, new_st = _hgrn(pt, lb, ng, mats, layer=l, batch=bp, t=tp, emit_state=True, state_prev=new_st)
        oc, new_ckv, new_kr = _mla(pr, qn, kvn, wq, wk, wv, layer=l, batch=bp, t=tp, tq=tp, emit_cache=True,
                                   cache_prev=None if l == 0 else (new_ckv, new_kr))
        od = _pool(pr, w_pool, pscale, layer=l, batch=bp, t=tp, seqs=POOL_CTX_SEQS)
        mix_p = (oa, ob, oc, od, wo)

        pr, pt = _mixer_in(xs, mods, norms, w_row, w_t, layer=l, tm=tm_mix, **grp_s)
        oa = _attn_lat(pr, sink, cache_attn_k, cache_attn_v, tab_a, layer=l, batch=bs, t=ts)
        (ob,) = _hgrn(pt, lb, ng, mats, layer=l, batch=bs, t=ts, s0=state_hgrn)
        (oc,) = _mla(pr, qn, kvn, wq, wk, wv, layer=l, batch=bs, t=ts, tq=512, ctx=ctx_c, tab=tab_c)
        od = _pool(pr, w_pool, pscale, layer=l, batch=bs, t=ts)
        mix_s = (oa, ob, oc, od, wo)

        fin = final_norm.reshape(1, D_MODEL) if l == DEPTH - 1 else None
        xp = ffn(xp, grp_p, 2, 1, mix_p, fin)
        xs = ffn(xs, grp_s, 2, 1, mix_s, fin)

    return (xp.reshape(bp, tp, D_MODEL), xs.reshape(bs, ts, D_MODEL), new_k, new_v, new_st, new_ckv, new_kr)
```

```python
import functools

import numpy as np
import jax
import jax.numpy as jnp
from jax import lax
from jax.experimental import pallas as pl
from jax.experimental.pallas import tpu as pltpu

f32 = jnp.float32
bf16 = jnp.bfloat16

D_MODEL = 1024
DEPTH = 2
GRID_W = 64
EPS = 1e-6
LOG2E = 1.4426950408889634
ROPE_THETA = 10000.0
NEG_INF = -1e30
N_MOD = 9
D_FF = 2816
H_A, KV_A, HD_A = 4, 2, 64
G_A = H_A // KV_A
WINDOW_A = 128
BLOCK_A = 128
H_B, DK_B, DV_B = 4, 64, 64
H_C, Q_LORA_C, KV_LORA_C, NOPE_C, ROPE_C, V_C = 4, 256, 128, 64, 32, 64
POOL_WINDOWS = (2, 4, 8, 16)
POOL_CH = 64

LANES = 128
SUBLANES = 8
CAST_BLOCKS = 16
ROW_REDUCE_CHAINS = 8
MLA_ONES = 2 * SUBLANES
VMEM_LIMIT = 56 * 1024 * 1024
HG_TILE = LANES
HG_UNROLL = 2
HG_HALF = HG_TILE // 2
HG_MAX_EXP = 80.0
HG_BLOCK = 16
HG_LEVELS = (16, 32, 64)
ATTN_CTX_SEQS = 2
POOL_CTX_SEQS = 8
ATTN_LAT_BLOCKS = 4
POOL_PAD = 16

PR_W = 1280
PT_W = 1280

_NT = (((1,), (1,)), ((), ()))
_TN = (((0,), (0,)), ((), ()))


def _params(*sem):
    return pltpu.CompilerParams(dimension_semantics=sem, vmem_limit_bytes=VMEM_LIMIT)


def _const_spec(shape, lead=None):
    zeros = (0,) * len(shape)
    if lead is None:
        return pl.BlockSpec(shape, lambda *_: zeros, pipeline_mode=pl.Buffered(1))
    return pl.BlockSpec((None,) + tuple(shape), lambda *_: (lead,) + zeros, pipeline_mode=pl.Buffered(1))


def _dot(a, b):
    return jnp.dot(a, b, preferred_element_type=f32)


def _sigmoid(x):
    return 1.0 / (1.0 + jnp.exp(-x))


def _rms(x, g):
    return x * lax.rsqrt(jnp.mean(x * x, axis=-1, keepdims=True) + EPS) * g


def _mod_kernel(c_ref, w_ref, b_ref, o_ref):
    c = c_ref[...]
    a = (c * _sigmoid(c)).astype(bf16)
    o_ref[0] = _dot(a, w_ref[0].astype(bf16)) + b_ref[0]


def _modulation(cvec, w_ada, b_ada):
    tn = 1536
    n_out = N_MOD * D_MODEL
    return pl.pallas_call(
        _mod_kernel,
        out_shape=jax.ShapeDtypeStruct((DEPTH, cvec.shape[0], n_out), f32),
        grid=(DEPTH, n_out // tn),
        in_specs=[pl.BlockSpec(cvec.shape, lambda l, j: (0, 0)),
                  pl.BlockSpec((1, D_MODEL, tn), lambda l, j: (l, 0, j)),
                  pl.BlockSpec((1, 1, tn), lambda l, j: (l, 0, j))],
        out_specs=pl.BlockSpec((1, cvec.shape[0], tn), lambda l, j: (l, 0, j)),
        compiler_params=_params("arbitrary", "arbitrary"),
        name="modulation",
    )(cvec, w_ada, b_ada.reshape(DEPTH, 1, n_out))


def _cast_kernel(x_ref, o_ref):
    o_ref[...] = x_ref[...].astype(bf16)


def _to_bf16(w, layer, idx):
    r, c = w.shape[2:]
    rows = r // CAST_BLOCKS
    return pl.pallas_call(
        _cast_kernel,
        out_shape=jax.ShapeDtypeStruct((r, c), bf16),
        grid=(CAST_BLOCKS,),
        in_specs=[pl.BlockSpec((None, None, rows, c), lambda i: (layer, idx, i, 0))],
        out_specs=pl.BlockSpec((rows, c), lambda i: (i, 0)),
        compiler_params=_params("parallel"),
        name="cast_bf16",
    )(w)


def _ffn_kernel(*refs, j, mix, final, side):
    it = iter(refs)
    x_ref, m_ref = next(it), next(it)
    mix_refs = [next(it) for _ in range(5)] if mix else None
    g_ref, wg_ref, wu_ref, wd_ref = next(it), next(it), next(it), next(it)
    fg_ref = next(it) if final else None
    side_in = [next(it) for _ in range(3)] if side else []
    o_ref = next(it)
    for src, dst in zip(side_in, [next(it) for _ in range(len(side_in))]):
        dst[...] = src[...].astype(bf16)
    x = x_ref[...]
    m = m_ref[0]
    if mix:
        wo_ref = mix_refs[4]
        acc = _dot(mix_refs[0][...], wo_ref[0:256, :])
        for p in range(1, 4):
            acc += _dot(mix_refs[p][...], wo_ref[256 * p:256 * (p + 1), :])
        x = x + m[5:6] * acc
    h = (_rms(x, g_ref[...]) * (1.0 + m[3 * j + 1:3 * j + 2]) + m[3 * j:3 * j + 1]).astype(bf16)
    gate = _dot(h, wg_ref[...])
    up = _dot(h, wu_ref[...])
    act = (gate * _sigmoid(gate) * up).astype(bf16)
    y = x + (0.5 * m[3 * j + 2:3 * j + 3]) * _dot(act, wd_ref[...])
    if final:
        y = _rms(y, fg_ref[...])
    o_ref[...] = y


def _mod_spec(layer, base, rpm, tm):
    return pl.BlockSpec((None, 1, N_MOD, D_MODEL), lambda i: (layer, base + (i * tm) // rpm, 0, 0))


def _ffn(x, mods, norms, w3, *, layer, j, base, rpm, tm, mix=None, final_g=None, side=None):
    n = x.shape[0]
    steps = n // tm
    in_specs = [pl.BlockSpec((tm, D_MODEL), lambda i: (i, 0)), _mod_spec(layer, base, rpm, tm)]
    args = [x, mods]
    if mix is not None:
        in_specs += [pl.BlockSpec((tm, 256), lambda i: (i, 0))] * 4 + [_const_spec((D_MODEL, D_MODEL), layer)]
        args += list(mix)
    in_specs += [_const_spec((1, D_MODEL), 3 * layer + j)] + [_const_spec(w.shape) for w in w3]
    args += [norms] + list(w3)
    if final_g is not None:
        in_specs.append(_const_spec((1, D_MODEL)))
        args.append(final_g)
    out_shape = [jax.ShapeDtypeStruct((n, D_MODEL), f32)]
    out_specs = [pl.BlockSpec((tm, D_MODEL), lambda i: (i, 0))]
    if side is not None:
        sl, sidx, arrays = side
        nblk = CAST_BLOCKS if steps % CAST_BLOCKS == 0 else steps
        for a in arrays:
            assert a.shape[2] % (nblk * 2 * SUBLANES) == 0, (a.shape, nblk)
            r, c = a.shape[2] // nblk, a.shape[3]
            in_specs.append(pl.BlockSpec((None, None, r, c), lambda i: (sl, sidx, (i * nblk) // steps, 0)))
            args.append(a)
            out_shape.append(jax.ShapeDtypeStruct(a.shape[2:], bf16))
            out_specs.append(pl.BlockSpec((r, c), lambda i: ((i * nblk) // steps, 0)))
    return pl.pallas_call(
        functools.partial(_ffn_kernel, j=j, mix=mix is not None, final=final_g is not None, side=side is not None),
        out_shape=tuple(out_shape),
        grid=(steps,),
        in_specs=in_specs,
        out_specs=tuple(out_specs),
        compiler_params=_params("arbitrary" if side is not None else "parallel"),
        name="ffn",
    )(*args)


def _mixin_kernel(x_ref, m_ref, g_ref, wr_ref, wt_ref, pr_ref, pt_ref):
    m = m_ref[0]
    h = (_rms(x_ref[...], g_ref[...]) * (1.0 + m[4:5]) + m[3:4]).astype(bf16)
    pr_ref[...] = _dot(h, wr_ref[...])
    pt_ref[...] = lax.dot_general(wt_ref[...], h, _NT, preferred_element_type=f32)


def _mixer_in(x, mods, norms, w_row, w_t, *, layer, base, rpm, tm):
    n = x.shape[0]
    return pl.pallas_call(
        _mixin_kernel,
        out_shape=(jax.ShapeDtypeStruct((n, PR_W), f32), jax.ShapeDtypeStruct((PT_W, n), f32)),
        grid=(n // tm,),
        in_specs=[pl.BlockSpec((tm, D_MODEL), lambda i: (i, 0)), _mod_spec(layer, base, rpm, tm),
                  _const_spec((1, D_MODEL), 3 * layer + 1), _const_spec((D_MODEL, PR_W), layer),
                  _const_spec((PT_W, D_MODEL), layer)],
        out_specs=(pl.BlockSpec((tm, PR_W), lambda i: (i, 0)), pl.BlockSpec((PT_W, tm), lambda i: (0, i))),
        compiler_params=_params("parallel"),
        name="mixer_in",
    )(x, mods, norms, w_row, w_t)


def _rope(x, tab_ref, rows, lanes, shift):
    width = x.shape[-1]
    c = tab_ref[0, rows, lanes]
    sa = tab_ref[1, rows, lanes]
    sb = tab_ref[2, rows, lanes]
    return x * c + pltpu.roll(x, shift, axis=1) * sa + pltpu.roll(x, width - shift, axis=1) * sb


def _softmax_parts(s_list, sink):
    sink2 = sink * LOG2E
    mx = functools.reduce(jnp.maximum, [jnp.max(s, axis=-1, keepdims=True) for s in s_list])
    mx = jnp.maximum(mx, sink2)
    ps = [jnp.exp2(s - mx) for s in s_list]
    den = functools.reduce(lambda a, b: a + b, [jnp.sum(p, axis=-1, keepdims=True) for p in ps])
    return ps, 1.0 / (den + jnp.exp2(sink2 - mx))


def _kv_per_query_head(x):
    return jnp.concatenate([x[:, HD_A * (h // G_A):HD_A * (h // G_A + 1)] for h in range(H_A)], axis=1)


def _stack_heads(q):
    head = lax.broadcasted_iota(jnp.int32, (1, H_A * HD_A), 1) // HD_A
    return jnp.concatenate([jnp.where(head == h, q, 0.0) for h in range(H_A)], axis=0)


def _pick_heads(o, r):
    head = lax.broadcasted_iota(jnp.int32, (1, H_A * HD_A), 1) // HD_A
    out = o[0:r]
    for h in range(1, H_A):
        out = jnp.where(head == h, o[h * r:(h + 1) * r], out)
    return out


def _sink_rows(sink_ref, layer, r):
    row = lax.broadcasted_iota(jnp.int32, (H_A * r, 1), 0)
    out = jnp.full((H_A * r, 1), sink_ref[layer, 0], f32)
    for h in range(1, H_A):
        out = jnp.where(row >= h * r, sink_ref[layer, h], out)
    return out


def _attn_ctx_kernel(*refs, t, n_prev, layer):
    if n_prev:
        sink_ref, q_ref, kv_ref, kp_ref, vp_ref, o_ref, ko_ref, vo_ref = refs
        ko_ref[:, 0:n_prev] = kp_ref[...]
        vo_ref[:, 0:n_prev] = vp_ref[...]
    else:
        sink_ref, q_ref, kv_ref, o_ref, ko_ref, vo_ref = refs
    kvw = KV_A * HD_A
    sink = _sink_rows(sink_ref, layer, t)
    for u in range(ATTN_CTX_SEQS):
        rows = slice(u * t, (u + 1) * t)
        kv = kv_ref[rows, :]
        for j in range(KV_A):
            ko_ref[u, n_prev, j] = kv[:, HD_A * j:HD_A * (j + 1)]
            vo_ref[u, n_prev, j] = kv[:, kvw + HD_A * j:kvw + HD_A * (j + 1)]
        kx = _kv_per_query_head(kv[:, 0:kvw]).astype(bf16)
        vx = _kv_per_query_head(kv[:, kvw:2 * kvw]).astype(bf16)
        q4 = _stack_heads(q_ref[rows, :] * (HD_A ** -0.5 * LOG2E)).astype(bf16)
        s = lax.dot_general(q4, kx, _NT, preferred_element_type=f32)
        (p,), inv = _softmax_parts([s], sink)
        o_ref[rows, :] = _pick_heads(_dot(p.astype(bf16), vx) * inv, t).astype(o_ref.dtype)


def _cache_spec(n_seq, n_layers, *dims):
    return pl.BlockSpec((n_seq, n_layers) + dims, lambda b: (b,) + (0,) * (1 + len(dims)))


def _attn_ctx(pr, sink, prev, *, layer, batch, t):
    n_prev = 0 if prev is None else prev[0].shape[1]
    ns = ATTN_CTX_SEQS
    kvshape = jax.ShapeDtypeStruct((batch, n_prev + 1, KV_A, t, HD_A), f32)
    in_specs = [pl.BlockSpec(memory_space=pltpu.SMEM),
                pl.BlockSpec((ns * t, 256), lambda b: (b, 0)), pl.BlockSpec((ns * t, 256), lambda b: (b, 1))]
    args = [sink, pr, pr]
    if n_prev:
        in_specs += [_cache_spec(ns, n_prev, KV_A, t, HD_A)] * 2
        args += list(prev)
    return pl.pallas_call(
        functools.partial(_attn_ctx_kernel, t=t, n_prev=n_prev, layer=layer),
        out_shape=(jax.ShapeDtypeStruct((batch * t, 256), bf16), kvshape, kvshape),
        grid=(batch // ns,),
        in_specs=in_specs,
        out_specs=(pl.BlockSpec((ns * t, 256), lambda b: (b, 0)),
                   _cache_spec(ns, n_prev + 1, KV_A, t, HD_A), _cache_spec(ns, n_prev + 1, KV_A, t, HD_A)),
        compiler_params=_params("parallel"),
        name="attn_ctx",
    )(*args)


def _attn_lat_kernel(sink_ref, q_ref, kv_ref, kc_ref, vc_ref, tab_ref, o_ref, kpad, vpad, kctx, vctx, *, t, layer):
    n = pl.program_id(1)
    kvw = KV_A * HD_A
    hw = H_A * HD_A

    @pl.when(n == 0)
    def _():
        zeros = jnp.zeros((BLOCK_A, hw), bf16)
        kv = kv_ref[...]
        k = _rope(kv[:, 0:kvw], tab_ref, slice(None), slice(0, kvw), HD_A // 2)
        kpad[0:BLOCK_A, :] = zeros
        vpad[0:BLOCK_A, :] = zeros
        kpad[BLOCK_A:BLOCK_A + t, :] = _kv_per_query_head(k).astype(bf16)
        vpad[BLOCK_A:BLOCK_A + t, :] = _kv_per_query_head(kv[:, kvw:2 * kvw]).astype(bf16)
        kpad[BLOCK_A + t:2 * BLOCK_A + t, :] = zeros
        vpad[BLOCK_A + t:2 * BLOCK_A + t, :] = zeros
        kctx[...] = jnp.concatenate([kc_ref[0, h // G_A] for h in range(H_A)], axis=1).astype(bf16)
        vctx[...] = jnp.concatenate([vc_ref[0, h // G_A] for h in range(H_A)], axis=1).astype(bf16)

    rows = H_A * BLOCK_A
    rr = lax.broadcasted_iota(jnp.int32, (rows, 1), 0) % BLOCK_A
    cidx = lax.broadcasted_iota(jnp.int32, (rows, 3 * BLOCK_A), 1)
    sink = _sink_rows(sink_ref, layer, BLOCK_A)
    for u in range(ATTN_LAT_BLOCKS):
        blk = n * ATTN_LAT_BLOCKS + u
        row0 = pl.multiple_of(blk * BLOCK_A, BLOCK_A)
        q = _rope(q_ref[u * BLOCK_A:(u + 1) * BLOCK_A, :], tab_ref, pl.ds(row0, BLOCK_A), slice(None), HD_A // 2)
        q4 = _stack_heads(q * (HD_A ** -0.5 * LOG2E)).astype(bf16)
        kband = kpad[pl.ds(row0, 3 * BLOCK_A), :]
        vband = vpad[pl.ds(row0, 3 * BLOCK_A), :]
        s_loc = lax.dot_general(q4, kband, _NT, preferred_element_type=f32)
        s_ctx = lax.dot_general(q4, kctx[...], _NT, preferred_element_type=f32)
        lo = jnp.maximum(rr + (BLOCK_A - WINDOW_A), jnp.maximum(0, (1 - blk) * BLOCK_A))
        hi = jnp.minimum(rr + (BLOCK_A + WINDOW_A), jnp.minimum(3 * BLOCK_A, t - (blk - 1) * BLOCK_A) - 1)
        s_loc = jnp.where((cidx >= lo) & (cidx <= hi), s_loc, NEG_INF)
        (p_loc, p_ctx), inv = _softmax_parts([s_loc, s_ctx], sink)
        o = (_dot(p_loc.astype(bf16), vband) + _dot(p_ctx.astype(bf16), vctx[...])) * inv
        o_ref[u * BLOCK_A:(u + 1) * BLOCK_A, :] = _pick_heads(o, BLOCK_A).astype(o_ref.dtype)


def _attn_lat(pr, sink, kc, vc, tab, *, layer, batch, t):
    rows = ATTN_LAT_BLOCKS * BLOCK_A
    nb = t // rows
    lc = kc.shape[3]
    cspec = pl.BlockSpec((1, None, KV_A, lc, HD_A), lambda b, n: (b, layer, 0, 0, 0))
    return pl.pallas_call(
        functools.partial(_attn_lat_kernel, t=t, layer=layer),
        out_shape=jax.ShapeDtypeStruct((batch * t, 256), bf16),
        grid=(batch, nb),
        in_specs=[pl.BlockSpec(memory_space=pltpu.SMEM),
                  pl.BlockSpec((rows, 256), lambda b, n: (b * nb + n, 0)),
                  pl.BlockSpec((t, 256), lambda b, n: (b, 1)),
                  cspec, cspec,
                  pl.BlockSpec((3, t, 256), lambda b, n: (0, 0, 0))],
        out_specs=pl.BlockSpec((rows, 256), lambda b, n: (b * nb + n, 0)),
        scratch_shapes=[pltpu.VMEM((t + 2 * BLOCK_A, H_A * HD_A), bf16),
                        pltpu.VMEM((t + 2 * BLOCK_A, H_A * HD_A), bf16),
                        pltpu.VMEM((lc, H_A * HD_A), bf16), pltpu.VMEM((lc, H_A * HD_A), bf16)],
        compiler_params=_params("parallel", "arbitrary"),
        name="attn_lat",
    )(sink, pr, pr, kc, vc, tab)


def _reduce_rows(x, reduce_fn):
    r, c = x.shape
    g = ROW_REDUCE_CHAINS
    if r % (g * SUBLANES):
        return reduce_fn(x, axis=0, keepdims=True)
    part = reduce_fn(x.reshape(g, r // (g * SUBLANES), SUBLANES, c), axis=1)
    return reduce_fn(reduce_fn(part, axis=0), axis=0, keepdims=True)


def _mla_kernel(*refs, t, lc, tq, rope, emit_cache, n_prev):
    it = iter(refs)
    cq_ref, ckv_ref, kr_ref = next(it), next(it), next(it)
    cc_ref = kc_ref = tab_ref = None
    if lc:
        cc_ref, kc_ref = next(it), next(it)
    if rope:
        tab_ref = next(it)
    qn_ref, kvn_ref, wq_ref, wk_ref, wv_ref = next(it), next(it), next(it), next(it), next(it)
    ckvp_ref, krp_ref = (next(it), next(it)) if n_prev else (None, None)
    o_ref = next(it)
    ckv_out_ref, kr_out_ref = (next(it), next(it)) if emit_cache else (None, None)
    kbuf, vbuf = next(it), next(it)
    i = pl.program_id(1)
    hw = LANES
    va = V_C + MLA_ONES

    @pl.when(i == 0)
    def _():
        ckv = _rms(ckv_ref[...], kvn_ref[...])
        kr = kr_ref[...]
        if emit_cache:
            if n_prev:
                ckv_out_ref[0, 0:n_prev] = ckvp_ref[0]
                kr_out_ref[0, 0:n_prev] = krp_ref[0]
            ckv_out_ref[0, n_prev] = ckv
            kr_out_ref[0, n_prev] = kr[:, NOPE_C:NOPE_C + ROPE_C]
        ckv_b = ckv.astype(bf16)
        if rope:
            kr = _rope(kr, tab_ref, slice(None), slice(None), ROPE_C // 2)
        kk = _dot(ckv_b, wk_ref[...])
        for h in range(H_C):
            kbuf[h, 0:t, :] = (kk[:, hw * h:hw * (h + 1)] + kr).astype(bf16)
        def put_values(vt, cols):
            for h in range(H_C):
                vbuf[va * h:va * h + V_C, cols] = vt[V_C * h:V_C * (h + 1)]

        for h in range(H_C):
            vbuf[va * h + V_C:va * (h + 1), :] = jnp.ones((MLA_ONES, t + lc), bf16)
        put_values(lax.dot_general(wv_ref[...], ckv_b, _NT, preferred_element_type=f32).astype(bf16), slice(0, t))
        if lc:
            cc_b = cc_ref[0].astype(bf16)
            kkc = _dot(cc_b, wk_ref[...])
            krc = kc_ref[0]
            for h in range(H_C):
                kbuf[h, t:t + lc, :] = (kkc[:, hw * h:hw * (h + 1)] + krc).astype(bf16)
            put_values(lax.dot_general(wv_ref[...], cc_b, _NT, preferred_element_type=f32).astype(bf16),
                       slice(t, t + lc))

    cqn = _rms(cq_ref[...], qn_ref[...]).astype(bf16)
    qf = _dot(cqn, wq_ref[...])
    c2 = (NOPE_C + ROPE_C) ** -0.5 * LOG2E
    row0 = pl.multiple_of(i * tq, tq)
    n_keys = t + lc
    half = n_keys // 2

    def scores(h):
        qh = qf[:, hw * h:hw * (h + 1)]
        if rope:
            qh = _rope(qh, tab_ref, pl.ds(row0, tq), slice(None), ROPE_C // 2)
        qb = (qh * c2).astype(bf16)
        return [lax.dot_general(kbuf[h, c * half:(c + 1) * half, :], qb, _NT, preferred_element_type=f32)
                for c in range(2)]

    def probs(s):
        m = jnp.maximum(_reduce_rows(s[0], jnp.max), _reduce_rows(s[1], jnp.max))
        return [jnp.exp2(x - m).astype(bf16) for x in s]

    def values(h, p):
        rows = slice(va * h, va * (h + 1))
        o = _dot(vbuf[rows, 0:half], p[0]) + _dot(vbuf[rows, half:2 * half], p[1])
        return o[0:V_C] * (1.0 / o[V_C:V_C + 1])

    s_next = scores(0)
    outs = []
    for h in range(H_C):
        s_cur = s_next
        if h + 1 < H_C:
            s_next = scores(h + 1)
        outs.append(values(h, probs(s_cur)))
    o_ref[...] = jnp.concatenate(outs, axis=0).T.astype(o_ref.dtype)


def _mla(pr, qn, kvn, wq, wk, wv, *, layer, batch, t, tq, ctx=None, tab=None, emit_cache=False, cache_prev=None):
    nq = t // tq
    lc = 0 if ctx is None else ctx[0].shape[2]
    n_prev = 0 if cache_prev is None else cache_prev[0].shape[1]
    in_specs = [pl.BlockSpec((tq, 256), lambda b, i: (b * nq + i, 2)),
                pl.BlockSpec((t, LANES), lambda b, i: (b, 6)),
                pl.BlockSpec((t, LANES), lambda b, i: (b, 7))]
    args = [pr, pr, pr]
    if lc:
        in_specs += [pl.BlockSpec((1, None, lc, LANES), lambda b, i: (b, layer, 0, 0))] * 2
        args += list(ctx)
    if tab is not None:
        in_specs.append(pl.BlockSpec((3, t, LANES), lambda b, i: (0, 0, 0)))
        args.append(tab)
    in_specs += [_const_spec(a.shape[1:], layer) for a in (qn, kvn, wq, wk, wv)]
    args += [qn, kvn, wq, wk, wv]

    def cspec(n_layers, width):
        return pl.BlockSpec((1, n_layers, t, width), lambda b, i: (b, 0, 0, 0))

    if n_prev:
        in_specs += [cspec(n_prev, KV_LORA_C), cspec(n_prev, ROPE_C)]
        args += list(cache_prev)
    out_shape = [jax.ShapeDtypeStruct((batch * t, 256), bf16)]
    out_specs = [pl.BlockSpec((tq, 256), lambda b, i: (b * nq + i, 0))]
    if emit_cache:
        out_shape += [jax.ShapeDtypeStruct((batch, n_prev + 1, t, KV_LORA_C), f32),
                      jax.ShapeDtypeStruct((batch, n_prev + 1, t, ROPE_C), f32)]
        out_specs += [cspec(n_prev + 1, KV_LORA_C), cspec(n_prev + 1, ROPE_C)]
    return pl.pallas_call(
        functools.partial(_mla_kernel, t=t, lc=lc, tq=tq, rope=tab is not None, emit_cache=emit_cache,
                          n_prev=n_prev),
        out_shape=tuple(out_shape),
        grid=(batch, nq),
        in_specs=in_specs,
        out_specs=tuple(out_specs),
        scratch_shapes=[pltpu.VMEM((H_C, t + lc, LANES), bf16),
                        pltpu.VMEM((H_C * (V_C + MLA_ONES), t + lc), bf16)],
        compiler_params=_params("parallel", "arbitrary"),
        name="mla",
    )(*args)


def _pool_counts(t):
    pos = np.arange(t)[:, None]
    half = np.repeat(np.asarray(POOL_WINDOWS) // 2, POOL_CH)[None, :]
    return (np.minimum(pos + half, t) - np.maximum(pos - half, 0)).astype(np.float32)


def _pool_kernel(x_ref, cnt_ref, w_ref, sc_ref, o_ref, buf_a, buf_b, buf_c, *, t, seqs):
    for u in range(seqs):
        _pool_sequence(x_ref, cnt_ref, w_ref, sc_ref, o_ref, buf_a, buf_b, buf_c, slice(u * t, (u + 1) * t), t)


def _pool_sequence(x_ref, cnt_ref, w_ref, sc_ref, o_ref, buf_a, buf_b, buf_c, seq, t):
    x = x_ref[seq, :]
    pad, edge = POOL_PAD, SUBLANES
    n = t + 2 * pad
    m = n - 2 * edge
    zeros = jnp.zeros((pad, 256), f32)
    buf_a[0:pad, :] = zeros
    buf_a[pad:pad + t, :] = x
    buf_a[pad + t:n, :] = zeros
    for buf in (buf_b, buf_c):
        buf[0:edge, :] = zeros[0:edge]
        buf[n - edge:n, :] = zeros[0:edge]
    inner = slice(edge, edge + m)
    buf_b[inner, :] = buf_a[pl.ds(edge - 1, m), :] + buf_a[pl.ds(edge, m), :]
    buf_c[inner, :] = buf_b[pl.ds(edge - 1, m), :] + buf_b[pl.ds(edge + 1, m), :]
    lo, hi = slice(0, LANES), slice(LANES, 2 * LANES)
    buf_a[inner, hi] = buf_c[pl.ds(edge - 2, m), hi] + buf_c[pl.ds(edge + 2, m), hi]
    s16 = buf_a[pl.ds(pad - 4, t), hi] + buf_a[pl.ds(pad + 4, t), hi]
    first = lax.broadcasted_iota(jnp.int32, (1, LANES), 1) < POOL_CH
    rows = slice(pad, pad + t)
    win_sum = jnp.concatenate([jnp.where(first, buf_b[rows, lo], buf_c[rows, lo]),
                               jnp.where(first, buf_a[rows, hi], s16)], axis=1)
    mean = win_sum / cnt_ref[...]
    o_ref[seq, :] = (_dot((mean - x).astype(bf16), w_ref[...]) * sc_ref[...]).astype(o_ref.dtype)


def _pool(pr, w_bd, scale, *, layer, batch, t, seqs=1):
    return pl.pallas_call(
        functools.partial(_pool_kernel, t=t, seqs=seqs),
        out_shape=jax.ShapeDtypeStruct((batch * t, 256), bf16),
        grid=(batch // seqs,),
        in_specs=[pl.BlockSpec((seqs * t, 256), lambda b: (b, 4)), _const_spec((t, 256)),
                  _const_spec((256, 256), layer), _const_spec((1, 256), layer)],
        out_specs=pl.BlockSpec((seqs * t, 256), lambda b: (b, 0)),
        scratch_shapes=[pltpu.VMEM((t + 2 * POOL_PAD, 256), f32)] * 3,
        compiler_params=_params("parallel"),
        name="pool",
    )(pr, jnp.asarray(_pool_counts(t)), w_bd, scale)


def _hgrn_mats():
    n = HG_TILE
    s = np.arange(n)[:, None]
    t = np.arange(n)[None, :]
    same_half = s // HG_HALF == t // HG_HALF
    mid = (t // HG_HALF) * HG_HALF + HG_HALF // 2 - 1
    rel_mid = (same_half & (s > mid) & (s <= t)).astype(np.float32) - (same_half & (s <= mid) & (s > t))
    mats = [rel_mid, same_half & (s <= t), same_half & (s > t), s <= t, s > t,
            (s // HG_BLOCK == t // HG_BLOCK) & (s <= t)]
    for m in HG_LEVELS:
        same_pair = s // (2 * m) == t // (2 * m)
        right_t = (t // m) % 2 == 1
        right_s = (s // m) % 2 == 1
        q_side = same_pair & right_t & right_s & (s <= t)
        k_side = same_pair & ~right_t & ~right_s & (s > t)
        mats.append(q_side | k_side)
    fwd = np.concatenate([m.astype(np.float32) for m in mats], axis=1)
    bwd = np.concatenate([m[::-1, ::-1].astype(np.float32) for m in mats], axis=1)
    both = np.stack([fwd, bwd])
    return np.concatenate([both, both], axis=1)


def _hgrn_exact_intra(q, k, v, vb, hl, mats, reverse):
    n = HG_TILE
    e_all = _dot(hl, mats)
    beta = e_all[:, 0:n]
    lane = lax.broadcasted_iota(jnp.int32, (1, n), 1)
    pos = lane % HG_BLOCK
    v3 = v.reshape(H_B, DV_B, n)

    o3 = jnp.sum((q * k).reshape(H_B, DK_B, n), axis=1, keepdims=True) * v3
    for d in range(1, HG_BLOCK):
        shift = (n - d) if reverse else d
        valid = (pos <= HG_BLOCK - 1 - d) if reverse else (pos >= d)
        kr = pltpu.roll(k, shift, axis=1)
        br = pltpu.roll(beta, shift, axis=1)
        vr = pltpu.roll(v, shift, axis=1)
        e = jnp.exp(jnp.where(valid, beta - br, NEG_INF))
        sc = jnp.sum((q * kr * e).reshape(H_B, DK_B, n), axis=1, keepdims=True)
        o3 = o3 + sc * vr.reshape(H_B, DV_B, n)

    srow = lax.broadcasted_iota(jnp.int32, (n, n), 0)
    tcol = lax.broadcasted_iota(jnp.int32, (n, n), 1)
    scores = [jnp.zeros((n, n), f32) for _ in range(H_B)]
    for li, m in enumerate(HG_LEVELS):
        ex = jnp.exp(e_all[:, (li + 1) * n:(li + 2) * n])
        is_q = ((lane // m) % 2) == (0 if reverse else 1)
        qs = jnp.where(is_q, q * ex, 0.0).astype(bf16)
        ks = jnp.where(is_q, 0.0, k * ex).astype(bf16)
        pair = (srow // (2 * m)) == (tcol // (2 * m))
        for h in range(H_B):
            a = lax.dot_general(ks[DK_B * h:DK_B * (h + 1)], qs[DK_B * h:DK_B * (h + 1)], _TN,
                                preferred_element_type=f32)
            scores[h] = scores[h] + jnp.where(pair, a, 0.0)

    return [o3[h] + _dot(vb[DV_B * h:DV_B * (h + 1)], scores[h].astype(bf16)) for h in range(H_B)]


def _hgrn_common(q, v, fpre, lb, mats_ref, sidx, st_ref):
    n = HG_TILE
    f = lb + (1.0 - lb) * _sigmoid(fpre)
    k = 1.0 - f
    logf = jnp.log(f)
    hi = logf.astype(bf16)
    hl = jnp.concatenate([hi, (logf - hi.astype(f32)).astype(bf16)], axis=1)
    ex = _dot(hl, mats_ref[sidx, :, 0:5 * n])
    ones = jnp.ones((8, 2 * n), bf16)
    g_row = lax.dot_general(ones, hl, _NT, preferred_element_type=f32)[0:1]
    d_mid = ex[:, 0:n]
    small = jnp.max(jnp.abs(d_mid)) <= HG_MAX_EXP
    q_in = (q * jnp.exp(ex[:, 3 * n:4 * n])).astype(bf16)
    k_out = (k * jnp.exp(ex[:, 4 * n:5 * n])).astype(bf16)
    vb = v.astype(bf16)
    o_state = []
    for h in range(H_B):
        rows = slice(DK_B * h, DK_B * (h + 1))
        st = st_ref[sidx, rows, :]
        o_state.append(_dot(st.astype(bf16), q_in[rows]))
        upd = lax.dot_general(vb[rows], k_out[rows], _NT, preferred_element_type=f32)
        st_ref[sidx, rows, :] = st * jnp.exp(g_row[:, rows]) + upd
    return dict(q=q, k=k, v=v, vb=vb, hl=hl, ex=ex, o_state=o_state, small=small)


def _hgrn_fast_intra(c, reverse):
    n = HG_TILE
    q, k, ex = c["q"], c["k"], c["ex"]
    lane = lax.broadcasted_iota(jnp.int32, (1, n), 1)
    in0 = (lane // HG_HALF) == 0
    q_half = in0 if reverse else jnp.logical_not(in0)
    d_mid = ex[:, 0:n]
    qd = q * jnp.exp(d_mid)
    kd = k * jnp.exp(-d_mid)
    ql = jnp.where(q_half, q * jnp.exp(ex[:, n:2 * n]), 0.0)
    kl = jnp.where(q_half, 0.0, k * jnp.exp(ex[:, 2 * n:3 * n]))
    parts_q = [jnp.where(in0, qd, 0.0).astype(bf16), jnp.where(in0, 0.0, qd).astype(bf16), ql.astype(bf16)]
    parts_k = [jnp.where(in0, kd, 0.0).astype(bf16), jnp.where(in0, 0.0, kd).astype(bf16), kl.astype(bf16)]
    srow = lax.broadcasted_iota(jnp.int32, (n, n), 0)
    tcol = lax.broadcasted_iota(jnp.int32, (n, n), 1)
    causal = (srow >= tcol) if reverse else (srow <= tcol)
    scores = []
    for h in range(H_B):
        rows = slice(DK_B * h, DK_B * (h + 1))
        k3 = jnp.concatenate([p[rows] for p in parts_k], axis=0)
        q3 = jnp.concatenate([p[rows] for p in parts_q], axis=0)
        scores.append(lax.dot_general(k3, q3, _TN, preferred_element_type=f32))
    scores = [jnp.where(causal, a, 0.0).astype(bf16) for a in scores]
    outs = [c["o_state"][h] + _dot(c["vb"][DV_B * h:DV_B * (h + 1)], scores[h]) for h in range(H_B)]
    return jnp.concatenate(outs, axis=0)


def _hgrn_exact(c, mats, reverse):
    intra = _hgrn_exact_intra(c["q"], c["k"], c["v"], c["vb"], c["hl"], mats, reverse)
    return jnp.concatenate([c["o_state"][h] + intra[h] for h in range(H_B)], axis=0)


def _hgrn_kernel(*refs, t, has_s0, emit_state, n_prev):
    it = iter(refs)
    q_ref, i_ref, ff_ref, fb_ref, g_ref, lb_ref, ng_ref, mats_ref = [next(it) for _ in range(8)]
    s0_ref = next(it) if has_s0 else None
    sprev_ref = next(it) if n_prev else None
    o_ref = next(it)
    sfin_ref = next(it) if emit_state else None
    st_ref, oacc_f, oacc_b = next(it), next(it), next(it)
    n = HG_TILE
    nt = t // n
    for d in range(2):
        for h in range(H_B):
            rows = slice(DV_B * h, DV_B * (h + 1))
            st_ref[d, rows, :] = s0_ref[0, d, h].T if has_s0 else jnp.zeros((DV_B, DK_B), f32)

    def scan(i, c):
        work = []
        small = None
        for u in range(HG_UNROLL):
            j = i * HG_UNROLL + u
            cols_f = pl.ds(pl.multiple_of(j * n, n), n)
            cols_b = pl.ds(pl.multiple_of((nt - 1 - j) * n, n), n)
            cf = _hgrn_common(q_ref[:, cols_f], i_ref[:, cols_f], ff_ref[:, cols_f], lb_ref[0], mats_ref, 0, st_ref)
            cb = _hgrn_common(q_ref[:, cols_b], i_ref[:, cols_b], fb_ref[:, cols_b], lb_ref[1], mats_ref, 1, st_ref)
            work.append((cols_f, cf, cols_b, cb))
            both = jnp.logical_and(cf["small"], cb["small"])
            small = both if small is None else jnp.logical_and(small, both)
        for cols_f, cf, cols_b, cb in work:
            oacc_f[:, cols_f] = _hgrn_fast_intra(cf, False)
            oacc_b[:, cols_b] = _hgrn_fast_intra(cb, True)

        @pl.when(jnp.logical_not(small))
        def _():
            for cols_f, cf, cols_b, cb in work:
                oacc_f[:, cols_f] = _hgrn_exact(cf, mats_ref[0, :, 5 * n:9 * n], False)
                oacc_b[:, cols_b] = _hgrn_exact(cb, mats_ref[1, :, 5 * n:9 * n], True)

        return c

    lax.fori_loop(0, nt // HG_UNROLL, scan, 0)

    def finish(i, c):
        off = pl.multiple_of(i * n, n)
        cols = pl.ds(off, n)
        o = oacc_f[:, cols] + oacc_b[:, cols]
        o3 = o.reshape(H_B, DV_B, n)
        y = o3 * lax.rsqrt(jnp.mean(o3 * o3, axis=1, keepdims=True) + EPS)
        gate = g_ref[:, cols]
        y = y.reshape(H_B * DV_B, n) * ng_ref[...] * (gate * _sigmoid(gate))
        o_ref[pl.ds(off, n), :] = y.T.astype(o_ref.dtype)
        return c

    lax.fori_loop(0, nt, finish, 0)
    if emit_state:
        if n_prev:
            sfin_ref[0, 0:n_prev] = sprev_ref[0]
        for d in range(2):
            for h in range(H_B):
                sfin_ref[0, n_prev, d, h] = st_ref[d, DV_B * h:DV_B * (h + 1), :].T


def _hgrn(pt, lb, ng, mats, *, batch, t, s0=None, layer=0, emit_state=False, state_prev=None):
    def row(r):
        return pl.BlockSpec((256, t), lambda b: (r, b))

    def sspec(n_layers):
        return pl.BlockSpec((1, n_layers, 2, H_B, DK_B, DV_B), lambda b: (b, 0, 0, 0, 0, 0))

    n_prev = 0 if state_prev is None else state_prev.shape[1]
    in_specs = [row(0), row(1), row(2), row(3), row(4),
                _const_spec((2, 256, LANES), layer), _const_spec((256, LANES), layer), _const_spec(mats.shape)]
    args = [pt, pt, pt, pt, pt, lb, ng, mats]
    if s0 is not None:
        in_specs.append(pl.BlockSpec((1, None, 2, H_B, DK_B, DV_B), lambda b: (b, layer, 0, 0, 0, 0)))
        args.append(s0)
    if n_prev:
        in_specs.append(sspec(n_prev))
        args.append(state_prev)
    out_shape = [jax.ShapeDtypeStruct((batch * t, 256), bf16)]
    out_specs = [pl.BlockSpec((t, 256), lambda b: (b, 0))]
    if emit_state:
        out_shape.append(jax.ShapeDtypeStruct((batch, n_prev + 1, 2, H_B, DK_B, DV_B), f32))
        out_specs.append(sspec(n_prev + 1))
    return pl.pallas_call(
        functools.partial(_hgrn_kernel, t=t, has_s0=s0 is not None, emit_state=emit_state, n_prev=n_prev),
        out_shape=tuple(out_shape),
        grid=(batch,),
        in_specs=in_specs,
        out_specs=tuple(out_specs),
        scratch_shapes=[pltpu.VMEM((2, H_B * DV_B, DK_B), f32), pltpu.VMEM((256, t), f32),
                        pltpu.VMEM((256, t), f32)],
        compiler_params=_params("parallel"),
        name="hgrn",
    )(*args)


def _rope_tables(t):
    def cos_sin(dim):
        n_freq = dim // 4
        inv = ROPE_THETA ** (-jnp.arange(n_freq, dtype=f32) / n_freq)
        rows = t // GRID_W
        row_id = jnp.repeat(jnp.arange(rows, dtype=f32), GRID_W)
        col_id = jnp.tile(jnp.arange(GRID_W, dtype=f32), rows)
        ang = jnp.concatenate([row_id[:, None] * inv, col_id[:, None] * inv], axis=-1)
        return jnp.cos(ang), jnp.sin(ang)

    cos, sin = cos_sin(HD_A)
    zero = jnp.zeros_like(sin)
    tab_a = jnp.stack([jnp.tile(jnp.concatenate([cos, cos], -1), (1, H_A)),
                       jnp.tile(jnp.concatenate([zero, sin], -1), (1, H_A)),
                       jnp.tile(jnp.concatenate([-sin, zero], -1), (1, H_A))])
    cos, sin = cos_sin(ROPE_C)
    zero = jnp.zeros_like(sin)
    one64 = jnp.ones((t, NOPE_C), f32)
    zero64 = jnp.zeros((t, NOPE_C), f32)
    pad32 = jnp.zeros((t, LANES - NOPE_C - ROPE_C), f32)
    tab_c = jnp.stack([jnp.concatenate([one64, cos, cos, pad32 + 1.0], -1),
                       jnp.concatenate([zero64, zero, sin, pad32], -1),
                       jnp.concatenate([zero64, -sin, zero, pad32], -1)])
    return tab_a, tab_c


def _pad_rope_cols(x):
    pads = [(0, 0)] * (x.ndim - 1) + [(NOPE_C, LANES - NOPE_C - ROPE_C)]
    return jnp.pad(x, pads)


def kernel(x_prompt, x_sample, c, c_ctx, cache_attn_k, cache_attn_v, state_hgrn, cache_mla_ckv, cache_mla_krope,
           w_ada, b_ada, norm_sub, w_ffn_gate, w_ffn_up, w_ffn_down, w_in, w_out, attn_sink, hgrn_lb_logits,
           hgrn_out_norm, mla_q_norm, mla_kv_norm, mla_w_uq, mla_w_ukv, pool_w, pool_scale, final_norm):
    bp, tp, _ = x_prompt.shape
    bs, ts, _ = x_sample.shape
    np_, ns = bp * tp, bs * ts
    xp = x_prompt.reshape(np_, D_MODEL)
    xs = x_sample.reshape(ns, D_MODEL)

    n_vec = 1 + bs
    cvec = jnp.zeros((-(-n_vec // 8) * 8, D_MODEL), f32).at[0].set(c_ctx).at[1:n_vec].set(c)
    mods = _modulation(cvec, w_ada, b_ada).reshape(DEPTH, cvec.shape[0], N_MOD, D_MODEL)

    p_lb = jax.nn.softmax(hgrn_lb_logits.astype(f32), axis=0)
    lb_all = jnp.cumsum(p_lb, axis=0) - p_lb[0:1]
    tab_a, tab_c = _rope_tables(ts)
    mats = jnp.asarray(_hgrn_mats(), bf16)
    ffn_f32 = (w_ffn_gate, w_ffn_up, w_ffn_down)
    ffn_w = {(0, 0): tuple(_to_bf16(w, 0, 0) for w in ffn_f32)}
    to_cast = [(l, idx) for l in range(DEPTH) for idx in range(2)][1:]
    grp_p = dict(base=0, rpm=np_)
    grp_s = dict(base=1, rpm=ts)
    tm, tm_mix = 512, 1024

    wb = lambda w: w.astype(bf16)
    w_row = wb(jnp.concatenate([w_in[:, :, 0:512], w_in[:, :, 1792:2176], _pad_rope_cols(w_in[:, :, 2176:2208]),
                                w_in[:, :, 2208:2464]], axis=2))
    w_t = wb(jnp.swapaxes(w_in[:, :, 512:1792], 1, 2))
    wq = wb(jnp.pad(mla_w_uq.reshape(DEPTH, Q_LORA_C, H_C, NOPE_C + ROPE_C),
                    ((0, 0), (0, 0), (0, 0), (0, LANES - NOPE_C - ROPE_C))).reshape(DEPTH, Q_LORA_C, H_C * LANES))
    wkv = mla_w_ukv.reshape(DEPTH, KV_LORA_C, H_C, NOPE_C + V_C)
    wk = wb(jnp.pad(wkv[..., :NOPE_C], ((0, 0), (0, 0), (0, 0), (0, LANES - NOPE_C)))
            .reshape(DEPTH, KV_LORA_C, H_C * LANES))
    wv = wb(jnp.swapaxes(wkv[..., NOPE_C:].reshape(DEPTH, KV_LORA_C, H_C * V_C), 1, 2))
    n_grp = len(POOL_WINDOWS)
    w_pool = wb(jnp.einsum('lgij,gh->lgihj', pool_w, jnp.eye(n_grp, dtype=f32))
                .reshape(DEPTH, n_grp * POOL_CH, n_grp * POOL_CH))
    lb = jnp.broadcast_to(lb_all[..., None], (DEPTH, 2, 256, LANES))
    ng = jnp.broadcast_to(jnp.tile(hgrn_out_norm, (1, H_B))[..., None], (DEPTH, 256, LANES))
    sink = attn_sink.astype(f32)
    qn = mla_q_norm.reshape(DEPTH, 1, Q_LORA_C)
    kvn = mla_kv_norm.reshape(DEPTH, 1, KV_LORA_C)
    pscale = pool_scale.reshape(DEPTH, 1, 256)
    wo = wb(w_out)
    norms = norm_sub.reshape(DEPTH * 3, 1, D_MODEL)
    ctx_c = (cache_mla_ckv, _pad_rope_cols(cache_mla_krope))

    new_k = new_v = new_st = new_ckv = new_kr = None
    for l in range(DEPTH):
        def ffn(x, grp, j, idx, mix=None, final_g=None):
            side = None
            if to_cast:
                nxt = to_cast.pop(0)
                side = nxt + (ffn_f32,)
            res = _ffn(x, mods, norms, ffn_w[(l, idx)], layer=l, j=j, tm=tm, mix=mix, final_g=final_g, side=side,
                       **grp)
            if side is not None:
                ffn_w[nxt] = tuple(res[1:])
            return res[0]

        xp = ffn(xp, grp_p, 0, 0)
        xs = ffn(xs, grp_s, 0, 0)

        pr, pt = _mixer_in(xp, mods, norms, w_row, w_t, layer=l, tm=tm_mix, **grp_p)
        oa, new_k, new_v = _attn_ctx(pr, sink, None if l == 0 else (new_k, new_v), layer=l, batch=bp, t=tp)
        ob, new_st = _hgrn(pt, lb, ng, mats, layer=l, batch=bp, t=tp, emit_state=True, state_prev=new_st)
        oc, new_ckv, new_kr = _mla(pr, qn, kvn, wq, wk, wv, layer=l, batch=bp, t=tp, tq=tp, emit_cache=True,
                                   cache_prev=None if l == 0 else (new_ckv, new_kr))
        od = _pool(pr, w_pool, pscale, layer=l, batch=bp, t=tp, seqs=POOL_CTX_SEQS)
        mix_p = (oa, ob, oc, od, wo)

        pr, pt = _mixer_in(xs, mods, norms, w_row, w_t, layer=l, tm=tm_mix, **grp_s)
        oa = _attn_lat(pr, sink, cache_attn_k, cache_attn_v, tab_a, layer=l, batch=bs, t=ts)
        (ob,) = _hgrn(pt, lb, ng, mats, layer=l, batch=bs, t=ts, s0=state_hgrn)
        (oc,) = _mla(pr, qn, kvn, wq, wk, wv, layer=l, batch=bs, t=ts, tq=512, ctx=ctx_c, tab=tab_c)
        od = _pool(pr, w_pool, pscale, layer=l, batch=bs, t=ts)
        mix_s = (oa, ob, oc, od, wo)

        fin = final_norm.reshape(1, D_MODEL) if l == DEPTH - 1 else None
        xp = ffn(xp, grp_p, 2, 1, mix_p, fin)
        xs = ffn(xs, grp_s, 2, 1, mix_s, fin)

    return (xp.reshape(bp, tp, D_MODEL), xs.reshape(bs, ts, D_MODEL), new_k, new_v, new_st, new_ckv, new_kr)
```

```python
import functools

import numpy as np
import jax
import jax.numpy as jnp
from jax import lax
from jax.experimental import pallas as pl
from jax.experimental.pallas import tpu as pltpu

f32 = jnp.float32
bf16 = jnp.bfloat16

D_MODEL = 1024
DEPTH = 2
GRID_W = 64
EPS = 1e-6
LOG2E = 1.4426950408889634
ROPE_THETA = 10000.0
NEG_INF = -1e30
N_MOD = 9
D_FF = 2816
H_A, KV_A, HD_A = 4, 2, 64
G_A = H_A // KV_A
WINDOW_A = 128
BLOCK_A = 128
H_B, DK_B, DV_B = 4, 64, 64
H_C, Q_LORA_C, KV_LORA_C, NOPE_C, ROPE_C, V_C = 4, 256, 128, 64, 32, 64
POOL_WINDOWS = (2, 4, 8, 16)
POOL_CH = 64

LANES = 128
SUBLANES = 8
CAST_BLOCKS = 16
ROW_REDUCE_CHAINS = 8
MLA_ONES = 2 * SUBLANES
VMEM_LIMIT = 56 * 1024 * 1024
HG_TILE = LANES
HG_UNROLL = 2
HG_HALF = HG_TILE // 2
HG_MAX_EXP = 80.0
HG_BLOCK = 16
HG_LEVELS = (16, 32, 64)
ATTN_CTX_SEQS = 2
POOL_CTX_SEQS = 8
ATTN_LAT_BLOCKS = 4
POOL_PAD = 16

PR_W = 1280
PT_W = 1280

_NT = (((1,), (1,)), ((), ()))
_TN = (((0,), (0,)), ((), ()))


def _params(*sem):
    return pltpu.CompilerParams(dimension_semantics=sem, vmem_limit_bytes=VMEM_LIMIT)


def _const_spec(shape, lead=None):
    zeros = (0,) * len(shape)
    if lead is None:
        return pl.BlockSpec(shape, lambda *_: zeros, pipeline_mode=pl.Buffered(1))
    return pl.BlockSpec((None,) + tuple(shape), lambda *_: (lead,) + zeros, pipeline_mode=pl.Buffered(1))


def _dot(a, b):
    return jnp.dot(a, b, preferred_element_type=f32)


def _sigmoid(x):
    return 1.0 / (1.0 + jnp.exp(-x))


def _rms(x, g):
    return x * lax.rsqrt(jnp.mean(x * x, axis=-1, keepdims=True) + EPS) * g


def _mod_kernel(c_ref, w_ref, b_ref, o_ref):
    c = c_ref[...]
    a = (c * _sigmoid(c)).astype(bf16)
    o_ref[0] = _dot(a, w_ref[0].astype(bf16)) + b_ref[0]


def _modulation(cvec, w_ada, b_ada):
    tn = 1536
    n_out = N_MOD * D_MODEL
    return pl.pallas_call(
        _mod_kernel,
        out_shape=jax.ShapeDtypeStruct((DEPTH, cvec.shape[0], n_out), f32),
        grid=(DEPTH, n_out // tn),
        in_specs=[pl.BlockSpec(cvec.shape, lambda l, j: (0, 0)),
                  pl.BlockSpec((1, D_MODEL, tn), lambda l, j: (l, 0, j)),
                  pl.BlockSpec((1, 1, tn), lambda l, j: (l, 0, j))],
        out_specs=pl.BlockSpec((1, cvec.shape[0], tn), lambda l, j: (l, 0, j)),
        compiler_params=_params("arbitrary", "arbitrary"),
        name="modulation",
    )(cvec, w_ada, b_ada.reshape(DEPTH, 1, n_out))


def _cast_kernel(*refs):
    n = len(refs) // 2
    for src, dst in zip(refs[:n], refs[n:]):
        dst[...] = src[...].astype(bf16)


def _to_bf16(ws, layer, idx):
    def blk(w):
        return w.shape[2] // CAST_BLOCKS, w.shape[3]

    return pl.pallas_call(
        _cast_kernel,
        out_shape=tuple(jax.ShapeDtypeStruct(w.shape[2:], bf16) for w in ws),
        grid=(CAST_BLOCKS,),
        in_specs=[pl.BlockSpec((None, None) + blk(w), lambda i: (layer, idx, i, 0)) for w in ws],
        out_specs=tuple(pl.BlockSpec(blk(w), lambda i: (i, 0)) for w in ws),
        compiler_params=_params("parallel"),
        name="cast_bf16",
    )(*ws)


def _ffn_kernel(*refs, j, mix, final, side):
    it = iter(refs)
    x_ref, m_ref = next(it), next(it)
    mix_refs = [next(it) for _ in range(5)] if mix else None
    g_ref, wg_ref, wu_ref, wd_ref = next(it), next(it), next(it), next(it)
    fg_ref = next(it) if final else None
    side_in = [next(it) for _ in range(3)] if side else []
    o_ref = next(it)
    for src, dst in zip(side_in, [next(it) for _ in range(len(side_in))]):
        dst[...] = src[...].astype(bf16)
    x = x_ref[...]
    m = m_ref[0]
    if mix:
        wo_ref = mix_refs[4]
        acc = _dot(mix_refs[0][...], wo_ref[0:256, :])
        for p in range(1, 4):
            acc += _dot(mix_refs[p][...], wo_ref[256 * p:256 * (p + 1), :])
        x = x + m[5:6] * acc
    h = (_rms(x, g_ref[...]) * (1.0 + m[3 * j + 1:3 * j + 2]) + m[3 * j:3 * j + 1]).astype(bf16)
    gate = _dot(h, wg_ref[...])
    up = _dot(h, wu_ref[...])
    act = (gate * _sigmoid(gate) * up).astype(bf16)
    y = x + (0.5 * m[3 * j + 2:3 * j + 3]) * _dot(act, wd_ref[...])
    if final:
        y = _rms(y, fg_ref[...])
    o_ref[...] = y


def _mod_spec(layer, base, rpm, tm):
    return pl.BlockSpec((None, 1, N_MOD, D_MODEL), lambda i: (layer, base + (i * tm) // rpm, 0, 0))


def _ffn(x, mods, norms, w3, *, layer, j, base, rpm, tm, mix=None, final_g=None, side=None):
    n = x.shape[0]
    steps = n // tm
    in_specs = [pl.BlockSpec((tm, D_MODEL), lambda i: (i, 0)), _mod_spec(layer, base, rpm, tm)]
    args = [x, mods]
    if mix is not None:
        in_specs += [pl.BlockSpec((tm, 256), lambda i: (i, 0))] * 4 + [_const_spec((D_MODEL, D_MODEL), layer)]
        args += list(mix)
    in_specs += [_const_spec((1, D_MODEL), 3 * layer + j)] + [_const_spec(w.shape) for w in w3]
    args += [norms] + list(w3)
    if final_g is not None:
        in_specs.append(_const_spec((1, D_MODEL)))
        args.append(final_g)
    out_shape = [jax.ShapeDtypeStruct((n, D_MODEL), f32)]
    out_specs = [pl.BlockSpec((tm, D_MODEL), lambda i: (i, 0))]
    if side is not None:
        sl, sidx, arrays = side
        nblk = CAST_BLOCKS if steps % CAST_BLOCKS == 0 else steps
        for a in arrays:
            assert a.shape[2] % (nblk * 2 * SUBLANES) == 0, (a.shape, nblk)
            r, c = a.shape[2] // nblk, a.shape[3]
            in_specs.append(pl.BlockSpec((None, None, r, c), lambda i: (sl, sidx, (i * nblk) // steps, 0)))
            args.append(a)
            out_shape.append(jax.ShapeDtypeStruct(a.shape[2:], bf16))
            out_specs.append(pl.BlockSpec((r, c), lambda i: ((i * nblk) // steps, 0)))
    return pl.pallas_call(
        functools.partial(_ffn_kernel, j=j, mix=mix is not None, final=final_g is not None, side=side is not None),
        out_shape=tuple(out_shape),
        grid=(steps,),
        in_specs=in_specs,
        out_specs=tuple(out_specs),
        compiler_params=_params("arbitrary" if side is not None else "parallel"),
        name="ffn",
    )(*args)


def _mixin_kernel(x_ref, m_ref, g_ref, wr_ref, wt_ref, pr_ref, pt_ref):
    m = m_ref[0]
    h = (_rms(x_ref[...], g_ref[...]) * (1.0 + m[4:5]) + m[3:4]).astype(bf16)
    pr_ref[...] = _dot(h, wr_ref[...])
    pt_ref[...] = lax.dot_general(wt_ref[...], h, _NT, preferred_element_type=f32)


def _mixer_in(x, mods, norms, w_row, w_t, *, layer, base, rpm, tm):
    n = x.shape[0]
    return pl.pallas_call(
        _mixin_kernel,
        out_shape=(jax.ShapeDtypeStruct((n, PR_W), f32), jax.ShapeDtypeStruct((PT_W, n), f32)),
        grid=(n // tm,),
        in_specs=[pl.BlockSpec((tm, D_MODEL), lambda i: (i, 0)), _mod_spec(layer, base, rpm, tm),
                  _const_spec((1, D_MODEL), 3 * layer + 1), _const_spec((D_MODEL, PR_W), layer),
                  _const_spec((PT_W, D_MODEL), layer)],
        out_specs=(pl.BlockSpec((tm, PR_W), lambda i: (i, 0)), pl.BlockSpec((PT_W, tm), lambda i: (0, i))),
        compiler_params=_params("parallel"),
        name="mixer_in",
    )(x, mods, norms, w_row, w_t)


def _rope(x, tab_ref, rows, lanes, shift):
    width = x.shape[-1]
    c = tab_ref[0, rows, lanes]
    sa = tab_ref[1, rows, lanes]
    sb = tab_ref[2, rows, lanes]
    return x * c + pltpu.roll(x, shift, axis=1) * sa + pltpu.roll(x, width - shift, axis=1) * sb


def _softmax_parts(s_list, sink):
    sink2 = sink * LOG2E
    mx = functools.reduce(jnp.maximum, [jnp.max(s, axis=-1, keepdims=True) for s in s_list])
    mx = jnp.maximum(mx, sink2)
    ps = [jnp.exp2(s - mx) for s in s_list]
    den = functools.reduce(lambda a, b: a + b, [jnp.sum(p, axis=-1, keepdims=True) for p in ps])
    return ps, 1.0 / (den + jnp.exp2(sink2 - mx))


def _kv_per_query_head(x):
    return jnp.concatenate([x[:, HD_A * (h // G_A):HD_A * (h // G_A + 1)] for h in range(H_A)], axis=1)


def _stack_heads(q):
    head = lax.broadcasted_iota(jnp.int32, (1, H_A * HD_A), 1) // HD_A
    return jnp.concatenate([jnp.where(head == h, q, 0.0) for h in range(H_A)], axis=0)


def _pick_heads(o, r):
    head = lax.broadcasted_iota(jnp.int32, (1, H_A * HD_A), 1) // HD_A
    out = o[0:r]
    for h in range(1, H_A):
        out = jnp.where(head == h, o[h * r:(h + 1) * r], out)
    return out


def _sink_rows(sink_ref, layer, r):
    row = lax.broadcasted_iota(jnp.int32, (H_A * r, 1), 0)
    out = jnp.full((H_A * r, 1), sink_ref[layer, 0], f32)
    for h in range(1, H_A):
        out = jnp.where(row >= h * r, sink_ref[layer, h], out)
    return out


def _attn_ctx_kernel(*refs, t, n_prev, layer):
    if n_prev:
        sink_ref, q_ref, kv_ref, kp_ref, vp_ref, o_ref, ko_ref, vo_ref = refs
        ko_ref[:, 0:n_prev] = kp_ref[...]
        vo_ref[:, 0:n_prev] = vp_ref[...]
    else:
        sink_ref, q_ref, kv_ref, o_ref, ko_ref, vo_ref = refs
    kvw = KV_A * HD_A
    sink = _sink_rows(sink_ref, layer, t)
    for u in range(ATTN_CTX_SEQS):
        rows = slice(u * t, (u + 1) * t)
        kv = kv_ref[rows, :]
        for j in range(KV_A):
            ko_ref[u, n_prev, j] = kv[:, HD_A * j:HD_A * (j + 1)]
            vo_ref[u, n_prev, j] = kv[:, kvw + HD_A * j:kvw + HD_A * (j + 1)]
        kx = _kv_per_query_head(kv[:, 0:kvw]).astype(bf16)
        vx = _kv_per_query_head(kv[:, kvw:2 * kvw]).astype(bf16)
        q4 = _stack_heads(q_ref[rows, :] * (HD_A ** -0.5 * LOG2E)).astype(bf16)
        s = lax.dot_general(q4, kx, _NT, preferred_element_type=f32)
        (p,), inv = _softmax_parts([s], sink)
        o_ref[rows, :] = _pick_heads(_dot(p.astype(bf16), vx) * inv, t).astype(o_ref.dtype)


def _cache_spec(n_seq, n_layers, *dims):
    return pl.BlockSpec((n_seq, n_layers) + dims, lambda b: (b,) + (0,) * (1 + len(dims)))


def _attn_ctx(pr, sink, prev, *, layer, batch, t):
    n_prev = 0 if prev is None else prev[0].shape[1]
    ns = ATTN_CTX_SEQS
    kvshape = jax.ShapeDtypeStruct((batch, n_prev + 1, KV_A, t, HD_A), f32)
    in_specs = [pl.BlockSpec(memory_space=pltpu.SMEM),
                pl.BlockSpec((ns * t, 256), lambda b: (b, 0)), pl.BlockSpec((ns * t, 256), lambda b: (b, 1))]
    args = [sink, pr, pr]
    if n_prev:
        in_specs += [_cache_spec(ns, n_prev, KV_A, t, HD_A)] * 2
        args += list(prev)
    return pl.pallas_call(
        functools.partial(_attn_ctx_kernel, t=t, n_prev=n_prev, layer=layer),
        out_shape=(jax.ShapeDtypeStruct((batch * t, 256), bf16), kvshape, kvshape),
        grid=(batch // ns,),
        in_specs=in_specs,
        out_specs=(pl.BlockSpec((ns * t, 256), lambda b: (b, 0)),
                   _cache_spec(ns, n_prev + 1, KV_A, t, HD_A), _cache_spec(ns, n_prev + 1, KV_A, t, HD_A)),
        compiler_params=_params("parallel"),
        name="attn_ctx",
    )(*args)


def _attn_lat_kernel(sink_ref, q_ref, kv_ref, kc_ref, vc_ref, tab_ref, o_ref, kpad, vpad, kctx, vctx, *, t, layer):
    n = pl.program_id(1)
    kvw = KV_A * HD_A
    hw = H_A * HD_A

    @pl.when(n == 0)
    def _():
        zeros = jnp.zeros((BLOCK_A, hw), bf16)
        kv = kv_ref[...]
        k = _rope(kv[:, 0:kvw], tab_ref, slice(None), slice(0, kvw), HD_A // 2)
        kpad[0:BLOCK_A, :] = zeros
        vpad[0:BLOCK_A, :] = zeros
        kpad[BLOCK_A:BLOCK_A + t, :] = _kv_per_query_head(k).astype(bf16)
        vpad[BLOCK_A:BLOCK_A + t, :] = _kv_per_query_head(kv[:, kvw:2 * kvw]).astype(bf16)
        kpad[BLOCK_A + t:2 * BLOCK_A + t, :] = zeros
        vpad[BLOCK_A + t:2 * BLOCK_A + t, :] = zeros
        kctx[...] = jnp.concatenate([kc_ref[0, h // G_A] for h in range(H_A)], axis=1).astype(bf16)
        vctx[...] = jnp.concatenate([vc_ref[0, h // G_A] for h in range(H_A)], axis=1).astype(bf16)

    rows = H_A * BLOCK_A
    rr = lax.broadcasted_iota(jnp.int32, (rows, 1), 0) % BLOCK_A
    cidx = lax.broadcasted_iota(jnp.int32, (rows, 3 * BLOCK_A), 1)
    sink = _sink_rows(sink_ref, layer, BLOCK_A)
    for u in range(ATTN_LAT_BLOCKS):
        blk = n * ATTN_LAT_BLOCKS + u
        row0 = pl.multiple_of(blk * BLOCK_A, BLOCK_A)
        q = _rope(q_ref[u * BLOCK_A:(u + 1) * BLOCK_A, :], tab_ref, pl.ds(row0, BLOCK_A), slice(None), HD_A // 2)
        q4 = _stack_heads(q * (HD_A ** -0.5 * LOG2E)).astype(bf16)
        kband = kpad[pl.ds(row0, 3 * BLOCK_A), :]
        vband = vpad[pl.ds(row0, 3 * BLOCK_A), :]
        s_loc = lax.dot_general(q4, kband, _NT, preferred_element_type=f32)
        s_ctx = lax.dot_general(q4, kctx[...], _NT, preferred_element_type=f32)
        lo = jnp.maximum(rr + (BLOCK_A - WINDOW_A), jnp.maximum(0, (1 - blk) * BLOCK_A))
        hi = jnp.minimum(rr + (BLOCK_A + WINDOW_A), jnp.minimum(3 * BLOCK_A, t - (blk - 1) * BLOCK_A) - 1)
        s_loc = jnp.where((cidx >= lo) & (cidx <= hi), s_loc, NEG_INF)
        (p_loc, p_ctx), inv = _softmax_parts([s_loc, s_ctx], sink)
        o = (_dot(p_loc.astype(bf16), vband) + _dot(p_ctx.astype(bf16), vctx[...])) * inv
        o_ref[u * BLOCK_A:(u + 1) * BLOCK_A, :] = _pick_heads(o, BLOCK_A).astype(o_ref.dtype)


def _attn_lat(pr, sink, kc, vc, tab, *, layer, batch, t):
    rows = ATTN_LAT_BLOCKS * BLOCK_A
    nb = t // rows
    lc = kc.shape[3]
    cspec = pl.BlockSpec((1, None, KV_A, lc, HD_A), lambda b, n: (b, layer, 0, 0, 0))
    return pl.pallas_call(
        functools.partial(_attn_lat_kernel, t=t, layer=layer),
        out_shape=jax.ShapeDtypeStruct((batch * t, 256), bf16),
        grid=(batch, nb),
        in_specs=[pl.BlockSpec(memory_space=pltpu.SMEM),
                  pl.BlockSpec((rows, 256), lambda b, n: (b * nb + n, 0)),
                  pl.BlockSpec((t, 256), lambda b, n: (b, 1)),
                  cspec, cspec,
                  pl.BlockSpec((3, t, 256), lambda b, n: (0, 0, 0))],
        out_specs=pl.BlockSpec((rows, 256), lambda b, n: (b * nb + n, 0)),
        scratch_shapes=[pltpu.VMEM((t + 2 * BLOCK_A, H_A * HD_A), bf16),
                        pltpu.VMEM((t + 2 * BLOCK_A, H_A * HD_A), bf16),
                        pltpu.VMEM((lc, H_A * HD_A), bf16), pltpu.VMEM((lc, H_A * HD_A), bf16)],
        compiler_params=_params("parallel", "arbitrary"),
        name="attn_lat",
    )(sink, pr, pr, kc, vc, tab)


def _reduce_rows(x, reduce_fn):
    r, c = x.shape
    g = ROW_REDUCE_CHAINS
    if r % (g * SUBLANES):
        return reduce_fn(x, axis=0, keepdims=True)
    part = reduce_fn(x.reshape(g, r // (g * SUBLANES), SUBLANES, c), axis=1)
    return reduce_fn(reduce_fn(part, axis=0), axis=0, keepdims=True)


def _mla_kernel(*refs, t, lc, tq, rope, emit_cache, n_prev):
    it = iter(refs)
    cq_ref, ckv_ref, kr_ref = next(it), next(it), next(it)
    cc_ref = kc_ref = tab_ref = None
    if lc:
        cc_ref, kc_ref = next(it), next(it)
    if rope:
        tab_ref = next(it)
    qn_ref, kvn_ref, wq_ref, wk_ref, wv_ref = next(it), next(it), next(it), next(it), next(it)
    ckvp_ref, krp_ref = (next(it), next(it)) if n_prev else (None, None)
    o_ref = next(it)
    ckv_out_ref, kr_out_ref = (next(it), next(it)) if emit_cache else (None, None)
    kbuf, vbuf = next(it), next(it)
    i = pl.program_id(1)
    hw = LANES
    va = V_C + MLA_ONES

    @pl.when(i == 0)
    def _():
        ckv = _rms(ckv_ref[...], kvn_ref[...])
        kr = kr_ref[...]
        if emit_cache:
            if n_prev:
                ckv_out_ref[0, 0:n_prev] = ckvp_ref[0]
                kr_out_ref[0, 0:n_prev] = krp_ref[0]
            ckv_out_ref[0, n_prev] = ckv
            kr_out_ref[0, n_prev] = kr[:, NOPE_C:NOPE_C + ROPE_C]
        ckv_b = ckv.astype(bf16)
        if rope:
            kr = _rope(kr, tab_ref, slice(None), slice(None), ROPE_C // 2)
        kk = _dot(ckv_b, wk_ref[...])
        for h in range(H_C):
            kbuf[h, 0:t, :] = (kk[:, hw * h:hw * (h + 1)] + kr).astype(bf16)
        def put_values(vt, cols):
            for h in range(H_C):
                vbuf[va * h:va * h + V_C, cols] = vt[V_C * h:V_C * (h + 1)]

        for h in range(H_C):
            vbuf[va * h + V_C:va * (h + 1), :] = jnp.ones((MLA_ONES, t + lc), bf16)
        put_values(lax.dot_general(wv_ref[...], ckv_b, _NT, preferred_element_type=f32).astype(bf16), slice(0, t))
        if lc:
            cc_b = cc_ref[0].astype(bf16)
            kkc = _dot(cc_b, wk_ref[...])
            krc = kc_ref[0]
            for h in range(H_C):
                kbuf[h, t:t + lc, :] = (kkc[:, hw * h:hw * (h + 1)] + krc).astype(bf16)
            put_values(lax.dot_general(wv_ref[...], cc_b, _NT, preferred_element_type=f32).astype(bf16),
                       slice(t, t + lc))

    cqn = _rms(cq_ref[...], qn_ref[...]).astype(bf16)
    qf = _dot(cqn, wq_ref[...])
    c2 = (NOPE_C + ROPE_C) ** -0.5 * LOG2E
    row0 = pl.multiple_of(i * tq, tq)
    n_keys = t + lc
    half = n_keys // 2

    def scores(h):
        qh = qf[:, hw * h:hw * (h + 1)]
        if rope:
            qh = _rope(qh, tab_ref, pl.ds(row0, tq), slice(None), ROPE_C // 2)
        qb = (qh * c2).astype(bf16)
        return [lax.dot_general(kbuf[h, c * half:(c + 1) * half, :], qb, _NT, preferred_element_type=f32)
                for c in range(2)]

    def probs(s):
        m = jnp.maximum(_reduce_rows(s[0], jnp.max), _reduce_rows(s[1], jnp.max))
        return [jnp.exp2(x - m).astype(bf16) for x in s]

    def values(h, p):
        rows = slice(va * h, va * (h + 1))
        o = _dot(vbuf[rows, 0:half], p[0]) + _dot(vbuf[rows, half:2 * half], p[1])
        return o[0:V_C] * (1.0 / o[V_C:V_C + 1])

    s_next = scores(0)
    outs = []
    for h in range(H_C):
        s_cur = s_next
        if h + 1 < H_C:
            s_next = scores(h + 1)
        outs.append(values(h, probs(s_cur)))
    o_ref[...] = jnp.concatenate(outs, axis=0).T.astype(o_ref.dtype)


def _mla(pr, qn, kvn, wq, wk, wv, *, layer, batch, t, tq, ctx=None, tab=None, emit_cache=False, cache_prev=None):
    nq = t // tq
    lc = 0 if ctx is None else ctx[0].shape[2]
    n_prev = 0 if cache_prev is None else cache_prev[0].shape[1]
    in_specs = [pl.BlockSpec((tq, 256), lambda b, i: (b * nq + i, 2)),
                pl.BlockSpec((t, LANES), lambda b, i: (b, 6)),
                pl.BlockSpec((t, LANES), lambda b, i: (b, 7))]
    args = [pr, pr, pr]
    if lc:
        in_specs += [pl.BlockSpec((1, None, lc, LANES), lambda b, i: (b, layer, 0, 0))] * 2
        args += list(ctx)
    if tab is not None:
        in_specs.append(pl.BlockSpec((3, t, LANES), lambda b, i: (0, 0, 0)))
        args.append(tab)
    in_specs += [_const_spec(a.shape[1:], layer) for a in (qn, kvn, wq, wk, wv)]
    args += [qn, kvn, wq, wk, wv]

    def cspec(n_layers, width):
        return pl.BlockSpec((1, n_layers, t, width), lambda b, i: (b, 0, 0, 0))

    if n_prev:
        in_specs += [cspec(n_prev, KV_LORA_C), cspec(n_prev, ROPE_C)]
        args += list(cache_prev)
    out_shape = [jax.ShapeDtypeStruct((batch * t, 256), bf16)]
    out_specs = [pl.BlockSpec((tq, 256), lambda b, i: (b * nq + i, 0))]
    if emit_cache:
        out_shape += [jax.ShapeDtypeStruct((batch, n_prev + 1, t, KV_LORA_C), f32),
                      jax.ShapeDtypeStruct((batch, n_prev + 1, t, ROPE_C), f32)]
        out_specs += [cspec(n_prev + 1, KV_LORA_C), cspec(n_prev + 1, ROPE_C)]
    return pl.pallas_call(
        functools.partial(_mla_kernel, t=t, lc=lc, tq=tq, rope=tab is not None, emit_cache=emit_cache,
                          n_prev=n_prev),
        out_shape=tuple(out_shape),
        grid=(batch, nq),
        in_specs=in_specs,
        out_specs=tuple(out_specs),
        scratch_shapes=[pltpu.VMEM((H_C, t + lc, LANES), bf16),
                        pltpu.VMEM((H_C * (V_C + MLA_ONES), t + lc), bf16)],
        compiler_params=_params("parallel", "arbitrary"),
        name="mla",
    )(*args)


def _pool_counts(t):
    pos = np.arange(t)[:, None]
    half = np.repeat(np.asarray(POOL_WINDOWS) // 2, POOL_CH)[None, :]
    return (np.minimum(pos + half, t) - np.maximum(pos - half, 0)).astype(np.float32)


def _pool_kernel(x_ref, cnt_ref, w_ref, sc_ref, o_ref, buf_a, buf_b, buf_c, *, t, seqs):
    for u in range(seqs):
        _pool_sequence(x_ref, cnt_ref, w_ref, sc_ref, o_ref, buf_a, buf_b, buf_c, slice(u * t, (u + 1) * t), t)


def _pool_sequence(x_ref, cnt_ref, w_ref, sc_ref, o_ref, buf_a, buf_b, buf_c, seq, t):
    x = x_ref[seq, :]
    pad, edge = POOL_PAD, SUBLANES
    n = t + 2 * pad
    m = n - 2 * edge
    zeros = jnp.zeros((pad, 256), f32)
    buf_a[0:pad, :] = zeros
    buf_a[pad:pad + t, :] = x
    buf_a[pad + t:n, :] = zeros
    for buf in (buf_b, buf_c):
        buf[0:edge, :] = zeros[0:edge]
        buf[n - edge:n, :] = zeros[0:edge]
    inner = slice(edge, edge + m)
    buf_b[inner, :] = buf_a[pl.ds(edge - 1, m), :] + buf_a[pl.ds(edge, m), :]
    buf_c[inner, :] = buf_b[pl.ds(edge - 1, m), :] + buf_b[pl.ds(edge + 1, m), :]
    lo, hi = slice(0, LANES), slice(LANES, 2 * LANES)
    buf_a[inner, hi] = buf_c[pl.ds(edge - 2, m), hi] + buf_c[pl.ds(edge + 2, m), hi]
    s16 = buf_a[pl.ds(pad - 4, t), hi] + buf_a[pl.ds(pad + 4, t), hi]
    first = lax.broadcasted_iota(jnp.int32, (1, LANES), 1) < POOL_CH
    rows = slice(pad, pad + t)
    win_sum = jnp.concatenate([jnp.where(first, buf_b[rows, lo], buf_c[rows, lo]),
                               jnp.where(first, buf_a[rows, hi], s16)], axis=1)
    mean = win_sum / cnt_ref[...]
    o_ref[seq, :] = (_dot((mean - x).astype(bf16), w_ref[...]) * sc_ref[...]).astype(o_ref.dtype)


def _pool(pr, w_bd, scale, *, layer, batch, t, seqs=1):
    return pl.pallas_call(
        functools.partial(_pool_kernel, t=t, seqs=seqs),
        out_shape=jax.ShapeDtypeStruct((batch * t, 256), bf16),
        grid=(batch // seqs,),
        in_specs=[pl.BlockSpec((seqs * t, 256), lambda b: (b, 4)), _const_spec((t, 256)),
                  _const_spec((256, 256), layer), _const_spec((1, 256), layer)],
        out_specs=pl.BlockSpec((seqs * t, 256), lambda b: (b, 0)),
        scratch_shapes=[pltpu.VMEM((t + 2 * POOL_PAD, 256), f32)] * 3,
        compiler_params=_params("parallel"),
        name="pool",
    )(pr, jnp.asarray(_pool_counts(t)), w_bd, scale)


def _hgrn_mats():
    n = HG_TILE
    s = np.arange(n)[:, None]
    t = np.arange(n)[None, :]
    same_half = s // HG_HALF == t // HG_HALF
    mid = (t // HG_HALF) * HG_HALF + HG_HALF // 2 - 1
    rel_mid = (same_half & (s > mid) & (s <= t)).astype(np.float32) - (same_half & (s <= mid) & (s > t))
    mats = [rel_mid, same_half & (s <= t), same_half & (s > t), s <= t, s > t,
            (s // HG_BLOCK == t // HG_BLOCK) & (s <= t)]
    for m in HG_LEVELS:
        same_pair = s // (2 * m) == t // (2 * m)
        right_t = (t // m) % 2 == 1
        right_s = (s // m) % 2 == 1
        q_side = same_pair & right_t & right_s & (s <= t)
        k_side = same_pair & ~right_t & ~right_s & (s > t)
        mats.append(q_side | k_side)
    fwd = np.concatenate([m.astype(np.float32) for m in mats], axis=1)
    bwd = np.concatenate([m[::-1, ::-1].astype(np.float32) for m in mats], axis=1)
    both = np.stack([fwd, bwd])
    return np.concatenate([both, both], axis=1)


def _hgrn_exact_intra(q, k, v, vb, hl, mats, reverse):
    n = HG_TILE
    e_all = _dot(hl, mats)
    beta = e_all[:, 0:n]
    lane = lax.broadcasted_iota(jnp.int32, (1, n), 1)
    pos = lane % HG_BLOCK
    v3 = v.reshape(H_B, DV_B, n)

    o3 = jnp.sum((q * k).reshape(H_B, DK_B, n), axis=1, keepdims=True) * v3
    for d in range(1, HG_BLOCK):
        shift = (n - d) if reverse else d
        valid = (pos <= HG_BLOCK - 1 - d) if reverse else (pos >= d)
        kr = pltpu.roll(k, shift, axis=1)
        br = pltpu.roll(beta, shift, axis=1)
        vr = pltpu.roll(v, shift, axis=1)
        e = jnp.exp(jnp.where(valid, beta - br, NEG_INF))
        sc = jnp.sum((q * kr * e).reshape(H_B, DK_B, n), axis=1, keepdims=True)
        o3 = o3 + sc * vr.reshape(H_B, DV_B, n)

    srow = lax.broadcasted_iota(jnp.int32, (n, n), 0)
    tcol = lax.broadcasted_iota(jnp.int32, (n, n), 1)
    scores = [jnp.zeros((n, n), f32) for _ in range(H_B)]
    for li, m in enumerate(HG_LEVELS):
        ex = jnp.exp(e_all[:, (li + 1) * n:(li + 2) * n])
        is_q = ((lane // m) % 2) == (0 if reverse else 1)
        qs = jnp.where(is_q, q * ex, 0.0).astype(bf16)
        ks = jnp.where(is_q, 0.0, k * ex).astype(bf16)
        pair = (srow // (2 * m)) == (tcol // (2 * m))
        for h in range(H_B):
            a = lax.dot_general(ks[DK_B * h:DK_B * (h + 1)], qs[DK_B * h:DK_B * (h + 1)], _TN,
                                preferred_element_type=f32)
            scores[h] = scores[h] + jnp.where(pair, a, 0.0)

    return [o3[h] + _dot(vb[DV_B * h:DV_B * (h + 1)], scores[h].astype(bf16)) for h in range(H_B)]


def _hgrn_common(q, v, fpre, lb, mats_ref, sidx, st_ref):
    n = HG_TILE
    f = lb + (1.0 - lb) * _sigmoid(fpre)
    k = 1.0 - f
    logf = jnp.log(f)
    hi = logf.astype(bf16)
    hl = jnp.concatenate([hi, (logf - hi.astype(f32)).astype(bf16)], axis=1)
    ex = _dot(hl, mats_ref[sidx, :, 0:5 * n])
    ones = jnp.ones((8, 2 * n), bf16)
    g_row = lax.dot_general(ones, hl, _NT, preferred_element_type=f32)[0:1]
    d_mid = ex[:, 0:n]
    small = jnp.max(jnp.abs(d_mid)) <= HG_MAX_EXP
    q_in = (q * jnp.exp(ex[:, 3 * n:4 * n])).astype(bf16)
    k_out = (k * jnp.exp(ex[:, 4 * n:5 * n])).astype(bf16)
    vb = v.astype(bf16)
    o_state = []
    for h in range(H_B):
        rows = slice(DK_B * h, DK_B * (h + 1))
        st = st_ref[sidx, rows, :]
        o_state.append(_dot(st.astype(bf16), q_in[rows]))
        upd = lax.dot_general(vb[rows], k_out[rows], _NT, preferred_element_type=f32)
        st_ref[sidx, rows, :] = st * jnp.exp(g_row[:, rows]) + upd
    return dict(q=q, k=k, v=v, vb=vb, hl=hl, ex=ex, o_state=o_state, small=small)


def _hgrn_fast_intra(c, reverse):
    n = HG_TILE
    q, k, ex = c["q"], c["k"], c["ex"]
    lane = lax.broadcasted_iota(jnp.int32, (1, n), 1)
    in0 = (lane // HG_HALF) == 0
    q_half = in0 if reverse else jnp.logical_not(in0)
    d_mid = ex[:, 0:n]
    qd = q * jnp.exp(d_mid)
    kd = k * jnp.exp(-d_mid)
    ql = jnp.where(q_half, q * jnp.exp(ex[:, n:2 * n]), 0.0)
    kl = jnp.where(q_half, 0.0, k * jnp.exp(ex[:, 2 * n:3 * n]))
    parts_q = [jnp.where(in0, qd, 0.0).astype(bf16), jnp.where(in0, 0.0, qd).astype(bf16), ql.astype(bf16)]
    parts_k = [jnp.where(in0, kd, 0.0).astype(bf16), jnp.where(in0, 0.0, kd).astype(bf16), kl.astype(bf16)]
    srow = lax.broadcasted_iota(jnp.int32, (n, n), 0)
    tcol = lax.broadcasted_iota(jnp.int32, (n, n), 1)
    causal = (srow >= tcol) if reverse else (srow <= tcol)
    scores = []
    for h in range(H_B):
        rows = slice(DK_B * h, DK_B * (h + 1))
        k3 = jnp.concatenate([p[rows] for p in parts_k], axis=0)
        q3 = jnp.concatenate([p[rows] for p in parts_q], axis=0)
        scores.append(lax.dot_general(k3, q3, _TN, preferred_element_type=f32))
    scores = [jnp.where(causal, a, 0.0).astype(bf16) for a in scores]
    outs = [c["o_state"][h] + _dot(c["vb"][DV_B * h:DV_B * (h + 1)], scores[h]) for h in range(H_B)]
    return jnp.concatenate(outs, axis=0)


def _hgrn_exact(c, mats, reverse):
    intra = _hgrn_exact_intra(c["q"], c["k"], c["v"], c["vb"], c["hl"], mats, reverse)
    return jnp.concatenate([c["o_state"][h] + intra[h] for h in range(H_B)], axis=0)


def _hgrn_kernel(*refs, t, has_s0, emit_state, n_prev):
    it = iter(refs)
    q_ref, i_ref, ff_ref, fb_ref, g_ref, lb_ref, ng_ref, mats_ref = [next(it) for _ in range(8)]
    s0_ref = next(it) if has_s0 else None
    sprev_ref = next(it) if n_prev else None
    o_ref = next(it)
    sfin_ref = next(it) if emit_state else None
    st_ref, oacc_f, oacc_b = next(it), next(it), next(it)
    n = HG_TILE
    nt = t // n
    for d in range(2):
        for h in range(H_B):
            rows = slice(DV_B * h, DV_B * (h + 1))
            st_ref[d, rows, :] = s0_ref[0, d, h].T if has_s0 else jnp.zeros((DV_B, DK_B), f32)

    def scan(i, c):
        work = []
        small = None
        for u in range(HG_UNROLL):
            j = i * HG_UNROLL + u
            cols_f = pl.ds(pl.multiple_of(j * n, n), n)
            cols_b = pl.ds(pl.multiple_of((nt - 1 - j) * n, n), n)
            cf = _hgrn_common(q_ref[:, cols_f], i_ref[:, cols_f], ff_ref[:, cols_f], lb_ref[0], mats_ref, 0, st_ref)
            cb = _hgrn_common(q_ref[:, cols_b], i_ref[:, cols_b], fb_ref[:, cols_b], lb_ref[1], mats_ref, 1, st_ref)
            work.append((cols_f, cf, cols_b, cb))
            both = jnp.logical_and(cf["small"], cb["small"])
            small = both if small is None else jnp.logical_and(small, both)
        for cols_f, cf, cols_b, cb in work:
            oacc_f[:, cols_f] = _hgrn_fast_intra(cf, False)
            oacc_b[:, cols_b] = _hgrn_fast_intra(cb, True)

        @pl.when(jnp.logical_not(small))
        def _():
            for cols_f, cf, cols_b, cb in work:
                oacc_f[:, cols_f] = _hgrn_exact(cf, mats_ref[0, :, 5 * n:9 * n], False)
                oacc_b[:, cols_b] = _hgrn_exact(cb, mats_ref[1, :, 5 * n:9 * n], True)

        return c

    lax.fori_loop(0, nt // HG_UNROLL, scan, 0)

    def finish(i, c):
        off = pl.multiple_of(i * n, n)
        cols = pl.ds(off, n)
        o = oacc_f[:, cols] + oacc_b[:, cols]
        o3 = o.reshape(H_B, DV_B, n)
        y = o3 * lax.rsqrt(jnp.mean(o3 * o3, axis=1, keepdims=True) + EPS)
        gate = g_ref[:, cols]
        y = y.reshape(H_B * DV_B, n) * ng_ref[...] * (gate * _sigmoid(gate))
        o_ref[pl.ds(off, n), :] = y.T.astype(o_ref.dtype)
        return c

    lax.fori_loop(0, nt, finish, 0)
    if emit_state:
        if n_prev:
            sfin_ref[0, 0:n_prev] = sprev_ref[0]
        for d in range(2):
            for h in range(H_B):
                sfin_ref[0, n_prev, d, h] = st_ref[d, DV_B * h:DV_B * (h + 1), :].T


def _hgrn(pt, lb, ng, mats, *, batch, t, s0=None, layer=0, emit_state=False, state_prev=None):
    def row(r):
        return pl.BlockSpec((256, t), lambda b: (r, b))

    def sspec(n_layers):
        return pl.BlockSpec((1, n_layers, 2, H_B, DK_B, DV_B), lambda b: (b, 0, 0, 0, 0, 0))

    n_prev = 0 if state_prev is None else state_prev.shape[1]
    in_specs = [row(0), row(1), row(2), row(3), row(4),
                _const_spec((2, 256, LANES), layer), _const_spec((256, LANES), layer), _const_spec(mats.shape)]
    args = [pt, pt, pt, pt, pt, lb, ng, mats]
    if s0 is not None:
        in_specs.append(pl.BlockSpec((1, None, 2, H_B, DK_B, DV_B), lambda b: (b, layer, 0, 0, 0, 0)))
        args.append(s0)
    if n_prev:
        in_specs.append(sspec(n_prev))
        args.append(state_prev)
    out_shape = [jax.ShapeDtypeStruct((batch * t, 256), bf16)]
    out_specs = [pl.BlockSpec((t, 256), lambda b: (b, 0))]
    if emit_state:
        out_shape.append(jax.ShapeDtypeStruct((batch, n_prev + 1, 2, H_B, DK_B, DV_B), f32))
        out_specs.append(sspec(n_prev + 1))
    return pl.pallas_call(
        functools.partial(_hgrn_kernel, t=t, has_s0=s0 is not None, emit_state=emit_state, n_prev=n_prev),
        out_shape=tuple(out_shape),
        grid=(batch,),
        in_specs=in_specs,
        out_specs=tuple(out_specs),
        scratch_shapes=[pltpu.VMEM((2, H_B * DV_B, DK_B), f32), pltpu.VMEM((256, t), f32),
                        pltpu.VMEM((256, t), f32)],
        compiler_params=_params("parallel"),
        name="hgrn",
    )(*args)


def _rope_tables(t):
    def cos_sin(dim):
        n_freq = dim // 4
        inv = ROPE_THETA ** (-jnp.arange(n_freq, dtype=f32) / n_freq)
        rows = t // GRID_W
        row_id = jnp.repeat(jnp.arange(rows, dtype=f32), GRID_W)
        col_id = jnp.tile(jnp.arange(GRID_W, dtype=f32), rows)
        ang = jnp.concatenate([row_id[:, None] * inv, col_id[:, None] * inv], axis=-1)
        return jnp.cos(ang), jnp.sin(ang)

    cos, sin = cos_sin(HD_A)
    zero = jnp.zeros_like(sin)
    tab_a = jnp.stack([jnp.tile(jnp.concatenate([cos, cos], -1), (1, H_A)),
                       jnp.tile(jnp.concatenate([zero, sin], -1), (1, H_A)),
                       jnp.tile(jnp.concatenate([-sin, zero], -1), (1, H_A))])
    cos, sin = cos_sin(ROPE_C)
    zero = jnp.zeros_like(sin)
    one64 = jnp.ones((t, NOPE_C), f32)
    zero64 = jnp.zeros((t, NOPE_C), f32)
    pad32 = jnp.zeros((t, LANES - NOPE_C - ROPE_C), f32)
    tab_c = jnp.stack([jnp.concatenate([one64, cos, cos, pad32 + 1.0], -1),
                       jnp.concatenate([zero64, zero, sin, pad32], -1),
                       jnp.concatenate([zero64, -sin, zero, pad32], -1)])
    return tab_a, tab_c


def _pad_rope_cols(x):
    pads = [(0, 0)] * (x.ndim - 1) + [(NOPE_C, LANES - NOPE_C - ROPE_C)]
    return jnp.pad(x, pads)


def kernel(x_prompt, x_sample, c, c_ctx, cache_attn_k, cache_attn_v, state_hgrn, cache_mla_ckv, cache_mla_krope,
           w_ada, b_ada, norm_sub, w_ffn_gate, w_ffn_up, w_ffn_down, w_in, w_out, attn_sink, hgrn_lb_logits,
           hgrn_out_norm, mla_q_norm, mla_kv_norm, mla_w_uq, mla_w_ukv, pool_w, pool_scale, final_norm):
    bp, tp, _ = x_prompt.shape
    bs, ts, _ = x_sample.shape
    np_, ns = bp * tp, bs * ts
    xp = x_prompt.reshape(np_, D_MODEL)
    xs = x_sample.reshape(ns, D_MODEL)

    n_vec = 1 + bs
    cvec = jnp.zeros((-(-n_vec // 8) * 8, D_MODEL), f32).at[0].set(c_ctx).at[1:n_vec].set(c)
    mods = _modulation(cvec, w_ada, b_ada).reshape(DEPTH, cvec.shape[0], N_MOD, D_MODEL)

    p_lb = jax.nn.softmax(hgrn_lb_logits.astype(f32), axis=0)
    lb_all = jnp.cumsum(p_lb, axis=0) - p_lb[0:1]
    tab_a, tab_c = _rope_tables(ts)
    mats = jnp.asarray(_hgrn_mats(), bf16)
    ffn_f32 = (w_ffn_gate, w_ffn_up, w_ffn_down)
    ffn_w = {(0, 0): tuple(_to_bf16(ffn_f32, 0, 0))}
    to_cast = [(l, idx) for l in range(DEPTH) for idx in range(2)][1:]
    grp_p = dict(base=0, rpm=np_)
    grp_s = dict(base=1, rpm=ts)
    tm, tm_mix = 512, 1024

    wb = lambda w: w.astype(bf16)
    w_row = wb(jnp.concatenate([w_in[:, :, 0:512], w_in[:, :, 1792:2176], _pad_rope_cols(w_in[:, :, 2176:2208]),
                                w_in[:, :, 2208:2464]], axis=2))
    w_t = wb(jnp.swapaxes(w_in[:, :, 512:1792], 1, 2))
    wq = wb(jnp.pad(mla_w_uq.reshape(DEPTH, Q_LORA_C, H_C, NOPE_C + ROPE_C),
                    ((0, 0), (0, 0), (0, 0), (0, LANES - NOPE_C - ROPE_C))).reshape(DEPTH, Q_LORA_C, H_C * LANES))
    wkv = mla_w_ukv.reshape(DEPTH, KV_LORA_C, H_C, NOPE_C + V_C)
    wk = wb(jnp.pad(wkv[..., :NOPE_C], ((0, 0), (0, 0), (0, 0), (0, LANES - NOPE_C)))
            .reshape(DEPTH, KV_LORA_C, H_C * LANES))
    wv = wb(jnp.swapaxes(wkv[..., NOPE_C:].reshape(DEPTH, KV_LORA_C, H_C * V_C), 1, 2))
    n_grp = len(POOL_WINDOWS)
    w_pool = wb(jnp.einsum('lgij,gh->lgihj', pool_w, jnp.eye(n_grp, dtype=f32))
                .reshape(DEPTH, n_grp * POOL_CH, n_grp * POOL_CH))
    lb = jnp.broadcast_to(lb_all[..., None], (DEPTH, 2, 256, LANES))
    ng = jnp.broadcast_to(jnp.tile(hgrn_out_norm, (1, H_B))[..., None], (DEPTH, 256, LANES))
    sink = attn_sink.astype(f32)
    qn = mla_q_norm.reshape(DEPTH, 1, Q_LORA_C)
    kvn = mla_kv_norm.reshape(DEPTH, 1, KV_LORA_C)
    pscale = pool_scale.reshape(DEPTH, 1, 256)
    wo = wb(w_out)
    norms = norm_sub.reshape(DEPTH * 3, 1, D_MODEL)
    ctx_c = (cache_mla_ckv, _pad_rope_cols(cache_mla_krope))

    new_k = new_v = new_st = new_ckv = new_kr = None
    for l in range(DEPTH):
        def ffn(x, grp, j, idx, mix=None, final_g=None):
            side = None
            if to_cast:
                nxt = to_cast.pop(0)
                side = nxt + (ffn_f32,)
            res = _ffn(x, mods, norms, ffn_w[(l, idx)], layer=l, j=j, tm=tm, mix=mix, final_g=final_g, side=side,
                       **grp)
            if side is not None:
                ffn_w[nxt] = tuple(res[1:])
            return res[0]

        xp = ffn(xp, grp_p, 0, 0)
        xs = ffn(xs, grp_s, 0, 0)

        pr, pt = _mixer_in(xp, mods, norms, w_row, w_t, layer=l, tm=tm_mix, **grp_p)
        oa, new_k, new_v = _attn_ctx(pr, sink, None if l == 0 else (new_k, new_v), layer=l, batch=bp, t=tp)
        ob, new_st = _hgrn(pt, lb, ng, mats, layer=l, batch=bp, t=tp, emit_state=True, state_prev=new_st)
        oc, new_ckv, new_kr = _mla(pr, qn, kvn, wq, wk, wv, layer=l, batch=bp, t=tp, tq=tp, emit_cache=True,
                                   cache_prev=None if l == 0 else (new_ckv, new_kr))
        od = _pool(pr, w_pool, pscale, layer=l, batch=bp, t=tp, seqs=POOL_CTX_SEQS)
        mix_p = (oa, ob, oc, od, wo)

        pr, pt = _mixer_in(xs, mods, norms, w_row, w_t, layer=l, tm=tm_mix, **grp_s)
        oa = _attn_lat(pr, sink, cache_attn_k, cache_attn_v, tab_a, layer=l, batch=bs, t=ts)
        (ob,) = _hgrn(pt, lb, ng, mats, layer=l, batch=bs, t=ts, s0=state_hgrn)
        (oc,) = _mla(pr, qn, kvn, wq, wk, wv, layer=l, batch=bs, t=ts, tq=512, ctx=ctx_c, tab=tab_c)
        od = _pool(pr, w_pool, pscale, layer=l, batch=bs, t=ts)
        mix_s = (oa, ob, oc, od, wo)

        fin = final_norm.reshape(1, D_MODEL) if l == DEPTH - 1 else None
        xp = ffn(xp, grp_p, 2, 1, mix_p, fin)
        xs = ffn(xs, grp_s, 2, 1, mix_s, fin)

    return (xp.reshape(bp, tp, D_MODEL), xs.reshape(bs, ts, D_MODEL), new_k, new_v, new_st, new_ckv, new_kr)
```

```python
import functools

import numpy as np
import jax
import jax.numpy as jnp
from jax import lax
from jax.experimental import pallas as pl
from jax.experimental.pallas import tpu as pltpu

f32 = jnp.float32
bf16 = jnp.bfloat16

D_MODEL = 1024
DEPTH = 2
GRID_W = 64
EPS = 1e-6
LOG2E = 1.4426950408889634
ROPE_THETA = 10000.0
NEG_INF = -1e30
N_MOD = 9
D_FF = 2816
H_A, KV_A, HD_A = 4, 2, 64
G_A = H_A // KV_A
WINDOW_A = 128
BLOCK_A = 128
H_B, DK_B, DV_B = 4, 64, 64
H_C, Q_LORA_C, KV_LORA_C, NOPE_C, ROPE_C, V_C = 4, 256, 128, 64, 32, 64
POOL_WINDOWS = (2, 4, 8, 16)
POOL_CH = 64

LANES = 128
SUBLANES = 8
CAST_BLOCKS = 16
ROW_REDUCE_CHAINS = 8
MLA_ONES = 2 * SUBLANES
VMEM_LIMIT = 56 * 1024 * 1024
HG_TILE = LANES
HG_UNROLL = 2
HG_HALF = HG_TILE // 2
HG_MAX_EXP = 80.0
HG_BLOCK = 16
HG_LEVELS = (16, 32, 64)
ATTN_CTX_SEQS = 2
POOL_CTX_SEQS = 8
ATTN_LAT_BLOCKS = 4
POOL_PAD = 16

PR_W = 1280
PT_W = 1280

_NT = (((1,), (1,)), ((), ()))
_TN = (((0,), (0,)), ((), ()))


def _params(*sem):
    return pltpu.CompilerParams(dimension_semantics=sem, vmem_limit_bytes=VMEM_LIMIT)


def _const_spec(shape, lead=None):
    zeros = (0,) * len(shape)
    if lead is None:
        return pl.BlockSpec(shape, lambda *_: zeros, pipeline_mode=pl.Buffered(1))
    return pl.BlockSpec((None,) + tuple(shape), lambda *_: (lead,) + zeros, pipeline_mode=pl.Buffered(1))


def _dot(a, b):
    return jnp.dot(a, b, preferred_element_type=f32)


def _sigmoid(x):
    return 1.0 / (1.0 + jnp.exp(-x))


def _rms(x, g):
    return x * lax.rsqrt(jnp.mean(x * x, axis=-1, keepdims=True) + EPS) * g


def _mod_kernel(c_ref, w_ref, b_ref, o_ref):
    c = c_ref[...]
    a = (c * _sigmoid(c)).astype(bf16)
    o_ref[0] = _dot(a, w_ref[0].astype(bf16)) + b_ref[0]


def _modulation(cvec, w_ada, b_ada):
    tn = 1536
    n_out = N_MOD * D_MODEL
    return pl.pallas_call(
        _mod_kernel,
        out_shape=jax.ShapeDtypeStruct((DEPTH, cvec.shape[0], n_out), f32),
        grid=(DEPTH, n_out // tn),
        in_specs=[pl.BlockSpec(cvec.shape, lambda l, j: (0, 0)),
                  pl.BlockSpec((1, D_MODEL, tn), lambda l, j: (l, 0, j)),
                  pl.BlockSpec((1, 1, tn), lambda l, j: (l, 0, j))],
        out_specs=pl.BlockSpec((1, cvec.shape[0], tn), lambda l, j: (l, 0, j)),
        compiler_params=_params("arbitrary", "arbitrary"),
        name="modulation",
    )(cvec, w_ada, b_ada.reshape(DEPTH, 1, n_out))


def _cast_kernel(*refs):
    n = len(refs) // 2
    for src, dst in zip(refs[:n], refs[n:]):
        dst[...] = src[...].astype(bf16)


def _to_bf16(ws, layer, idx):
    def blk(w):
        return w.shape[2] // CAST_BLOCKS, w.shape[3]

    return pl.pallas_call(
        _cast_kernel,
        out_shape=tuple(jax.ShapeDtypeStruct(w.shape[2:], bf16) for w in ws),
        grid=(CAST_BLOCKS,),
        in_specs=[pl.BlockSpec((None, None) + blk(w), lambda i: (layer, idx, i, 0)) for w in ws],
        out_specs=tuple(pl.BlockSpec(blk(w), lambda i: (i, 0)) for w in ws),
        compiler_params=_params("parallel"),
        name="cast_bf16",
    )(*ws)


def _ffn_kernel(*refs, j, mix, final, side):
    it = iter(refs)
    x_ref, m_ref = next(it), next(it)
    mix_refs = [next(it) for _ in range(5)] if mix else None
    g_ref, wg_ref, wu_ref, wd_ref = next(it), next(it), next(it), next(it)
    fg_ref = next(it) if final else None
    side_in = [next(it) for _ in range(3)] if side else []
    o_ref = next(it)
    for src, dst in zip(side_in, [next(it) for _ in range(len(side_in))]):
        dst[...] = src[...].astype(bf16)
    x = x_ref[...]
    m = m_ref[0]
    if mix:
        wo_ref = mix_refs[4]
        acc = _dot(mix_refs[0][...], wo_ref[0:256, :])
        for p in range(1, 4):
            acc += _dot(mix_refs[p][...], wo_ref[256 * p:256 * (p + 1), :])
        x = x + m[5:6] * acc
    h = (_rms(x, g_ref[...]) * (1.0 + m[3 * j + 1:3 * j + 2]) + m[3 * j:3 * j + 1]).astype(bf16)
    gate = _dot(h, wg_ref[...])
    up = _dot(h, wu_ref[...])
    act = (gate * _sigmoid(gate) * up).astype(bf16)
    y = x + (0.5 * m[3 * j + 2:3 * j + 3]) * _dot(act, wd_ref[...])
    if final:
        y = _rms(y, fg_ref[...])
    o_ref[...] = y


def _mod_spec(layer, base, rpm, tm):
    return pl.BlockSpec((None, 1, N_MOD, D_MODEL), lambda i: (layer, base + (i * tm) // rpm, 0, 0))


def _ffn(x, mods, norms, w3, *, layer, j, base, rpm, tm, mix=None, final_g=None, side=None):
    n = x.shape[0]
    steps = n // tm
    in_specs = [pl.BlockSpec((tm, D_MODEL), lambda i: (i, 0)), _mod_spec(layer, base, rpm, tm)]
    args = [x, mods]
    if mix is not None:
        in_specs += [pl.BlockSpec((tm, 256), lambda i: (i, 0))] * 4 + [_const_spec((D_MODEL, D_MODEL), layer)]
        args += list(mix)
    in_specs += [_const_spec((1, D_MODEL), 3 * layer + j)] + [_const_spec(w.shape) for w in w3]
    args += [norms] + list(w3)
    if final_g is not None:
        in_specs.append(_const_spec((1, D_MODEL)))
        args.append(final_g)
    out_shape = [jax.ShapeDtypeStruct((n, D_MODEL), f32)]
    out_specs = [pl.BlockSpec((tm, D_MODEL), lambda i: (i, 0))]
    if side is not None:
        sl, sidx, arrays = side
        nblk = CAST_BLOCKS if steps % CAST_BLOCKS == 0 else steps
        for a in arrays:
            assert a.shape[2] % (nblk * 2 * SUBLANES) == 0, (a.shape, nblk)
            r, c = a.shape[2] // nblk, a.shape[3]
            in_specs.append(pl.BlockSpec((None, None, r, c), lambda i: (sl, sidx, (i * nblk) // steps, 0)))
            args.append(a)
            out_shape.append(jax.ShapeDtypeStruct(a.shape[2:], bf16))
            out_specs.append(pl.BlockSpec((r, c), lambda i: ((i * nblk) // steps, 0)))
    return pl.pallas_call(
        functools.partial(_ffn_kernel, j=j, mix=mix is not None, final=final_g is not None, side=side is not None),
        out_shape=tuple(out_shape),
        grid=(steps,),
        in_specs=in_specs,
        out_specs=tuple(out_specs),
        compiler_params=_params("arbitrary" if side is not None else "parallel"),
        name="ffn",
    )(*args)


def _mixin_kernel(x_ref, m_ref, g_ref, wr_ref, wt_ref, pr_ref, pt_ref):
    m = m_ref[0]
    h = (_rms(x_ref[...], g_ref[...]) * (1.0 + m[4:5]) + m[3:4]).astype(bf16)
    pr_ref[...] = _dot(h, wr_ref[...])
    pt_ref[...] = lax.dot_general(wt_ref[...], h, _NT, preferred_element_type=f32)


def _mixer_in(x, mods, norms, w_row, w_t, *, layer, base, rpm, tm):
    n = x.shape[0]
    return pl.pallas_call(
        _mixin_kernel,
        out_shape=(jax.ShapeDtypeStruct((n, PR_W), f32), jax.ShapeDtypeStruct((PT_W, n), f32)),
        grid=(n // tm,),
        in_specs=[pl.BlockSpec((tm, D_MODEL), lambda i: (i, 0)), _mod_spec(layer, base, rpm, tm),
                  _const_spec((1, D_MODEL), 3 * layer + 1), _const_spec((D_MODEL, PR_W), layer),
                  _const_spec((PT_W, D_MODEL), layer)],
        out_specs=(pl.BlockSpec((tm, PR_W), lambda i: (i, 0)), pl.BlockSpec((PT_W, tm), lambda i: (0, i))),
        compiler_params=_params("parallel"),
        name="mixer_in",
    )(x, mods, norms, w_row, w_t)


def _rope(x, tab_ref, rows, lanes, shift):
    width = x.shape[-1]
    c = tab_ref[0, rows, lanes]
    sa = tab_ref[1, rows, lanes]
    sb = tab_ref[2, rows, lanes]
    return x * c + pltpu.roll(x, shift, axis=1) * sa + pltpu.roll(x, width - shift, axis=1) * sb


def _softmax_parts(s_list, sink):
    sink2 = sink * LOG2E
    mx = functools.reduce(jnp.maximum, [jnp.max(s, axis=-1, keepdims=True) for s in s_list])
    mx = jnp.maximum(mx, sink2)
    ps = [jnp.exp2(s - mx) for s in s_list]
    den = functools.reduce(lambda a, b: a + b, [jnp.sum(p, axis=-1, keepdims=True) for p in ps])
    return ps, 1.0 / (den + jnp.exp2(sink2 - mx))


def _kv_per_query_head(x):
    return jnp.concatenate([x[:, HD_A * (h // G_A):HD_A * (h // G_A + 1)] for h in range(H_A)], axis=1)


def _stack_heads(q):
    head = lax.broadcasted_iota(jnp.int32, (1, H_A * HD_A), 1) // HD_A
    return jnp.concatenate([jnp.where(head == h, q, 0.0) for h in range(H_A)], axis=0)


def _pick_heads(o, r):
    head = lax.broadcasted_iota(jnp.int32, (1, H_A * HD_A), 1) // HD_A
    out = o[0:r]
    for h in range(1, H_A):
        out = jnp.where(head == h, o[h * r:(h + 1) * r], out)
    return out


def _sink_rows(sink_ref, layer, r):
    row = lax.broadcasted_iota(jnp.int32, (H_A * r, 1), 0)
    out = jnp.full((H_A * r, 1), sink_ref[layer, 0], f32)
    for h in range(1, H_A):
        out = jnp.where(row >= h * r, sink_ref[layer, h], out)
    return out


def _attn_ctx_kernel(*refs, t, n_prev, layer):
    if n_prev:
        sink_ref, q_ref, kv_ref, kp_ref, vp_ref, o_ref, ko_ref, vo_ref = refs
        ko_ref[:, 0:n_prev] = kp_ref[...]
        vo_ref[:, 0:n_prev] = vp_ref[...]
    else:
        sink_ref, q_ref, kv_ref, o_ref, ko_ref, vo_ref = refs
    kvw = KV_A * HD_A
    sink = _sink_rows(sink_ref, layer, t)
    for u in range(ATTN_CTX_SEQS):
        rows = slice(u * t, (u + 1) * t)
        kv = kv_ref[rows, :]
        for j in range(KV_A):
            ko_ref[u, n_prev, j] = kv[:, HD_A * j:HD_A * (j + 1)]
            vo_ref[u, n_prev, j] = kv[:, kvw + HD_A * j:kvw + HD_A * (j + 1)]
        kx = _kv_per_query_head(kv[:, 0:kvw]).astype(bf16)
        vx = _kv_per_query_head(kv[:, kvw:2 * kvw]).astype(bf16)
        q4 = _stack_heads(q_ref[rows, :] * (HD_A ** -0.5 * LOG2E)).astype(bf16)
        s = lax.dot_general(q4, kx, _NT, preferred_element_type=f32)
        (p,), inv = _softmax_parts([s], sink)
        o_ref[rows, :] = _pick_heads(_dot(p.astype(bf16), vx) * inv, t).astype(o_ref.dtype)


def _cache_spec(n_seq, n_layers, *dims):
    return pl.BlockSpec((n_seq, n_layers) + dims, lambda b: (b,) + (0,) * (1 + len(dims)))


def _attn_ctx(pr, sink, prev, *, layer, batch, t):
    n_prev = 0 if prev is None else prev[0].shape[1]
    ns = ATTN_CTX_SEQS
    kvshape = jax.ShapeDtypeStruct((batch, n_prev + 1, KV_A, t, HD_A), f32)
    in_specs = [pl.BlockSpec(memory_space=pltpu.SMEM),
                pl.BlockSpec((ns * t, 256), lambda b: (b, 0)), pl.BlockSpec((ns * t, 256), lambda b: (b, 1))]
    args = [sink, pr, pr]
    if n_prev:
        in_specs += [_cache_spec(ns, n_prev, KV_A, t, HD_A)] * 2
        args += list(prev)
    return pl.pallas_call(
        functools.partial(_attn_ctx_kernel, t=t, n_prev=n_prev, layer=layer),
        out_shape=(jax.ShapeDtypeStruct((batch * t, 256), bf16), kvshape, kvshape),
        grid=(batch // ns,),
        in_specs=in_specs,
        out_specs=(pl.BlockSpec((ns * t, 256), lambda b: (b, 0)),
                   _cache_spec(ns, n_prev + 1, KV_A, t, HD_A), _cache_spec(ns, n_prev + 1, KV_A, t, HD_A)),
        compiler_params=_params("parallel"),
        name="attn_ctx",
    )(*args)


def _attn_lat_kernel(sink_ref, q_ref, kv_ref, kc_ref, vc_ref, tab_ref, o_ref, kpad, vpad, kctx, vctx, *, t, layer):
    n = pl.program_id(1)
    kvw = KV_A * HD_A
    hw = H_A * HD_A

    @pl.when(n == 0)
    def _():
        zeros = jnp.zeros((BLOCK_A, hw), bf16)
        kv = kv_ref[...]
        k = _rope(kv[:, 0:kvw], tab_ref, slice(None), slice(0, kvw), HD_A // 2)
        kpad[0:BLOCK_A, :] = zeros
        vpad[0:BLOCK_A, :] = zeros
        kpad[BLOCK_A:BLOCK_A + t, :] = _kv_per_query_head(k).astype(bf16)
        vpad[BLOCK_A:BLOCK_A + t, :] = _kv_per_query_head(kv[:, kvw:2 * kvw]).astype(bf16)
        kpad[BLOCK_A + t:2 * BLOCK_A + t, :] = zeros
        vpad[BLOCK_A + t:2 * BLOCK_A + t, :] = zeros
        kctx[...] = jnp.concatenate([kc_ref[0, h // G_A] for h in range(H_A)], axis=1).astype(bf16)
        vctx[...] = jnp.concatenate([vc_ref[0, h // G_A] for h in range(H_A)], axis=1).astype(bf16)

    rows = H_A * BLOCK_A
    rr = lax.broadcasted_iota(jnp.int32, (rows, 1), 0) % BLOCK_A
    cidx = lax.broadcasted_iota(jnp.int32, (rows, 3 * BLOCK_A), 1)
    sink = _sink_rows(sink_ref, layer, BLOCK_A)
    for u in range(ATTN_LAT_BLOCKS):
        blk = n * ATTN_LAT_BLOCKS + u
        row0 = pl.multiple_of(blk * BLOCK_A, BLOCK_A)
        q = _rope(q_ref[u * BLOCK_A:(u + 1) * BLOCK_A, :], tab_ref, pl.ds(row0, BLOCK_A), slice(None), HD_A // 2)
        q4 = _stack_heads(q * (HD_A ** -0.5 * LOG2E)).astype(bf16)
        kband = kpad[pl.ds(row0, 3 * BLOCK_A), :]
        vband = vpad[pl.ds(row0, 3 * BLOCK_A), :]
        s_loc = lax.dot_general(q4, kband, _NT, preferred_element_type=f32)
        s_ctx = lax.dot_general(q4, kctx[...], _NT, preferred_element_type=f32)
        lo = jnp.maximum(rr + (BLOCK_A - WINDOW_A), jnp.maximum(0, (1 - blk) * BLOCK_A))
        hi = jnp.minimum(rr + (BLOCK_A + WINDOW_A), jnp.minimum(3 * BLOCK_A, t - (blk - 1) * BLOCK_A) - 1)
        s_loc = jnp.where((cidx >= lo) & (cidx <= hi), s_loc, NEG_INF)
        (p_loc, p_ctx), inv = _softmax_parts([s_loc, s_ctx], sink)
        o = (_dot(p_loc.astype(bf16), vband) + _dot(p_ctx.astype(bf16), vctx[...])) * inv
        o_ref[u * BLOCK_A:(u + 1) * BLOCK_A, :] = _pick_heads(o, BLOCK_A).astype(o_ref.dtype)


def _attn_lat(pr, sink, kc, vc, tab, *, layer, batch, t):
    rows = ATTN_LAT_BLOCKS * BLOCK_A
    nb = t // rows
    lc = kc.shape[3]
    cspec = pl.BlockSpec((1, None, KV_A, lc, HD_A), lambda b, n: (b, layer, 0, 0, 0))
    return pl.pallas_call(
        functools.partial(_attn_lat_kernel, t=t, layer=layer),
        out_shape=jax.ShapeDtypeStruct((batch * t, 256), bf16),
        grid=(batch, nb),
        in_specs=[pl.BlockSpec(memory_space=pltpu.SMEM),
                  pl.BlockSpec((rows, 256), lambda b, n: (b * nb + n, 0)),
                  pl.BlockSpec((t, 256), lambda b, n: (b, 1)),
                  cspec, cspec,
                  pl.BlockSpec((3, t, 256), lambda b, n: (0, 0, 0))],
        out_specs=pl.BlockSpec((rows, 256), lambda b, n: (b * nb + n, 0)),
        scratch_shapes=[pltpu.VMEM((t + 2 * BLOCK_A, H_A * HD_A), bf16),
                        pltpu.VMEM((t + 2 * BLOCK_A, H_A * HD_A), bf16),
                        pltpu.VMEM((lc, H_A * HD_A), bf16), pltpu.VMEM((lc, H_A * HD_A), bf16)],
        compiler_params=_params("parallel", "arbitrary"),
        name="attn_lat",
    )(sink, pr, pr, kc, vc, tab)


def _reduce_rows(x, reduce_fn):
    r, c = x.shape
    g = ROW_REDUCE_CHAINS
    if r % (g * SUBLANES):
        return reduce_fn(x, axis=0, keepdims=True)
    part = reduce_fn(x.reshape(g, r // (g * SUBLANES), SUBLANES, c), axis=1)
    return reduce_fn(reduce_fn(part, axis=0), axis=0, keepdims=True)


def _mla_kernel(*refs, t, lc, tq, rope, emit_cache, n_prev):
    it = iter(refs)
    cq_ref, ckv_ref, kr_ref = next(it), next(it), next(it)
    cc_ref = kc_ref = tab_ref = None
    if lc:
        cc_ref, kc_ref = next(it), next(it)
    if rope:
        tab_ref = next(it)
    qn_ref, kvn_ref, wq_ref, wk_ref, wv_ref = next(it), next(it), next(it), next(it), next(it)
    ckvp_ref, krp_ref = (next(it), next(it)) if n_prev else (None, None)
    o_ref = next(it)
    ckv_out_ref, kr_out_ref = (next(it), next(it)) if emit_cache else (None, None)
    kbuf, vbuf = next(it), next(it)
    i = pl.program_id(1)
    hw = LANES
    va = V_C + MLA_ONES

    @pl.when(i == 0)
    def _():
        ckv = _rms(ckv_ref[...], kvn_ref[...])
        kr = kr_ref[...]
        if emit_cache:
            if n_prev:
                ckv_out_ref[0, 0:n_prev] = ckvp_ref[0]
                kr_out_ref[0, 0:n_prev] = krp_ref[0]
            ckv_out_ref[0, n_prev] = ckv
            kr_out_ref[0, n_prev] = kr[:, NOPE_C:NOPE_C + ROPE_C]
        ckv_b = ckv.astype(bf16)
        if rope:
            kr = _rope(kr, tab_ref, slice(None), slice(None), ROPE_C // 2)
        kk = _dot(ckv_b, wk_ref[...])
        for h in range(H_C):
            kbuf[h, 0:t, :] = (kk[:, hw * h:hw * (h + 1)] + kr).astype(bf16)
        def put_values(vt, cols):
            for h in range(H_C):
                vbuf[va * h:va * h + V_C, cols] = vt[V_C * h:V_C * (h + 1)]

        for h in range(H_C):
            vbuf[va * h + V_C:va * (h + 1), :] = jnp.ones((MLA_ONES, t + lc), bf16)
        put_values(lax.dot_general(wv_ref[...], ckv_b, _NT, preferred_element_type=f32).astype(bf16), slice(0, t))
        if lc:
            cc_b = cc_ref[0].astype(bf16)
            kkc = _dot(cc_b, wk_ref[...])
            krc = kc_ref[0]
            for h in range(H_C):
                kbuf[h, t:t + lc, :] = (kkc[:, hw * h:hw * (h + 1)] + krc).astype(bf16)
            put_values(lax.dot_general(wv_ref[...], cc_b, _NT, preferred_element_type=f32).astype(bf16),
                       slice(t, t + lc))

    cqn = _rms(cq_ref[...], qn_ref[...]).astype(bf16)
    qf = _dot(cqn, wq_ref[...])
    c2 = (NOPE_C + ROPE_C) ** -0.5 * LOG2E
    row0 = pl.multiple_of(i * tq, tq)
    n_keys = t + lc
    half = n_keys // 2

    def scores(h):
        qh = qf[:, hw * h:hw * (h + 1)]
        if rope:
            qh = _rope(qh, tab_ref, pl.ds(row0, tq), slice(None), ROPE_C // 2)
        qb = (qh * c2).astype(bf16)
        return [lax.dot_general(kbuf[h, c * half:(c + 1) * half, :], qb, _NT, preferred_element_type=f32)
                for c in range(2)]

    def probs(s):
        m = jnp.maximum(_reduce_rows(s[0], jnp.max), _reduce_rows(s[1], jnp.max))
        return [jnp.exp2(x - m).astype(bf16) for x in s]

    def values(h, p):
        rows = slice(va * h, va * (h + 1))
        o = _dot(vbuf[rows, 0:half], p[0]) + _dot(vbuf[rows, half:2 * half], p[1])
        return o[0:V_C] * (1.0 / o[V_C:V_C + 1])

    s_next = scores(0)
    outs = []
    for h in range(H_C):
        s_cur = s_next
        if h + 1 < H_C:
            s_next = scores(h + 1)
        outs.append(values(h, probs(s_cur)))
    o_ref[...] = jnp.concatenate(outs, axis=0).T.astype(o_ref.dtype)


def _mla(pr, qn, kvn, wq, wk, wv, *, layer, batch, t, tq, ctx=None, tab=None, emit_cache=False, cache_prev=None):
    nq = t // tq
    lc = 0 if ctx is None else ctx[0].shape[2]
    n_prev = 0 if cache_prev is None else cache_prev[0].shape[1]
    in_specs = [pl.BlockSpec((tq, 256), lambda b, i: (b * nq + i, 2)),
                pl.BlockSpec((t, LANES), lambda b, i: (b, 6)),
                pl.BlockSpec((t, LANES), lambda b, i: (b, 7))]
    args = [pr, pr, pr]
    if lc:
        in_specs += [pl.BlockSpec((1, None, lc, LANES), lambda b, i: (b, layer, 0, 0))] * 2
        args += list(ctx)
    if tab is not None:
        in_specs.append(pl.BlockSpec((3, t, LANES), lambda b, i: (0, 0, 0)))
        args.append(tab)
    in_specs += [_const_spec(a.shape[1:], layer) for a in (qn, kvn, wq, wk, wv)]
    args += [qn, kvn, wq, wk, wv]

    def cspec(n_layers, width):
        return pl.BlockSpec((1, n_layers, t, width), lambda b, i: (b, 0, 0, 0))

    if n_prev:
        in_specs += [cspec(n_prev, KV_LORA_C), cspec(n_prev, ROPE_C)]
        args += list(cache_prev)
    out_shape = [jax.ShapeDtypeStruct((batch * t, 256), bf16)]
    out_specs = [pl.BlockSpec((tq, 256), lambda b, i: (b * nq + i, 0))]
    if emit_cache:
        out_shape += [jax.ShapeDtypeStruct((batch, n_prev + 1, t, KV_LORA_C), f32),
                      jax.ShapeDtypeStruct((batch, n_prev + 1, t, ROPE_C), f32)]
        out_specs += [cspec(n_prev + 1, KV_LORA_C), cspec(n_prev + 1, ROPE_C)]
    return pl.pallas_call(
        functools.partial(_mla_kernel, t=t, lc=lc, tq=tq, rope=tab is not None, emit_cache=emit_cache,
                          n_prev=n_prev),
        out_shape=tuple(out_shape),
        grid=(batch, nq),
        in_specs=in_specs,
        out_specs=tuple(out_specs),
        scratch_shapes=[pltpu.VMEM((H_C, t + lc, LANES), bf16),
                        pltpu.VMEM((H_C * (V_C + MLA_ONES), t + lc), bf16)],
        compiler_params=_params("parallel", "arbitrary"),
        name="mla",
    )(*args)


def _pool_counts(t):
    pos = np.arange(t)[:, None]
    half = np.repeat(np.asarray(POOL_WINDOWS) // 2, POOL_CH)[None, :]
    return (np.minimum(pos + half, t) - np.maximum(pos - half, 0)).astype(np.float32)


def _pool_kernel(x_ref, cnt_ref, w_ref, sc_ref, o_ref, buf_a, buf_b, buf_c, *, t, seqs):
    for u in range(seqs):
        _pool_sequence(x_ref, cnt_ref, w_ref, sc_ref, o_ref, buf_a, buf_b, buf_c, slice(u * t, (u + 1) * t), t)


def _pool_sequence(x_ref, cnt_ref, w_ref, sc_ref, o_ref, buf_a, buf_b, buf_c, seq, t):
    x = x_ref[seq, :]
    pad, edge = POOL_PAD, SUBLANES
    n = t + 2 * pad
    m = n - 2 * edge
    zeros = jnp.zeros((pad, 256), f32)
    buf_a[0:pad, :] = zeros
    buf_a[pad:pad + t, :] = x
    buf_a[pad + t:n, :] = zeros
    for buf in (buf_b, buf_c):
        buf[0:edge, :] = zeros[0:edge]
        buf[n - edge:n, :] = zeros[0:edge]
    inner = slice(edge, edge + m)
    buf_b[inner, :] = buf_a[pl.ds(edge - 1, m), :] + buf_a[pl.ds(edge, m), :]
    buf_c[inner, :] = buf_b[pl.ds(edge - 1, m), :] + buf_b[pl.ds(edge + 1, m), :]
    lo, hi = slice(0, LANES), slice(LANES, 2 * LANES)
    buf_a[inner, hi] = buf_c[pl.ds(edge - 2, m), hi] + buf_c[pl.ds(edge + 2, m), hi]
    s16 = buf_a[pl.ds(pad - 4, t), hi] + buf_a[pl.ds(pad + 4, t), hi]
    first = lax.broadcasted_iota(jnp.int32, (1, LANES), 1) < POOL_CH
    rows = slice(pad, pad + t)
    win_sum = jnp.concatenate([jnp.where(first, buf_b[rows, lo], buf_c[rows, lo]),
                               jnp.where(first, buf_a[rows, hi], s16)], axis=1)
    mean = win_sum / cnt_ref[...]
    o_ref[seq, :] = (_dot((mean - x).astype(bf16), w_ref[...]) * sc_ref[...]).astype(o_ref.dtype)


def _pool(pr, w_bd, scale, *, layer, batch, t, seqs=1):
    return pl.pallas_call(
        functools.partial(_pool_kernel, t=t, seqs=seqs),
        out_shape=jax.ShapeDtypeStruct((batch * t, 256), bf16),
        grid=(batch // seqs,),
        in_specs=[pl.BlockSpec((seqs * t, 256), lambda b: (b, 4)), _const_spec((t, 256)),
                  _const_spec((256, 256), layer), _const_spec((1, 256), layer)],
        out_specs=pl.BlockSpec((seqs * t, 256), lambda b: (b, 0)),
        scratch_shapes=[pltpu.VMEM((t + 2 * POOL_PAD, 256), f32)] * 3,
        compiler_params=_params("parallel"),
        name="pool",
    )(pr, jnp.asarray(_pool_counts(t)), w_bd, scale)


def _hgrn_mats():
    n = HG_TILE
    s = np.arange(n)[:, None]
    t = np.arange(n)[None, :]
    same_half = s // HG_HALF == t // HG_HALF
    mid = (t // HG_HALF) * HG_HALF + HG_HALF // 2 - 1
    rel_mid = (same_half & (s > mid) & (s <= t)).astype(np.float32) - (same_half & (s <= mid) & (s > t))
    mats = [rel_mid, same_half & (s <= t), s <= t, s > t, (s // HG_BLOCK == t // HG_BLOCK) & (s <= t)]
    for m in HG_LEVELS:
        same_pair = s // (2 * m) == t // (2 * m)
        right_t = (t // m) % 2 == 1
        right_s = (s // m) % 2 == 1
        q_side = same_pair & right_t & right_s & (s <= t)
        k_side = same_pair & ~right_t & ~right_s & (s > t)
        mats.append(q_side | k_side)
    fwd = np.concatenate([m.astype(np.float32) for m in mats], axis=1)
    bwd = np.concatenate([m[::-1, ::-1].astype(np.float32) for m in mats], axis=1)
    both = np.stack([fwd, bwd])
    return np.concatenate([both, both], axis=1)


def _hgrn_exact_intra(q, k, v, vb, hl, mats, reverse):
    n = HG_TILE
    e_all = _dot(hl, mats)
    beta = e_all[:, 0:n]
    lane = lax.broadcasted_iota(jnp.int32, (1, n), 1)
    pos = lane % HG_BLOCK
    v3 = v.reshape(H_B, DV_B, n)

    o3 = jnp.sum((q * k).reshape(H_B, DK_B, n), axis=1, keepdims=True) * v3
    for d in range(1, HG_BLOCK):
        shift = (n - d) if reverse else d
        valid = (pos <= HG_BLOCK - 1 - d) if reverse else (pos >= d)
        kr = pltpu.roll(k, shift, axis=1)
        br = pltpu.roll(beta, shift, axis=1)
        vr = pltpu.roll(v, shift, axis=1)
        e = jnp.exp(jnp.where(valid, beta - br, NEG_INF))
        sc = jnp.sum((q * kr * e).reshape(H_B, DK_B, n), axis=1, keepdims=True)
        o3 = o3 + sc * vr.reshape(H_B, DV_B, n)

    srow = lax.broadcasted_iota(jnp.int32, (n, n), 0)
    tcol = lax.broadcasted_iota(jnp.int32, (n, n), 1)
    scores = [jnp.zeros((n, n), f32) for _ in range(H_B)]
    for li, m in enumerate(HG_LEVELS):
        ex = jnp.exp(e_all[:, (li + 1) * n:(li + 2) * n])
        is_q = ((lane // m) % 2) == (0 if reverse else 1)
        qs = jnp.where(is_q, q * ex, 0.0).astype(bf16)
        ks = jnp.where(is_q, 0.0, k * ex).astype(bf16)
        pair = (srow // (2 * m)) == (tcol // (2 * m))
        for h in range(H_B):
            a = lax.dot_general(ks[DK_B * h:DK_B * (h + 1)], qs[DK_B * h:DK_B * (h + 1)], _TN,
                                preferred_element_type=f32)
            scores[h] = scores[h] + jnp.where(pair, a, 0.0)

    return [o3[h] + _dot(vb[DV_B * h:DV_B * (h + 1)], scores[h].astype(bf16)) for h in range(H_B)]


def _hgrn_common(q, v, fpre, lb, mats_ref, sidx, st_ref):
    n = HG_TILE
    f = lb + (1.0 - lb) * _sigmoid(fpre)
    k = 1.0 - f
    logf = jnp.log(f)
    hi = logf.astype(bf16)
    hl = jnp.concatenate([hi, (logf - hi.astype(f32)).astype(bf16)], axis=1)
    ex = _dot(hl, mats_ref[sidx, :, 0:4 * n])
    ones = jnp.ones((8, 2 * n), bf16)
    g_row = lax.dot_general(ones, hl, _NT, preferred_element_type=f32)[0:1]
    d_mid = ex[:, 0:n]
    small = jnp.max(jnp.abs(d_mid)) <= HG_MAX_EXP
    q_in = (q * jnp.exp(ex[:, 2 * n:3 * n])).astype(bf16)
    k_out = (k * jnp.exp(ex[:, 3 * n:4 * n])).astype(bf16)
    vb = v.astype(bf16)
    o_state = []
    for h in range(H_B):
        rows = slice(DK_B * h, DK_B * (h + 1))
        st = st_ref[sidx, rows, :]
        o_state.append(_dot(st.astype(bf16), q_in[rows]))
        upd = lax.dot_general(vb[rows], k_out[rows], _NT, preferred_element_type=f32)
        st_ref[sidx, rows, :] = st * jnp.exp(g_row[:, rows]) + upd
    return dict(q=q, k=k, v=v, vb=vb, hl=hl, ex=ex, o_state=o_state, small=small)


def _hgrn_fast_intra(c, reverse):
    n = HG_TILE
    q, k, ex = c["q"], c["k"], c["ex"]
    lane = lax.broadcasted_iota(jnp.int32, (1, n), 1)
    in0 = (lane // HG_HALF) == 0
    q_half = in0 if reverse else jnp.logical_not(in0)
    d_mid = ex[:, 0:n]
    qd = q * jnp.exp(d_mid)
    kd = k * jnp.exp(-d_mid)
    b_in = ex[:, n:2 * n]
    last = (0, HG_HALF) if reverse else (HG_HALF - 1, n - 1)
    total = jnp.where(in0, jnp.broadcast_to(b_in[:, last[0]:last[0] + 1], b_in.shape),
                      jnp.broadcast_to(b_in[:, last[1]:last[1] + 1], b_in.shape))
    ql = jnp.where(q_half, q * jnp.exp(b_in), 0.0)
    kl = jnp.where(q_half, 0.0, k * jnp.exp(total - b_in))
    parts_q = [jnp.where(in0, qd, 0.0).astype(bf16), jnp.where(in0, 0.0, qd).astype(bf16), ql.astype(bf16)]
    parts_k = [jnp.where(in0, kd, 0.0).astype(bf16), jnp.where(in0, 0.0, kd).astype(bf16), kl.astype(bf16)]
    srow = lax.broadcasted_iota(jnp.int32, (n, n), 0)
    tcol = lax.broadcasted_iota(jnp.int32, (n, n), 1)
    causal = (srow >= tcol) if reverse else (srow <= tcol)
    scores = []
    for h in range(H_B):
        rows = slice(DK_B * h, DK_B * (h + 1))
        k3 = jnp.concatenate([p[rows] for p in parts_k], axis=0)
        q3 = jnp.concatenate([p[rows] for p in parts_q], axis=0)
        scores.append(lax.dot_general(k3, q3, _TN, preferred_element_type=f32))
    scores = [jnp.where(causal, a, 0.0).astype(bf16) for a in scores]
    outs = [c["o_state"][h] + _dot(c["vb"][DV_B * h:DV_B * (h + 1)], scores[h]) for h in range(H_B)]
    return jnp.concatenate(outs, axis=0)


def _hgrn_exact(c, mats, reverse):
    intra = _hgrn_exact_intra(c["q"], c["k"], c["v"], c["vb"], c["hl"], mats, reverse)
    return jnp.concatenate([c["o_state"][h] + intra[h] for h in range(H_B)], axis=0)


def _hgrn_kernel(*refs, t, has_s0, emit_state, n_prev):
    it = iter(refs)
    q_ref, i_ref, ff_ref, fb_ref, g_ref, lb_ref, ng_ref, mats_ref = [next(it) for _ in range(8)]
    s0_ref = next(it) if has_s0 else None
    sprev_ref = next(it) if n_prev else None
    o_ref = next(it)
    sfin_ref = next(it) if emit_state else None
    st_ref, oacc_f, oacc_b = next(it), next(it), next(it)
    n = HG_TILE
    nt = t // n
    for d in range(2):
        for h in range(H_B):
            rows = slice(DV_B * h, DV_B * (h + 1))
            st_ref[d, rows, :] = s0_ref[0, d, h].T if has_s0 else jnp.zeros((DV_B, DK_B), f32)

    def scan(i, c):
        work = []
        small = None
        for u in range(HG_UNROLL):
            j = i * HG_UNROLL + u
            cols_f = pl.ds(pl.multiple_of(j * n, n), n)
            cols_b = pl.ds(pl.multiple_of((nt - 1 - j) * n, n), n)
            cf = _hgrn_common(q_ref[:, cols_f], i_ref[:, cols_f], ff_ref[:, cols_f], lb_ref[0], mats_ref, 0, st_ref)
            cb = _hgrn_common(q_ref[:, cols_b], i_ref[:, cols_b], fb_ref[:, cols_b], lb_ref[1], mats_ref, 1, st_ref)
            work.append((cols_f, cf, cols_b, cb))
            both = jnp.logical_and(cf["small"], cb["small"])
            small = both if small is None else jnp.logical_and(small, both)
        for cols_f, cf, cols_b, cb in work:
            oacc_f[:, cols_f] = _hgrn_fast_intra(cf, False)
            oacc_b[:, cols_b] = _hgrn_fast_intra(cb, True)

        @pl.when(jnp.logical_not(small))
        def _():
            for cols_f, cf, cols_b, cb in work:
                oacc_f[:, cols_f] = _hgrn_exact(cf, mats_ref[0, :, 4 * n:8 * n], False)
                oacc_b[:, cols_b] = _hgrn_exact(cb, mats_ref[1, :, 4 * n:8 * n], True)

        return c

    lax.fori_loop(0, nt // HG_UNROLL, scan, 0)

    def finish(i, c):
        off = pl.multiple_of(i * n, n)
        cols = pl.ds(off, n)
        o = oacc_f[:, cols] + oacc_b[:, cols]
        o3 = o.reshape(H_B, DV_B, n)
        y = o3 * lax.rsqrt(jnp.mean(o3 * o3, axis=1, keepdims=True) + EPS)
        gate = g_ref[:, cols]
        y = y.reshape(H_B * DV_B, n) * ng_ref[...] * (gate * _sigmoid(gate))
        o_ref[pl.ds(off, n), :] = y.T.astype(o_ref.dtype)
        return c

    lax.fori_loop(0, nt, finish, 0)
    if emit_state:
        if n_prev:
            sfin_ref[0, 0:n_prev] = sprev_ref[0]
        for d in range(2):
            for h in range(H_B):
                sfin_ref[0, n_prev, d, h] = st_ref[d, DV_B * h:DV_B * (h + 1), :].T


def _hgrn(pt, lb, ng, mats, *, batch, t, s0=None, layer=0, emit_state=False, state_prev=None):
    def row(r):
        return pl.BlockSpec((256, t), lambda b: (r, b))

    def sspec(n_layers):
        return pl.BlockSpec((1, n_layers, 2, H_B, DK_B, DV_B), lambda b: (b, 0, 0, 0, 0, 0))

    n_prev = 0 if state_prev is None else state_prev.shape[1]
    in_specs = [row(0), row(1), row(2), row(3), row(4),
                _const_spec((2, 256, LANES), layer), _const_spec((256, LANES), layer), _const_spec(mats.shape)]
    args = [pt, pt, pt, pt, pt, lb, ng, mats]
    if s0 is not None:
        in_specs.append(pl.BlockSpec((1, None, 2, H_B, DK_B, DV_B), lambda b: (b, layer, 0, 0, 0, 0)))
        args.append(s0)
    if n_prev:
        in_specs.append(sspec(n_prev))
        args.append(state_prev)
    out_shape = [jax.ShapeDtypeStruct((batch * t, 256), bf16)]
    out_specs = [pl.BlockSpec((t, 256), lambda b: (b, 0))]
    if emit_state:
        out_shape.append(jax.ShapeDtypeStruct((batch, n_prev + 1, 2, H_B, DK_B, DV_B), f32))
        out_specs.append(sspec(n_prev + 1))
    return pl.pallas_call(
        functools.partial(_hgrn_kernel, t=t, has_s0=s0 is not None, emit_state=emit_state, n_prev=n_prev),
        out_shape=tuple(out_shape),
        grid=(batch,),
        in_specs=in_specs,
        out_specs=tuple(out_specs),
        scratch_shapes=[pltpu.VMEM((2, H_B * DV_B, DK_B), f32), pltpu.VMEM((256, t), f32),
                        pltpu.VMEM((256, t), f32)],
        compiler_params=_params("parallel"),
        name="hgrn",
    )(*args)


def _rope_tables(t):
    def cos_sin(dim):
        n_freq = dim // 4
        inv = ROPE_THETA ** (-jnp.arange(n_freq, dtype=f32) / n_freq)
        rows = t // GRID_W
        row_id = jnp.repeat(jnp.arange(rows, dtype=f32), GRID_W)
        col_id = jnp.tile(jnp.arange(GRID_W, dtype=f32), rows)
        ang = jnp.concatenate([row_id[:, None] * inv, col_id[:, None] * inv], axis=-1)
        return jnp.cos(ang), jnp.sin(ang)

    cos, sin = cos_sin(HD_A)
    zero = jnp.zeros_like(sin)
    tab_a = jnp.stack([jnp.tile(jnp.concatenate([cos, cos], -1), (1, H_A)),
                       jnp.tile(jnp.concatenate([zero, sin], -1), (1, H_A)),
                       jnp.tile(jnp.concatenate([-sin, zero], -1), (1, H_A))])
    cos, sin = cos_sin(ROPE_C)
    zero = jnp.zeros_like(sin)
    one64 = jnp.ones((t, NOPE_C), f32)
    zero64 = jnp.zeros((t, NOPE_C), f32)
    pad32 = jnp.zeros((t, LANES - NOPE_C - ROPE_C), f32)
    tab_c = jnp.stack([jnp.concatenate([one64, cos, cos, pad32 + 1.0], -1),
                       jnp.concatenate([zero64, zero, sin, pad32], -1),
                       jnp.concatenate([zero64, -sin, zero, pad32], -1)])
    return tab_a, tab_c


def _pad_rope_cols(x):
    pads = [(0, 0)] * (x.ndim - 1) + [(NOPE_C, LANES - NOPE_C - ROPE_C)]
    return jnp.pad(x, pads)


def kernel(x_prompt, x_sample, c, c_ctx, cache_attn_k, cache_attn_v, state_hgrn, cache_mla_ckv, cache_mla_krope,
           w_ada, b_ada, norm_sub, w_ffn_gate, w_ffn_up, w_ffn_down, w_in, w_out, attn_sink, hgrn_lb_logits,
           hgrn_out_norm, mla_q_norm, mla_kv_norm, mla_w_uq, mla_w_ukv, pool_w, pool_scale, final_norm):
    bp, tp, _ = x_prompt.shape
    bs, ts, _ = x_sample.shape
    np_, ns = bp * tp, bs * ts
    xp = x_prompt.reshape(np_, D_MODEL)
    xs = x_sample.reshape(ns, D_MODEL)

    n_vec = 1 + bs
    cvec = jnp.zeros((-(-n_vec // 8) * 8, D_MODEL), f32).at[0].set(c_ctx).at[1:n_vec].set(c)
    mods = _modulation(cvec, w_ada, b_ada).reshape(DEPTH, cvec.shape[0], N_MOD, D_MODEL)

    p_lb = jax.nn.softmax(hgrn_lb_logits.astype(f32), axis=0)
    lb_all = jnp.cumsum(p_lb, axis=0) - p_lb[0:1]
    tab_a, tab_c = _rope_tables(ts)
    mats = jnp.asarray(_hgrn_mats(), bf16)
    ffn_f32 = (w_ffn_gate, w_ffn_up, w_ffn_down)
    ffn_w = {(0, 0): tuple(_to_bf16(ffn_f32, 0, 0))}
    to_cast = [(l, idx) for l in range(DEPTH) for idx in range(2)][1:]
    grp_p = dict(base=0, rpm=np_)
    grp_s = dict(base=1, rpm=ts)
    tm, tm_mix = 512, 1024

    wb = lambda w: w.astype(bf16)
    w_row = wb(jnp.concatenate([w_in[:, :, 0:512], w_in[:, :, 1792:2176], _pad_rope_cols(w_in[:, :, 2176:2208]),
                                w_in[:, :, 2208:2464]], axis=2))
    w_t = wb(jnp.swapaxes(w_in[:, :, 512:1792], 1, 2))
    wq = wb(jnp.pad(mla_w_uq.reshape(DEPTH, Q_LORA_C, H_C, NOPE_C + ROPE_C),
                    ((0, 0), (0, 0), (0, 0), (0, LANES - NOPE_C - ROPE_C))).reshape(DEPTH, Q_LORA_C, H_C * LANES))
    wkv = mla_w_ukv.reshape(DEPTH, KV_LORA_C, H_C, NOPE_C + V_C)
    wk = wb(jnp.pad(wkv[..., :NOPE_C], ((0, 0), (0, 0), (0, 0), (0, LANES - NOPE_C)))
            .reshape(DEPTH, KV_LORA_C, H_C * LANES))
    wv = wb(jnp.swapaxes(wkv[..., NOPE_C:].reshape(DEPTH, KV_LORA_C, H_C * V_C), 1, 2))
    n_grp = len(POOL_WINDOWS)
    w_pool = wb(jnp.einsum('lgij,gh->lgihj', pool_w, jnp.eye(n_grp, dtype=f32))
                .reshape(DEPTH, n_grp * POOL_CH, n_grp * POOL_CH))
    lb = jnp.broadcast_to(lb_all[..., None], (DEPTH, 2, 256, LANES))
    ng = jnp.broadcast_to(jnp.tile(hgrn_out_norm, (1, H_B))[..., None], (DEPTH, 256, LANES))
    sink = attn_sink.astype(f32)
    qn = mla_q_norm.reshape(DEPTH, 1, Q_LORA_C)
    kvn = mla_kv_norm.reshape(DEPTH, 1, KV_LORA_C)
    pscale = pool_scale.reshape(DEPTH, 1, 256)
    wo = wb(w_out)
    norms = norm_sub.reshape(DEPTH * 3, 1, D_MODEL)
    ctx_c = (cache_mla_ckv, _pad_rope_cols(cache_mla_krope))

    new_k = new_v = new_st = new_ckv = new_kr = None
    for l in range(DEPTH):
        def ffn(x, grp, j, idx, mix=None, final_g=None):
            side = None
            if to_cast:
                nxt = to_cast.pop(0)
                side = nxt + (ffn_f32,)
            res = _ffn(x, mods, norms, ffn_w[(l, idx)], layer=l, j=j, tm=tm, mix=mix, final_g=final_g, side=side,
                       **grp)
            if side is not None:
                ffn_w[nxt] = tuple(res[1:])
            return res[0]

        xp = ffn(xp, grp_p, 0, 0)
        xs = ffn(xs, grp_s, 0, 0)

        pr, pt = _mixer_in(xp, mods, norms, w_row, w_t, layer=l, tm=tm_mix, **grp_p)
        oa, new_k, new_v = _attn_ctx(pr, sink, None if l == 0 else (new_k, new_v), layer=l, batch=bp, t=tp)
        ob, new_st = _hgrn(pt, lb, ng, mats, layer=l, batch=bp, t=tp, emit_state=True, state_prev=new_st)
        oc, new_ckv, new_kr = _mla(pr, qn, kvn, wq, wk, wv, layer=l, batch=bp, t=tp, tq=tp, emit_cache=True,
                                   cache_prev=None if l == 0 else (new_ckv, new_kr))
        od = _pool(pr, w_pool, pscale, layer=l, batch=bp, t=tp, seqs=POOL_CTX_SEQS)
        mix_p = (oa, ob, oc, od, wo)

        pr, pt = _mixer_in(xs, mods, norms, w_row, w_t, layer=l, tm=tm_mix, **grp_s)
        oa = _attn_lat(pr, sink, cache_attn_k, cache_attn_v, tab_a, layer=l, batch=bs, t=ts)
        (ob,) = _hgrn(pt, lb, ng, mats, layer=l, batch=bs, t=ts, s0=state_hgrn)
        (oc,) = _mla(pr, qn, kvn, wq, wk, wv, layer=l, batch=bs, t=ts, tq=512, ctx=ctx_c, tab=tab_c)
        od = _pool(pr, w_pool, pscale, layer=l, batch=bs, t=ts)
        mix_s = (oa, ob, oc, od, wo)

        fin = final_norm.reshape(1, D_MODEL) if l == DEPTH - 1 else None
        xp = ffn(xp, grp_p, 2, 1, mix_p, fin)
        xs = ffn(xs, grp_s, 2, 1, mix_s, fin)

    return (xp.reshape(bp, tp, D_MODEL), xs.reshape(bs, ts, D_MODEL), new_k, new_v, new_st, new_ckv, new_kr)
```

```python
import functools

import numpy as np
import jax
import jax.numpy as jnp
from jax import lax
from jax.experimental import pallas as pl
from jax.experimental.pallas import tpu as pltpu

f32 = jnp.float32
bf16 = jnp.bfloat16

D_MODEL = 1024
DEPTH = 2
GRID_W = 64
EPS = 1e-6
LOG2E = 1.4426950408889634
ROPE_THETA = 10000.0
NEG_INF = -1e30
N_MOD = 9
D_FF = 2816
H_A, KV_A, HD_A = 4, 2, 64
G_A = H_A // KV_A
WINDOW_A = 128
BLOCK_A = 128
H_B, DK_B, DV_B = 4, 64, 64
H_C, Q_LORA_C, KV_LORA_C, NOPE_C, ROPE_C, V_C = 4, 256, 128, 64, 32, 64
POOL_WINDOWS = (2, 4, 8, 16)
POOL_CH = 64

LANES = 128
SUBLANES = 8
CAST_BLOCKS = 16
ROW_REDUCE_CHAINS = 8
MLA_ONES = 2 * SUBLANES
VMEM_LIMIT = 56 * 1024 * 1024
HG_TILE = LANES
HG_UNROLL = 2
HG_HALF = HG_TILE // 2
HG_MAX_EXP = 80.0
HG_BLOCK = 16
HG_LEVELS = (16, 32, 64)
ATTN_CTX_SEQS = 2
POOL_CTX_SEQS = 8
ATTN_LAT_BLOCKS = 4
POOL_PAD = 16

PR_W = 1280
PT_W = 1280

_NT = (((1,), (1,)), ((), ()))
_TN = (((0,), (0,)), ((), ()))


def _params(*sem):
    return pltpu.CompilerParams(dimension_semantics=sem, vmem_limit_bytes=VMEM_LIMIT)


def _const_spec(shape, lead=None):
    zeros = (0,) * len(shape)
    if lead is None:
        return pl.BlockSpec(shape, lambda *_: zeros, pipeline_mode=pl.Buffered(1))
    return pl.BlockSpec((None,) + tuple(shape), lambda *_: (lead,) + zeros, pipeline_mode=pl.Buffered(1))


def _dot(a, b):
    return jnp.dot(a, b, preferred_element_type=f32)


def _sigmoid(x):
    return 1.0 / (1.0 + jnp.exp(-x))


def _rms(x, g):
    return x * lax.rsqrt(jnp.mean(x * x, axis=-1, keepdims=True) + EPS) * g


def _mod_kernel(c_ref, w_ref, b_ref, o_ref):
    c = c_ref[...]
    a = (c * _sigmoid(c)).astype(bf16)
    o_ref[0] = _dot(a, w_ref[0].astype(bf16)) + b_ref[0]


def _modulation(cvec, w_ada, b_ada):
    tn = 1536
    n_out = N_MOD * D_MODEL
    return pl.pallas_call(
        _mod_kernel,
        out_shape=jax.ShapeDtypeStruct((DEPTH, cvec.shape[0], n_out), f32),
        grid=(DEPTH, n_out // tn),
        in_specs=[pl.BlockSpec(cvec.shape, lambda l, j: (0, 0)),
                  pl.BlockSpec((1, D_MODEL, tn), lambda l, j: (l, 0, j)),
                  pl.BlockSpec((1, 1, tn), lambda l, j: (l, 0, j))],
        out_specs=pl.BlockSpec((1, cvec.shape[0], tn), lambda l, j: (l, 0, j)),
        compiler_params=_params("arbitrary", "arbitrary"),
        name="modulation",
    )(cvec, w_ada, b_ada.reshape(DEPTH, 1, n_out))


def _cast_kernel(*refs):
    n = len(refs) // 2
    for src, dst in zip(refs[:n], refs[n:]):
        dst[...] = src[...].astype(bf16)


def _to_bf16(ws, layer, idx):
    def blk(w):
        return w.shape[2] // CAST_BLOCKS, w.shape[3]

    return pl.pallas_call(
        _cast_kernel,
        out_shape=tuple(jax.ShapeDtypeStruct(w.shape[2:], bf16) for w in ws),
        grid=(CAST_BLOCKS,),
        in_specs=[pl.BlockSpec((None, None) + blk(w), lambda i: (layer, idx, i, 0)) for w in ws],
        out_specs=tuple(pl.BlockSpec(blk(w), lambda i: (i, 0)) for w in ws),
        compiler_params=_params("parallel"),
        name="cast_bf16",
    )(*ws)


def _ffn_kernel(*refs, j, mix, final, side):
    it = iter(refs)
    x_ref, m_ref = next(it), next(it)
    mix_refs = [next(it) for _ in range(5)] if mix else None
    g_ref, wg_ref, wu_ref, wd_ref = next(it), next(it), next(it), next(it)
    fg_ref = next(it) if final else None
    side_in = [next(it) for _ in range(3)] if side else []
    o_ref = next(it)
    for src, dst in zip(side_in, [next(it) for _ in range(len(side_in))]):
        dst[...] = src[...].astype(bf16)
    x = x_ref[...]
    m = m_ref[0]
    if mix:
        wo_ref = mix_refs[4]
        acc = _dot(mix_refs[0][...], wo_ref[0:256, :])
        for p in range(1, 4):
            acc += _dot(mix_refs[p][...], wo_ref[256 * p:256 * (p + 1), :])
        x = x + m[5:6] * acc
    h = (_rms(x, g_ref[...]) * (1.0 + m[3 * j + 1:3 * j + 2]) + m[3 * j:3 * j + 1]).astype(bf16)
    gate = _dot(h, wg_ref[...])
    up = _dot(h, wu_ref[...])
    act = (gate * _sigmoid(gate) * up).astype(bf16)
    y = x + (0.5 * m[3 * j + 2:3 * j + 3]) * _dot(act, wd_ref[...])
    if final:
        y = _rms(y, fg_ref[...])
    o_ref[...] = y


def _mod_spec(layer, base, rpm, tm):
    return pl.BlockSpec((None, 1, N_MOD, D_MODEL), lambda i: (layer, base + (i * tm) // rpm, 0, 0))


def _ffn(x, mods, norms, w3, *, layer, j, base, rpm, tm, mix=None, final_g=None, side=None):
    n = x.shape[0]
    steps = n // tm
    in_specs = [pl.BlockSpec((tm, D_MODEL), lambda i: (i, 0)), _mod_spec(layer, base, rpm, tm)]
    args = [x, mods]
    if mix is not None:
        in_specs += [pl.BlockSpec((tm, 256), lambda i: (i, 0))] * 4 + [_const_spec((D_MODEL, D_MODEL), layer)]
        args += list(mix)
    in_specs += [_const_spec((1, D_MODEL), 3 * layer + j)] + [_const_spec(w.shape) for w in w3]
    args += [norms] + list(w3)
    if final_g is not None:
        in_specs.append(_const_spec((1, D_MODEL)))
        args.append(final_g)
    out_shape = [jax.ShapeDtypeStruct((n, D_MODEL), f32)]
    out_specs = [pl.BlockSpec((tm, D_MODEL), lambda i: (i, 0))]
    if side is not None:
        sl, sidx, arrays = side
        nblk = CAST_BLOCKS if steps % CAST_BLOCKS == 0 else steps
        for a in arrays:
            assert a.shape[2] % (nblk * 2 * SUBLANES) == 0, (a.shape, nblk)
            r, c = a.shape[2] // nblk, a.shape[3]
            in_specs.append(pl.BlockSpec((None, None, r, c), lambda i: (sl, sidx, (i * nblk) // steps, 0)))
            args.append(a)
            out_shape.append(jax.ShapeDtypeStruct(a.shape[2:], bf16))
            out_specs.append(pl.BlockSpec((r, c), lambda i: ((i * nblk) // steps, 0)))
    return pl.pallas_call(
        functools.partial(_ffn_kernel, j=j, mix=mix is not None, final=final_g is not None, side=side is not None),
        out_shape=tuple(out_shape),
        grid=(steps,),
        in_specs=in_specs,
        out_specs=tuple(out_specs),
        compiler_params=_params("arbitrary" if side is not None else "parallel"),
        name="ffn",
    )(*args)


def _mixin_kernel(x_ref, m_ref, g_ref, wr_ref, wt_ref, pr_ref, pt_ref):
    m = m_ref[0]
    h = (_rms(x_ref[...], g_ref[...]) * (1.0 + m[4:5]) + m[3:4]).astype(bf16)
    pr_ref[...] = _dot(h, wr_ref[...])
    pt_ref[...] = lax.dot_general(wt_ref[...], h, _NT, preferred_element_type=f32)


def _mixer_in(x, mods, norms, w_row, w_t, *, layer, base, rpm, tm):
    n = x.shape[0]
    return pl.pallas_call(
        _mixin_kernel,
        out_shape=(jax.ShapeDtypeStruct((n, PR_W), f32), jax.ShapeDtypeStruct((PT_W, n), f32)),
        grid=(n // tm,),
        in_specs=[pl.BlockSpec((tm, D_MODEL), lambda i: (i, 0)), _mod_spec(layer, base, rpm, tm),
                  _const_spec((1, D_MODEL), 3 * layer + 1), _const_spec((D_MODEL, PR_W), layer),
                  _const_spec((PT_W, D_MODEL), layer)],
        out_specs=(pl.BlockSpec((tm, PR_W), lambda i: (i, 0)), pl.BlockSpec((PT_W, tm), lambda i: (0, i))),
        compiler_params=_params("parallel"),
        name="mixer_in",
    )(x, mods, norms, w_row, w_t)


def _rope(x, tab_ref, rows, lanes, shift):
    width = x.shape[-1]
    c = tab_ref[0, rows, lanes]
    sa = tab_ref[1, rows, lanes]
    sb = tab_ref[2, rows, lanes]
    return x * c + pltpu.roll(x, shift, axis=1) * sa + pltpu.roll(x, width - shift, axis=1) * sb


def _softmax_parts(s_list, sink):
    sink2 = sink * LOG2E
    mx = functools.reduce(jnp.maximum, [jnp.max(s, axis=-1, keepdims=True) for s in s_list])
    mx = jnp.maximum(mx, sink2)
    ps = [jnp.exp2(s - mx) for s in s_list]
    den = functools.reduce(lambda a, b: a + b, [jnp.sum(p, axis=-1, keepdims=True) for p in ps])
    return ps, 1.0 / (den + jnp.exp2(sink2 - mx))


def _kv_per_query_head(x):
    return jnp.concatenate([x[:, HD_A * (h // G_A):HD_A * (h // G_A + 1)] for h in range(H_A)], axis=1)


def _stack_heads(q):
    head = lax.broadcasted_iota(jnp.int32, (1, H_A * HD_A), 1) // HD_A
    return jnp.concatenate([jnp.where(head == h, q, 0.0) for h in range(H_A)], axis=0)


def _pick_heads(o, r):
    head = lax.broadcasted_iota(jnp.int32, (1, H_A * HD_A), 1) // HD_A
    out = o[0:r]
    for h in range(1, H_A):
        out = jnp.where(head == h, o[h * r:(h + 1) * r], out)
    return out


def _sink_rows(sink_ref, layer, r):
    row = lax.broadcasted_iota(jnp.int32, (H_A * r, 1), 0)
    out = jnp.full((H_A * r, 1), sink_ref[layer, 0], f32)
    for h in range(1, H_A):
        out = jnp.where(row >= h * r, sink_ref[layer, h], out)
    return out


def _attn_ctx_kernel(*refs, t, n_prev, layer):
    if n_prev:
        sink_ref, q_ref, kv_ref, kp_ref, vp_ref, o_ref, ko_ref, vo_ref = refs
        ko_ref[:, 0:n_prev] = kp_ref[...]
        vo_ref[:, 0:n_prev] = vp_ref[...]
    else:
        sink_ref, q_ref, kv_ref, o_ref, ko_ref, vo_ref = refs
    kvw = KV_A * HD_A
    sink = _sink_rows(sink_ref, layer, t)
    for u in range(ATTN_CTX_SEQS):
        rows = slice(u * t, (u + 1) * t)
        kv = kv_ref[rows, :]
        for j in range(KV_A):
            ko_ref[u, n_prev, j] = kv[:, HD_A * j:HD_A * (j + 1)]
            vo_ref[u, n_prev, j] = kv[:, kvw + HD_A * j:kvw + HD_A * (j + 1)]
        kx = _kv_per_query_head(kv[:, 0:kvw]).astype(bf16)
        vx = _kv_per_query_head(kv[:, kvw:2 * kvw]).astype(bf16)
        q4 = _stack_heads(q_ref[rows, :] * (HD_A ** -0.5 * LOG2E)).astype(bf16)
        s = lax.dot_general(q4, kx, _NT, preferred_element_type=f32)
        (p,), inv = _softmax_parts([s], sink)
        o_ref[rows, :] = _pick_heads(_dot(p.astype(bf16), vx) * inv, t).astype(o_ref.dtype)


def _cache_spec(n_seq, n_layers, *dims):
    return pl.BlockSpec((n_seq, n_layers) + dims, lambda b: (b,) + (0,) * (1 + len(dims)))


def _attn_ctx(pr, sink, prev, *, layer, batch, t):
    n_prev = 0 if prev is None else prev[0].shape[1]
    ns = ATTN_CTX_SEQS
    kvshape = jax.ShapeDtypeStruct((batch, n_prev + 1, KV_A, t, HD_A), f32)
    in_specs = [pl.BlockSpec(memory_space=pltpu.SMEM),
                pl.BlockSpec((ns * t, 256), lambda b: (b, 0)), pl.BlockSpec((ns * t, 256), lambda b: (b, 1))]
    args = [sink, pr, pr]
    if n_prev:
        in_specs += [_cache_spec(ns, n_prev, KV_A, t, HD_A)] * 2
        args += list(prev)
    return pl.pallas_call(
        functools.partial(_attn_ctx_kernel, t=t, n_prev=n_prev, layer=layer),
        out_shape=(jax.ShapeDtypeStruct((batch * t, 256), bf16), kvshape, kvshape),
        grid=(batch // ns,),
        in_specs=in_specs,
        out_specs=(pl.BlockSpec((ns * t, 256), lambda b: (b, 0)),
                   _cache_spec(ns, n_prev + 1, KV_A, t, HD_A), _cache_spec(ns, n_prev + 1, KV_A, t, HD_A)),
        compiler_params=_params("parallel"),
        name="attn_ctx",
    )(*args)


def _attn_lat_kernel(sink_ref, q_ref, kv_ref, kc_ref, vc_ref, tab_ref, o_ref, kpad, vpad, kctx, vctx, *, t, layer):
    n = pl.program_id(1)
    kvw = KV_A * HD_A
    hw = H_A * HD_A

    @pl.when(n == 0)
    def _():
        zeros = jnp.zeros((BLOCK_A, hw), bf16)
        kv = kv_ref[...]
        k = _rope(kv[:, 0:kvw], tab_ref, slice(None), slice(0, kvw), HD_A // 2)
        kpad[0:BLOCK_A, :] = zeros
        vpad[0:BLOCK_A, :] = zeros
        kpad[BLOCK_A:BLOCK_A + t, :] = _kv_per_query_head(k).astype(bf16)
        vpad[BLOCK_A:BLOCK_A + t, :] = _kv_per_query_head(kv[:, kvw:2 * kvw]).astype(bf16)
        kpad[BLOCK_A + t:2 * BLOCK_A + t, :] = zeros
        vpad[BLOCK_A + t:2 * BLOCK_A + t, :] = zeros
        kctx[...] = jnp.concatenate([kc_ref[0, h // G_A] for h in range(H_A)], axis=1).astype(bf16)
        vctx[...] = jnp.concatenate([vc_ref[0, h // G_A] for h in range(H_A)], axis=1).astype(bf16)

    rows = H_A * BLOCK_A
    rr = lax.broadcasted_iota(jnp.int32, (rows, 1), 0) % BLOCK_A
    cidx = lax.broadcasted_iota(jnp.int32, (rows, 3 * BLOCK_A), 1)
    sink = _sink_rows(sink_ref, layer, BLOCK_A)
    for u in range(ATTN_LAT_BLOCKS):
        blk = n * ATTN_LAT_BLOCKS + u
        row0 = pl.multiple_of(blk * BLOCK_A, BLOCK_A)
        q = _rope(q_ref[u * BLOCK_A:(u + 1) * BLOCK_A, :], tab_ref, pl.ds(row0, BLOCK_A), slice(None), HD_A // 2)
        q4 = _stack_heads(q * (HD_A ** -0.5 * LOG2E)).astype(bf16)
        kband = kpad[pl.ds(row0, 3 * BLOCK_A), :]
        vband = vpad[pl.ds(row0, 3 * BLOCK_A), :]
        s_loc = lax.dot_general(q4, kband, _NT, preferred_element_type=f32)
        s_ctx = lax.dot_general(q4, kctx[...], _NT, preferred_element_type=f32)
        lo = jnp.maximum(rr + (BLOCK_A - WINDOW_A), jnp.maximum(0, (1 - blk) * BLOCK_A))
        hi = jnp.minimum(rr + (BLOCK_A + WINDOW_A), jnp.minimum(3 * BLOCK_A, t - (blk - 1) * BLOCK_A) - 1)
        s_loc = jnp.where((cidx >= lo) & (cidx <= hi), s_loc, NEG_INF)
        (p_loc, p_ctx), inv = _softmax_parts([s_loc, s_ctx], sink)
        o = (_dot(p_loc.astype(bf16), vband) + _dot(p_ctx.astype(bf16), vctx[...])) * inv
        o_ref[u * BLOCK_A:(u + 1) * BLOCK_A, :] = _pick_heads(o, BLOCK_A).astype(o_ref.dtype)


def _attn_lat(pr, sink, kc, vc, tab, *, layer, batch, t):
    rows = ATTN_LAT_BLOCKS * BLOCK_A
    nb = t // rows
    lc = kc.shape[3]
    cspec = pl.BlockSpec((1, None, KV_A, lc, HD_A), lambda b, n: (b, layer, 0, 0, 0))
    return pl.pallas_call(
        functools.partial(_attn_lat_kernel, t=t, layer=layer),
        out_shape=jax.ShapeDtypeStruct((batch * t, 256), bf16),
        grid=(batch, nb),
        in_specs=[pl.BlockSpec(memory_space=pltpu.SMEM),
                  pl.BlockSpec((rows, 256), lambda b, n: (b * nb + n, 0)),
                  pl.BlockSpec((t, 256), lambda b, n: (b, 1)),
                  cspec, cspec,
                  pl.BlockSpec((3, t, 256), lambda b, n: (0, 0, 0))],
        out_specs=pl.BlockSpec((rows, 256), lambda b, n: (b * nb + n, 0)),
        scratch_shapes=[pltpu.VMEM((t + 2 * BLOCK_A, H_A * HD_A), bf16),
                        pltpu.VMEM((t + 2 * BLOCK_A, H_A * HD_A), bf16),
                        pltpu.VMEM((lc, H_A * HD_A), bf16), pltpu.VMEM((lc, H_A * HD_A), bf16)],
        compiler_params=_params("parallel", "arbitrary"),
        name="attn_lat",
    )(sink, pr, pr, kc, vc, tab)


def _reduce_rows(x, reduce_fn):
    r, c = x.shape
    g = ROW_REDUCE_CHAINS
    if r % (g * SUBLANES):
        return reduce_fn(x, axis=0, keepdims=True)
    part = reduce_fn(x.reshape(g, r // (g * SUBLANES), SUBLANES, c), axis=1)
    return reduce_fn(reduce_fn(part, axis=0), axis=0, keepdims=True)


def _mla_kernel(*refs, t, lc, tq, rope, emit_cache, n_prev):
    it = iter(refs)
    cq_ref, ckv_ref, kr_ref = next(it), next(it), next(it)
    cc_ref = kc_ref = tab_ref = None
    if lc:
        cc_ref, kc_ref = next(it), next(it)
    if rope:
        tab_ref = next(it)
    qn_ref, kvn_ref, wq_ref, wk_ref, wv_ref = next(it), next(it), next(it), next(it), next(it)
    ckvp_ref, krp_ref = (next(it), next(it)) if n_prev else (None, None)
    o_ref = next(it)
    ckv_out_ref, kr_out_ref = (next(it), next(it)) if emit_cache else (None, None)
    kbuf, vbuf = next(it), next(it)
    i = pl.program_id(1)
    hw = LANES
    va = V_C + MLA_ONES

    @pl.when(i == 0)
    def _():
        ckv = _rms(ckv_ref[...], kvn_ref[...])
        kr = kr_ref[...]
        if emit_cache:
            if n_prev:
                ckv_out_ref[0, 0:n_prev] = ckvp_ref[0]
                kr_out_ref[0, 0:n_prev] = krp_ref[0]
            ckv_out_ref[0, n_prev] = ckv
            kr_out_ref[0, n_prev] = kr[:, NOPE_C:NOPE_C + ROPE_C]
        ckv_b = ckv.astype(bf16)
        if rope:
            kr = _rope(kr, tab_ref, slice(None), slice(None), ROPE_C // 2)
        kk = _dot(ckv_b, wk_ref[...])
        for h in range(H_C):
            kbuf[h, 0:t, :] = (kk[:, hw * h:hw * (h + 1)] + kr).astype(bf16)
        def put_values(vt, cols):
            for h in range(H_C):
                vbuf[va * h:va * h + V_C, cols] = vt[V_C * h:V_C * (h + 1)]

        for h in range(H_C):
            vbuf[va * h + V_C:va * (h + 1), :] = jnp.ones((MLA_ONES, t + lc), bf16)
        put_values(lax.dot_general(wv_ref[...], ckv_b, _NT, preferred_element_type=f32).astype(bf16), slice(0, t))
        if lc:
            cc_b = cc_ref[0].astype(bf16)
            kkc = _dot(cc_b, wk_ref[...])
            krc = kc_ref[0]
            for h in range(H_C):
                kbuf[h, t:t + lc, :] = (kkc[:, hw * h:hw * (h + 1)] + krc).astype(bf16)
            put_values(lax.dot_general(wv_ref[...], cc_b, _NT, preferred_element_type=f32).astype(bf16),
                       slice(t, t + lc))

    cqn = _rms(cq_ref[...], qn_ref[...]).astype(bf16)
    qf = _dot(cqn, wq_ref[...])
    c2 = (NOPE_C + ROPE_C) ** -0.5 * LOG2E
    row0 = pl.multiple_of(i * tq, tq)
    n_keys = t + lc
    half = n_keys // 2

    def scores(h):
        qh = qf[:, hw * h:hw * (h + 1)]
        if rope:
            qh = _rope(qh, tab_ref, pl.ds(row0, tq), slice(None), ROPE_C // 2)
        qb = (qh * c2).astype(bf16)
        return [lax.dot_general(kbuf[h, c * half:(c + 1) * half, :], qb, _NT, preferred_element_type=f32)
                for c in range(2)]

    def probs(s):
        m = jnp.maximum(_reduce_rows(s[0], jnp.max), _reduce_rows(s[1], jnp.max))
        return [jnp.exp2(x - m).astype(bf16) for x in s]

    def values(h, p):
        rows = slice(va * h, va * (h + 1))
        o = _dot(vbuf[rows, 0:half], p[0]) + _dot(vbuf[rows, half:2 * half], p[1])
        return o[0:V_C] * (1.0 / o[V_C:V_C + 1])

    s_next = scores(0)
    outs = []
    for h in range(H_C):
        s_cur = s_next
        if h + 1 < H_C:
            s_next = scores(h + 1)
        outs.append(values(h, probs(s_cur)))
    o_ref[...] = jnp.concatenate(outs, axis=0).T.astype(o_ref.dtype)


def _mla(pr, qn, kvn, wq, wk, wv, *, layer, batch, t, tq, ctx=None, tab=None, emit_cache=False, cache_prev=None):
    nq = t // tq
    lc = 0 if ctx is None else ctx[0].shape[2]
    n_prev = 0 if cache_prev is None else cache_prev[0].shape[1]
    in_specs = [pl.BlockSpec((tq, 256), lambda b, i: (b * nq + i, 2)),
                pl.BlockSpec((t, LANES), lambda b, i: (b, 6)),
                pl.BlockSpec((t, LANES), lambda b, i: (b, 7))]
    args = [pr, pr, pr]
    if lc:
        in_specs += [pl.BlockSpec((1, None, lc, LANES), lambda b, i: (b, layer, 0, 0))] * 2
        args += list(ctx)
    if tab is not None:
        in_specs.append(pl.BlockSpec((3, t, LANES), lambda b, i: (0, 0, 0)))
        args.append(tab)
    in_specs += [_const_spec(a.shape[1:], layer) for a in (qn, kvn, wq, wk, wv)]
    args += [qn, kvn, wq, wk, wv]

    def cspec(n_layers, width):
        return pl.BlockSpec((1, n_layers, t, width), lambda b, i: (b, 0, 0, 0))

    if n_prev:
        in_specs += [cspec(n_prev, KV_LORA_C), cspec(n_prev, ROPE_C)]
        args += list(cache_prev)
    out_shape = [jax.ShapeDtypeStruct((batch * t, 256), bf16)]
    out_specs = [pl.BlockSpec((tq, 256), lambda b, i: (b * nq + i, 0))]
    if emit_cache:
        out_shape += [jax.ShapeDtypeStruct((batch, n_prev + 1, t, KV_LORA_C), f32),
                      jax.ShapeDtypeStruct((batch, n_prev + 1, t, ROPE_C), f32)]
        out_specs += [cspec(n_prev + 1, KV_LORA_C), cspec(n_prev + 1, ROPE_C)]
    return pl.pallas_call(
        functools.partial(_mla_kernel, t=t, lc=lc, tq=tq, rope=tab is not None, emit_cache=emit_cache,
                          n_prev=n_prev),
        out_shape=tuple(out_shape),
        grid=(batch, nq),
        in_specs=in_specs,
        out_specs=tuple(out_specs),
        scratch_shapes=[pltpu.VMEM((H_C, t + lc, LANES), bf16),
                        pltpu.VMEM((H_C * (V_C + MLA_ONES), t + lc), bf16)],
        compiler_params=_params("parallel", "arbitrary"),
        name="mla",
    )(*args)


def _pool_counts(t):
    pos = np.arange(t)[:, None]
    half = np.repeat(np.asarray(POOL_WINDOWS) // 2, POOL_CH)[None, :]
    return (np.minimum(pos + half, t) - np.maximum(pos - half, 0)).astype(np.float32)


def _pool_kernel(x_ref, cnt_ref, w_ref, sc_ref, o_ref, buf_a, buf_b, buf_c, *, t, seqs):
    for u in range(seqs):
        _pool_sequence(x_ref, cnt_ref, w_ref, sc_ref, o_ref, buf_a, buf_b, buf_c, slice(u * t, (u + 1) * t), t)


def _pool_sequence(x_ref, cnt_ref, w_ref, sc_ref, o_ref, buf_a, buf_b, buf_c, seq, t):
    x = x_ref[seq, :]
    pad, edge = POOL_PAD, SUBLANES
    n = t + 2 * pad
    m = n - 2 * edge
    zeros = jnp.zeros((pad, 256), f32)
    buf_a[0:pad, :] = zeros
    buf_a[pad:pad + t, :] = x
    buf_a[pad + t:n, :] = zeros
    for buf in (buf_b, buf_c):
        buf[0:edge, :] = zeros[0:edge]
        buf[n - edge:n, :] = zeros[0:edge]
    inner = slice(edge, edge + m)
    buf_b[inner, :] = buf_a[pl.ds(edge - 1, m), :] + buf_a[pl.ds(edge, m), :]
    buf_c[inner, :] = buf_b[pl.ds(edge - 1, m), :] + buf_b[pl.ds(edge + 1, m), :]
    lo, hi = slice(0, LANES), slice(LANES, 2 * LANES)
    buf_a[inner, hi] = buf_c[pl.ds(edge - 2, m), hi] + buf_c[pl.ds(edge + 2, m), hi]
    s16 = buf_a[pl.ds(pad - 4, t), hi] + buf_a[pl.ds(pad + 4, t), hi]
    first = lax.broadcasted_iota(jnp.int32, (1, LANES), 1) < POOL_CH
    rows = slice(pad, pad + t)
    win_sum = jnp.concatenate([jnp.where(first, buf_b[rows, lo], buf_c[rows, lo]),
                               jnp.where(first, buf_a[rows, hi], s16)], axis=1)
    mean = win_sum / cnt_ref[...]
    o_ref[seq, :] = (_dot((mean - x).astype(bf16), w_ref[...]) * sc_ref[...]).astype(o_ref.dtype)


def _pool(pr, w_bd, scale, *, layer, batch, t, seqs=1):
    return pl.pallas_call(
        functools.partial(_pool_kernel, t=t, seqs=seqs),
        out_shape=jax.ShapeDtypeStruct((batch * t, 256), bf16),
        grid=(batch // seqs,),
        in_specs=[pl.BlockSpec((seqs * t, 256), lambda b: (b, 4)), _const_spec((t, 256)),
                  _const_spec((256, 256), layer), _const_spec((1, 256), layer)],
        out_specs=pl.BlockSpec((seqs * t, 256), lambda b: (b, 0)),
        scratch_shapes=[pltpu.VMEM((t + 2 * POOL_PAD, 256), f32)] * 3,
        compiler_params=_params("parallel"),
        name="pool",
    )(pr, jnp.asarray(_pool_counts(t)), w_bd, scale)


def _hgrn_mats():
    n = HG_TILE
    s = np.arange(n)[:, None]
    t = np.arange(n)[None, :]
    same_half = s // HG_HALF == t // HG_HALF
    mid = (t // HG_HALF) * HG_HALF + HG_HALF // 2 - 1
    rel_mid = (same_half & (s > mid) & (s <= t)).astype(np.float32) - (same_half & (s <= mid) & (s > t))
    mats = [rel_mid, same_half & (s <= t), s <= t, s > t, (s // HG_BLOCK == t // HG_BLOCK) & (s <= t)]
    for m in HG_LEVELS:
        same_pair = s // (2 * m) == t // (2 * m)
        right_t = (t // m) % 2 == 1
        right_s = (s // m) % 2 == 1
        q_side = same_pair & right_t & right_s & (s <= t)
        k_side = same_pair & ~right_t & ~right_s & (s > t)
        mats.append(q_side | k_side)
    fwd = np.concatenate([m.astype(np.float32) for m in mats], axis=1)
    bwd = np.concatenate([m[::-1, ::-1].astype(np.float32) for m in mats], axis=1)
    both = np.stack([fwd, bwd])
    return np.concatenate([both, both], axis=1)


def _hgrn_exact_intra(q, k, v, vb, hl, mats, reverse):
    n = HG_TILE
    e_all = _dot(hl, mats)
    beta = e_all[:, 0:n]
    lane = lax.broadcasted_iota(jnp.int32, (1, n), 1)
    pos = lane % HG_BLOCK
    v3 = v.reshape(H_B, DV_B, n)

    o3 = jnp.sum((q * k).reshape(H_B, DK_B, n), axis=1, keepdims=True) * v3
    for d in range(1, HG_BLOCK):
        shift = (n - d) if reverse else d
        valid = (pos <= HG_BLOCK - 1 - d) if reverse else (pos >= d)
        kr = pltpu.roll(k, shift, axis=1)
        br = pltpu.roll(beta, shift, axis=1)
        vr = pltpu.roll(v, shift, axis=1)
        e = jnp.exp(jnp.where(valid, beta - br, NEG_INF))
        sc = jnp.sum((q * kr * e).reshape(H_B, DK_B, n), axis=1, keepdims=True)
        o3 = o3 + sc * vr.reshape(H_B, DV_B, n)

    srow = lax.broadcasted_iota(jnp.int32, (n, n), 0)
    tcol = lax.broadcasted_iota(jnp.int32, (n, n), 1)
    scores = [jnp.zeros((n, n), f32) for _ in range(H_B)]
    for li, m in enumerate(HG_LEVELS):
        ex = jnp.exp(e_all[:, (li + 1) * n:(li + 2) * n])
        is_q = ((lane // m) % 2) == (0 if reverse else 1)
        qs = jnp.where(is_q, q * ex, 0.0).astype(bf16)
        ks = jnp.where(is_q, 0.0, k * ex).astype(bf16)
        pair = (srow // (2 * m)) == (tcol // (2 * m))
        for h in range(H_B):
            a = lax.dot_general(ks[DK_B * h:DK_B * (h + 1)], qs[DK_B * h:DK_B * (h + 1)], _TN,
                                preferred_element_type=f32)
            scores[h] = scores[h] + jnp.where(pair, a, 0.0)

    return [o3[h] + _dot(vb[DV_B * h:DV_B * (h + 1)], scores[h].astype(bf16)) for h in range(H_B)]


def _hgrn_common(q, v, fpre, lb, mats_ref, sidx, st_ref):
    n = HG_TILE
    f = lb + (1.0 - lb) * _sigmoid(fpre)
    k = 1.0 - f
    logf = jnp.log(f)
    hi = logf.astype(bf16)
    hl = jnp.concatenate([hi, (logf - hi.astype(f32)).astype(bf16)], axis=1)
    ex = _dot(hl, mats_ref[sidx, :, 0:4 * n])
    ones = jnp.ones((8, 2 * n), bf16)
    g_row = lax.dot_general(ones, hl, _NT, preferred_element_type=f32)[0:1]
    d_mid = ex[:, 0:n]
    small = jnp.max(jnp.abs(d_mid)) <= HG_MAX_EXP
    q_in = (q * jnp.exp(ex[:, 2 * n:3 * n])).astype(bf16)
    k_out = (k * jnp.exp(ex[:, 3 * n:4 * n])).astype(bf16)
    vb = v.astype(bf16)
    o_state = []
    for h in range(H_B):
        rows = slice(DK_B * h, DK_B * (h + 1))
        st = st_ref[sidx, rows, :]
        o_state.append(_dot(st.astype(bf16), q_in[rows]))
        upd = lax.dot_general(vb[rows], k_out[rows], _NT, preferred_element_type=f32)
        st_ref[sidx, rows, :] = st * jnp.exp(g_row[:, rows]) + upd
    return dict(q=q, k=k, v=v, vb=vb, hl=hl, ex=ex, o_state=o_state, small=small)


def _hgrn_fast_intra(c, reverse):
    n = HG_TILE
    q, k, ex = c["q"], c["k"], c["ex"]
    lane = lax.broadcasted_iota(jnp.int32, (1, n), 1)
    in0 = (lane // HG_HALF) == 0
    q_half = in0 if reverse else jnp.logical_not(in0)
    d_mid = ex[:, 0:n]
    qd = q * jnp.exp(d_mid)
    kd = k * jnp.exp(-d_mid)
    b_in = ex[:, n:2 * n]
    last = (0, HG_HALF) if reverse else (HG_HALF - 1, n - 1)
    total = jnp.where(in0, jnp.broadcast_to(b_in[:, last[0]:last[0] + 1], b_in.shape),
                      jnp.broadcast_to(b_in[:, last[1]:last[1] + 1], b_in.shape))
    ql = jnp.where(q_half, q * jnp.exp(b_in), 0.0)
    kl = jnp.where(q_half, 0.0, k * jnp.exp(total - b_in))
    parts_q = [jnp.where(in0, qd, 0.0).astype(bf16), jnp.where(in0, 0.0, qd).astype(bf16), ql.astype(bf16)]
    parts_k = [jnp.where(in0, kd, 0.0).astype(bf16), jnp.where(in0, 0.0, kd).astype(bf16), kl.astype(bf16)]
    srow = lax.broadcasted_iota(jnp.int32, (n, n), 0)
    tcol = lax.broadcasted_iota(jnp.int32, (n, n), 1)
    causal = (srow >= tcol) if reverse else (srow <= tcol)
    scores = []
    for h in range(H_B):
        rows = slice(DK_B * h, DK_B * (h + 1))
        k3 = jnp.concatenate([p[rows] for p in parts_k], axis=0)
        q3 = jnp.concatenate([p[rows] for p in parts_q], axis=0)
        scores.append(lax.dot_general(k3, q3, _TN, preferred_element_type=f32))
    scores = [jnp.where(causal, a, 0.0).astype(bf16) for a in scores]
    outs = [c["o_state"][h] + _dot(c["vb"][DV_B * h:DV_B * (h + 1)], scores[h]) for h in range(H_B)]
    return jnp.concatenate(outs, axis=0)


def _hgrn_exact(c, mats, reverse):
    intra = _hgrn_exact_intra(c["q"], c["k"], c["v"], c["vb"], c["hl"], mats, reverse)
    return jnp.concatenate([c["o_state"][h] + intra[h] for h in range(H_B)], axis=0)


def _hgrn_kernel(*refs, t, has_s0, emit_state, n_prev):
    it = iter(refs)
    q_ref, i_ref, ff_ref, fb_ref, g_ref, lb_ref, ng_ref, mats_ref = [next(it) for _ in range(8)]
    s0_ref = next(it) if has_s0 else None
    sprev_ref = next(it) if n_prev else None
    o_ref = next(it)
    sfin_ref = next(it) if emit_state else None
    st_ref, oacc_f, oacc_b = next(it), next(it), next(it)
    n = HG_TILE
    nt = t // n
    for d in range(2):
        for h in range(H_B):
            rows = slice(DV_B * h, DV_B * (h + 1))
            st_ref[d, rows, :] = s0_ref[0, d, h].T if has_s0 else jnp.zeros((DV_B, DK_B), f32)

    def finish_tile(idx):
        off = pl.multiple_of(idx * n, n)
        cols = pl.ds(off, n)
        o = oacc_f[:, cols] + oacc_b[:, cols]
        o3 = o.reshape(H_B, DV_B, n)
        y = o3 * lax.rsqrt(jnp.mean(o3 * o3, axis=1, keepdims=True) + EPS)
        gate = g_ref[:, cols]
        y = y.reshape(H_B * DV_B, n) * ng_ref[...] * (gate * _sigmoid(gate))
        o_ref[pl.ds(off, n), :] = y.T.astype(o_ref.dtype)

    def tiles_of(i):
        js = [i * HG_UNROLL + u for u in range(HG_UNROLL)]
        return js + [nt - 1 - j for j in js]

    def scan(i, c, finish_prev=False):
        if finish_prev:
            for t_idx in tiles_of(i - 1):
                finish_tile(t_idx)
        work = []
        small = None
        for u in range(HG_UNROLL):
            j = i * HG_UNROLL + u
            cols_f = pl.ds(pl.multiple_of(j * n, n), n)
            cols_b = pl.ds(pl.multiple_of((nt - 1 - j) * n, n), n)
            cf = _hgrn_common(q_ref[:, cols_f], i_ref[:, cols_f], ff_ref[:, cols_f], lb_ref[0], mats_ref, 0, st_ref)
            cb = _hgrn_common(q_ref[:, cols_b], i_ref[:, cols_b], fb_ref[:, cols_b], lb_ref[1], mats_ref, 1, st_ref)
            work.append((cols_f, cf, cols_b, cb))
            both = jnp.logical_and(cf["small"], cb["small"])
            small = both if small is None else jnp.logical_and(small, both)
        for cols_f, cf, cols_b, cb in work:
            oacc_f[:, cols_f] = _hgrn_fast_intra(cf, False)
            oacc_b[:, cols_b] = _hgrn_fast_intra(cb, True)

        @pl.when(jnp.logical_not(small))
        def _():
            for cols_f, cf, cols_b, cb in work:
                oacc_f[:, cols_f] = _hgrn_exact(cf, mats_ref[0, :, 4 * n:8 * n], False)
                oacc_b[:, cols_b] = _hgrn_exact(cb, mats_ref[1, :, 4 * n:8 * n], True)

        return c

    n_it = nt // HG_UNROLL
    if n_it % 2 == 0:
        lax.fori_loop(0, n_it // 2 + 1, scan, 0)
        lax.fori_loop(n_it // 2 + 1, n_it, functools.partial(scan, finish_prev=True), 0)
        for t_idx in tiles_of(n_it - 1):
            finish_tile(t_idx)
    else:
        lax.fori_loop(0, n_it, scan, 0)
        lax.fori_loop(0, nt, lambda idx, c: (finish_tile(idx), c)[1], 0)
    if emit_state:
        if n_prev:
            sfin_ref[0, 0:n_prev] = sprev_ref[0]
        for d in range(2):
            for h in range(H_B):
                sfin_ref[0, n_prev, d, h] = st_ref[d, DV_B * h:DV_B * (h + 1), :].T


def _hgrn(pt, lb, ng, mats, *, batch, t, s0=None, layer=0, emit_state=False, state_prev=None):
    def row(r):
        return pl.BlockSpec((256, t), lambda b: (r, b))

    def sspec(n_layers):
        return pl.BlockSpec((1, n_layers, 2, H_B, DK_B, DV_B), lambda b: (b, 0, 0, 0, 0, 0))

    n_prev = 0 if state_prev is None else state_prev.shape[1]
    in_specs = [row(0), row(1), row(2), row(3), row(4),
                _const_spec((2, 256, LANES), layer), _const_spec((256, LANES), layer), _const_spec(mats.shape)]
    args = [pt, pt, pt, pt, pt, lb, ng, mats]
    if s0 is not None:
        in_specs.append(pl.BlockSpec((1, None, 2, H_B, DK_B, DV_B), lambda b: (b, layer, 0, 0, 0, 0)))
        args.append(s0)
    if n_prev:
        in_specs.append(sspec(n_prev))
        args.append(state_prev)
    out_shape = [jax.ShapeDtypeStruct((batch * t, 256), bf16)]
    out_specs = [pl.BlockSpec((t, 256), lambda b: (b, 0))]
    if emit_state:
        out_shape.append(jax.ShapeDtypeStruct((batch, n_prev + 1, 2, H_B, DK_B, DV_B), f32))
        out_specs.append(sspec(n_prev + 1))
    return pl.pallas_call(
        functools.partial(_hgrn_kernel, t=t, has_s0=s0 is not None, emit_state=emit_state, n_prev=n_prev),
        out_shape=tuple(out_shape),
        grid=(batch,),
        in_specs=in_specs,
        out_specs=tuple(out_specs),
        scratch_shapes=[pltpu.VMEM((2, H_B * DV_B, DK_B), f32), pltpu.VMEM((256, t), f32),
                        pltpu.VMEM((256, t), f32)],
        compiler_params=_params("parallel"),
        name="hgrn",
    )(*args)


def _rope_tables(t):
    def cos_sin(dim):
        n_freq = dim // 4
        inv = ROPE_THETA ** (-jnp.arange(n_freq, dtype=f32) / n_freq)
        rows = t // GRID_W
        row_id = jnp.repeat(jnp.arange(rows, dtype=f32), GRID_W)
        col_id = jnp.tile(jnp.arange(GRID_W, dtype=f32), rows)
        ang = jnp.concatenate([row_id[:, None] * inv, col_id[:, None] * inv], axis=-1)
        return jnp.cos(ang), jnp.sin(ang)

    cos, sin = cos_sin(HD_A)
    zero = jnp.zeros_like(sin)
    tab_a = jnp.stack([jnp.tile(jnp.concatenate([cos, cos], -1), (1, H_A)),
                       jnp.tile(jnp.concatenate([zero, sin], -1), (1, H_A)),
                       jnp.tile(jnp.concatenate([-sin, zero], -1), (1, H_A))])
    cos, sin = cos_sin(ROPE_C)
    zero = jnp.zeros_like(sin)
    one64 = jnp.ones((t, NOPE_C), f32)
    zero64 = jnp.zeros((t, NOPE_C), f32)
    pad32 = jnp.zeros((t, LANES - NOPE_C - ROPE_C), f32)
    tab_c = jnp.stack([jnp.concatenate([one64, cos, cos, pad32 + 1.0], -1),
                       jnp.concatenate([zero64, zero, sin, pad32], -1),
                       jnp.concatenate([zero64, -sin, zero, pad32], -1)])
    return tab_a, tab_c


def _pad_rope_cols(x):
    pads = [(0, 0)] * (x.ndim - 1) + [(NOPE_C, LANES - NOPE_C - ROPE_C)]
    return jnp.pad(x, pads)


def kernel(x_prompt, x_sample, c, c_ctx, cache_attn_k, cache_attn_v, state_hgrn, cache_mla_ckv, cache_mla_krope,
           w_ada, b_ada, norm_sub, w_ffn_gate, w_ffn_up, w_ffn_down, w_in, w_out, attn_sink, hgrn_lb_logits,
           hgrn_out_norm, mla_q_norm, mla_kv_norm, mla_w_uq, mla_w_ukv, pool_w, pool_scale, final_norm):
    bp, tp, _ = x_prompt.shape
    bs, ts, _ = x_sample.shape
    np_, ns = bp * tp, bs * ts
    xp = x_prompt.reshape(np_, D_MODEL)
    xs = x_sample.reshape(ns, D_MODEL)

    n_vec = 1 + bs
    cvec = jnp.zeros((-(-n_vec // 8) * 8, D_MODEL), f32).at[0].set(c_ctx).at[1:n_vec].set(c)
    mods = _modulation(cvec, w_ada, b_ada).reshape(DEPTH, cvec.shape[0], N_MOD, D_MODEL)

    p_lb = jax.nn.softmax(hgrn_lb_logits.astype(f32), axis=0)
    lb_all = jnp.cumsum(p_lb, axis=0) - p_lb[0:1]
    tab_a, tab_c = _rope_tables(ts)
    mats = jnp.asarray(_hgrn_mats(), bf16)
    ffn_f32 = (w_ffn_gate, w_ffn_up, w_ffn_down)
    ffn_w = {(0, 0): tuple(_to_bf16(ffn_f32, 0, 0))}
    to_cast = [(l, idx) for l in range(DEPTH) for idx in range(2)][1:]
    grp_p = dict(base=0, rpm=np_)
    grp_s = dict(base=1, rpm=ts)
    tm, tm_mix = 512, 1024

    wb = lambda w: w.astype(bf16)
    w_row = wb(jnp.concatenate([w_in[:, :, 0:512], w_in[:, :, 1792:2176], _pad_rope_cols(w_in[:, :, 2176:2208]),
                                w_in[:, :, 2208:2464]], axis=2))
    w_t = wb(jnp.swapaxes(w_in[:, :, 512:1792], 1, 2))
    wq = wb(jnp.pad(mla_w_uq.reshape(DEPTH, Q_LORA_C, H_C, NOPE_C + ROPE_C),
                    ((0, 0), (0, 0), (0, 0), (0, LANES - NOPE_C - ROPE_C))).reshape(DEPTH, Q_LORA_C, H_C * LANES))
    wkv = mla_w_ukv.reshape(DEPTH, KV_LORA_C, H_C, NOPE_C + V_C)
    wk = wb(jnp.pad(wkv[..., :NOPE_C], ((0, 0), (0, 0), (0, 0), (0, LANES - NOPE_C)))
            .reshape(DEPTH, KV_LORA_C, H_C * LANES))
    wv = wb(jnp.swapaxes(wkv[..., NOPE_C:].reshape(DEPTH, KV_LORA_C, H_C * V_C), 1, 2))
    n_grp = len(POOL_WINDOWS)
    w_pool = wb(jnp.einsum('lgij,gh->lgihj', pool_w, jnp.eye(n_grp, dtype=f32))
                .reshape(DEPTH, n_grp * POOL_CH, n_grp * POOL_CH))
    lb = jnp.broadcast_to(lb_all[..., None], (DEPTH, 2, 256, LANES))
    ng = jnp.broadcast_to(jnp.tile(hgrn_out_norm, (1, H_B))[..., None], (DEPTH, 256, LANES))
    sink = attn_sink.astype(f32)
    qn = mla_q_norm.reshape(DEPTH, 1, Q_LORA_C)
    kvn = mla_kv_norm.reshape(DEPTH, 1, KV_LORA_C)
    pscale = pool_scale.reshape(DEPTH, 1, 256)
    wo = wb(w_out)
    norms = norm_sub.reshape(DEPTH * 3, 1, D_MODEL)
    ctx_c = (cache_mla_ckv, _pad_rope_cols(cache_mla_krope))

    new_k = new_v = new_st = new_ckv = new_kr = None
    for l in range(DEPTH):
        def ffn(x, grp, j, idx, mix=None, final_g=None):
            side = None
            if to_cast:
                nxt = to_cast.pop(0)
                side = nxt + (ffn_f32,)
            res = _ffn(x, mods, norms, ffn_w[(l, idx)], layer=l, j=j, tm=tm, mix=mix, final_g=final_g, side=side,
                       **grp)
            if side is not None:
                ffn_w[nxt] = tuple(res[1:])
            return res[0]

        xp = ffn(xp, grp_p, 0, 0)
        xs = ffn(xs, grp_s, 0, 0)

        pr, pt = _mixer_in(xp, mods, norms, w_row, w_t, layer=l, tm=tm_mix, **grp_p)
        oa, new_k, new_v = _attn_ctx(pr, sink, None if l == 0 else (new_k, new_v), layer=l, batch=bp, t=tp)
        ob, new_st = _hgrn(pt, lb, ng, mats, layer=l, batch=bp, t=tp, emit_state=True, state_prev=new_st)
        oc, new_ckv, new_kr = _mla(pr, qn, kvn, wq, wk, wv, layer=l, batch=bp, t=tp, tq=tp, emit_cache=True,
                                   cache_prev=None if l == 0 else (new_ckv, new_kr))
        od = _pool(pr, w_pool, pscale, layer=l, batch=bp, t=tp, seqs=POOL_CTX_SEQS)
        mix_p = (oa, ob, oc, od, wo)

        pr, pt = _mixer_in(xs, mods, norms, w_row, w_t, layer=l, tm=tm_mix, **grp_s)
        oa = _attn_lat(pr, sink, cache_attn_k, cache_attn_v, tab_a, layer=l, batch=bs, t=ts)
        (ob,) = _hgrn(pt, lb, ng, mats, layer=l, batch=bs, t=ts, s0=state_hgrn)
        (oc,) = _mla(pr, qn, kvn, wq, wk, wv, layer=l, batch=bs, t=ts, tq=512, ctx=ctx_c, tab=tab_c)
        od = _pool(pr, w_pool, pscale, layer=l, batch=bs, t=ts)
        mix_s = (oa, ob, oc, od, wo)

        fin = final_norm.reshape(1, D_MODEL) if l == DEPTH - 1 else None
        xp = ffn(xp, grp_p, 2, 1, mix_p, fin)
        xs = ffn(xs, grp_s, 2, 1, mix_s, fin)

    return (xp.reshape(bp, tp, D_MODEL), xs.reshape(bs, ts, D_MODEL), new_k, new_v, new_st, new_ckv, new_kr)
```

```python
import functools

import numpy as np
import jax
import jax.numpy as jnp
from jax import lax
from jax.experimental import pallas as pl
from jax.experimental.pallas import tpu as pltpu

f32 = jnp.float32
bf16 = jnp.bfloat16

D_MODEL = 1024
DEPTH = 2
GRID_W = 64
EPS = 1e-6
LOG2E = 1.4426950408889634
ROPE_THETA = 10000.0
NEG_INF = -1e30
N_MOD = 9
D_FF = 2816
H_A, KV_A, HD_A = 4, 2, 64
G_A = H_A // KV_A
WINDOW_A = 128
BLOCK_A = 128
H_B, DK_B, DV_B = 4, 64, 64
H_C, Q_LORA_C, KV_LORA_C, NOPE_C, ROPE_C, V_C = 4, 256, 128, 64, 32, 64
POOL_WINDOWS = (2, 4, 8, 16)
POOL_CH = 64

LANES = 128
SUBLANES = 8
CAST_BLOCKS = 16
ROW_REDUCE_CHAINS = 8
MLA_ONES = 2 * SUBLANES
VMEM_LIMIT = 56 * 1024 * 1024
HG_TILE = LANES
HG_UNROLL = 2
HG_HALF = HG_TILE // 2
HG_MAX_EXP = 80.0
HG_BLOCK = 16
HG_LEVELS = (16, 32, 64)
ATTN_CTX_SEQS = 2
POOL_CTX_SEQS = 8
ATTN_LAT_BLOCKS = 8
POOL_PAD = 16

PR_W = 1280
PT_W = 1280

_NT = (((1,), (1,)), ((), ()))
_TN = (((0,), (0,)), ((), ()))


def _params(*sem):
    return pltpu.CompilerParams(dimension_semantics=sem, vmem_limit_bytes=VMEM_LIMIT)


def _const_spec(shape, lead=None):
    zeros = (0,) * len(shape)
    if lead is None:
        return pl.BlockSpec(shape, lambda *_: zeros, pipeline_mode=pl.Buffered(1))
    return pl.BlockSpec((None,) + tuple(shape), lambda *_: (lead,) + zeros, pipeline_mode=pl.Buffered(1))


def _dot(a, b):
    return jnp.dot(a, b, preferred_element_type=f32)


def _sigmoid(x):
    return 1.0 / (1.0 + jnp.exp(-x))


def _rms(x, g):
    return x * lax.rsqrt(jnp.mean(x * x, axis=-1, keepdims=True) + EPS) * g


def _mod_kernel(c_ref, w_ref, b_ref, o_ref):
    c = c_ref[...]
    a = (c * _sigmoid(c)).astype(bf16)
    o_ref[0] = _dot(a, w_ref[0].astype(bf16)) + b_ref[0]


def _modulation(cvec, w_ada, b_ada):
    tn = 1536
    n_out = N_MOD * D_MODEL
    return pl.pallas_call(
        _mod_kernel,
        out_shape=jax.ShapeDtypeStruct((DEPTH, cvec.shape[0], n_out), f32),
        grid=(DEPTH, n_out // tn),
        in_specs=[pl.BlockSpec(cvec.shape, lambda l, j: (0, 0)),
                  pl.BlockSpec((1, D_MODEL, tn), lambda l, j: (l, 0, j)),
                  pl.BlockSpec((1, 1, tn), lambda l, j: (l, 0, j))],
        out_specs=pl.BlockSpec((1, cvec.shape[0], tn), lambda l, j: (l, 0, j)),
        compiler_params=_params("arbitrary", "arbitrary"),
        name="modulation",
    )(cvec, w_ada, b_ada.reshape(DEPTH, 1, n_out))


def _cast_kernel(*refs):
    n = len(refs) // 2
    for src, dst in zip(refs[:n], refs[n:]):
        dst[...] = src[...].astype(bf16)


def _to_bf16(ws, layer, idx):
    def blk(w):
        return w.shape[2] // CAST_BLOCKS, w.shape[3]

    return pl.pallas_call(
        _cast_kernel,
        out_shape=tuple(jax.ShapeDtypeStruct(w.shape[2:], bf16) for w in ws),
        grid=(CAST_BLOCKS,),
        in_specs=[pl.BlockSpec((None, None) + blk(w), lambda i: (layer, idx, i, 0)) for w in ws],
        out_specs=tuple(pl.BlockSpec(blk(w), lambda i: (i, 0)) for w in ws),
        compiler_params=_params("parallel"),
        name="cast_bf16",
    )(*ws)


def _ffn_kernel(*refs, j, mix, final, side):
    it = iter(refs)
    x_ref, m_ref = next(it), next(it)
    mix_refs = [next(it) for _ in range(5)] if mix else None
    g_ref, wg_ref, wu_ref, wd_ref = next(it), next(it), next(it), next(it)
    fg_ref = next(it) if final else None
    side_in = [next(it) for _ in range(3)] if side else []
    o_ref = next(it)
    for src, dst in zip(side_in, [next(it) for _ in range(len(side_in))]):
        dst[...] = src[...].astype(bf16)
    x = x_ref[...]
    m = m_ref[0]
    if mix:
        wo_ref = mix_refs[4]
        acc = _dot(mix_refs[0][...], wo_ref[0:256, :])
        for p in range(1, 4):
            acc += _dot(mix_refs[p][...], wo_ref[256 * p:256 * (p + 1), :])
        x = x + m[5:6] * acc
    h = (_rms(x, g_ref[...]) * (1.0 + m[3 * j + 1:3 * j + 2]) + m[3 * j:3 * j + 1]).astype(bf16)
    gate = _dot(h, wg_ref[...])
    up = _dot(h, wu_ref[...])
    act = (gate * _sigmoid(gate) * up).astype(bf16)
    y = x + (0.5 * m[3 * j + 2:3 * j + 3]) * _dot(act, wd_ref[...])
    if final:
        y = _rms(y, fg_ref[...])
    o_ref[...] = y


def _mod_spec(layer, base, rpm, tm):
    return pl.BlockSpec((None, 1, N_MOD, D_MODEL), lambda i: (layer, base + (i * tm) // rpm, 0, 0))


def _ffn(x, mods, norms, w3, *, layer, j, base, rpm, tm, mix=None, final_g=None, side=None):
    n = x.shape[0]
    steps = n // tm
    in_specs = [pl.BlockSpec((tm, D_MODEL), lambda i: (i, 0)), _mod_spec(layer, base, rpm, tm)]
    args = [x, mods]
    if mix is not None:
        in_specs += [pl.BlockSpec((tm, 256), lambda i: (i, 0))] * 4 + [_const_spec((D_MODEL, D_MODEL), layer)]
        args += list(mix)
    in_specs += [_const_spec((1, D_MODEL), 3 * layer + j)] + [_const_spec(w.shape) for w in w3]
    args += [norms] + list(w3)
    if final_g is not None:
        in_specs.append(_const_spec((1, D_MODEL)))
        args.append(final_g)
    out_shape = [jax.ShapeDtypeStruct((n, D_MODEL), f32)]
    out_specs = [pl.BlockSpec((tm, D_MODEL), lambda i: (i, 0))]
    if side is not None:
        sl, sidx, arrays = side
        nblk = CAST_BLOCKS if steps % CAST_BLOCKS == 0 else steps
        for a in arrays:
            assert a.shape[2] % (nblk * 2 * SUBLANES) == 0, (a.shape, nblk)
            r, c = a.shape[2] // nblk, a.shape[3]
            in_specs.append(pl.BlockSpec((None, None, r, c), lambda i: (sl, sidx, (i * nblk) // steps, 0)))
            args.append(a)
            out_shape.append(jax.ShapeDtypeStruct(a.shape[2:], bf16))
            out_specs.append(pl.BlockSpec((r, c), lambda i: ((i * nblk) // steps, 0)))
    return pl.pallas_call(
        functools.partial(_ffn_kernel, j=j, mix=mix is not None, final=final_g is not None, side=side is not None),
        out_shape=tuple(out_shape),
        grid=(steps,),
        in_specs=in_specs,
        out_specs=tuple(out_specs),
        compiler_params=_params("arbitrary" if side is not None else "parallel"),
        name="ffn",
    )(*args)


def _mixin_kernel(x_ref, m_ref, g_ref, wr_ref, wt_ref, pr_ref, pt_ref):
    m = m_ref[0]
    h = (_rms(x_ref[...], g_ref[...]) * (1.0 + m[4:5]) + m[3:4]).astype(bf16)
    pr_ref[...] = _dot(h, wr_ref[...])
    pt_ref[...] = lax.dot_general(wt_ref[...], h, _NT, preferred_element_type=f32)


def _mixer_in(x, mods, norms, w_row, w_t, *, layer, base, rpm, tm):
    n = x.shape[0]
    return pl.pallas_call(
        _mixin_kernel,
        out_shape=(jax.ShapeDtypeStruct((n, PR_W), f32), jax.ShapeDtypeStruct((PT_W, n), f32)),
        grid=(n // tm,),
        in_specs=[pl.BlockSpec((tm, D_MODEL), lambda i: (i, 0)), _mod_spec(layer, base, rpm, tm),
                  _const_spec((1, D_MODEL), 3 * layer + 1), _const_spec((D_MODEL, PR_W), layer),
                  _const_spec((PT_W, D_MODEL), layer)],
        out_specs=(pl.BlockSpec((tm, PR_W), lambda i: (i, 0)), pl.BlockSpec((PT_W, tm), lambda i: (0, i))),
        compiler_params=_params("parallel"),
        name="mixer_in",
    )(x, mods, norms, w_row, w_t)


def _rope(x, tab_ref, rows, lanes, shift):
    width = x.shape[-1]
    c = tab_ref[0, rows, lanes]
    sa = tab_ref[1, rows, lanes]
    sb = tab_ref[2, rows, lanes]
    return x * c + pltpu.roll(x, shift, axis=1) * sa + pltpu.roll(x, width - shift, axis=1) * sb


def _softmax_parts(s_list, sink):
    sink2 = sink * LOG2E
    mx = functools.reduce(jnp.maximum, [jnp.max(s, axis=-1, keepdims=True) for s in s_list])
    mx = jnp.maximum(mx, sink2)
    ps = [jnp.exp2(s - mx) for s in s_list]
    den = functools.reduce(lambda a, b: a + b, [jnp.sum(p, axis=-1, keepdims=True) for p in ps])
    return ps, 1.0 / (den + jnp.exp2(sink2 - mx))


def _kv_per_query_head(x):
    return jnp.concatenate([x[:, HD_A * (h // G_A):HD_A * (h // G_A + 1)] for h in range(H_A)], axis=1)


def _stack_heads(q):
    head = lax.broadcasted_iota(jnp.int32, (1, H_A * HD_A), 1) // HD_A
    return jnp.concatenate([jnp.where(head == h, q, 0.0) for h in range(H_A)], axis=0)


def _pick_heads(o, r):
    head = lax.broadcasted_iota(jnp.int32, (1, H_A * HD_A), 1) // HD_A
    out = o[0:r]
    for h in range(1, H_A):
        out = jnp.where(head == h, o[h * r:(h + 1) * r], out)
    return out


def _sink_rows(sink_ref, layer, r):
    row = lax.broadcasted_iota(jnp.int32, (H_A * r, 1), 0)
    out = jnp.full((H_A * r, 1), sink_ref[layer, 0], f32)
    for h in range(1, H_A):
        out = jnp.where(row >= h * r, sink_ref[layer, h], out)
    return out


def _attn_ctx_kernel(*refs, t, n_prev, layer):
    if n_prev:
        sink_ref, q_ref, kv_ref, kp_ref, vp_ref, o_ref, ko_ref, vo_ref = refs
        ko_ref[:, 0:n_prev] = kp_ref[...]
        vo_ref[:, 0:n_prev] = vp_ref[...]
    else:
        sink_ref, q_ref, kv_ref, o_ref, ko_ref, vo_ref = refs
    kvw = KV_A * HD_A
    sink = _sink_rows(sink_ref, layer, t)
    for u in range(ATTN_CTX_SEQS):
        rows = slice(u * t, (u + 1) * t)
        kv = kv_ref[rows, :]
        for j in range(KV_A):
            ko_ref[u, n_prev, j] = kv[:, HD_A * j:HD_A * (j + 1)]
            vo_ref[u, n_prev, j] = kv[:, kvw + HD_A * j:kvw + HD_A * (j + 1)]
        kx = _kv_per_query_head(kv[:, 0:kvw]).astype(bf16)
        vx = _kv_per_query_head(kv[:, kvw:2 * kvw]).astype(bf16)
        q4 = _stack_heads(q_ref[rows, :] * (HD_A ** -0.5 * LOG2E)).astype(bf16)
        s = lax.dot_general(q4, kx, _NT, preferred_element_type=f32)
        (p,), inv = _softmax_parts([s], sink)
        o_ref[rows, :] = _pick_heads(_dot(p.astype(bf16), vx) * inv, t).astype(o_ref.dtype)


def _cache_spec(n_seq, n_layers, *dims):
    return pl.BlockSpec((n_seq, n_layers) + dims, lambda b: (b,) + (0,) * (1 + len(dims)))


def _attn_ctx(pr, sink, prev, *, layer, batch, t):
    n_prev = 0 if prev is None else prev[0].shape[1]
    ns = ATTN_CTX_SEQS
    kvshape = jax.ShapeDtypeStruct((batch, n_prev + 1, KV_A, t, HD_A), f32)
    in_specs = [pl.BlockSpec(memory_space=pltpu.SMEM),
                pl.BlockSpec((ns * t, 256), lambda b: (b, 0)), pl.BlockSpec((ns * t, 256), lambda b: (b, 1))]
    args = [sink, pr, pr]
    if n_prev:
        in_specs += [_cache_spec(ns, n_prev, KV_A, t, HD_A)] * 2
        args += list(prev)
    return pl.pallas_call(
        functools.partial(_attn_ctx_kernel, t=t, n_prev=n_prev, layer=layer),
        out_shape=(jax.ShapeDtypeStruct((batch * t, 256), bf16), kvshape, kvshape),
        grid=(batch // ns,),
        in_specs=in_specs,
        out_specs=(pl.BlockSpec((ns * t, 256), lambda b: (b, 0)),
                   _cache_spec(ns, n_prev + 1, KV_A, t, HD_A), _cache_spec(ns, n_prev + 1, KV_A, t, HD_A)),
        compiler_params=_params("parallel"),
        name="attn_ctx",
    )(*args)


def _attn_lat_kernel(sink_ref, q_ref, kv_ref, kc_ref, vc_ref, tab_ref, o_ref, kpad, vpad, kctx, vctx, *, t, layer):
    n = pl.program_id(1)
    kvw = KV_A * HD_A
    hw = H_A * HD_A

    @pl.when(n == 0)
    def _():
        zeros = jnp.zeros((BLOCK_A, hw), bf16)
        kv = kv_ref[...]
        k = _rope(kv[:, 0:kvw], tab_ref, slice(None), slice(0, kvw), HD_A // 2)
        kpad[0:BLOCK_A, :] = zeros
        vpad[0:BLOCK_A, :] = zeros
        kpad[BLOCK_A:BLOCK_A + t, :] = _kv_per_query_head(k).astype(bf16)
        vpad[BLOCK_A:BLOCK_A + t, :] = _kv_per_query_head(kv[:, kvw:2 * kvw]).astype(bf16)
        kpad[BLOCK_A + t:2 * BLOCK_A + t, :] = zeros
        vpad[BLOCK_A + t:2 * BLOCK_A + t, :] = zeros
        kctx[...] = jnp.concatenate([kc_ref[0, h // G_A] for h in range(H_A)], axis=1).astype(bf16)
        vctx[...] = jnp.concatenate([vc_ref[0, h // G_A] for h in range(H_A)], axis=1).astype(bf16)

    rows = H_A * BLOCK_A
    rr = lax.broadcasted_iota(jnp.int32, (rows, 1), 0) % BLOCK_A
    cidx = lax.broadcasted_iota(jnp.int32, (rows, 3 * BLOCK_A), 1)
    sink = _sink_rows(sink_ref, layer, BLOCK_A)
    for u in range(ATTN_LAT_BLOCKS):
        blk = n * ATTN_LAT_BLOCKS + u
        row0 = pl.multiple_of(blk * BLOCK_A, BLOCK_A)
        q = _rope(q_ref[u * BLOCK_A:(u + 1) * BLOCK_A, :], tab_ref, pl.ds(row0, BLOCK_A), slice(None), HD_A // 2)
        q4 = _stack_heads(q * (HD_A ** -0.5 * LOG2E)).astype(bf16)
        kband = kpad[pl.ds(row0, 3 * BLOCK_A), :]
        vband = vpad[pl.ds(row0, 3 * BLOCK_A), :]
        s_loc = lax.dot_general(q4, kband, _NT, preferred_element_type=f32)
        s_ctx = lax.dot_general(q4, kctx[...], _NT, preferred_element_type=f32)
        lo = jnp.maximum(rr + (BLOCK_A - WINDOW_A), jnp.maximum(0, (1 - blk) * BLOCK_A))
        hi = jnp.minimum(rr + (BLOCK_A + WINDOW_A), jnp.minimum(3 * BLOCK_A, t - (blk - 1) * BLOCK_A) - 1)
        s_loc = jnp.where((cidx >= lo) & (cidx <= hi), s_loc, NEG_INF)
        (p_loc, p_ctx), inv = _softmax_parts([s_loc, s_ctx], sink)
        o = (_dot(p_loc.astype(bf16), vband) + _dot(p_ctx.astype(bf16), vctx[...])) * inv
        o_ref[u * BLOCK_A:(u + 1) * BLOCK_A, :] = _pick_heads(o, BLOCK_A).astype(o_ref.dtype)


def _attn_lat(pr, sink, kc, vc, tab, *, layer, batch, t):
    rows = ATTN_LAT_BLOCKS * BLOCK_A
    nb = t // rows
    lc = kc.shape[3]
    cspec = pl.BlockSpec((1, None, KV_A, lc, HD_A), lambda b, n: (b, layer, 0, 0, 0))
    return pl.pallas_call(
        functools.partial(_attn_lat_kernel, t=t, layer=layer),
        out_shape=jax.ShapeDtypeStruct((batch * t, 256), bf16),
        grid=(batch, nb),
        in_specs=[pl.BlockSpec(memory_space=pltpu.SMEM),
                  pl.BlockSpec((rows, 256), lambda b, n: (b * nb + n, 0)),
                  pl.BlockSpec((t, 256), lambda b, n: (b, 1)),
                  cspec, cspec,
                  pl.BlockSpec((3, t, 256), lambda b, n: (0, 0, 0))],
        out_specs=pl.BlockSpec((rows, 256), lambda b, n: (b * nb + n, 0)),
        scratch_shapes=[pltpu.VMEM((t + 2 * BLOCK_A, H_A * HD_A), bf16),
                        pltpu.VMEM((t + 2 * BLOCK_A, H_A * HD_A), bf16),
                        pltpu.VMEM((lc, H_A * HD_A), bf16), pltpu.VMEM((lc, H_A * HD_A), bf16)],
        compiler_params=_params("parallel", "arbitrary"),
        name="attn_lat",
    )(sink, pr, pr, kc, vc, tab)


def _reduce_rows(x, reduce_fn):
    r, c = x.shape
    g = ROW_REDUCE_CHAINS
    if r % (g * SUBLANES):
        return reduce_fn(x, axis=0, keepdims=True)
    part = reduce_fn(x.reshape(g, r // (g * SUBLANES), SUBLANES, c), axis=1)
    return reduce_fn(reduce_fn(part, axis=0), axis=0, keepdims=True)


def _mla_kernel(*refs, t, lc, tq, rope, emit_cache, n_prev):
    it = iter(refs)
    cq_ref, ckv_ref, kr_ref = next(it), next(it), next(it)
    cc_ref = kc_ref = tab_ref = None
    if lc:
        cc_ref, kc_ref = next(it), next(it)
    if rope:
        tab_ref = next(it)
    qn_ref, kvn_ref, wq_ref, wk_ref, wv_ref = next(it), next(it), next(it), next(it), next(it)
    ckvp_ref, krp_ref = (next(it), next(it)) if n_prev else (None, None)
    o_ref = next(it)
    ckv_out_ref, kr_out_ref = (next(it), next(it)) if emit_cache else (None, None)
    kbuf, vbuf = next(it), next(it)
    i = pl.program_id(1)
    hw = LANES
    va = V_C + MLA_ONES

    @pl.when(i == 0)
    def _():
        ckv = _rms(ckv_ref[...], kvn_ref[...])
        kr = kr_ref[...]
        if emit_cache:
            if n_prev:
                ckv_out_ref[0, 0:n_prev] = ckvp_ref[0]
                kr_out_ref[0, 0:n_prev] = krp_ref[0]
            ckv_out_ref[0, n_prev] = ckv
            kr_out_ref[0, n_prev] = kr[:, NOPE_C:NOPE_C + ROPE_C]
        ckv_b = ckv.astype(bf16)
        if rope:
            kr = _rope(kr, tab_ref, slice(None), slice(None), ROPE_C // 2)
        kk = _dot(ckv_b, wk_ref[...])
        for h in range(H_C):
            kbuf[h, 0:t, :] = (kk[:, hw * h:hw * (h + 1)] + kr).astype(bf16)
        def put_values(vt, cols):
            for h in range(H_C):
                vbuf[va * h:va * h + V_C, cols] = vt[V_C * h:V_C * (h + 1)]

        for h in range(H_C):
            vbuf[va * h + V_C:va * (h + 1), :] = jnp.ones((MLA_ONES, t + lc), bf16)
        put_values(lax.dot_general(wv_ref[...], ckv_b, _NT, preferred_element_type=f32).astype(bf16), slice(0, t))
        if lc:
            cc_b = cc_ref[0].astype(bf16)
            kkc = _dot(cc_b, wk_ref[...])
            krc = kc_ref[0]
            for h in range(H_C):
                kbuf[h, t:t + lc, :] = (kkc[:, hw * h:hw * (h + 1)] + krc).astype(bf16)
            put_values(lax.dot_general(wv_ref[...], cc_b, _NT, preferred_element_type=f32).astype(bf16),
                       slice(t, t + lc))

    cqn = _rms(cq_ref[...], qn_ref[...]).astype(bf16)
    qf = _dot(cqn, wq_ref[...])
    c2 = (NOPE_C + ROPE_C) ** -0.5 * LOG2E
    row0 = pl.multiple_of(i * tq, tq)
    n_keys = t + lc
    half = n_keys // 2

    def scores(h):
        qh = qf[:, hw * h:hw * (h + 1)]
        if rope:
            qh = _rope(qh, tab_ref, pl.ds(row0, tq), slice(None), ROPE_C // 2)
        qb = (qh * c2).astype(bf16)
        return [lax.dot_general(kbuf[h, c * half:(c + 1) * half, :], qb, _NT, preferred_element_type=f32)
                for c in range(2)]

    def probs(s):
        m = jnp.maximum(_reduce_rows(s[0], jnp.max), _reduce_rows(s[1], jnp.max))
        return [jnp.exp2(x - m).astype(bf16) for x in s]

    def values(h, p):
        rows = slice(va * h, va * (h + 1))
        o = _dot(vbuf[rows, 0:half], p[0]) + _dot(vbuf[rows, half:2 * half], p[1])
        return o[0:V_C] * (1.0 / o[V_C:V_C + 1])

    s_next = scores(0)
    outs = []
    for h in range(H_C):
        s_cur = s_next
        if h + 1 < H_C:
            s_next = scores(h + 1)
        outs.append(values(h, probs(s_cur)))
    o_ref[...] = jnp.concatenate(outs, axis=0).T.astype(o_ref.dtype)


def _mla(pr, qn, kvn, wq, wk, wv, *, layer, batch, t, tq, ctx=None, tab=None, emit_cache=False, cache_prev=None):
    nq = t // tq
    lc = 0 if ctx is None else ctx[0].shape[2]
    n_prev = 0 if cache_prev is None else cache_prev[0].shape[1]
    in_specs = [pl.BlockSpec((tq, 256), lambda b, i: (b * nq + i, 2)),
                pl.BlockSpec((t, LANES), lambda b, i: (b, 6)),
                pl.BlockSpec((t, LANES), lambda b, i: (b, 7))]
    args = [pr, pr, pr]
    if lc:
        in_specs += [pl.BlockSpec((1, None, lc, LANES), lambda b, i: (b, layer, 0, 0))] * 2
        args += list(ctx)
    if tab is not None:
        in_specs.append(pl.BlockSpec((3, t, LANES), lambda b, i: (0, 0, 0)))
        args.append(tab)
    in_specs += [_const_spec(a.shape[1:], layer) for a in (qn, kvn, wq, wk, wv)]
    args += [qn, kvn, wq, wk, wv]

    def cspec(n_layers, width):
        return pl.BlockSpec((1, n_layers, t, width), lambda b, i: (b, 0, 0, 0))

    if n_prev:
        in_specs += [cspec(n_prev, KV_LORA_C), cspec(n_prev, ROPE_C)]
        args += list(cache_prev)
    out_shape = [jax.ShapeDtypeStruct((batch * t, 256), bf16)]
    out_specs = [pl.BlockSpec((tq, 256), lambda b, i: (b * nq + i, 0))]
    if emit_cache:
        out_shape += [jax.ShapeDtypeStruct((batch, n_prev + 1, t, KV_LORA_C), f32),
                      jax.ShapeDtypeStruct((batch, n_prev + 1, t, ROPE_C), f32)]
        out_specs += [cspec(n_prev + 1, KV_LORA_C), cspec(n_prev + 1, ROPE_C)]
    return pl.pallas_call(
        functools.partial(_mla_kernel, t=t, lc=lc, tq=tq, rope=tab is not None, emit_cache=emit_cache,
                          n_prev=n_prev),
        out_shape=tuple(out_shape),
        grid=(batch, nq),
        in_specs=in_specs,
        out_specs=tuple(out_specs),
        scratch_shapes=[pltpu.VMEM((H_C, t + lc, LANES), bf16),
                        pltpu.VMEM((H_C * (V_C + MLA_ONES), t + lc), bf16)],
        compiler_params=_params("parallel", "arbitrary"),
        name="mla",
    )(*args)


def _pool_counts(t):
    pos = np.arange(t)[:, None]
    half = np.repeat(np.asarray(POOL_WINDOWS) // 2, POOL_CH)[None, :]
    return (np.minimum(pos + half, t) - np.maximum(pos - half, 0)).astype(np.float32)


def _pool_kernel(x_ref, cnt_ref, w_ref, sc_ref, o_ref, buf_a, buf_b, buf_c, *, t, seqs):
    for u in range(seqs):
        _pool_sequence(x_ref, cnt_ref, w_ref, sc_ref, o_ref, buf_a, buf_b, buf_c, slice(u * t, (u + 1) * t), t)


def _pool_sequence(x_ref, cnt_ref, w_ref, sc_ref, o_ref, buf_a, buf_b, buf_c, seq, t):
    x = x_ref[seq, :]
    pad, edge = POOL_PAD, SUBLANES
    n = t + 2 * pad
    m = n - 2 * edge
    zeros = jnp.zeros((pad, 256), f32)
    buf_a[0:pad, :] = zeros
    buf_a[pad:pad + t, :] = x
    buf_a[pad + t:n, :] = zeros
    for buf in (buf_b, buf_c):
        buf[0:edge, :] = zeros[0:edge]
        buf[n - edge:n, :] = zeros[0:edge]
    inner = slice(edge, edge + m)
    buf_b[inner, :] = buf_a[pl.ds(edge - 1, m), :] + buf_a[pl.ds(edge, m), :]
    buf_c[inner, :] = buf_b[pl.ds(edge - 1, m), :] + buf_b[pl.ds(edge + 1, m), :]
    lo, hi = slice(0, LANES), slice(LANES, 2 * LANES)
    buf_a[inner, hi] = buf_c[pl.ds(edge - 2, m), hi] + buf_c[pl.ds(edge + 2, m), hi]
    s16 = buf_a[pl.ds(pad - 4, t), hi] + buf_a[pl.ds(pad + 4, t), hi]
    first = lax.broadcasted_iota(jnp.int32, (1, LANES), 1) < POOL_CH
    rows = slice(pad, pad + t)
    win_sum = jnp.concatenate([jnp.where(first, buf_b[rows, lo], buf_c[rows, lo]),
                               jnp.where(first, buf_a[rows, hi], s16)], axis=1)
    mean = win_sum / cnt_ref[...]
    o_ref[seq, :] = (_dot((mean - x).astype(bf16), w_ref[...]) * sc_ref[...]).astype(o_ref.dtype)


def _pool(pr, w_bd, scale, *, layer, batch, t, seqs=1):
    return pl.pallas_call(
        functools.partial(_pool_kernel, t=t, seqs=seqs),
        out_shape=jax.ShapeDtypeStruct((batch * t, 256), bf16),
        grid=(batch // seqs,),
        in_specs=[pl.BlockSpec((seqs * t, 256), lambda b: (b, 4)), _const_spec((t, 256)),
                  _const_spec((256, 256), layer), _const_spec((1, 256), layer)],
        out_specs=pl.BlockSpec((seqs * t, 256), lambda b: (b, 0)),
        scratch_shapes=[pltpu.VMEM((t + 2 * POOL_PAD, 256), f32)] * 3,
        compiler_params=_params("parallel"),
        name="pool",
    )(pr, jnp.asarray(_pool_counts(t)), w_bd, scale)


def _hgrn_mats():
    n = HG_TILE
    s = np.arange(n)[:, None]
    t = np.arange(n)[None, :]
    same_half = s // HG_HALF == t // HG_HALF
    mid = (t // HG_HALF) * HG_HALF + HG_HALF // 2 - 1
    rel_mid = (same_half & (s > mid) & (s <= t)).astype(np.float32) - (same_half & (s <= mid) & (s > t))
    mats = [rel_mid, same_half & (s <= t), s <= t, s > t, (s // HG_BLOCK == t // HG_BLOCK) & (s <= t)]
    for m in HG_LEVELS:
        same_pair = s // (2 * m) == t // (2 * m)
        right_t = (t // m) % 2 == 1
        right_s = (s // m) % 2 == 1
        q_side = same_pair & right_t & right_s & (s <= t)
        k_side = same_pair & ~right_t & ~right_s & (s > t)
        mats.append(q_side | k_side)
    fwd = np.concatenate([m.astype(np.float32) for m in mats], axis=1)
    bwd = np.concatenate([m[::-1, ::-1].astype(np.float32) for m in mats], axis=1)
    both = np.stack([fwd, bwd])
    return np.concatenate([both, both], axis=1)


def _hgrn_exact_intra(q, k, v, vb, hl, mats, reverse):
    n = HG_TILE
    e_all = _dot(hl, mats)
    beta = e_all[:, 0:n]
    lane = lax.broadcasted_iota(jnp.int32, (1, n), 1)
    pos = lane % HG_BLOCK
    v3 = v.reshape(H_B, DV_B, n)

    o3 = jnp.sum((q * k).reshape(H_B, DK_B, n), axis=1, keepdims=True) * v3
    for d in range(1, HG_BLOCK):
        shift = (n - d) if reverse else d
        valid = (pos <= HG_BLOCK - 1 - d) if reverse else (pos >= d)
        kr = pltpu.roll(k, shift, axis=1)
        br = pltpu.roll(beta, shift, axis=1)
        vr = pltpu.roll(v, shift, axis=1)
        e = jnp.exp(jnp.where(valid, beta - br, NEG_INF))
        sc = jnp.sum((q * kr * e).reshape(H_B, DK_B, n), axis=1, keepdims=True)
        o3 = o3 + sc * vr.reshape(H_B, DV_B, n)

    srow = lax.broadcasted_iota(jnp.int32, (n, n), 0)
    tcol = lax.broadcasted_iota(jnp.int32, (n, n), 1)
    scores = [jnp.zeros((n, n), f32) for _ in range(H_B)]
    for li, m in enumerate(HG_LEVELS):
        ex = jnp.exp(e_all[:, (li + 1) * n:(li + 2) * n])
        is_q = ((lane // m) % 2) == (0 if reverse else 1)
        qs = jnp.where(is_q, q * ex, 0.0).astype(bf16)
        ks = jnp.where(is_q, 0.0, k * ex).astype(bf16)
        pair = (srow // (2 * m)) == (tcol // (2 * m))
        for h in range(H_B):
            a = lax.dot_general(ks[DK_B * h:DK_B * (h + 1)], qs[DK_B * h:DK_B * (h + 1)], _TN,
                                preferred_element_type=f32)
            scores[h] = scores[h] + jnp.where(pair, a, 0.0)

    return [o3[h] + _dot(vb[DV_B * h:DV_B * (h + 1)], scores[h].astype(bf16)) for h in range(H_B)]


def _hgrn_common(q, v, fpre, lb, mats_ref, sidx, st_ref):
    n = HG_TILE
    f = lb + (1.0 - lb) * _sigmoid(fpre)
    k = 1.0 - f
    logf = jnp.log(f)
    hi = logf.astype(bf16)
    hl = jnp.concatenate([hi, (logf - hi.astype(f32)).astype(bf16)], axis=1)
    ex = _dot(hl, mats_ref[sidx, :, 0:4 * n])
    ones = jnp.ones((8, 2 * n), bf16)
    g_row = lax.dot_general(ones, hl, _NT, preferred_element_type=f32)[0:1]
    d_mid = ex[:, 0:n]
    small = jnp.max(jnp.abs(d_mid)) <= HG_MAX_EXP
    q_in = (q * jnp.exp(ex[:, 2 * n:3 * n])).astype(bf16)
    k_out = (k * jnp.exp(ex[:, 3 * n:4 * n])).astype(bf16)
    vb = v.astype(bf16)
    o_state = []
    for h in range(H_B):
        rows = slice(DK_B * h, DK_B * (h + 1))
        st = st_ref[sidx, rows, :]
        o_state.append(_dot(st.astype(bf16), q_in[rows]))
        upd = lax.dot_general(vb[rows], k_out[rows], _NT, preferred_element_type=f32)
        st_ref[sidx, rows, :] = st * jnp.exp(g_row[:, rows]) + upd
    return dict(q=q, k=k, v=v, vb=vb, hl=hl, ex=ex, o_state=o_state, small=small)


def _hgrn_fast_intra(c, reverse):
    n = HG_TILE
    q, k, ex = c["q"], c["k"], c["ex"]
    lane = lax.broadcasted_iota(jnp.int32, (1, n), 1)
    in0 = (lane // HG_HALF) == 0
    q_half = in0 if reverse else jnp.logical_not(in0)
    d_mid = ex[:, 0:n]
    qd = q * jnp.exp(d_mid)
    kd = k * jnp.exp(-d_mid)
    b_in = ex[:, n:2 * n]
    last = (0, HG_HALF) if reverse else (HG_HALF - 1, n - 1)
    total = jnp.where(in0, jnp.broadcast_to(b_in[:, last[0]:last[0] + 1], b_in.shape),
                      jnp.broadcast_to(b_in[:, last[1]:last[1] + 1], b_in.shape))
    ql = jnp.where(q_half, q * jnp.exp(b_in), 0.0)
    kl = jnp.where(q_half, 0.0, k * jnp.exp(total - b_in))
    parts_q = [jnp.where(in0, qd, 0.0).astype(bf16), jnp.where(in0, 0.0, qd).astype(bf16), ql.astype(bf16)]
    parts_k = [jnp.where(in0, kd, 0.0).astype(bf16), jnp.where(in0, 0.0, kd).astype(bf16), kl.astype(bf16)]
    srow = lax.broadcasted_iota(jnp.int32, (n, n), 0)
    tcol = lax.broadcasted_iota(jnp.int32, (n, n), 1)
    causal = (srow >= tcol) if reverse else (srow <= tcol)
    scores = []
    for h in range(H_B):
        rows = slice(DK_B * h, DK_B * (h + 1))
        k3 = jnp.concatenate([p[rows] for p in parts_k], axis=0)
        q3 = jnp.concatenate([p[rows] for p in parts_q], axis=0)
        scores.append(lax.dot_general(k3, q3, _TN, preferred_element_type=f32))
    scores = [jnp.where(causal, a, 0.0).astype(bf16) for a in scores]
    outs = [c["o_state"][h] + _dot(c["vb"][DV_B * h:DV_B * (h + 1)], scores[h]) for h in range(H_B)]
    return jnp.concatenate(outs, axis=0)


def _hgrn_exact(c, mats, reverse):
    intra = _hgrn_exact_intra(c["q"], c["k"], c["v"], c["vb"], c["hl"], mats, reverse)
    return jnp.concatenate([c["o_state"][h] + intra[h] for h in range(H_B)], axis=0)


def _hgrn_kernel(*refs, t, has_s0, emit_state, n_prev):
    it = iter(refs)
    q_ref, i_ref, ff_ref, fb_ref, g_ref, lb_ref, ng_ref, mats_ref = [next(it) for _ in range(8)]
    s0_ref = next(it) if has_s0 else None
    sprev_ref = next(it) if n_prev else None
    o_ref = next(it)
    sfin_ref = next(it) if emit_state else None
    st_ref, oacc_f, oacc_b = next(it), next(it), next(it)
    n = HG_TILE
    nt = t // n
    for d in range(2):
        for h in range(H_B):
            rows = slice(DV_B * h, DV_B * (h + 1))
            st_ref[d, rows, :] = s0_ref[0, d, h].T if has_s0 else jnp.zeros((DV_B, DK_B), f32)

    def finish_tile(idx):
        off = pl.multiple_of(idx * n, n)
        cols = pl.ds(off, n)
        o = oacc_f[:, cols] + oacc_b[:, cols]
        o3 = o.reshape(H_B, DV_B, n)
        y = o3 * lax.rsqrt(jnp.mean(o3 * o3, axis=1, keepdims=True) + EPS)
        gate = g_ref[:, cols]
        y = y.reshape(H_B * DV_B, n) * ng_ref[...] * (gate * _sigmoid(gate))
        o_ref[pl.ds(off, n), :] = y.T.astype(o_ref.dtype)

    def tiles_of(i):
        js = [i * HG_UNROLL + u for u in range(HG_UNROLL)]
        return js + [nt - 1 - j for j in js]

    def scan(i, c, finish_prev=False):
        if finish_prev:
            for t_idx in tiles_of(i - 1):
                finish_tile(t_idx)
        work = []
        small = None
        for u in range(HG_UNROLL):
            j = i * HG_UNROLL + u
            cols_f = pl.ds(pl.multiple_of(j * n, n), n)
            cols_b = pl.ds(pl.multiple_of((nt - 1 - j) * n, n), n)
            cf = _hgrn_common(q_ref[:, cols_f], i_ref[:, cols_f], ff_ref[:, cols_f], lb_ref[0], mats_ref, 0, st_ref)
            cb = _hgrn_common(q_ref[:, cols_b], i_ref[:, cols_b], fb_ref[:, cols_b], lb_ref[1], mats_ref, 1, st_ref)
            work.append((cols_f, cf, cols_b, cb))
            both = jnp.logical_and(cf["small"], cb["small"])
            small = both if small is None else jnp.logical_and(small, both)
        for cols_f, cf, cols_b, cb in work:
            oacc_f[:, cols_f] = _hgrn_fast_intra(cf, False)
            oacc_b[:, cols_b] = _hgrn_fast_intra(cb, True)

        @pl.when(jnp.logical_not(small))
        def _():
            for cols_f, cf, cols_b, cb in work:
                oacc_f[:, cols_f] = _hgrn_exact(cf, mats_ref[0, :, 4 * n:8 * n], False)
                oacc_b[:, cols_b] = _hgrn_exact(cb, mats_ref[1, :, 4 * n:8 * n], True)

        return c

    n_it = nt // HG_UNROLL
    if n_it % 2 == 0:
        lax.fori_loop(0, n_it // 2 + 1, scan, 0)
        lax.fori_loop(n_it // 2 + 1, n_it, functools.partial(scan, finish_prev=True), 0)
        for t_idx in tiles_of(n_it - 1):
            finish_tile(t_idx)
    else:
        lax.fori_loop(0, n_it, scan, 0)
        lax.fori_loop(0, nt, lambda idx, c: (finish_tile(idx), c)[1], 0)
    if emit_state:
        if n_prev:
            sfin_ref[0, 0:n_prev] = sprev_ref[0]
        for d in range(2):
            for h in range(H_B):
                sfin_ref[0, n_prev, d, h] = st_ref[d, DV_B * h:DV_B * (h + 1), :].T


def _hgrn(pt, lb, ng, mats, *, batch, t, s0=None, layer=0, emit_state=False, state_prev=None):
    def row(r):
        return pl.BlockSpec((256, t), lambda b: (r, b))

    def sspec(n_layers):
        return pl.BlockSpec((1, n_layers, 2, H_B, DK_B, DV_B), lambda b: (b, 0, 0, 0, 0, 0))

    n_prev = 0 if state_prev is None else state_prev.shape[1]
    in_specs = [row(0), row(1), row(2), row(3), row(4),
                _const_spec((2, 256, LANES), layer), _const_spec((256, LANES), layer), _const_spec(mats.shape)]
    args = [pt, pt, pt, pt, pt, lb, ng, mats]
    if s0 is not None:
        in_specs.append(pl.BlockSpec((1, None, 2, H_B, DK_B, DV_B), lambda b: (b, layer, 0, 0, 0, 0)))
        args.append(s0)
    if n_prev:
        in_specs.append(sspec(n_prev))
        args.append(state_prev)
    out_shape = [jax.ShapeDtypeStruct((batch * t, 256), bf16)]
    out_specs = [pl.BlockSpec((t, 256), lambda b: (b, 0))]
    if emit_state:
        out_shape.append(jax.ShapeDtypeStruct((batch, n_prev + 1, 2, H_B, DK_B, DV_B), f32))
        out_specs.append(sspec(n_prev + 1))
    return pl.pallas_call(
        functools.partial(_hgrn_kernel, t=t, has_s0=s0 is not None, emit_state=emit_state, n_prev=n_prev),
        out_shape=tuple(out_shape),
        grid=(batch,),
        in_specs=in_specs,
        out_specs=tuple(out_specs),
        scratch_shapes=[pltpu.VMEM((2, H_B * DV_B, DK_B), f32), pltpu.VMEM((256, t), f32),
                        pltpu.VMEM((256, t), f32)],
        compiler_params=_params("parallel"),
        name="hgrn",
    )(*args)


def _rope_tables(t):
    def cos_sin(dim):
        n_freq = dim // 4
        inv = ROPE_THETA ** (-jnp.arange(n_freq, dtype=f32) / n_freq)
        rows = t // GRID_W
        row_id = jnp.repeat(jnp.arange(rows, dtype=f32), GRID_W)
        col_id = jnp.tile(jnp.arange(GRID_W, dtype=f32), rows)
        ang = jnp.concatenate([row_id[:, None] * inv, col_id[:, None] * inv], axis=-1)
        return jnp.cos(ang), jnp.sin(ang)

    cos, sin = cos_sin(HD_A)
    zero = jnp.zeros_like(sin)
    tab_a = jnp.stack([jnp.tile(jnp.concatenate([cos, cos], -1), (1, H_A)),
                       jnp.tile(jnp.concatenate([zero, sin], -1), (1, H_A)),
                       jnp.tile(jnp.concatenate([-sin, zero], -1), (1, H_A))])
    cos, sin = cos_sin(ROPE_C)
    zero = jnp.zeros_like(sin)
    one64 = jnp.ones((t, NOPE_C), f32)
    zero64 = jnp.zeros((t, NOPE_C), f32)
    pad32 = jnp.zeros((t, LANES - NOPE_C - ROPE_C), f32)
    tab_c = jnp.stack([jnp.concatenate([one64, cos, cos, pad32 + 1.0], -1),
                       jnp.concatenate([zero64, zero, sin, pad32], -1),
                       jnp.concatenate([zero64, -sin, zero, pad32], -1)])
    return tab_a, tab_c


def _pad_rope_cols(x):
    pads = [(0, 0)] * (x.ndim - 1) + [(NOPE_C, LANES - NOPE_C - ROPE_C)]
    return jnp.pad(x, pads)


def kernel(x_prompt, x_sample, c, c_ctx, cache_attn_k, cache_attn_v, state_hgrn, cache_mla_ckv, cache_mla_krope,
           w_ada, b_ada, norm_sub, w_ffn_gate, w_ffn_up, w_ffn_down, w_in, w_out, attn_sink, hgrn_lb_logits,
           hgrn_out_norm, mla_q_norm, mla_kv_norm, mla_w_uq, mla_w_ukv, pool_w, pool_scale, final_norm):
    bp, tp, _ = x_prompt.shape
    bs, ts, _ = x_sample.shape
    np_, ns = bp * tp, bs * ts
    xp = x_prompt.reshape(np_, D_MODEL)
    xs = x_sample.reshape(ns, D_MODEL)

    n_vec = 1 + bs
    cvec = jnp.zeros((-(-n_vec // 8) * 8, D_MODEL), f32).at[0].set(c_ctx).at[1:n_vec].set(c)
    mods = _modulation(cvec, w_ada, b_ada).reshape(DEPTH, cvec.shape[0], N_MOD, D_MODEL)

    p_lb = jax.nn.softmax(hgrn_lb_logits.astype(f32), axis=0)
    lb_all = jnp.cumsum(p_lb, axis=0) - p_lb[0:1]
    tab_a, tab_c = _rope_tables(ts)
    mats = jnp.asarray(_hgrn_mats(), bf16)
    ffn_f32 = (w_ffn_gate, w_ffn_up, w_ffn_down)
    ffn_w = {(0, 0): tuple(_to_bf16(ffn_f32, 0, 0))}
    to_cast = [(l, idx) for l in range(DEPTH) for idx in range(2)][1:]
    grp_p = dict(base=0, rpm=np_)
    grp_s = dict(base=1, rpm=ts)
    tm, tm_mix = 512, 1024

    wb = lambda w: w.astype(bf16)
    w_row = wb(jnp.concatenate([w_in[:, :, 0:512], w_in[:, :, 1792:2176], _pad_rope_cols(w_in[:, :, 2176:2208]),
                                w_in[:, :, 2208:2464]], axis=2))
    w_t = wb(jnp.swapaxes(w_in[:, :, 512:1792], 1, 2))
    wq = wb(jnp.pad(mla_w_uq.reshape(DEPTH, Q_LORA_C, H_C, NOPE_C + ROPE_C),
                    ((0, 0), (0, 0), (0, 0), (0, LANES - NOPE_C - ROPE_C))).reshape(DEPTH, Q_LORA_C, H_C * LANES))
    wkv = mla_w_ukv.reshape(DEPTH, KV_LORA_C, H_C, NOPE_C + V_C)
    wk = wb(jnp.pad(wkv[..., :NOPE_C], ((0, 0), (0, 0), (0, 0), (0, LANES - NOPE_C)))
            .reshape(DEPTH, KV_LORA_C, H_C * LANES))
    wv = wb(jnp.swapaxes(wkv[..., NOPE_C:].reshape(DEPTH, KV_LORA_C, H_C * V_C), 1, 2))
    n_grp = len(POOL_WINDOWS)
    w_pool = wb(jnp.einsum('lgij,gh->lgihj', pool_w, jnp.eye(n_grp, dtype=f32))
                .reshape(DEPTH, n_grp * POOL_CH, n_grp * POOL_CH))
    lb = jnp.broadcast_to(lb_all[..., None], (DEPTH, 2, 256, LANES))
    ng = jnp.broadcast_to(jnp.tile(hgrn_out_norm, (1, H_B))[..., None], (DEPTH, 256, LANES))
    sink = attn_sink.astype(f32)
    qn = mla_q_norm.reshape(DEPTH, 1, Q_LORA_C)
    kvn = mla_kv_norm.reshape(DEPTH, 1, KV_LORA_C)
    pscale = pool_scale.reshape(DEPTH, 1, 256)
    wo = wb(w_out)
    norms = norm_sub.reshape(DEPTH * 3, 1, D_MODEL)
    ctx_c = (cache_mla_ckv, _pad_rope_cols(cache_mla_krope))

    new_k = new_v = new_st = new_ckv = new_kr = None
    for l in range(DEPTH):
        def ffn(x, grp, j, idx, mix=None, final_g=None):
            side = None
            if to_cast:
                nxt = to_cast.pop(0)
                side = nxt + (ffn_f32,)
            res = _ffn(x, mods, norms, ffn_w[(l, idx)], layer=l, j=j, tm=tm, mix=mix, final_g=final_g, side=side,
                       **grp)
            if side is not None:
                ffn_w[nxt] = tuple(res[1:])
            return res[0]

        xp = ffn(xp, grp_p, 0, 0)
        xs = ffn(xs, grp_s, 0, 0)

        pr, pt = _mixer_in(xp, mods, norms, w_row, w_t, layer=l, tm=tm_mix, **grp_p)
        oa, new_k, new_v = _attn_ctx(pr, sink, None if l == 0 else (new_k, new_v), layer=l, batch=bp, t=tp)
        ob, new_st = _hgrn(pt, lb, ng, mats, layer=l, batch=bp, t=tp, emit_state=True, state_prev=new_st)
        oc, new_ckv, new_kr = _mla(pr, qn, kvn, wq, wk, wv, layer=l, batch=bp, t=tp, tq=tp, emit_cache=True,
                                   cache_prev=None if l == 0 else (new_ckv, new_kr))
        od = _pool(pr, w_pool, pscale, layer=l, batch=bp, t=tp, seqs=POOL_CTX_SEQS)
        mix_p = (oa, ob, oc, od, wo)

        pr, pt = _mixer_in(xs, mods, norms, w_row, w_t, layer=l, tm=tm_mix, **grp_s)
        oa = _attn_lat(pr, sink, cache_attn_k, cache_attn_v, tab_a, layer=l, batch=bs, t=ts)
        (ob,) = _hgrn(pt, lb, ng, mats, layer=l, batch=bs, t=ts, s0=state_hgrn)
        (oc,) = _mla(pr, qn, kvn, wq, wk, wv, layer=l, batch=bs, t=ts, tq=512, ctx=ctx_c, tab=tab_c)
        od = _pool(pr, w_pool, pscale, layer=l, batch=bs, t=ts)
        mix_s = (oa, ob, oc, od, wo)

        fin = final_norm.reshape(1, D_MODEL) if l == DEPTH - 1 else None
        xp = ffn(xp, grp_p, 2, 1, mix_p, fin)
        xs = ffn(xs, grp_s, 2, 1, mix_s, fin)

    return (xp.reshape(bp, tp, D_MODEL), xs.reshape(bs, ts, D_MODEL), new_k, new_v, new_st, new_ckv, new_kr)
```

```python
import functools

import numpy as np
import jax
import jax.numpy as jnp
from jax import lax
from jax.experimental import pallas as pl
from jax.experimental.pallas import tpu as pltpu

f32 = jnp.float32
bf16 = jnp.bfloat16

D_MODEL = 1024
DEPTH = 2
GRID_W = 64
EPS = 1e-6
LOG2E = 1.4426950408889634
ROPE_THETA = 10000.0
NEG_INF = -1e30
N_MOD = 9
D_FF = 2816
H_A, KV_A, HD_A = 4, 2, 64
G_A = H_A // KV_A
WINDOW_A = 128
BLOCK_A = 128
H_B, DK_B, DV_B = 4, 64, 64
H_C, Q_LORA_C, KV_LORA_C, NOPE_C, ROPE_C, V_C = 4, 256, 128, 64, 32, 64
POOL_WINDOWS = (2, 4, 8, 16)
POOL_CH = 64

LANES = 128
SUBLANES = 8
CAST_BLOCKS = 16
ROW_REDUCE_CHAINS = 8
MLA_ONES = 2 * SUBLANES
VMEM_LIMIT = 56 * 1024 * 1024
HG_TILE = LANES
HG_UNROLL = 2
HG_HALF = HG_TILE // 2
HG_MAX_EXP = 80.0
HG_BLOCK = 16
HG_LEVELS = (16, 32, 64)
ATTN_CTX_SEQS = 2
POOL_CTX_SEQS = 8
ATTN_LAT_BLOCKS = 8
POOL_PAD = 16

PR_W = 1280
PT_W = 1280

_NT = (((1,), (1,)), ((), ()))
_TN = (((0,), (0,)), ((), ()))


def _params(*sem):
    return pltpu.CompilerParams(dimension_semantics=sem, vmem_limit_bytes=VMEM_LIMIT)


def _const_spec(shape, lead=None):
    zeros = (0,) * len(shape)
    if lead is None:
        return pl.BlockSpec(shape, lambda *_: zeros, pipeline_mode=pl.Buffered(1))
    return pl.BlockSpec((None,) + tuple(shape), lambda *_: (lead,) + zeros, pipeline_mode=pl.Buffered(1))


def _dot(a, b):
    return jnp.dot(a, b, preferred_element_type=f32)


def _sigmoid(x):
    return 1.0 / (1.0 + jnp.exp(-x))


def _rms(x, g):
    return x * lax.rsqrt(jnp.mean(x * x, axis=-1, keepdims=True) + EPS) * g


def _mod_kernel(c_ref, w_ref, b_ref, o_ref):
    c = c_ref[...]
    a = (c * _sigmoid(c)).astype(bf16)
    o_ref[0] = _dot(a, w_ref[0].astype(bf16)) + b_ref[0]


def _modulation(cvec, w_ada, b_ada):
    tn = 2304
    n_out = N_MOD * D_MODEL
    return pl.pallas_call(
        _mod_kernel,
        out_shape=jax.ShapeDtypeStruct((DEPTH, cvec.shape[0], n_out), f32),
        grid=(DEPTH, n_out // tn),
        in_specs=[pl.BlockSpec(cvec.shape, lambda l, j: (0, 0)),
                  pl.BlockSpec((1, D_MODEL, tn), lambda l, j: (l, 0, j)),
                  pl.BlockSpec((1, 1, tn), lambda l, j: (l, 0, j))],
        out_specs=pl.BlockSpec((1, cvec.shape[0], tn), lambda l, j: (l, 0, j)),
        compiler_params=_params("arbitrary", "arbitrary"),
        name="modulation",
    )(cvec, w_ada, b_ada.reshape(DEPTH, 1, n_out))


def _cast_kernel(*refs):
    n = len(refs) // 2
    for src, dst in zip(refs[:n], refs[n:]):
        dst[...] = src[...].astype(bf16)


def _to_bf16(ws, layer, idx):
    def blk(w):
        return w.shape[2] // CAST_BLOCKS, w.shape[3]

    return pl.pallas_call(
        _cast_kernel,
        out_shape=tuple(jax.ShapeDtypeStruct(w.shape[2:], bf16) for w in ws),
        grid=(CAST_BLOCKS,),
        in_specs=[pl.BlockSpec((None, None) + blk(w), lambda i: (layer, idx, i, 0)) for w in ws],
        out_specs=tuple(pl.BlockSpec(blk(w), lambda i: (i, 0)) for w in ws),
        compiler_params=_params("parallel"),
        name="cast_bf16",
    )(*ws)


def _ffn_kernel(*refs, j, mix, final, side):
    it = iter(refs)
    x_ref, m_ref = next(it), next(it)
    mix_refs = [next(it) for _ in range(5)] if mix else None
    g_ref, wg_ref, wu_ref, wd_ref = next(it), next(it), next(it), next(it)
    fg_ref = next(it) if final else None
    side_in = [next(it) for _ in range(3)] if side else []
    o_ref = next(it)
    for src, dst in zip(side_in, [next(it) for _ in range(len(side_in))]):
        dst[...] = src[...].astype(bf16)
    x = x_ref[...]
    m = m_ref[0]
    if mix:
        wo_ref = mix_refs[4]
        acc = _dot(mix_refs[0][...], wo_ref[0:256, :])
        for p in range(1, 4):
            acc += _dot(mix_refs[p][...], wo_ref[256 * p:256 * (p + 1), :])
        x = x + m[5:6] * acc
    h = (_rms(x, g_ref[...]) * (1.0 + m[3 * j + 1:3 * j + 2]) + m[3 * j:3 * j + 1]).astype(bf16)
    gate = _dot(h, wg_ref[...])
    up = _dot(h, wu_ref[...])
    act = (gate * _sigmoid(gate) * up).astype(bf16)
    y = x + (0.5 * m[3 * j + 2:3 * j + 3]) * _dot(act, wd_ref[...])
    if final:
        y = _rms(y, fg_ref[...])
    o_ref[...] = y


def _mod_spec(layer, base, rpm, tm):
    return pl.BlockSpec((None, 1, N_MOD, D_MODEL), lambda i: (layer, base + (i * tm) // rpm, 0, 0))


def _ffn(x, mods, norms, w3, *, layer, j, base, rpm, tm, mix=None, final_g=None, side=None):
    n = x.shape[0]
    steps = n // tm
    in_specs = [pl.BlockSpec((tm, D_MODEL), lambda i: (i, 0)), _mod_spec(layer, base, rpm, tm)]
    args = [x, mods]
    if mix is not None:
        in_specs += [pl.BlockSpec((tm, 256), lambda i: (i, 0))] * 4 + [_const_spec((D_MODEL, D_MODEL), layer)]
        args += list(mix)
    in_specs += [_const_spec((1, D_MODEL), 3 * layer + j)] + [_const_spec(w.shape) for w in w3]
    args += [norms] + list(w3)
    if final_g is not None:
        in_specs.append(_const_spec((1, D_MODEL)))
        args.append(final_g)
    out_shape = [jax.ShapeDtypeStruct((n, D_MODEL), f32)]
    out_specs = [pl.BlockSpec((tm, D_MODEL), lambda i: (i, 0))]
    if side is not None:
        sl, sidx, arrays = side
        nblk = CAST_BLOCKS if steps % CAST_BLOCKS == 0 else steps
        for a in arrays:
            assert a.shape[2] % (nblk * 2 * SUBLANES) == 0, (a.shape, nblk)
            r, c = a.shape[2] // nblk, a.shape[3]
            in_specs.append(pl.BlockSpec((None, None, r, c), lambda i: (sl, sidx, (i * nblk) // steps, 0)))
            args.append(a)
            out_shape.append(jax.ShapeDtypeStruct(a.shape[2:], bf16))
            out_specs.append(pl.BlockSpec((r, c), lambda i: ((i * nblk) // steps, 0)))
    return pl.pallas_call(
        functools.partial(_ffn_kernel, j=j, mix=mix is not None, final=final_g is not None, side=side is not None),
        out_shape=tuple(out_shape),
        grid=(steps,),
        in_specs=in_specs,
        out_specs=tuple(out_specs),
        compiler_params=_params("arbitrary" if side is not None else "parallel"),
        name="ffn",
    )(*args)


def _mixin_kernel(x_ref, m_ref, g_ref, wr_ref, wt_ref, pr_ref, pt_ref):
    m = m_ref[0]
    h = (_rms(x_ref[...], g_ref[...]) * (1.0 + m[4:5]) + m[3:4]).astype(bf16)
    pr_ref[...] = _dot(h, wr_ref[...])
    pt_ref[...] = lax.dot_general(wt_ref[...], h, _NT, preferred_element_type=f32)


def _mixer_in(x, mods, norms, w_row, w_t, *, layer, base, rpm, tm):
    n = x.shape[0]
    return pl.pallas_call(
        _mixin_kernel,
        out_shape=(jax.ShapeDtypeStruct((n, PR_W), f32), jax.ShapeDtypeStruct((PT_W, n), f32)),
        grid=(n // tm,),
        in_specs=[pl.BlockSpec((tm, D_MODEL), lambda i: (i, 0)), _mod_spec(layer, base, rpm, tm),
                  _const_spec((1, D_MODEL), 3 * layer + 1), _const_spec((D_MODEL, PR_W), layer),
                  _const_spec((PT_W, D_MODEL), layer)],
        out_specs=(pl.BlockSpec((tm, PR_W), lambda i: (i, 0)), pl.BlockSpec((PT_W, tm), lambda i: (0, i))),
        compiler_params=_params("parallel"),
        name="mixer_in",
    )(x, mods, norms, w_row, w_t)


def _rope(x, tab_ref, rows, lanes, shift):
    width = x.shape[-1]
    c = tab_ref[0, rows, lanes]
    sa = tab_ref[1, rows, lanes]
    sb = tab_ref[2, rows, lanes]
    return x * c + pltpu.roll(x, shift, axis=1) * sa + pltpu.roll(x, width - shift, axis=1) * sb


def _softmax_parts(s_list, sink):
    sink2 = sink * LOG2E
    mx = functools.reduce(jnp.maximum, [jnp.max(s, axis=-1, keepdims=True) for s in s_list])
    mx = jnp.maximum(mx, sink2)
    ps = [jnp.exp2(s - mx) for s in s_list]
    den = functools.reduce(lambda a, b: a + b, [jnp.sum(p, axis=-1, keepdims=True) for p in ps])
    return ps, 1.0 / (den + jnp.exp2(sink2 - mx))


def _kv_per_query_head(x):
    return jnp.concatenate([x[:, HD_A * (h // G_A):HD_A * (h // G_A + 1)] for h in range(H_A)], axis=1)


def _stack_heads(q):
    head = lax.broadcasted_iota(jnp.int32, (1, H_A * HD_A), 1) // HD_A
    return jnp.concatenate([jnp.where(head == h, q, 0.0) for h in range(H_A)], axis=0)


def _pick_heads(o, r):
    head = lax.broadcasted_iota(jnp.int32, (1, H_A * HD_A), 1) // HD_A
    out = o[0:r]
    for h in range(1, H_A):
        out = jnp.where(head == h, o[h * r:(h + 1) * r], out)
    return out


def _sink_rows(sink_ref, layer, r):
    row = lax.broadcasted_iota(jnp.int32, (H_A * r, 1), 0)
    out = jnp.full((H_A * r, 1), sink_ref[layer, 0], f32)
    for h in range(1, H_A):
        out = jnp.where(row >= h * r, sink_ref[layer, h], out)
    return out


def _attn_ctx_kernel(*refs, t, n_prev, layer):
    if n_prev:
        sink_ref, q_ref, kv_ref, kp_ref, vp_ref, o_ref, ko_ref, vo_ref = refs
        ko_ref[:, 0:n_prev] = kp_ref[...]
        vo_ref[:, 0:n_prev] = vp_ref[...]
    else:
        sink_ref, q_ref, kv_ref, o_ref, ko_ref, vo_ref = refs
    kvw = KV_A * HD_A
    sink = _sink_rows(sink_ref, layer, t)
    for u in range(ATTN_CTX_SEQS):
        rows = slice(u * t, (u + 1) * t)
        kv = kv_ref[rows, :]
        for j in range(KV_A):
            ko_ref[u, n_prev, j] = kv[:, HD_A * j:HD_A * (j + 1)]
            vo_ref[u, n_prev, j] = kv[:, kvw + HD_A * j:kvw + HD_A * (j + 1)]
        kx = _kv_per_query_head(kv[:, 0:kvw]).astype(bf16)
        vx = _kv_per_query_head(kv[:, kvw:2 * kvw]).astype(bf16)
        q4 = _stack_heads(q_ref[rows, :] * (HD_A ** -0.5 * LOG2E)).astype(bf16)
        s = lax.dot_general(q4, kx, _NT, preferred_element_type=f32)
        (p,), inv = _softmax_parts([s], sink)
        o_ref[rows, :] = _pick_heads(_dot(p.astype(bf16), vx) * inv, t).astype(o_ref.dtype)


def _cache_spec(n_seq, n_layers, *dims):
    return pl.BlockSpec((n_seq, n_layers) + dims, lambda b: (b,) + (0,) * (1 + len(dims)))


def _attn_ctx(pr, sink, prev, *, layer, batch, t):
    n_prev = 0 if prev is None else prev[0].shape[1]
    ns = ATTN_CTX_SEQS
    kvshape = jax.ShapeDtypeStruct((batch, n_prev + 1, KV_A, t, HD_A), f32)
    in_specs = [pl.BlockSpec(memory_space=pltpu.SMEM),
                pl.BlockSpec((ns * t, 256), lambda b: (b, 0)), pl.BlockSpec((ns * t, 256), lambda b: (b, 1))]
    args = [sink, pr, pr]
    if n_prev:
        in_specs += [_cache_spec(ns, n_prev, KV_A, t, HD_A)] * 2
        args += list(prev)
    return pl.pallas_call(
        functools.partial(_attn_ctx_kernel, t=t, n_prev=n_prev, layer=layer),
        out_shape=(jax.ShapeDtypeStruct((batch * t, 256), bf16), kvshape, kvshape),
        grid=(batch // ns,),
        in_specs=in_specs,
        out_specs=(pl.BlockSpec((ns * t, 256), lambda b: (b, 0)),
                   _cache_spec(ns, n_prev + 1, KV_A, t, HD_A), _cache_spec(ns, n_prev + 1, KV_A, t, HD_A)),
        compiler_params=_params("parallel"),
        name="attn_ctx",
    )(*args)


def _attn_lat_kernel(sink_ref, q_ref, kv_ref, kc_ref, vc_ref, tab_ref, o_ref, kpad, vpad, kctx, vctx, *, t, layer):
    n = pl.program_id(1)
    kvw = KV_A * HD_A
    hw = H_A * HD_A

    @pl.when(n == 0)
    def _():
        zeros = jnp.zeros((BLOCK_A, hw), bf16)
        kv = kv_ref[...]
        k = _rope(kv[:, 0:kvw], tab_ref, slice(None), slice(0, kvw), HD_A // 2)
        kpad[0:BLOCK_A, :] = zeros
        vpad[0:BLOCK_A, :] = zeros
        kpad[BLOCK_A:BLOCK_A + t, :] = _kv_per_query_head(k).astype(bf16)
        vpad[BLOCK_A:BLOCK_A + t, :] = _kv_per_query_head(kv[:, kvw:2 * kvw]).astype(bf16)
        kpad[BLOCK_A + t:2 * BLOCK_A + t, :] = zeros
        vpad[BLOCK_A + t:2 * BLOCK_A + t, :] = zeros
        kctx[...] = jnp.concatenate([kc_ref[0, h // G_A] for h in range(H_A)], axis=1).astype(bf16)
        vctx[...] = jnp.concatenate([vc_ref[0, h // G_A] for h in range(H_A)], axis=1).astype(bf16)

    rows = H_A * BLOCK_A
    rr = lax.broadcasted_iota(jnp.int32, (rows, 1), 0) % BLOCK_A
    cidx = lax.broadcasted_iota(jnp.int32, (rows, 3 * BLOCK_A), 1)
    sink = _sink_rows(sink_ref, layer, BLOCK_A)
    for u in range(ATTN_LAT_BLOCKS):
        blk = n * ATTN_LAT_BLOCKS + u
        row0 = pl.multiple_of(blk * BLOCK_A, BLOCK_A)
        q = _rope(q_ref[u * BLOCK_A:(u + 1) * BLOCK_A, :], tab_ref, pl.ds(row0, BLOCK_A), slice(None), HD_A // 2)
        q4 = _stack_heads(q * (HD_A ** -0.5 * LOG2E)).astype(bf16)
        kband = kpad[pl.ds(row0, 3 * BLOCK_A), :]
        vband = vpad[pl.ds(row0, 3 * BLOCK_A), :]
        s_loc = lax.dot_general(q4, kband, _NT, preferred_element_type=f32)
        s_ctx = lax.dot_general(q4, kctx[...], _NT, preferred_element_type=f32)
        lo = jnp.maximum(rr + (BLOCK_A - WINDOW_A), jnp.maximum(0, (1 - blk) * BLOCK_A))
        hi = jnp.minimum(rr + (BLOCK_A + WINDOW_A), jnp.minimum(3 * BLOCK_A, t - (blk - 1) * BLOCK_A) - 1)
        s_loc = jnp.where((cidx >= lo) & (cidx <= hi), s_loc, NEG_INF)
        (p_loc, p_ctx), inv = _softmax_parts([s_loc, s_ctx], sink)
        o = (_dot(p_loc.astype(bf16), vband) + _dot(p_ctx.astype(bf16), vctx[...])) * inv
        o_ref[u * BLOCK_A:(u + 1) * BLOCK_A, :] = _pick_heads(o, BLOCK_A).astype(o_ref.dtype)


def _attn_lat(pr, sink, kc, vc, tab, *, layer, batch, t):
    rows = ATTN_LAT_BLOCKS * BLOCK_A
    nb = t // rows
    lc = kc.shape[3]
    cspec = pl.BlockSpec((1, None, KV_A, lc, HD_A), lambda b, n: (b, layer, 0, 0, 0))
    return pl.pallas_call(
        functools.partial(_attn_lat_kernel, t=t, layer=layer),
        out_shape=jax.ShapeDtypeStruct((batch * t, 256), bf16),
        grid=(batch, nb),
        in_specs=[pl.BlockSpec(memory_space=pltpu.SMEM),
                  pl.BlockSpec((rows, 256), lambda b, n: (b * nb + n, 0)),
                  pl.BlockSpec((t, 256), lambda b, n: (b, 1)),
                  cspec, cspec,
                  pl.BlockSpec((3, t, 256), lambda b, n: (0, 0, 0))],
        out_specs=pl.BlockSpec((rows, 256), lambda b, n: (b * nb + n, 0)),
        scratch_shapes=[pltpu.VMEM((t + 2 * BLOCK_A, H_A * HD_A), bf16),
                        pltpu.VMEM((t + 2 * BLOCK_A, H_A * HD_A), bf16),
                        pltpu.VMEM((lc, H_A * HD_A), bf16), pltpu.VMEM((lc, H_A * HD_A), bf16)],
        compiler_params=_params("parallel", "arbitrary"),
        name="attn_lat",
    )(sink, pr, pr, kc, vc, tab)


def _reduce_rows(x, reduce_fn):
    r, c = x.shape
    g = ROW_REDUCE_CHAINS
    if r % (g * SUBLANES):
        return reduce_fn(x, axis=0, keepdims=True)
    part = reduce_fn(x.reshape(g, r // (g * SUBLANES), SUBLANES, c), axis=1)
    return reduce_fn(reduce_fn(part, axis=0), axis=0, keepdims=True)


def _mla_kernel(*refs, t, lc, tq, rope, emit_cache, n_prev):
    it = iter(refs)
    cq_ref, ckv_ref, kr_ref = next(it), next(it), next(it)
    cc_ref = kc_ref = tab_ref = None
    if lc:
        cc_ref, kc_ref = next(it), next(it)
    if rope:
        tab_ref = next(it)
    qn_ref, kvn_ref, wq_ref, wk_ref, wv_ref = next(it), next(it), next(it), next(it), next(it)
    ckvp_ref, krp_ref = (next(it), next(it)) if n_prev else (None, None)
    o_ref = next(it)
    ckv_out_ref, kr_out_ref = (next(it), next(it)) if emit_cache else (None, None)
    kbuf, vbuf = next(it), next(it)
    i = pl.program_id(1)
    hw = LANES
    va = V_C + MLA_ONES

    @pl.when(i == 0)
    def _():
        ckv = _rms(ckv_ref[...], kvn_ref[...])
        kr = kr_ref[...]
        if emit_cache:
            if n_prev:
                ckv_out_ref[0, 0:n_prev] = ckvp_ref[0]
                kr_out_ref[0, 0:n_prev] = krp_ref[0]
            ckv_out_ref[0, n_prev] = ckv
            kr_out_ref[0, n_prev] = kr[:, NOPE_C:NOPE_C + ROPE_C]
        ckv_b = ckv.astype(bf16)
        if rope:
            kr = _rope(kr, tab_ref, slice(None), slice(None), ROPE_C // 2)
        kk = _dot(ckv_b, wk_ref[...])
        for h in range(H_C):
            kbuf[h, 0:t, :] = (kk[:, hw * h:hw * (h + 1)] + kr).astype(bf16)
        def put_values(vt, cols):
            for h in range(H_C):
                vbuf[va * h:va * h + V_C, cols] = vt[V_C * h:V_C * (h + 1)]

        for h in range(H_C):
            vbuf[va * h + V_C:va * (h + 1), :] = jnp.ones((MLA_ONES, t + lc), bf16)
        put_values(lax.dot_general(wv_ref[...], ckv_b, _NT, preferred_element_type=f32).astype(bf16), slice(0, t))
        if lc:
            cc_b = cc_ref[0].astype(bf16)
            kkc = _dot(cc_b, wk_ref[...])
            krc = kc_ref[0]
            for h in range(H_C):
                kbuf[h, t:t + lc, :] = (kkc[:, hw * h:hw * (h + 1)] + krc).astype(bf16)
            put_values(lax.dot_general(wv_ref[...], cc_b, _NT, preferred_element_type=f32).astype(bf16),
                       slice(t, t + lc))

    cqn = _rms(cq_ref[...], qn_ref[...]).astype(bf16)
    qf = _dot(cqn, wq_ref[...])
    c2 = (NOPE_C + ROPE_C) ** -0.5 * LOG2E
    row0 = pl.multiple_of(i * tq, tq)
    n_keys = t + lc
    half = n_keys // 2

    def scores(h):
        qh = qf[:, hw * h:hw * (h + 1)]
        if rope:
            qh = _rope(qh, tab_ref, pl.ds(row0, tq), slice(None), ROPE_C // 2)
        qb = (qh * c2).astype(bf16)
        return [lax.dot_general(kbuf[h, c * half:(c + 1) * half, :], qb, _NT, preferred_element_type=f32)
                for c in range(2)]

    def probs(s):
        m = jnp.maximum(_reduce_rows(s[0], jnp.max), _reduce_rows(s[1], jnp.max))
        return [jnp.exp2(x - m).astype(bf16) for x in s]

    def values(h, p):
        rows = slice(va * h, va * (h + 1))
        o = _dot(vbuf[rows, 0:half], p[0]) + _dot(vbuf[rows, half:2 * half], p[1])
        return o[0:V_C] * (1.0 / o[V_C:V_C + 1])

    s_next = scores(0)
    outs = []
    for h in range(H_C):
        s_cur = s_next
        if h + 1 < H_C:
            s_next = scores(h + 1)
        outs.append(values(h, probs(s_cur)))
    o_ref[...] = jnp.concatenate(outs, axis=0).T.astype(o_ref.dtype)


def _mla(pr, qn, kvn, wq, wk, wv, *, layer, batch, t, tq, ctx=None, tab=None, emit_cache=False, cache_prev=None):
    nq = t // tq
    lc = 0 if ctx is None else ctx[0].shape[2]
    n_prev = 0 if cache_prev is None else cache_prev[0].shape[1]
    in_specs = [pl.BlockSpec((tq, 256), lambda b, i: (b * nq + i, 2)),
                pl.BlockSpec((t, LANES), lambda b, i: (b, 6)),
                pl.BlockSpec((t, LANES), lambda b, i: (b, 7))]
    args = [pr, pr, pr]
    if lc:
        in_specs += [pl.BlockSpec((1, None, lc, LANES), lambda b, i: (b, layer, 0, 0))] * 2
        args += list(ctx)
    if tab is not None:
        in_specs.append(pl.BlockSpec((3, t, LANES), lambda b, i: (0, 0, 0)))
        args.append(tab)
    in_specs += [_const_spec(a.shape[1:], layer) for a in (qn, kvn, wq, wk, wv)]
    args += [qn, kvn, wq, wk, wv]

    def cspec(n_layers, width):
        return pl.BlockSpec((1, n_layers, t, width), lambda b, i: (b, 0, 0, 0))

    if n_prev:
        in_specs += [cspec(n_prev, KV_LORA_C), cspec(n_prev, ROPE_C)]
        args += list(cache_prev)
    out_shape = [jax.ShapeDtypeStruct((batch * t, 256), bf16)]
    out_specs = [pl.BlockSpec((tq, 256), lambda b, i: (b * nq + i, 0))]
    if emit_cache:
        out_shape += [jax.ShapeDtypeStruct((batch, n_prev + 1, t, KV_LORA_C), f32),
                      jax.ShapeDtypeStruct((batch, n_prev + 1, t, ROPE_C), f32)]
        out_specs += [cspec(n_prev + 1, KV_LORA_C), cspec(n_prev + 1, ROPE_C)]
    return pl.pallas_call(
        functools.partial(_mla_kernel, t=t, lc=lc, tq=tq, rope=tab is not None, emit_cache=emit_cache,
                          n_prev=n_prev),
        out_shape=tuple(out_shape),
        grid=(batch, nq),
        in_specs=in_specs,
        out_specs=tuple(out_specs),
        scratch_shapes=[pltpu.VMEM((H_C, t + lc, LANES), bf16),
                        pltpu.VMEM((H_C * (V_C + MLA_ONES), t + lc), bf16)],
        compiler_params=_params("parallel", "arbitrary"),
        name="mla",
    )(*args)


def _pool_counts(t):
    pos = np.arange(t)[:, None]
    half = np.repeat(np.asarray(POOL_WINDOWS) // 2, POOL_CH)[None, :]
    return (np.minimum(pos + half, t) - np.maximum(pos - half, 0)).astype(np.float32)


def _pool_kernel(x_ref, cnt_ref, w_ref, sc_ref, o_ref, buf_a, buf_b, buf_c, *, t, seqs):
    for u in range(seqs):
        _pool_sequence(x_ref, cnt_ref, w_ref, sc_ref, o_ref, buf_a, buf_b, buf_c, slice(u * t, (u + 1) * t), t)


def _pool_sequence(x_ref, cnt_ref, w_ref, sc_ref, o_ref, buf_a, buf_b, buf_c, seq, t):
    x = x_ref[seq, :]
    pad, edge = POOL_PAD, SUBLANES
    n = t + 2 * pad
    m = n - 2 * edge
    zeros = jnp.zeros((pad, 256), f32)
    buf_a[0:pad, :] = zeros
    buf_a[pad:pad + t, :] = x
    buf_a[pad + t:n, :] = zeros
    for buf in (buf_b, buf_c):
        buf[0:edge, :] = zeros[0:edge]
        buf[n - edge:n, :] = zeros[0:edge]
    inner = slice(edge, edge + m)
    buf_b[inner, :] = buf_a[pl.ds(edge - 1, m), :] + buf_a[pl.ds(edge, m), :]
    buf_c[inner, :] = buf_b[pl.ds(edge - 1, m), :] + buf_b[pl.ds(edge + 1, m), :]
    lo, hi = slice(0, LANES), slice(LANES, 2 * LANES)
    buf_a[inner, hi] = buf_c[pl.ds(edge - 2, m), hi] + buf_c[pl.ds(edge + 2, m), hi]
    s16 = buf_a[pl.ds(pad - 4, t), hi] + buf_a[pl.ds(pad + 4, t), hi]
    first = lax.broadcasted_iota(jnp.int32, (1, LANES), 1) < POOL_CH
    rows = slice(pad, pad + t)
    win_sum = jnp.concatenate([jnp.where(first, buf_b[rows, lo], buf_c[rows, lo]),
                               jnp.where(first, buf_a[rows, hi], s16)], axis=1)
    mean = win_sum / cnt_ref[...]
    o_ref[seq, :] = (_dot((mean - x).astype(bf16), w_ref[...]) * sc_ref[...]).astype(o_ref.dtype)


def _pool(pr, w_bd, scale, *, layer, batch, t, seqs=1):
    return pl.pallas_call(
        functools.partial(_pool_kernel, t=t, seqs=seqs),
        out_shape=jax.ShapeDtypeStruct((batch * t, 256), bf16),
        grid=(batch // seqs,),
        in_specs=[pl.BlockSpec((seqs * t, 256), lambda b: (b, 4)), _const_spec((t, 256)),
                  _const_spec((256, 256), layer), _const_spec((1, 256), layer)],
        out_specs=pl.BlockSpec((seqs * t, 256), lambda b: (b, 0)),
        scratch_shapes=[pltpu.VMEM((t + 2 * POOL_PAD, 256), f32)] * 3,
        compiler_params=_params("parallel"),
        name="pool",
    )(pr, jnp.asarray(_pool_counts(t)), w_bd, scale)


def _hgrn_mats():
    n = HG_TILE
    s = np.arange(n)[:, None]
    t = np.arange(n)[None, :]
    same_half = s // HG_HALF == t // HG_HALF
    mid = (t // HG_HALF) * HG_HALF + HG_HALF // 2 - 1
    rel_mid = (same_half & (s > mid) & (s <= t)).astype(np.float32) - (same_half & (s <= mid) & (s > t))
    mats = [rel_mid, same_half & (s <= t), s <= t, s > t, (s // HG_BLOCK == t // HG_BLOCK) & (s <= t)]
    for m in HG_LEVELS:
        same_pair = s // (2 * m) == t // (2 * m)
        right_t = (t // m) % 2 == 1
        right_s = (s // m) % 2 == 1
        q_side = same_pair & right_t & right_s & (s <= t)
        k_side = same_pair & ~right_t & ~right_s & (s > t)
        mats.append(q_side | k_side)
    fwd = np.concatenate([m.astype(np.float32) for m in mats], axis=1)
    bwd = np.concatenate([m[::-1, ::-1].astype(np.float32) for m in mats], axis=1)
    both = np.stack([fwd, bwd])
    return np.concatenate([both, both], axis=1)


def _hgrn_exact_intra(q, k, v, vb, hl, mats, reverse):
    n = HG_TILE
    e_all = _dot(hl, mats)
    beta = e_all[:, 0:n]
    lane = lax.broadcasted_iota(jnp.int32, (1, n), 1)
    pos = lane % HG_BLOCK
    v3 = v.reshape(H_B, DV_B, n)

    o3 = jnp.sum((q * k).reshape(H_B, DK_B, n), axis=1, keepdims=True) * v3
    for d in range(1, HG_BLOCK):
        shift = (n - d) if reverse else d
        valid = (pos <= HG_BLOCK - 1 - d) if reverse else (pos >= d)
        kr = pltpu.roll(k, shift, axis=1)
        br = pltpu.roll(beta, shift, axis=1)
        vr = pltpu.roll(v, shift, axis=1)
        e = jnp.exp(jnp.where(valid, beta - br, NEG_INF))
        sc = jnp.sum((q * kr * e).reshape(H_B, DK_B, n), axis=1, keepdims=True)
        o3 = o3 + sc * vr.reshape(H_B, DV_B, n)

    srow = lax.broadcasted_iota(jnp.int32, (n, n), 0)
    tcol = lax.broadcasted_iota(jnp.int32, (n, n), 1)
    scores = [jnp.zeros((n, n), f32) for _ in range(H_B)]
    for li, m in enumerate(HG_LEVELS):
        ex = jnp.exp(e_all[:, (li + 1) * n:(li + 2) * n])
        is_q = ((lane // m) % 2) == (0 if reverse else 1)
        qs = jnp.where(is_q, q * ex, 0.0).astype(bf16)
        ks = jnp.where(is_q, 0.0, k * ex).astype(bf16)
        pair = (srow // (2 * m)) == (tcol // (2 * m))
        for h in range(H_B):
            a = lax.dot_general(ks[DK_B * h:DK_B * (h + 1)], qs[DK_B * h:DK_B * (h + 1)], _TN,
                                preferred_element_type=f32)
            scores[h] = scores[h] + jnp.where(pair, a, 0.0)

    return [o3[h] + _dot(vb[DV_B * h:DV_B * (h + 1)], scores[h].astype(bf16)) for h in range(H_B)]


def _hgrn_common(q, v, fpre, lb, mats_ref, sidx, st_ref):
    n = HG_TILE
    f = lb + (1.0 - lb) * _sigmoid(fpre)
    k = 1.0 - f
    logf = jnp.log(f)
    hi = logf.astype(bf16)
    hl = jnp.concatenate([hi, (logf - hi.astype(f32)).astype(bf16)], axis=1)
    ex = _dot(hl, mats_ref[sidx, :, 0:4 * n])
    ones = jnp.ones((8, 2 * n), bf16)
    g_row = lax.dot_general(ones, hl, _NT, preferred_element_type=f32)[0:1]
    d_mid = ex[:, 0:n]
    small = jnp.max(jnp.abs(d_mid)) <= HG_MAX_EXP
    q_in = (q * jnp.exp(ex[:, 2 * n:3 * n])).astype(bf16)
    k_out = (k * jnp.exp(ex[:, 3 * n:4 * n])).astype(bf16)
    vb = v.astype(bf16)
    o_state = []
    for h in range(H_B):
        rows = slice(DK_B * h, DK_B * (h + 1))
        st = st_ref[sidx, rows, :]
        o_state.append(_dot(st.astype(bf16), q_in[rows]))
        upd = lax.dot_general(vb[rows], k_out[rows], _NT, preferred_element_type=f32)
        st_ref[sidx, rows, :] = st * jnp.exp(g_row[:, rows]) + upd
    return dict(q=q, k=k, v=v, vb=vb, hl=hl, ex=ex, o_state=o_state, small=small)


def _hgrn_fast_intra(c, reverse):
    n = HG_TILE
    q, k, ex = c["q"], c["k"], c["ex"]
    lane = lax.broadcasted_iota(jnp.int32, (1, n), 1)
    in0 = (lane // HG_HALF) == 0
    q_half = in0 if reverse else jnp.logical_not(in0)
    d_mid = ex[:, 0:n]
    qd = q * jnp.exp(d_mid)
    kd = k * jnp.exp(-d_mid)
    b_in = ex[:, n:2 * n]
    last = (0, HG_HALF) if reverse else (HG_HALF - 1, n - 1)
    total = jnp.where(in0, jnp.broadcast_to(b_in[:, last[0]:last[0] + 1], b_in.shape),
                      jnp.broadcast_to(b_in[:, last[1]:last[1] + 1], b_in.shape))
    ql = jnp.where(q_half, q * jnp.exp(b_in), 0.0)
    kl = jnp.where(q_half, 0.0, k * jnp.exp(total - b_in))
    parts_q = [jnp.where(in0, qd, 0.0).astype(bf16), jnp.where(in0, 0.0, qd).astype(bf16), ql.astype(bf16)]
    parts_k = [jnp.where(in0, kd, 0.0).astype(bf16), jnp.where(in0, 0.0, kd).astype(bf16), kl.astype(bf16)]
    srow = lax.broadcasted_iota(jnp.int32, (n, n), 0)
    tcol = lax.broadcasted_iota(jnp.int32, (n, n), 1)
    causal = (srow >= tcol) if reverse else (srow <= tcol)
    scores = []
    for h in range(H_B):
        rows = slice(DK_B * h, DK_B * (h + 1))
        k3 = jnp.concatenate([p[rows] for p in parts_k], axis=0)
        q3 = jnp.concatenate([p[rows] for p in parts_q], axis=0)
        scores.append(lax.dot_general(k3, q3, _TN, preferred_element_type=f32))
    scores = [jnp.where(causal, a, 0.0).astype(bf16) for a in scores]
    outs = [c["o_state"][h] + _dot(c["vb"][DV_B * h:DV_B * (h + 1)], scores[h]) for h in range(H_B)]
    return jnp.concatenate(outs, axis=0)


def _hgrn_exact(c, mats, reverse):
    intra = _hgrn_exact_intra(c["q"], c["k"], c["v"], c["vb"], c["hl"], mats, reverse)
    return jnp.concatenate([c["o_state"][h] + intra[h] for h in range(H_B)], axis=0)


def _hgrn_kernel(*refs, t, has_s0, emit_state, n_prev):
    it = iter(refs)
    q_ref, i_ref, ff_ref, fb_ref, g_ref, lb_ref, ng_ref, mats_ref = [next(it) for _ in range(8)]
    s0_ref = next(it) if has_s0 else None
    sprev_ref = next(it) if n_prev else None
    o_ref = next(it)
    sfin_ref = next(it) if emit_state else None
    st_ref, oacc_f, oacc_b = next(it), next(it), next(it)
    n = HG_TILE
    nt = t // n
    for d in range(2):
        for h in range(H_B):
            rows = slice(DV_B * h, DV_B * (h + 1))
            st_ref[d, rows, :] = s0_ref[0, d, h].T if has_s0 else jnp.zeros((DV_B, DK_B), f32)

    def finish_tile(idx):
        off = pl.multiple_of(idx * n, n)
        cols = pl.ds(off, n)
        o = oacc_f[:, cols] + oacc_b[:, cols]
        o3 = o.reshape(H_B, DV_B, n)
        y = o3 * lax.rsqrt(jnp.mean(o3 * o3, axis=1, keepdims=True) + EPS)
        gate = g_ref[:, cols]
        y = y.reshape(H_B * DV_B, n) * ng_ref[...] * (gate * _sigmoid(gate))
        o_ref[pl.ds(off, n), :] = y.T.astype(o_ref.dtype)

    def tiles_of(i):
        js = [i * HG_UNROLL + u for u in range(HG_UNROLL)]
        return js + [nt - 1 - j for j in js]

    def scan(i, c, finish_prev=False):
        if finish_prev:
            for t_idx in tiles_of(i - 1):
                finish_tile(t_idx)
        work = []
        small = None
        for u in range(HG_UNROLL):
            j = i * HG_UNROLL + u
            cols_f = pl.ds(pl.multiple_of(j * n, n), n)
            cols_b = pl.ds(pl.multiple_of((nt - 1 - j) * n, n), n)
            cf = _hgrn_common(q_ref[:, cols_f], i_ref[:, cols_f], ff_ref[:, cols_f], lb_ref[0], mats_ref, 0, st_ref)
            cb = _hgrn_common(q_ref[:, cols_b], i_ref[:, cols_b], fb_ref[:, cols_b], lb_ref[1], mats_ref, 1, st_ref)
            work.append((cols_f, cf, cols_b, cb))
            both = jnp.logical_and(cf["small"], cb["small"])
            small = both if small is None else jnp.logical_and(small, both)
        for cols_f, cf, cols_b, cb in work:
            oacc_f[:, cols_f] = _hgrn_fast_intra(cf, False)
            oacc_b[:, cols_b] = _hgrn_fast_intra(cb, True)

        @pl.when(jnp.logical_not(small))
        def _():
            for cols_f, cf, cols_b, cb in work:
                oacc_f[:, cols_f] = _hgrn_exact(cf, mats_ref[0, :, 4 * n:8 * n], False)
                oacc_b[:, cols_b] = _hgrn_exact(cb, mats_ref[1, :, 4 * n:8 * n], True)

        return c

    n_it = nt // HG_UNROLL
    if n_it % 2 == 0:
        lax.fori_loop(0, n_it // 2 + 1, scan, 0)
        lax.fori_loop(n_it // 2 + 1, n_it, functools.partial(scan, finish_prev=True), 0)
        for t_idx in tiles_of(n_it - 1):
            finish_tile(t_idx)
    else:
        lax.fori_loop(0, n_it, scan, 0)
        lax.fori_loop(0, nt, lambda idx, c: (finish_tile(idx), c)[1], 0)
    if emit_state:
        if n_prev:
            sfin_ref[0, 0:n_prev] = sprev_ref[0]
        for d in range(2):
            for h in range(H_B):
                sfin_ref[0, n_prev, d, h] = st_ref[d, DV_B * h:DV_B * (h + 1), :].T


def _hgrn(pt, lb, ng, mats, *, batch, t, s0=None, layer=0, emit_state=False, state_prev=None):
    def row(r):
        return pl.BlockSpec((256, t), lambda b: (r, b))

    def sspec(n_layers):
        return pl.BlockSpec((1, n_layers, 2, H_B, DK_B, DV_B), lambda b: (b, 0, 0, 0, 0, 0))

    n_prev = 0 if state_prev is None else state_prev.shape[1]
    in_specs = [row(0), row(1), row(2), row(3), row(4),
                _const_spec((2, 256, LANES), layer), _const_spec((256, LANES), layer), _const_spec(mats.shape)]
    args = [pt, pt, pt, pt, pt, lb, ng, mats]
    if s0 is not None:
        in_specs.append(pl.BlockSpec((1, None, 2, H_B, DK_B, DV_B), lambda b: (b, layer, 0, 0, 0, 0)))
        args.append(s0)
    if n_prev:
        in_specs.append(sspec(n_prev))
        args.append(state_prev)
    out_shape = [jax.ShapeDtypeStruct((batch * t, 256), bf16)]
    out_specs = [pl.BlockSpec((t, 256), lambda b: (b, 0))]
    if emit_state:
        out_shape.append(jax.ShapeDtypeStruct((batch, n_prev + 1, 2, H_B, DK_B, DV_B), f32))
        out_specs.append(sspec(n_prev + 1))
    return pl.pallas_call(
        functools.partial(_hgrn_kernel, t=t, has_s0=s0 is not None, emit_state=emit_state, n_prev=n_prev),
        out_shape=tuple(out_shape),
        grid=(batch,),
        in_specs=in_specs,
        out_specs=tuple(out_specs),
        scratch_shapes=[pltpu.VMEM((2, H_B * DV_B, DK_B), f32), pltpu.VMEM((256, t), f32),
                        pltpu.VMEM((256, t), f32)],
        compiler_params=_params("parallel"),
        name="hgrn",
    )(*args)


def _rope_tables(t):
    def cos_sin(dim):
        n_freq = dim // 4
        inv = ROPE_THETA ** (-jnp.arange(n_freq, dtype=f32) / n_freq)
        rows = t // GRID_W
        row_id = jnp.repeat(jnp.arange(rows, dtype=f32), GRID_W)
        col_id = jnp.tile(jnp.arange(GRID_W, dtype=f32), rows)
        ang = jnp.concatenate([row_id[:, None] * inv, col_id[:, None] * inv], axis=-1)
        return jnp.cos(ang), jnp.sin(ang)

    cos, sin = cos_sin(HD_A)
    zero = jnp.zeros_like(sin)
    tab_a = jnp.stack([jnp.tile(jnp.concatenate([cos, cos], -1), (1, H_A)),
                       jnp.tile(jnp.concatenate([zero, sin], -1), (1, H_A)),
                       jnp.tile(jnp.concatenate([-sin, zero], -1), (1, H_A))])
    cos, sin = cos_sin(ROPE_C)
    zero = jnp.zeros_like(sin)
    one64 = jnp.ones((t, NOPE_C), f32)
    zero64 = jnp.zeros((t, NOPE_C), f32)
    pad32 = jnp.zeros((t, LANES - NOPE_C - ROPE_C), f32)
    tab_c = jnp.stack([jnp.concatenate([one64, cos, cos, pad32 + 1.0], -1),
                       jnp.concatenate([zero64, zero, sin, pad32], -1),
                       jnp.concatenate([zero64, -sin, zero, pad32], -1)])
    return tab_a, tab_c


def _pad_rope_cols(x):
    pads = [(0, 0)] * (x.ndim - 1) + [(NOPE_C, LANES - NOPE_C - ROPE_C)]
    return jnp.pad(x, pads)


def kernel(x_prompt, x_sample, c, c_ctx, cache_attn_k, cache_attn_v, state_hgrn, cache_mla_ckv, cache_mla_krope,
           w_ada, b_ada, norm_sub, w_ffn_gate, w_ffn_up, w_ffn_down, w_in, w_out, attn_sink, hgrn_lb_logits,
           hgrn_out_norm, mla_q_norm, mla_kv_norm, mla_w_uq, mla_w_ukv, pool_w, pool_scale, final_norm):
    bp, tp, _ = x_prompt.shape
    bs, ts, _ = x_sample.shape
    np_, ns = bp * tp, bs * ts
    xp = x_prompt.reshape(np_, D_MODEL)
    xs = x_sample.reshape(ns, D_MODEL)

    n_vec = 1 + bs
    cvec = jnp.zeros((-(-n_vec // 8) * 8, D_MODEL), f32).at[0].set(c_ctx).at[1:n_vec].set(c)
    mods = _modulation(cvec, w_ada, b_ada).reshape(DEPTH, cvec.shape[0], N_MOD, D_MODEL)

    p_lb = jax.nn.softmax(hgrn_lb_logits.astype(f32), axis=0)
    lb_all = jnp.cumsum(p_lb, axis=0) - p_lb[0:1]
    tab_a, tab_c = _rope_tables(ts)
    mats = jnp.asarray(_hgrn_mats(), bf16)
    ffn_f32 = (w_ffn_gate, w_ffn_up, w_ffn_down)
    ffn_w = {(0, 0): tuple(_to_bf16(ffn_f32, 0, 0))}
    to_cast = [(l, idx) for l in range(DEPTH) for idx in range(2)][1:]
    grp_p = dict(base=0, rpm=np_)
    grp_s = dict(base=1, rpm=ts)
    tm, tm_mix = 512, 1024

    wb = lambda w: w.astype(bf16)
    w_row = wb(jnp.concatenate([w_in[:, :, 0:512], w_in[:, :, 1792:2176], _pad_rope_cols(w_in[:, :, 2176:2208]),
                                w_in[:, :, 2208:2464]], axis=2))
    w_t = wb(jnp.swapaxes(w_in[:, :, 512:1792], 1, 2))
    wq = wb(jnp.pad(mla_w_uq.reshape(DEPTH, Q_LORA_C, H_C, NOPE_C + ROPE_C),
                    ((0, 0), (0, 0), (0, 0), (0, LANES - NOPE_C - ROPE_C))).reshape(DEPTH, Q_LORA_C, H_C * LANES))
    wkv = mla_w_ukv.reshape(DEPTH, KV_LORA_C, H_C, NOPE_C + V_C)
    wk = wb(jnp.pad(wkv[..., :NOPE_C], ((0, 0), (0, 0), (0, 0), (0, LANES - NOPE_C)))
            .reshape(DEPTH, KV_LORA_C, H_C * LANES))
    wv = wb(jnp.swapaxes(wkv[..., NOPE_C:].reshape(DEPTH, KV_LORA_C, H_C * V_C), 1, 2))
    n_grp = len(POOL_WINDOWS)
    w_pool = wb(jnp.einsum('lgij,gh->lgihj', pool_w, jnp.eye(n_grp, dtype=f32))
                .reshape(DEPTH, n_grp * POOL_CH, n_grp * POOL_CH))
    lb = jnp.broadcast_to(lb_all[..., None], (DEPTH, 2, 256, LANES))
    ng = jnp.broadcast_to(jnp.tile(hgrn_out_norm, (1, H_B))[..., None], (DEPTH, 256, LANES))
    sink = attn_sink.astype(f32)
    qn = mla_q_norm.reshape(DEPTH, 1, Q_LORA_C)
    kvn = mla_kv_norm.reshape(DEPTH, 1, KV_LORA_C)
    pscale = pool_scale.reshape(DEPTH, 1, 256)
    wo = wb(w_out)
    norms = norm_sub.reshape(DEPTH * 3, 1, D_MODEL)
    ctx_c = (cache_mla_ckv, _pad_rope_cols(cache_mla_krope))

    new_k = new_v = new_st = new_ckv = new_kr = None
    for l in range(DEPTH):
        def ffn(x, grp, j, idx, mix=None, final_g=None):
            side = None
            if to_cast:
                nxt = to_cast.pop(0)
                side = nxt + (ffn_f32,)
            res = _ffn(x, mods, norms, ffn_w[(l, idx)], layer=l, j=j, tm=tm, mix=mix, final_g=final_g, side=side,
                       **grp)
            if side is not None:
                ffn_w[nxt] = tuple(res[1:])
            return res[0]

        xp = ffn(xp, grp_p, 0, 0)
        xs = ffn(xs, grp_s, 0, 0)

        pr, pt = _mixer_in(xp, mods, norms, w_row, w_t, layer=l, tm=tm_mix, **grp_p)
        oa, new_k, new_v = _attn_ctx(pr, sink, None if l == 0 else (new_k, new_v), layer=l, batch=bp, t=tp)
        ob, new_st = _hgrn(pt, lb, ng, mats, layer=l, batch=bp, t=tp, emit_state=True, state_prev=new_st)
        oc, new_ckv, new_kr = _mla(pr, qn, kvn, wq, wk, wv, layer=l, batch=bp, t=tp, tq=tp, emit_cache=True,
                                   cache_prev=None if l == 0 else (new_ckv, new_kr))
        od = _pool(pr, w_pool, pscale, layer=l, batch=bp, t=tp, seqs=POOL_CTX_SEQS)
        mix_p = (oa, ob, oc, od, wo)

        pr, pt = _mixer_in(xs, mods, norms, w_row, w_t, layer=l, tm=tm_mix, **grp_s)
        oa = _attn_lat(pr, sink, cache_attn_k, cache_attn_v, tab_a, layer=l, batch=bs, t=ts)
        (ob,) = _hgrn(pt, lb, ng, mats, layer=l, batch=bs, t=ts, s0=state_hgrn)
        (oc,) = _mla(pr, qn, kvn, wq, wk, wv, layer=l, batch=bs, t=ts, tq=512, ctx=ctx_c, tab=tab_c)
        od = _pool(pr, w_pool, pscale, layer=l, batch=bs, t=ts, seqs=2)
        mix_s = (oa, ob, oc, od, wo)

        fin = final_norm.reshape(1, D_MODEL) if l == DEPTH - 1 else None
        xp = ffn(xp, grp_p, 2, 1, mix_p, fin)
        xs = ffn(xs, grp_s, 2, 1, mix_s, fin)

    return (xp.reshape(bp, tp, D_MODEL), xs.reshape(bs, ts, D_MODEL), new_k, new_v, new_st, new_ckv, new_kr)
```
